```python
import jax, jax.numpy as jnp
from jax import lax
import numpy as np

D_MODEL = 1024
BATCH = 4
SEQ = 4096
DEPTH = 4
DEC_BATCH = 32
DEC_SEQ = 1
PAST_LEN = 8192
PAGE_SIZE = 128

N_MIXERS = 4
N_REPEAT = DEPTH // N_MIXERS
P_DIM = 256
EPS = 1e-6

A_HEADS = 8
A_DK = 128
A_DV = 256
A_WIDTH = A_HEADS * A_DV
A_CHUNK = 128
A_IN = 2 * A_HEADS * A_DK + 3 * A_WIDTH + 2 * A_HEADS
F_BIAS = 3.0

B_WIDTH = 2 * D_MODEL
CONV_W = 3

C_HEADS = 8
C_DH = 128
C_WIDTH = C_HEADS * C_DH
C_GROUPS = ((128, 1), (512, 4), (2048, 16))
C_BLOCK = 128
ROPE_DIM = C_DH // 4
ROPE_THETA = 500000.0
C_IN = 3 * len(C_GROUPS) * C_WIDTH + C_WIDTH

D_WIDTH = 2 * D_MODEL
D_WINDOWS = (2, 4, 8, 16)
D_GROUP = D_WIDTH // len(D_WINDOWS)
POOL_STATE = max(D_WINDOWS) - 1

kernel_name = 'hybrid_mlstm_conv_dilated_pool_step'

ATTN_KEYS = tuple(kv + '_w' + str(win) for win, _ in C_GROUPS for kv in ('k', 'v'))
STATE_KEYS = ('mlstm_C', 'mlstm_n', 'mlstm_m', 'conv') + ATTN_KEYS + ('pool',)


def rms_norm(x, g):
    x32 = x.astype(jnp.float32)
    y = x32 * lax.rsqrt(jnp.mean(x32 * x32, axis=-1, keepdims=True) + EPS)
    return (y * g.astype(jnp.float32)).astype(x.dtype)


def rope_partial(x, pos):
    half = ROPE_DIM // 2
    inv = ROPE_THETA ** (-jnp.arange(half, dtype=jnp.float32) / half)
    ang = pos.astype(jnp.float32)[:, None] * inv[None, :]
    cos = jnp.cos(ang)[None, :, None, :]
    sin = jnp.sin(ang)[None, :, None, :]
    x32 = x.astype(jnp.float32)
    x1 = x32[..., :half]
    x2 = x32[..., half:ROPE_DIM]
    out = jnp.concatenate([x1 * cos - x2 * sin, x2 * cos + x1 * sin, x32[..., ROPE_DIM:]], axis=-1)
    return out.astype(x.dtype)


def mlstm_chunk(state, chunk):
    C, n, m = state
    q, k, v, ig, lf = chunk
    L = q.shape[2]
    b = jnp.cumsum(lf, axis=-1)
    causal = jnp.tril(jnp.ones((L, L), dtype=bool))
    log_d = jnp.where(causal, b[..., :, None] - b[..., None, :] + ig[..., None, :], -jnp.inf)
    inter = b + m[..., None]
    m_t = jnp.maximum(inter, jnp.max(log_d, axis=-1))
    dmat = jnp.exp(log_d - m_t[..., None])
    g = jnp.exp(inter - m_t)
    s = jnp.einsum('bhtk,bhsk->bhts', q, k) * dmat
    num = jnp.einsum('bhts,bhsv->bhtv', s, v) + g[..., None] * jnp.einsum('bhvk,bhtk->bhtv', C, q)
    den = jnp.sum(s, axis=-1) + g * jnp.einsum('bhk,bhtk->bht', n, q)
    h = num / jnp.maximum(jnp.abs(den), jnp.exp(-m_t))[..., None]
    b_last = b[..., -1]
    a = b_last[..., None] - b + ig
    m_new = jnp.maximum(b_last + m, jnp.max(a, axis=-1))
    decay = jnp.exp(b_last + m - m_new)
    w = jnp.exp(a - m_new[..., None])
    C_new = decay[..., None, None] * C + jnp.einsum('bhs,bhsv,bhsk->bhvk', w, v, k)
    n_new = decay[..., None] * n + jnp.einsum('bhs,bhsk->bhk', w, k)
    return (C_new, n_new, m_new), h


def mlstm_mixer(h, w_in, b_if, norm_g, w_out, state, chunk):
    f32 = jnp.float32
    Bn, T = h.shape[:2]
    u = h @ w_in
    qk = A_HEADS * A_DK
    q, k, v, o, z, gates = jnp.split(
        u, [qk, 2 * qk, 2 * qk + A_WIDTH, 2 * qk + 2 * A_WIDTH, 2 * qk + 3 * A_WIDTH], axis=-1)
    gates = gates.astype(f32) + b_if.astype(f32)
    ig = gates[..., :A_HEADS]
    lf = jax.nn.log_sigmoid(gates[..., A_HEADS:])
    nc = T // chunk

    def to_chunks(a, dh):
        return a.astype(f32).reshape(Bn, nc, chunk, A_HEADS, dh).transpose(1, 0, 3, 2, 4)

    def gate_chunks(a):
        return a.reshape(Bn, nc, chunk, A_HEADS).transpose(1, 0, 3, 2)

    qs = to_chunks(q, A_DK)
    ks = to_chunks(k, A_DK) * (A_DK ** -0.5)
    vs = to_chunks(v, A_DV)
    state = tuple(s_.astype(f32) for s_ in state)
    new_state, hs = lax.scan(mlstm_chunk, state, (qs, ks, vs, gate_chunks(ig), gate_chunks(lf)))
    hs = hs.transpose(1, 0, 3, 2, 4).reshape(Bn, T, A_HEADS, A_DV)
    hs = hs * lax.rsqrt(jnp.mean(hs * hs, axis=-1, keepdims=True) + EPS)
    hs = hs.reshape(Bn, T, A_WIDTH) * norm_g.astype(f32)
    y = (hs * jax.nn.sigmoid(o.astype(f32))).astype(h.dtype) * jax.nn.silu(z)
    return y @ w_out, new_state


def shortconv_mixer(h, w_in, conv_w, w_out, buf):
    u = h @ w_in
    bg, cg, xb, z = jnp.split(u, 4, axis=-1)
    cx = cg * xb
    T = cx.shape[1]
    cpad = jnp.concatenate([buf.astype(cx.dtype), cx], axis=1)
    y = conv_w[0] * cpad[:, 0:T]
    for j in range(1, CONV_W):
        y = y + conv_w[j] * cpad[:, j:j + T]
    out = (bg * y * jax.nn.silu(z)) @ w_out
    return out, cpad[:, -(CONV_W - 1):]


def masked_softmax_stats(s, mask):
    s = jnp.where(mask, s, -1e30)
    mx = jnp.max(s, axis=-1)
    pr = jnp.where(mask, jnp.exp(s - mx[..., None]), 0.0)
    l = jnp.sum(pr, axis=-1)
    return pr / l[..., None], mx, l


def dilated_prompt(q, k, v, dil, span):
    f32 = jnp.float32
    Bn, S, H, Dh = q.shape
    n = S // dil
    nb = -(-n // C_BLOCK)
    npad = nb * C_BLOCK

    def split(a):
        a = a.astype(f32).reshape(Bn, n, dil, H, Dh).transpose(0, 2, 1, 3, 4)
        a = jnp.pad(a, ((0, 0), (0, 0), (0, npad - n), (0, 0), (0, 0)))
        return a.reshape(Bn, dil, nb, C_BLOCK, H, Dh)

    def with_prev(a):
        prev = jnp.pad(a[:, :, :-1], ((0, 0), (0, 0), (1, 0), (0, 0), (0, 0), (0, 0)))
        return jnp.concatenate([prev, a], axis=3)

    qb = split(q)
    kk = with_prev(split(k))
    vv = with_prev(split(v))
    s = jnp.einsum('brnqhe,brnkhe->brnhqk', qb, kk) * (Dh ** -0.5)
    qi = jnp.arange(C_BLOCK)[:, None]
    kj = jnp.arange(2 * C_BLOCK)[None, :]
    dist = qi + C_BLOCK - kj
    band = (dist >= 0) & (dist <= span)
    first = (jnp.arange(nb) > 0)[:, None, None] | (kj >= C_BLOCK)[None]
    mask = (band[None] & first)[None, None, :, None]
    pr, mx, l = masked_softmax_stats(s, mask)
    o = jnp.einsum('brnhqk,brnkhe->brnqhe', pr, vv)
    o = o.reshape(Bn, dil, npad, H, Dh)[:, :, :n].transpose(0, 2, 1, 3, 4).reshape(Bn, S, H, Dh)

    def back(a):
        a = a.transpose(0, 1, 2, 4, 3).reshape(Bn, dil, npad, H)[:, :, :n]
        return a.transpose(0, 2, 1, 3).reshape(Bn, S, H)

    return o, back(mx), back(l)


def dilated_sample(q, kall, vall, dil, span, n_buf):
    f32 = jnp.float32
    T, Dh = q.shape[1], q.shape[-1]
    idx = n_buf + jnp.arange(T)[:, None] - dil * jnp.arange(span + 1)[None, :]
    valid = idx >= 0
    idx = jnp.maximum(idx, 0)
    kg = kall[:, idx].astype(f32)
    vg = vall[:, idx].astype(f32)
    s = jnp.einsum('bthe,btjhe->bhtj', q.astype(f32), kg) * (Dh ** -0.5)
    pr, mx, l = masked_softmax_stats(s, valid[None, None])
    o = jnp.einsum('bhtj,btjhe->bthe', pr, vg)
    return o, mx.transpose(0, 2, 1), l.transpose(0, 2, 1)


def merge_groups(outs):
    mx = jnp.max(jnp.stack([m for _, m, _ in outs]), axis=0)
    ws = [l * jnp.exp(m - mx) for _, m, l in outs]
    num = ws[0][..., None] * outs[0][0]
    den = ws[0]
    for (o, _, _), w in zip(outs[1:], ws[1:]):
        num = num + w[..., None] * o
        den = den + w
    return num / den[..., None]


def dilated_mixer(h, w_in, w_out, pos, caches):
    Bn, T = h.shape[:2]
    u = h @ w_in
    parts = jnp.split(u, 3 * len(C_GROUPS) + 1, axis=-1)
    z = parts[-1]
    outs, new = [], []
    for g, (win, dil) in enumerate(C_GROUPS):
        q, k, v = [a.reshape(Bn, T, C_HEADS, C_DH) for a in parts[3 * g:3 * g + 3]]
        q = rope_partial(q, pos)
        k = rope_partial(k, pos)
        span = win // dil
        if caches is None:
            outs.append(dilated_prompt(q, k, v, dil, span))
            nkeep = min(win, T)
            new += [k[:, T - nkeep:], v[:, T - nkeep:]]
        else:
            kc, vc = caches[2 * g], caches[2 * g + 1]
            kall = jnp.concatenate([kc.astype(k.dtype), k], axis=1)
            vall = jnp.concatenate([vc.astype(v.dtype), v], axis=1)
            outs.append(dilated_sample(q, kall, vall, dil, span, kc.shape[1]))
            new += [k, v]
    o = merge_groups(outs).reshape(Bn, T, C_WIDTH).astype(h.dtype)
    return (o * jax.nn.silu(z)) @ w_out, new


def pool_mixer(h, w_in, w_grp, scale, w_out, buf, pos):
    f32 = jnp.float32
    u = h @ w_in
    xp, z = jnp.split(u, 2, axis=-1)
    T = xp.shape[1]
    xpad = jnp.concatenate([buf.astype(xp.dtype), xp], axis=1).astype(f32)
    cs = jnp.pad(jnp.cumsum(xpad, axis=1), ((0, 0), (1, 0), (0, 0)))
    end = cs[:, POOL_STATE + 1:]
    xcur = xpad[:, POOL_STATE:]
    outs = []
    for g, w in enumerate(D_WINDOWS):
        sl = slice(g * D_GROUP, (g + 1) * D_GROUP)
        start = cs[:, POOL_STATE + 1 - w:POOL_STATE + 1 - w + T, sl]
        cnt = jnp.minimum(w, pos + 1).astype(f32)[None, :, None]
        r = (end[..., sl] - start) / cnt - xcur[..., sl]
        outs.append(r @ w_grp[g].astype(f32))
    y = jnp.concatenate(outs, axis=-1) * scale.astype(f32)
    out = (y.astype(h.dtype) * jax.nn.silu(z)) @ w_out
    return out, xpad[:, -POOL_STATE:].astype(xp.dtype)


def per_layer_embed(x, p, pe_w, pg_w):
    gate = jax.nn.sigmoid(x @ pg_w)
    return x + gate * (p.astype(x.dtype) @ pe_w)


def run_group(x, p, pos, W, st):
    f32 = jnp.float32
    Bn, T = x.shape[:2]
    new = {name: [] for name in STATE_KEYS}
    for i in range(DEPTH):
        kind, r = i % N_MIXERS, i // N_MIXERS
        h = rms_norm(x, W['norm_g'][i])
        if kind == 0:
            if st is None:
                s0 = (jnp.zeros((Bn, A_HEADS, A_DV, A_DK), f32), jnp.zeros((Bn, A_HEADS, A_DK), f32),
                      jnp.zeros((Bn, A_HEADS), f32))
                chunk = A_CHUNK
            else:
                s0 = (st['mlstm_C'][r], st['mlstm_n'][r], st['mlstm_m'][r])
                chunk = T
            y, (c_new, n_new, m_new) = mlstm_mixer(h, W['a_w_in'][r], W['a_b_if'][r], W['a_norm_g'][r],
                                                   W['a_w_out'][r], s0, chunk)
            new['mlstm_C'].append(c_new)
            new['mlstm_n'].append(n_new)
            new['mlstm_m'].append(m_new)
        elif kind == 1:
            buf = jnp.zeros((Bn, CONV_W - 1, B_WIDTH), x.dtype) if st is None else st['conv'][r]
            y, nbuf = shortconv_mixer(h, W['b_w_in'][r], W['b_conv_w'][r], W['b_w_out'][r], buf)
            new['conv'].append(nbuf)
        elif kind == 2:
            caches = None if st is None else [st[name][r] for name in ATTN_KEYS]
            y, nkv = dilated_mixer(h, W['c_w_in'][r], W['c_w_out'][r], pos, caches)
            for name, a in zip(ATTN_KEYS, nkv):
                new[name].append(a)
        else:
            buf = jnp.zeros((Bn, POOL_STATE, D_WIDTH), x.dtype) if st is None else st['pool'][r]
            y, nbuf = pool_mixer(h, W['d_w_in'][r], W['d_w_grp'][r], W['d_scale'][r], W['d_w_out'][r], buf, pos)
            new['pool'].append(nbuf)
        x = x + y.astype(x.dtype)
        x = per_layer_embed(x, p[i], W['pe_w'][i], W['pg_w'][i])
    return rms_norm(x, W['final_g']), {name: jnp.stack(v) for name, v in new.items()}


def setup_inputs(seed: int = 0) -> dict:
    key = jax.random.key(seed)
    ks = iter(jax.random.split(key, 48))
    f32 = jnp.float32

    def nrm(shape, scale=1.0):
        return jax.random.normal(next(ks), shape, f32) * scale

    R = N_REPEAT
    w128, w512, w2048 = [min(win, PAST_LEN) for win, _ in C_GROUPS]
    kv = lambda nbuf: nrm((R, DEC_BATCH, nbuf, C_HEADS, C_DH))
    return {
        'x_prompt': nrm((BATCH, SEQ, D_MODEL)),
        'x_sample': nrm((DEC_BATCH, DEC_SEQ, D_MODEL)),
        'state_mlstm_C': nrm((R, DEC_BATCH, A_HEADS, A_DV, A_DK), 0.1),
        'state_mlstm_n': nrm((R, DEC_BATCH, A_HEADS, A_DK), 0.1),
        'state_mlstm_m': nrm((R, DEC_BATCH, A_HEADS)),
        'state_conv': nrm((R, DEC_BATCH, CONV_W - 1, B_WIDTH)),
        'cache_k_w128': kv(w128),
        'cache_v_w128': kv(w128),
        'cache_k_w512': kv(w512),
        'cache_v_w512': kv(w512),
        'cache_k_w2048': kv(w2048),
        'cache_v_w2048': kv(w2048),
        'state_pool': nrm((R, DEC_BATCH, POOL_STATE, D_WIDTH)),
        'p_prompt': nrm((DEPTH, BATCH, SEQ, P_DIM)),
        'p_sample': nrm((DEPTH, DEC_BATCH, DEC_SEQ, P_DIM)),
        'norm_g': 1.0 + nrm((DEPTH, D_MODEL), 0.02),
        'pe_w': nrm((DEPTH, P_DIM, D_MODEL), P_DIM ** -0.5),
        'pg_w': nrm((DEPTH, D_MODEL, D_MODEL), D_MODEL ** -0.5),
        'final_g': 1.0 + nrm((D_MODEL,), 0.02),
        'a_w_in': nrm((R, D_MODEL, A_IN), D_MODEL ** -0.5),
        'a_b_if': jnp.concatenate([nrm((R, A_HEADS), 0.1), F_BIAS + nrm((R, A_HEADS), 0.1)], axis=-1),
        'a_norm_g': 1.0 + nrm((R, A_WIDTH), 0.02),
        'a_w_out': nrm((R, A_WIDTH, D_MODEL), A_WIDTH ** -0.5),
        'b_w_in': nrm((R, D_MODEL, 4 * B_WIDTH), D_MODEL ** -0.5),
        'b_conv_w': nrm((R, CONV_W, B_WIDTH), CONV_W ** -0.5),
        'b_w_out': nrm((R, B_WIDTH, D_MODEL), B_WIDTH ** -0.5),
        'c_w_in': nrm((R, D_MODEL, C_IN), D_MODEL ** -0.5),
        'c_w_out': nrm((R, C_WIDTH, D_MODEL), C_WIDTH ** -0.5),
        'd_w_in': nrm((R, D_MODEL, 2 * D_WIDTH), D_MODEL ** -0.5),
        'd_w_grp': nrm((R, len(D_WINDOWS), D_GROUP, D_GROUP), D_GROUP ** -0.5),
        'd_scale': 1.0 + nrm((R, D_WIDTH), 0.02),
        'd_w_out': nrm((R, D_WIDTH, D_MODEL), D_WIDTH ** -0.5),
    }


def reference(x_prompt, x_sample, state_mlstm_C, state_mlstm_n, state_mlstm_m, state_conv,
              cache_k_w128, cache_v_w128, cache_k_w512, cache_v_w512, cache_k_w2048, cache_v_w2048,
              state_pool, p_prompt, p_sample, norm_g, pe_w, pg_w, final_g,
              a_w_in, a_b_if, a_norm_g, a_w_out, b_w_in, b_conv_w, b_w_out,
              c_w_in, c_w_out, d_w_in, d_w_grp, d_scale, d_w_out):
    W = {'norm_g': norm_g, 'pe_w': pe_w, 'pg_w': pg_w, 'final_g': final_g,
         'a_w_in': a_w_in, 'a_b_if': a_b_if, 'a_norm_g': a_norm_g, 'a_w_out': a_w_out,
         'b_w_in': b_w_in, 'b_conv_w': b_conv_w, 'b_w_out': b_w_out,
         'c_w_in': c_w_in, 'c_w_out': c_w_out,
         'd_w_in': d_w_in, 'd_w_grp': d_w_grp, 'd_scale': d_scale, 'd_w_out': d_w_out}
    st = {'mlstm_C': state_mlstm_C, 'mlstm_n': state_mlstm_n, 'mlstm_m': state_mlstm_m,
          'conv': state_conv, 'k_w128': cache_k_w128, 'v_w128': cache_v_w128,
          'k_w512': cache_k_w512, 'v_w512': cache_v_w512,
          'k_w2048': cache_k_w2048, 'v_w2048': cache_v_w2048, 'pool': state_pool}
    pos_p = jnp.arange(x_prompt.shape[1])
    pos_s = PAST_LEN + jnp.arange(x_sample.shape[1])
    y_prompt, np_ = run_group(x_prompt, p_prompt, pos_p, W, None)
    y_sample, ns = run_group(x_sample, p_sample, pos_s, W, st)
    return (y_prompt, y_sample,
            np_['mlstm_C'], ns['mlstm_C'], np_['mlstm_n'], ns['mlstm_n'], np_['mlstm_m'], ns['mlstm_m'],
            np_['conv'], ns['conv'],
            np_['k_w128'], ns['k_w128'], np_['v_w128'], ns['v_w128'],
            np_['k_w512'], ns['k_w512'], np_['v_w512'], ns['v_w512'],
            np_['k_w2048'], ns['k_w2048'], np_['v_w2048'], ns['v_w2048'],
            np_['pool'], ns['pool'])
```

```python
import functools

import jax
import jax.numpy as jnp
from jax import lax
from jax.experimental import pallas as pl
from jax.experimental.pallas import tpu as pltpu

F32 = jnp.float32
BF16 = jnp.bfloat16

EPS = 1e-6
PAST_LEN = 8192

A_HEADS = 8
A_DK = 128
A_DV = 256
A_WIDTH = A_HEADS * A_DV
A_CHUNK = 128

CONV_W = 3

C_HEADS = 8
C_DH = 128
C_WIDTH = C_HEADS * C_DH
C_GROUPS = ((128, 1), (512, 4), (2048, 16))
C_BLOCK = 128
ROPE_DIM = C_DH // 4
ROPE_THETA = 500000.0

D_WINDOWS = (2, 4, 8, 16)
POOL_STATE = max(D_WINDOWS) - 1
POOL_PAD = POOL_STATE + 1

LANES = 128
VMEM_LIMIT_BYTES = 56 * 2 ** 20

NT_DIMS = (((1,), (1,)), ((), ()))
TN_DIMS = (((0,), (0,)), ((), ()))


def _params(*semantics):
    return pltpu.CompilerParams(dimension_semantics=semantics, vmem_limit_bytes=VMEM_LIMIT_BYTES)


def _resident(shape):
    zeros = (0,) * len(shape)
    return pl.BlockSpec(shape, lambda *_: zeros, pipeline_mode=pl.Buffered(1))


def _whole(shape):
    zeros = (0,) * len(shape)
    return pl.BlockSpec(shape, lambda *_: zeros)


def _mm(a, b):
    return jnp.dot(a, b, preferred_element_type=F32)


def _rms(x, g):
    return x * lax.rsqrt(jnp.mean(x * x, axis=-1, keepdims=True) + EPS) * g


def _sigmoid(x):
    return 1.0 / (1.0 + jnp.exp(-x))


def _silu(x):
    return x * _sigmoid(x)


def _log_sigmoid(x):
    return jnp.minimum(x, 0.0) - jnp.log(1.0 + jnp.exp(-jnp.abs(x)))


def _embed(x1, p, pg_ref, pe_ref):
    gate = _sigmoid(_mm(x1.astype(BF16), pg_ref[...]))
    return x1 + gate * _mm(p.astype(BF16), pe_ref[...])


def _a_in_kernel(x_ref, g_ref, w_ref, wg_ref, bif_ref,
                 q_ref, k_ref, v_ref, o_ref, z_ref, gate_ref):
    h = _rms(x_ref[0], g_ref[...]).astype(BF16)
    qk = A_HEADS * A_DK

    def proj(lo, width):
        return _mm(h, w_ref[:, lo:lo + width])

    q_ref[0] = proj(0, qk).astype(BF16)
    k_ref[0] = (proj(qk, qk) * (A_DK ** -0.5)).astype(BF16)
    v_ref[0] = proj(2 * qk, A_WIDTH).astype(BF16)
    o_ref[0] = proj(2 * qk + A_WIDTH, A_WIDTH).astype(BF16)
    z_ref[0] = proj(2 * qk + 2 * A_WIDTH, A_WIDTH).astype(BF16)
    gates = _mm(h, wg_ref[...]) + bif_ref[...]
    lane = lax.broadcasted_iota(jnp.int32, gates.shape, 1)
    gate_ref[0] = jnp.where(lane < A_HEADS, gates, _log_sigmoid(gates))


def _a_in(x, g, w, wg, bif, tm):
    B, T, D = x.shape
    qk = A_HEADS * A_DK
    row = lambda width: pl.BlockSpec((1, tm, width), lambda b, t: (b, t, 0))
    return pl.pallas_call(
        _a_in_kernel,
        grid=(B, T // tm),
        in_specs=[row(D), _resident(g.shape), _resident(w.shape), _resident(wg.shape), _resident(bif.shape)],
        out_specs=[row(qk), row(qk), row(A_WIDTH), row(A_WIDTH), row(A_WIDTH), row(LANES)],
        out_shape=[jax.ShapeDtypeStruct((B, T, qk), BF16), jax.ShapeDtypeStruct((B, T, qk), BF16),
                   jax.ShapeDtypeStruct((B, T, A_WIDTH), BF16), jax.ShapeDtypeStruct((B, T, A_WIDTH), BF16),
                   jax.ShapeDtypeStruct((B, T, A_WIDTH), BF16), jax.ShapeDtypeStruct((B, T, LANES), F32)],
        compiler_params=_params("parallel", "parallel"),
        name="mlstm_in",
    )(x, g, w, wg, bif)


def _head_out(hh, ng, o, z):
    hn = hh * lax.rsqrt(jnp.mean(hh * hh, axis=-1, keepdims=True) + EPS) * ng
    return (hn * _sigmoid(o) * _silu(z)).astype(BF16)


def _mlstm_chunk_kernel(q_ref, k_ref, v_ref, o_ref, z_ref, gate_ref, ng_ref,
                        y_ref, c_out, n_out, m_out, c_s, n_s, m_s):
    chunk = pl.program_id(1)

    @pl.when(chunk == 0)
    def _():
        c_s[...] = jnp.zeros_like(c_s)
        n_s[...] = jnp.zeros_like(n_s)
        m_s[...] = jnp.zeros_like(m_s)

    L = q_ref.shape[1]
    gates = gate_ref[0]
    row = lax.broadcasted_iota(jnp.int32, gates.shape, 0)
    csum = gates
    step = 1
    while step < L:
        csum = csum + jnp.where(row >= step, pltpu.roll(csum, step, 0), 0.0)
        step *= 2
    gates_t = gates.T
    csum_t = csum.T
    ti = lax.broadcasted_iota(jnp.int32, (L, L), 0)
    si = lax.broadcasted_iota(jnp.int32, (L, L), 1)
    causal = ti >= si

    for h in range(A_HEADS):
        ks = slice(h * A_DK, (h + 1) * A_DK)
        vs = slice(h * A_DV, (h + 1) * A_DV)
        ig_col = gates[:, h:h + 1]
        b_col = csum[:, A_HEADS + h:A_HEADS + h + 1]
        ig_row = gates_t[h:h + 1, :]
        b_row = csum_t[A_HEADS + h:A_HEADS + h + 1, :]
        m_prev = m_s[h:h + 1, 0:1]
        qh = q_ref[0, :, ks]
        kh = k_ref[0, :, ks]
        vh = v_ref[0, :, vs]
        c_h = c_s[h]
        n_h = n_s[h:h + 1, :]

        log_d = jnp.where(causal, b_col + (ig_row - b_row), -jnp.inf)
        inter = b_col + m_prev
        m_t = jnp.maximum(inter, jnp.max(log_d, axis=-1, keepdims=True))
        dmat = jnp.exp(log_d - m_t)
        g = jnp.exp(inter - m_t)
        s = lax.dot_general(qh, kh, NT_DIMS, preferred_element_type=F32) * dmat
        num = _mm(s.astype(BF16), vh) + g * lax.dot_general(qh, c_h.astype(BF16), NT_DIMS,
                                                            preferred_element_type=F32)
        den = (jnp.sum(s, axis=-1, keepdims=True)
               + g * jnp.sum(qh.astype(F32) * n_h, axis=-1, keepdims=True))
        hh = num / jnp.maximum(jnp.abs(den), jnp.exp(-m_t))

        b_last = b_col[L - 1:L, :]
        a_col = b_last - b_col + ig_col
        m_new = jnp.maximum(b_last + m_prev, jnp.max(a_col, axis=0, keepdims=True))
        decay = jnp.exp(b_last + m_prev - m_new)
        w_col = jnp.exp(a_col - m_new)
        wv = (w_col * vh.astype(F32)).astype(BF16)
        c_s[h] = decay * c_h + lax.dot_general(wv, kh, TN_DIMS, preferred_element_type=F32)
        n_s[h:h + 1, :] = decay * n_h + jnp.sum(w_col * kh.astype(F32), axis=0, keepdims=True)
        m_s[h:h + 1, :] = jnp.broadcast_to(m_new, (1, LANES))

        y_ref[0, :, vs] = _head_out(hh, ng_ref[:, vs], o_ref[0, :, vs].astype(F32),
                                    z_ref[0, :, vs].astype(F32))

    @pl.when(chunk == pl.num_programs(1) - 1)
    def _():
        c_out[0] = c_s[...]
        n_out[0] = n_s[...]
        m_out[0] = m_s[...]


def _mlstm_prompt(q, k, v, o, z, gates, ng):
    B, T, _ = q.shape
    L = A_CHUNK
    qk = A_HEADS * A_DK
    row = lambda width: pl.BlockSpec((1, L, width), lambda b, c: (b, c, 0))
    state = lambda *shape: pl.BlockSpec((1,) + shape, lambda b, c: (b,) + (0,) * len(shape))
    return pl.pallas_call(
        _mlstm_chunk_kernel,
        grid=(B, T // L),
        in_specs=[row(qk), row(qk), row(A_WIDTH), row(A_WIDTH), row(A_WIDTH), row(LANES), _resident(ng.shape)],
        out_specs=[row(A_WIDTH), state(A_HEADS, A_DV, A_DK), state(A_HEADS, A_DK), state(A_HEADS, LANES)],
        out_shape=[jax.ShapeDtypeStruct((B, T, A_WIDTH), BF16),
                   jax.ShapeDtypeStruct((B, A_HEADS, A_DV, A_DK), F32),
                   jax.ShapeDtypeStruct((B, A_HEADS, A_DK), F32),
                   jax.ShapeDtypeStruct((B, A_HEADS, LANES), F32)],
        scratch_shapes=[pltpu.VMEM((A_HEADS, A_DV, A_DK), F32), pltpu.VMEM((A_HEADS, A_DK), F32),
                        pltpu.VMEM((A_HEADS, LANES), F32)],
        compiler_params=_params("parallel", "arbitrary"),
        name="mlstm_chunks",
    )(q, k, v, o, z, gates, ng)


def _mlstm_step_kernel(q_ref, k_ref, v_ref, o_ref, z_ref, gate_ref, ng_ref, c_ref, n_ref, m_ref,
                       y_ref, c_out, n_out, m_out):
    gates = gate_ref[0]
    row = lax.broadcasted_iota(jnp.int32, (LANES, A_DV), 0)
    for h in range(A_HEADS):
        ks = slice(h * A_DK, (h + 1) * A_DK)
        vs = slice(h * A_DV, (h + 1) * A_DV)
        ig = gates[:, h:h + 1]
        lf = gates[:, A_HEADS + h:A_HEADS + h + 1]
        m_prev = m_ref[0, :, h:h + 1]
        qh = q_ref[0, :, ks].astype(F32)
        kh = k_ref[0, :, ks].astype(F32)
        vh = v_ref[0, :, vs].astype(F32)
        c_h = c_ref[0, h]
        n_h = n_ref[0, h:h + 1, :]

        inter = lf + m_prev
        m_t = jnp.maximum(inter, ig)
        dm = jnp.exp(ig - m_t)
        g = jnp.exp(inter - m_t)
        s = jnp.sum(qh * kh, axis=-1, keepdims=True) * dm
        q8 = jnp.broadcast_to(qh, (8, A_DK)).astype(BF16)
        cq = lax.dot_general(q8, c_h.astype(BF16), NT_DIMS, preferred_element_type=F32)[0:1, :]
        num = s * vh + g * cq
        den = s + g * jnp.sum(qh * n_h, axis=-1, keepdims=True)
        hh = num / jnp.maximum(jnp.abs(den), jnp.exp(-m_t))

        v_pad = jnp.where(row == 0, jnp.broadcast_to(dm * vh, (LANES, A_DV)), 0.0).astype(BF16)
        k_pad = jnp.broadcast_to(kh, (LANES, A_DK)).astype(BF16)
        c_out[0, h] = g * c_h + lax.dot_general(v_pad, k_pad, TN_DIMS, preferred_element_type=F32)
        n_out[0, h:h + 1, :] = g * n_h + dm * kh
        m_out[0, :, h:h + 1] = m_t

        y_ref[0, :, vs] = _head_out(hh, ng_ref[:, vs], o_ref[0, :, vs].astype(F32),
                                    z_ref[0, :, vs].astype(F32))


def _mlstm_sample(q, k, v, o, z, gates, ng, c0, n0, m0):
    Bs = q.shape[0]
    qk = A_HEADS * A_DK
    row = lambda width: pl.BlockSpec((1, 1, width), lambda b: (b, 0, 0))
    c_spec = pl.BlockSpec((1, A_HEADS, A_DV, A_DK), lambda b: (b, 0, 0, 0))
    n_spec = pl.BlockSpec((1, A_HEADS, A_DK), lambda b: (b, 0, 0))
    m_spec = pl.BlockSpec((1, 1, A_HEADS), lambda b: (b, 0, 0))
    return pl.pallas_call(
        _mlstm_step_kernel,
        grid=(Bs,),
        in_specs=[row(qk), row(qk), row(A_WIDTH), row(A_WIDTH), row(A_WIDTH), row(LANES), _resident(ng.shape),
                  c_spec, n_spec, m_spec],
        out_specs=[row(A_WIDTH), c_spec, n_spec, m_spec],
        out_shape=[jax.ShapeDtypeStruct((Bs, 1, A_WIDTH), BF16),
                   jax.ShapeDtypeStruct(c0.shape, F32), jax.ShapeDtypeStruct(n0.shape, F32),
                   jax.ShapeDtypeStruct(m0.shape, F32)],
        compiler_params=_params("parallel"),
        name="mlstm_step",
    )(q, k, v, o, z, gates, ng, c0, n0, m0)


def _out_kernel(y_ref, x_ref, p_ref, wo_ref, pg_ref, pe_ref, xo_ref):
    x1 = x_ref[0] + _mm(y_ref[0], wo_ref[...])
    xo_ref[0] = _embed(x1, p_ref[0], pg_ref, pe_ref)


def _out_embed(y, x, p_all, layer, wo, pg, pe, tm):
    B, T, D = x.shape
    row = lambda width: pl.BlockSpec((1, tm, width), lambda b, t: (b, t, 0))
    p_spec = pl.BlockSpec((None, 1, tm, p_all.shape[-1]), lambda b, t: (layer, b, t, 0))
    return pl.pallas_call(
        _out_kernel,
        grid=(B, T // tm),
        in_specs=[row(y.shape[-1]), row(D), p_spec, _resident(wo.shape), _resident(pg.shape), _resident(pe.shape)],
        out_specs=row(D),
        out_shape=jax.ShapeDtypeStruct((B, T, D), F32),
        compiler_params=_params("parallel", "parallel"),
        name="out_embed",
    )(y, x, p_all, wo, pg, pe)


CONV_COLS = 512


def _conv_prompt_kernel(x_ref, p_ref, g_ref, win_ref, cw_ref, wo_ref, pg_ref, pe_ref,
                        xo_ref, st_ref, carry_s):
    E = wo_ref.shape[0]

    @pl.when(pl.program_id(1) == 0)
    def _():
        carry_s[...] = jnp.zeros_like(carry_s)

    x = x_ref[0]
    tm = x.shape[0]
    h = _rms(x, g_ref[...]).astype(BF16)
    row = lax.broadcasted_iota(jnp.int32, (tm, CONV_COLS), 0)
    acc = jnp.zeros(x.shape, F32)
    for c in range(E // CONV_COLS):
        cols = slice(c * CONV_COLS, (c + 1) * CONV_COLS)
        part = lambda i: _mm(h, win_ref[:, i * E + c * CONV_COLS:i * E + (c + 1) * CONV_COLS])
        bg, cg, xb, z = part(0), part(1), part(2), part(3)
        cx = cg * xb
        old = carry_s[0:1, cols]
        new = carry_s[1:2, cols]
        prev1 = jnp.where(row == 0, new, pltpu.roll(cx, 1, 0))
        prev2 = jnp.where(row == 0, old, jnp.where(row == 1, new, pltpu.roll(cx, 2, 0)))
        y = cw_ref[0:1, cols] * prev2 + cw_ref[1:2, cols] * prev1 + cw_ref[2:3, cols] * cx
        acc = acc + _mm((bg * y * _silu(z)).astype(BF16), wo_ref[cols, :])
        carry_s[0:2, cols] = cx[tm - 2:tm, :]
    xo_ref[0] = _embed(x + acc, p_ref[0], pg_ref, pe_ref)
    st_ref[0] = carry_s[0:2, :]


def _conv_prompt(x, p_all, layer, g, win, cw, wo, pg, pe, tm):
    B, T, D = x.shape
    E = wo.shape[0]
    row = lambda width: pl.BlockSpec((1, tm, width), lambda b, t: (b, t, 0))
    p_spec = pl.BlockSpec((None, 1, tm, p_all.shape[-1]), lambda b, t: (layer, b, t, 0))
    return pl.pallas_call(
        _conv_prompt_kernel,
        grid=(B, T // tm),
        in_specs=[row(D), p_spec, _resident(g.shape), _resident(win.shape), _resident(cw.shape),
                  _resident(wo.shape), _resident(pg.shape), _resident(pe.shape)],
        out_specs=[row(D), pl.BlockSpec((1, CONV_W - 1, E), lambda b, t: (b, 0, 0))],
        out_shape=[jax.ShapeDtypeStruct((B, T, D), F32), jax.ShapeDtypeStruct((B, CONV_W - 1, E), F32)],
        scratch_shapes=[pltpu.VMEM((8, E), F32)],
        compiler_params=_params("parallel", "arbitrary"),
        name="conv_prompt",
    )(x, p_all, g, win, cw, wo, pg, pe)


def _conv_sample_kernel(x_ref, p_ref, g_ref, win_ref, cw_ref, wo_ref, pg_ref, pe_ref, s0_ref, s1_ref,
                        xo_ref, cx_ref):
    E = wo_ref.shape[0]
    x = x_ref[0]
    h = _rms(x, g_ref[...]).astype(BF16)
    acc = jnp.zeros(x.shape, F32)
    for c in range(E // CONV_COLS):
        cols = slice(c * CONV_COLS, (c + 1) * CONV_COLS)
        part = lambda i: _mm(h, win_ref[:, i * E + c * CONV_COLS:i * E + (c + 1) * CONV_COLS])
        bg, cg, xb, z = part(0), part(1), part(2), part(3)
        cx = cg * xb
        y = cw_ref[0:1, cols] * s0_ref[:, cols] + cw_ref[1:2, cols] * s1_ref[:, cols] + cw_ref[2:3, cols] * cx
        acc = acc + _mm((bg * y * _silu(z)).astype(BF16), wo_ref[cols, :])
        cx_ref[:, cols] = cx
    xo_ref[0] = _embed(x + acc, p_ref[0], pg_ref, pe_ref)


def _conv_sample(x, p_all, layer, g, win, cw, wo, pg, pe, s0, s1):
    _, M, D = x.shape
    E = wo.shape[0]
    p_spec = pl.BlockSpec((None, 1, M, p_all.shape[-1]), lambda i: (layer, 0, 0, 0))
    return pl.pallas_call(
        _conv_sample_kernel,
        grid=(1,),
        in_specs=[_resident(x.shape), p_spec, _resident(g.shape), _resident(win.shape), _resident(cw.shape),
                  _resident(wo.shape), _resident(pg.shape), _resident(pe.shape),
                  _resident(s0.shape), _resident(s1.shape)],
        out_specs=[_whole(x.shape), _whole((M, E))],
        out_shape=[jax.ShapeDtypeStruct(x.shape, F32), jax.ShapeDtypeStruct((M, E), F32)],
        compiler_params=_params("arbitrary"),
        name="conv_sample",
    )(x, p_all, g, win, cw, wo, pg, pe, s0, s1)


def _rope(x, cos, sin_lo, sin_hi):
    half = ROPE_DIM // 2
    return x * cos + pltpu.roll(x, C_DH - half, 1) * sin_lo + pltpu.roll(x, half, 1) * sin_hi


def _c_in_kernel(x_ref, g_ref, w_ref, cos_ref, slo_ref, shi_ref, *out_refs, tails):
    n_groups = len(C_GROUPS)
    qkv_refs = out_refs[:3 * n_groups]
    z_ref = out_refs[3 * n_groups]
    tail_refs = out_refs[3 * n_groups + 1:]
    h = _rms(x_ref[0], g_ref[...]).astype(BF16)
    tm = h.shape[0]
    cos, slo, shi = cos_ref[...], slo_ref[...], shi_ref[...]
    tile_end = (pl.program_id(1) + 1) * tm

    def rope_all(u):
        return jnp.concatenate(
            [_rope(u[:, hd * C_DH:(hd + 1) * C_DH], cos, slo, shi) for hd in range(C_HEADS)], axis=1)

    for grp in range(n_groups):
        base = 3 * grp * C_WIDTH
        q = rope_all(_mm(h, w_ref[:, base:base + C_WIDTH])) * (C_DH ** -0.5)
        k = rope_all(_mm(h, w_ref[:, base + C_WIDTH:base + 2 * C_WIDTH]))
        v = _mm(h, w_ref[:, base + 2 * C_WIDTH:base + 3 * C_WIDTH])
        qkv_refs[3 * grp][0] = q.astype(BF16)
        qkv_refs[3 * grp + 1][0] = k.astype(BF16)
        qkv_refs[3 * grp + 2][0] = v.astype(BF16)
        first_row, rows = tails[grp]

        @pl.when(tile_end > first_row)
        def _(k=k, v=v, grp=grp, rows=rows):
            tail_refs[2 * grp][0] = k[tm - rows:, :]
            tail_refs[2 * grp + 1][0] = v[tm - rows:, :]

    zb = 3 * n_groups * C_WIDTH
    z_ref[0] = _mm(h, w_ref[:, zb:zb + C_WIDTH]).astype(BF16)


def _c_in(x, g, w, cos, slo, shi, tm, keep_all=False):
    B, T, D = x.shape
    row = lambda width: pl.BlockSpec((1, tm, width), lambda b, t: (b, t, 0))
    table = pl.BlockSpec((tm, C_DH), lambda b, t: (t, 0))
    tails, tail_specs, tail_shapes = [], [], []
    for win, _ in C_GROUPS:
        keep = T if keep_all else min(win, T)
        rows = min(tm, keep)
        first_row = T - keep
        tails.append((first_row, rows))

        def tail_index(b, t, first_row=first_row, rows=rows):
            return (b, jnp.maximum(((t + 1) * tm - first_row) // rows - 1, 0), 0)

        tail_specs += [pl.BlockSpec((1, rows, C_WIDTH), tail_index)] * 2
        tail_shapes += [jax.ShapeDtypeStruct((B, keep, C_WIDTH), F32)] * 2
    n_qkv = 3 * len(C_GROUPS) + 1
    outs = pl.pallas_call(
        functools.partial(_c_in_kernel, tails=tuple(tails)),
        grid=(B, T // tm),
        in_specs=[row(D), _resident(g.shape), _resident(w.shape), table, table, table],
        out_specs=[row(C_WIDTH)] * n_qkv + tail_specs,
        out_shape=[jax.ShapeDtypeStruct((B, T, C_WIDTH), BF16)] * n_qkv + tail_shapes,
        compiler_params=_params("parallel", "arbitrary"),
        name="attn_in",
    )(x, g, w, cos, slo, shi)
    return outs[:n_qkv - 1], outs[n_qkv - 1], outs[n_qkv:]


def _attn_block_kernel(q_ref, kp_ref, kc_ref, vp_ref, vc_ref, acc_ref, st_ref, *, span):
    blk = pl.program_id(2)
    qi = lax.broadcasted_iota(jnp.int32, (C_BLOCK, C_BLOCK), 0)
    kj = lax.broadcasted_iota(jnp.int32, (C_BLOCK, C_BLOCK), 1)
    mask_prev = qi + C_BLOCK - kj <= jnp.where(blk > 0, span, -1)
    mask_cur = (qi >= kj) & (qi - kj <= span)
    lane = lax.broadcasted_iota(jnp.int32, (C_BLOCK, LANES), 1)
    stats = jnp.zeros((C_BLOCK, LANES), F32)
    for hd in range(C_HEADS):
        hs = slice(hd * C_DH, (hd + 1) * C_DH)
        qh = q_ref[0, :, hs]
        sp = jnp.where(mask_prev, lax.dot_general(qh, kp_ref[0, :, hs], NT_DIMS, preferred_element_type=F32), -1e30)
        sc = jnp.where(mask_cur, lax.dot_general(qh, kc_ref[0, :, hs], NT_DIMS, preferred_element_type=F32), -1e30)
        mx = jnp.maximum(jnp.max(sp, axis=-1, keepdims=True), jnp.max(sc, axis=-1, keepdims=True))
        pp = jnp.where(mask_prev, jnp.exp(sp - mx), 0.0)
        pc = jnp.where(mask_cur, jnp.exp(sc - mx), 0.0)
        l = jnp.sum(pp, axis=-1, keepdims=True) + jnp.sum(pc, axis=-1, keepdims=True)
        acc = _mm(pp.astype(BF16), vp_ref[0, :, hs]) + _mm(pc.astype(BF16), vc_ref[0, :, hs])
        acc_ref[0, :, hs] = acc.astype(BF16)
        stats = jnp.where(lane == hd, mx, jnp.where(lane == C_HEADS + hd, l, stats))
    st_ref[0] = stats


def _attn_prompt_group(q, k, v, dil, span):
    B, T, _ = q.shape
    n = T // dil
    nb = n // C_BLOCK
    view = lambda a: a.reshape(B, n, dil * a.shape[-1])
    cur = lambda width: pl.BlockSpec((1, C_BLOCK, width), lambda b, r, i: (b, i, r))
    prev = lambda width: pl.BlockSpec((1, C_BLOCK, width), lambda b, r, i: (b, jnp.maximum(i - 1, 0), r))
    acc, st = pl.pallas_call(
        functools.partial(_attn_block_kernel, span=span),
        grid=(B, dil, nb),
        in_specs=[cur(C_WIDTH), prev(C_WIDTH), cur(C_WIDTH), prev(C_WIDTH), cur(C_WIDTH)],
        out_specs=[cur(C_WIDTH), cur(LANES)],
        out_shape=[jax.ShapeDtypeStruct((B, n, dil * C_WIDTH), BF16),
                   jax.ShapeDtypeStruct((B, n, dil * LANES), F32)],
        compiler_params=_params("parallel", "parallel", "arbitrary"),
        name="attn_blocks_d%d" % dil,
    )(view(q), view(k), view(k), view(v), view(v))
    return acc.reshape(B, T, C_WIDTH), st.reshape(B, T, LANES)


def _merge_out_kernel(a0_ref, a1_ref, a2_ref, s0_ref, s1_ref, s2_ref, z_ref, x_ref, p_ref,
                      wo_ref, pg_ref, pe_ref, xo_ref, y_s):
    accs = (a0_ref, a1_ref, a2_ref)
    stats = [s[0] for s in (s0_ref, s1_ref, s2_ref)]
    mx = [s[:, 0:C_HEADS] for s in stats]
    ls = [s[:, C_HEADS:2 * C_HEADS] for s in stats]
    top = jnp.maximum(jnp.maximum(mx[0], mx[1]), mx[2])
    es = [jnp.exp(m - top) for m in mx]
    den = es[0] * ls[0] + es[1] * ls[1] + es[2] * ls[2]
    coef = [e / den for e in es]
    for hd in range(C_HEADS):
        hs = slice(hd * C_DH, (hd + 1) * C_DH)
        o = coef[0][:, hd:hd + 1] * accs[0][0, :, hs].astype(F32)
        for grp in (1, 2):
            o = o + coef[grp][:, hd:hd + 1] * accs[grp][0, :, hs].astype(F32)
        y_s[:, hs] = (o * _silu(z_ref[0, :, hs].astype(F32))).astype(BF16)
    x1 = x_ref[0] + _mm(y_s[...], wo_ref[...])
    xo_ref[0] = _embed(x1, p_ref[0], pg_ref, pe_ref)


def _merge_out(accs, stats, z, x, p_all, layer, wo, pg, pe, tm):
    B, T, D = x.shape
    row = lambda width: pl.BlockSpec((1, tm, width), lambda b, t: (b, t, 0))
    p_spec = pl.BlockSpec((None, 1, tm, p_all.shape[-1]), lambda b, t: (layer, b, t, 0))
    return pl.pallas_call(
        _merge_out_kernel,
        grid=(B, T // tm),
        in_specs=[row(C_WIDTH)] * 3 + [row(LANES)] * 3 + [row(C_WIDTH), row(D), p_spec,
                  _resident(wo.shape), _resident(pg.shape), _resident(pe.shape)],
        out_specs=row(D),
        out_shape=jax.ShapeDtypeStruct((B, T, D), F32),
        scratch_shapes=[pltpu.VMEM((tm, C_WIDTH), BF16)],
        compiler_params=_params("parallel", "parallel"),
        name="attn_merge_out",
    )(*accs, *stats, z, x, p_all, wo, pg, pe)


def _attn_sample_kernel(*refs):
    n_groups = len(C_GROUPS)
    z_ref, y_ref = refs[5 * n_groups], refs[5 * n_groups + 1]
    for hd in range(C_HEADS):
        hs = slice(hd * C_DH, (hd + 1) * C_DH)
        parts = []
        for grp in range(n_groups):
            q_ref, kn_ref, vn_ref, kc_ref, vc_ref = refs[5 * grp:5 * grp + 5]
            qh = q_ref[0, :, hs].astype(F32)
            s_old = jnp.sum(kc_ref[0, :, hs] * qh, axis=-1, keepdims=True)
            s_new = jnp.sum(kn_ref[0, :, hs] * qh, axis=-1, keepdims=True)
            mx = jnp.maximum(jnp.max(s_old, axis=0, keepdims=True), s_new)
            p_old = jnp.exp(s_old - mx)
            p_new = jnp.exp(s_new - mx)
            l = jnp.sum(p_old, axis=0, keepdims=True) + p_new
            acc = jnp.sum(p_old * vc_ref[0, :, hs], axis=0, keepdims=True) + p_new * vn_ref[0, :, hs]
            parts.append((acc, mx, l))
        top = jnp.maximum(jnp.maximum(parts[0][1], parts[1][1]), parts[2][1])
        es = [jnp.exp(m - top) for _, m, _ in parts]
        num = es[0] * parts[0][0] + es[1] * parts[1][0] + es[2] * parts[2][0]
        den = es[0] * parts[0][2] + es[1] * parts[1][2] + es[2] * parts[2][2]
        y_ref[0, :, hs] = (num / den * _silu(z_ref[0, :, hs].astype(F32))).astype(BF16)


def _attn_sample(qkv, tails, caches, z):
    Bs = z.shape[1]
    row = pl.BlockSpec((1, 1, C_WIDTH), lambda b: (b, 0, 0))
    args, specs = [], []
    for grp, (win, dil) in enumerate(C_GROUPS):
        kc, vc = caches[grp]
        n_buf = kc.shape[1]
        assert n_buf == win and win % dil == 0, "sample window must be fully cached"
        span = win // dil
        view = lambda a: a.reshape(Bs, span, dil * C_WIDTH)
        cache_spec = pl.BlockSpec((1, span, C_WIDTH), lambda b: (b, 0, 0))
        to_rows = lambda a: a.reshape(Bs, 1, C_WIDTH)
        args += [to_rows(qkv[3 * grp]), to_rows(tails[2 * grp]), to_rows(tails[2 * grp + 1]), view(kc), view(vc)]
        specs += [row, row, row, cache_spec, cache_spec]
    return pl.pallas_call(
        _attn_sample_kernel,
        grid=(Bs,),
        in_specs=specs + [row],
        out_specs=row,
        out_shape=jax.ShapeDtypeStruct((Bs, 1, C_WIDTH), BF16),
        compiler_params=_params("parallel"),
        name="attn_sample",
    )(*args, z.reshape(Bs, 1, C_WIDTH))


def _pool_tail(x, acc, p, pg_ref, pe_ref, fg_ref):
    return _rms(_embed(x + acc, p, pg_ref, pe_ref), fg_ref[...])


def _pool_prompt_kernel(x_ref, p_ref, g_ref, win_ref, wgrp_ref, sc_ref, wo_ref, pg_ref, pe_ref, fg_ref,
                        xo_ref, st_ref, pad_s):
    E = wo_ref.shape[0]
    G = E // len(D_WINDOWS)
    t = pl.program_id(1)

    @pl.when(t == 0)
    def _():
        pad_s[0:POOL_PAD, :] = jnp.zeros((POOL_PAD, E), F32)

    x = x_ref[0]
    tm = x.shape[0]
    h = _rms(x, g_ref[...]).astype(BF16)
    pos = t * tm + lax.broadcasted_iota(jnp.int32, (tm, 1), 0)
    acc = jnp.zeros(x.shape, F32)
    for grp, w in enumerate(D_WINDOWS):
        cols = slice(grp * G, (grp + 1) * G)
        xp = _mm(h, win_ref[:, cols])
        z = _mm(h, win_ref[:, E + grp * G:E + (grp + 1) * G])
        pad_s[POOL_PAD:, cols] = xp
        wsum = pad_s[:, cols]
        shift = 1
        while shift < w:
            wsum = wsum + pltpu.roll(wsum, shift, 0)
            shift *= 2
        cnt = jnp.minimum(w, pos + 1).astype(F32)
        r = wsum[POOL_PAD:, :] / cnt - xp
        y = _mm(r.astype(BF16), wgrp_ref[grp]) * sc_ref[:, cols]
        acc = acc + _mm((y * _silu(z)).astype(BF16), wo_ref[cols, :])
        pad_s[0:POOL_PAD, cols] = xp[tm - POOL_PAD:, :]
    xo_ref[0] = _pool_tail(x, acc, p_ref[0], pg_ref, pe_ref, fg_ref)
    st_ref[0] = pad_s[0:POOL_PAD, :]


def _pool_prompt(x, p_all, layer, g, win, wgrp, sc, wo, pg, pe, fg, tm):
    B, T, D = x.shape
    E = wo.shape[0]
    row = lambda width: pl.BlockSpec((1, tm, width), lambda b, t: (b, t, 0))
    p_spec = pl.BlockSpec((None, 1, tm, p_all.shape[-1]), lambda b, t: (layer, b, t, 0))
    return pl.pallas_call(
        _pool_prompt_kernel,
        grid=(B, T // tm),
        in_specs=[row(D), p_spec] + [_resident(a.shape) for a in (g, win, wgrp, sc, wo, pg, pe, fg)],
        out_specs=[row(D), pl.BlockSpec((1, POOL_PAD, E), lambda b, t: (b, 0, 0))],
        out_shape=[jax.ShapeDtypeStruct((B, T, D), F32), jax.ShapeDtypeStruct((B, POOL_PAD, E), F32)],
        scratch_shapes=[pltpu.VMEM((POOL_PAD + tm, E), F32)],
        compiler_params=_params("parallel", "arbitrary"),
        name="pool_prompt",
    )(x, p_all, g, win, wgrp, sc, wo, pg, pe, fg)


def _pool_sample_kernel(x_ref, p_ref, g_ref, win_ref, wgrp_ref, sc_ref, wo_ref, pg_ref, pe_ref, fg_ref, st_ref,
                        xo_ref, xp_ref, *, pos):
    E = wo_ref.shape[0]
    G = E // len(D_WINDOWS)
    x = x_ref[0]
    h = _rms(x, g_ref[...]).astype(BF16)
    acc = jnp.zeros(x.shape, F32)
    for grp, w in enumerate(D_WINDOWS):
        cols = slice(grp * G, (grp + 1) * G)
        xp = _mm(h, win_ref[:, cols])
        z = _mm(h, win_ref[:, E + grp * G:E + (grp + 1) * G])
        wsum = xp
        for back in range(1, w):
            lo = (POOL_STATE - back) * E + grp * G
            wsum = wsum + st_ref[:, lo:lo + G]
        r = wsum / float(min(w, pos + 1)) - xp
        y = _mm(r.astype(BF16), wgrp_ref[grp]) * sc_ref[:, cols]
        acc = acc + _mm((y * _silu(z)).astype(BF16), wo_ref[cols, :])
        xp_ref[:, cols] = xp
    xo_ref[0] = _pool_tail(x, acc, p_ref[0], pg_ref, pe_ref, fg_ref)


def _pool_sample(x, p_all, layer, g, win, wgrp, sc, wo, pg, pe, fg, st, pos):
    _, M, D = x.shape
    E = wo.shape[0]
    p_spec = pl.BlockSpec((None, 1, M, p_all.shape[-1]), lambda i: (layer, 0, 0, 0))
    return pl.pallas_call(
        functools.partial(_pool_sample_kernel, pos=pos),
        grid=(1,),
        in_specs=[_resident(x.shape), p_spec] + [_resident(a.shape) for a in (g, win, wgrp, sc, wo, pg, pe, fg, st)],
        out_specs=[_whole(x.shape), _whole((M, E))],
        out_shape=[jax.ShapeDtypeStruct(x.shape, F32), jax.ShapeDtypeStruct((M, E), F32)],
        compiler_params=_params("arbitrary"),
        name="pool_sample",
    )(x, p_all, g, win, wgrp, sc, wo, pg, pe, fg, st)


def _rope_tables(pos):
    half = ROPE_DIM // 2
    inv = ROPE_THETA ** (-jnp.arange(half, dtype=F32) / half)
    ang = pos.astype(F32)[:, None] * inv[None, :]
    cos, sin = jnp.cos(ang), jnp.sin(ang)
    rest = C_DH - ROPE_DIM
    n = pos.shape[0]
    cos_t = jnp.concatenate([cos, cos, jnp.ones((n, rest), F32)], axis=1)
    sin_lo = jnp.concatenate([-sin, jnp.zeros((n, half + rest), F32)], axis=1)
    sin_hi = jnp.concatenate([jnp.zeros((n, half), F32), sin, jnp.zeros((n, rest), F32)], axis=1)
    return cos_t, sin_lo, sin_hi


def _row(a):
    return a.reshape(1, -1).astype(F32)


def kernel(x_prompt, x_sample, state_mlstm_C, state_mlstm_n, state_mlstm_m, state_conv, cache_k_w128, cache_v_w128, cache_k_w512, cache_v_w512, cache_k_w2048, cache_v_w2048, state_pool, p_prompt, p_sample, norm_g, pe_w, pg_w, final_g, a_w_in, a_b_if, a_norm_g, a_w_out, b_w_in, b_conv_w, b_w_out, c_w_in, c_w_out, d_w_in, d_w_grp, d_scale, d_w_out):
    B, T, D = x_prompt.shape
    Bs, Ts, _ = x_sample.shape
    assert Ts == 1 and norm_g.shape[0] == 4 and a_w_in.shape[0] == 1
    assert T % 512 == 0 and all(T % (dil * C_BLOCK) == 0 for _, dil in C_GROUPS)
    assert Bs % 8 == 0

    bf = lambda a: a.astype(BF16)
    qk = A_HEADS * A_DK
    n_main = 2 * qk + 3 * A_WIDTH
    a_w = bf(a_w_in[0, :, :n_main])
    a_wg = bf(jnp.pad(a_w_in[0, :, n_main:], ((0, 0), (0, LANES - 2 * A_HEADS))))
    a_bif = jnp.pad(a_b_if[0], (0, LANES - 2 * A_HEADS)).reshape(1, LANES)
    a_ng = _row(a_norm_g[0])
    a_wo, b_wi, b_wo, c_wi, c_wo, d_wi, d_wg, d_wo = (
        bf(a_w_out[0]), bf(b_w_in[0]), bf(b_w_out[0]), bf(c_w_in[0]), bf(c_w_out[0]),
        bf(d_w_in[0]), bf(d_w_grp[0]), bf(d_w_out[0]))
    pg, pe = bf(pg_w), bf(pe_w)
    gs = [_row(norm_g[i]) for i in range(4)]
    fg = _row(final_g)
    b_cw = b_conv_w[0]
    d_sc = _row(d_scale[0])
    caches = [(cache_k_w128[0], cache_v_w128[0]), (cache_k_w512[0], cache_v_w512[0]),
              (cache_k_w2048[0], cache_v_w2048[0])]

    x = x_prompt
    q, k, v, o, z, gates = _a_in(x, gs[0], a_w, a_wg, a_bif, 512)
    y, c_p, n_p, m_p = _mlstm_prompt(q, k, v, o, z, gates, a_ng)
    x = _out_embed(y, x, p_prompt, 0, a_wo, pg[0], pe[0], 512)
    x, conv_p = _conv_prompt(x, p_prompt, 1, gs[1], b_wi, b_cw, b_wo, pg[1], pe[1], 512)
    qkv, z, tails_p = _c_in(x, gs[2], c_wi, *_rope_tables(jnp.arange(T)), 256)
    accs, stats = [], []
    for grp, (win, dil) in enumerate(C_GROUPS):
        acc, st = _attn_prompt_group(qkv[3 * grp], qkv[3 * grp + 1], qkv[3 * grp + 2], dil, win // dil)
        accs.append(acc)
        stats.append(st)
    x = _merge_out(accs, stats, z, x, p_prompt, 2, c_wo, pg[2], pe[2], 512)
    y_prompt, pool_p = _pool_prompt(x, p_prompt, 3, gs[3], d_wi, d_wg, d_sc, d_wo, pg[3], pe[3], fg, 512)

    xs = x_sample.reshape(1, Bs, D)
    ps = p_sample.reshape(p_sample.shape[0], 1, Bs, p_sample.shape[-1])
    q, k, v, o, z, gates = _a_in(xs, gs[0], a_w, a_wg, a_bif, Bs)
    per_seq = lambda a: a.reshape(Bs, 1, a.shape[-1])
    y, c_s, n_s, m_s = _mlstm_sample(per_seq(q), per_seq(k), per_seq(v), per_seq(o), per_seq(z), per_seq(gates),
                                     a_ng, state_mlstm_C[0], state_mlstm_n[0],
                                     state_mlstm_m[0].reshape(Bs, 1, A_HEADS))
    xs = _out_embed(y.reshape(1, Bs, A_WIDTH), xs, ps, 0, a_wo, pg[0], pe[0], Bs)
    xs, cx = _conv_sample(xs, ps, 1, gs[1], b_wi, b_cw, b_wo, pg[1], pe[1],
                          state_conv[0, :, 0, :], state_conv[0, :, 1, :])
    conv_s = jnp.stack([state_conv[0, :, 1, :], cx], axis=1)
    tables = [jnp.broadcast_to(t, (Bs, C_DH)) for t in _rope_tables(PAST_LEN + jnp.arange(1))]
    qkv, z, tails_s = _c_in(xs, gs[2], c_wi, *tables, Bs, keep_all=True)
    y = _attn_sample(qkv, tails_s, caches, z)
    xs = _out_embed(y.reshape(1, Bs, C_WIDTH), xs, ps, 2, c_wo, pg[2], pe[2], Bs)
    pool_flat = state_pool[0].reshape(Bs, POOL_STATE * state_pool.shape[-1])
    ys, xp = _pool_sample(xs, ps, 3, gs[3], d_wi, d_wg, d_sc, d_wo, pg[3], pe[3], fg, pool_flat, PAST_LEN)
    pool_s = jnp.concatenate([state_pool[0, :, 1:, :], xp[:, None, :]], axis=1)

    heads = lambda a, lead: a.reshape(1, lead, a.shape[1], C_HEADS, C_DH)
    kv_out = []
    for grp in range(len(C_GROUPS)):
        for j in range(2):
            kv_out += [heads(tails_p[2 * grp + j], B), tails_s[2 * grp + j].reshape(1, Bs, 1, C_HEADS, C_DH)]
    return (y_prompt, ys.reshape(Bs, 1, D),
            c_p[None], c_s[None], n_p[None], n_s[None], m_p[None, :, :, 0], m_s.reshape(1, Bs, A_HEADS),
            conv_p[None], conv_s[None],
            *kv_out,
            pool_p[None, :, 1:, :], pool_s[None])
```

```python
import functools

import jax
import jax.numpy as jnp
from jax import lax
from jax.experimental import pallas as pl
from jax.experimental.pallas import tpu as pltpu

F32 = jnp.float32
BF16 = jnp.bfloat16

EPS = 1e-6
PAST_LEN = 8192

A_HEADS = 8
A_DK = 128
A_DV = 256
A_WIDTH = A_HEADS * A_DV
A_CHUNK = 128

CONV_W = 3

C_HEADS = 8
C_DH = 128
C_WIDTH = C_HEADS * C_DH
C_GROUPS = ((128, 1), (512, 4), (2048, 16))
C_BLOCK = 128
ROPE_DIM = C_DH // 4
ROPE_THETA = 500000.0

D_WINDOWS = (2, 4, 8, 16)
POOL_STATE = max(D_WINDOWS) - 1
POOL_PAD = POOL_STATE + 1

LANES = 128
VMEM_LIMIT_BYTES = 56 * 2 ** 20

NT_DIMS = (((1,), (1,)), ((), ()))
TN_DIMS = (((0,), (0,)), ((), ()))


def _params(*semantics):
    return pltpu.CompilerParams(dimension_semantics=semantics, vmem_limit_bytes=VMEM_LIMIT_BYTES)


def _resident(shape):
    zeros = (0,) * len(shape)
    return pl.BlockSpec(shape, lambda *_: zeros, pipeline_mode=pl.Buffered(1))


def _whole(shape):
    zeros = (0,) * len(shape)
    return pl.BlockSpec(shape, lambda *_: zeros)


def _mm(a, b):
    return jnp.dot(a, b, preferred_element_type=F32)


def _rms(x, g):
    return x * lax.rsqrt(jnp.mean(x * x, axis=-1, keepdims=True) + EPS) * g


def _sigmoid(x):
    return 1.0 / (1.0 + jnp.exp(-x))


def _silu(x):
    return x * _sigmoid(x)


def _log_sigmoid(x):
    return jnp.minimum(x, 0.0) - jnp.log(1.0 + jnp.exp(-jnp.abs(x)))


def _embed(x1, p, pg_ref, pe_ref):
    gate = _sigmoid(_mm(x1.astype(BF16), pg_ref[...]))
    return x1 + gate * _mm(p.astype(BF16), pe_ref[...])


def _a_in_kernel(x_ref, g_ref, w_ref, wg_ref, bif_ref,
                 q_ref, k_ref, v_ref, o_ref, z_ref, gate_ref):
    h = _rms(x_ref[0], g_ref[...]).astype(BF16)
    qk = A_HEADS * A_DK

    def proj(lo, width):
        return _mm(h, w_ref[:, lo:lo + width])

    q_ref[0] = proj(0, qk).astype(BF16)
    k_ref[0] = (proj(qk, qk) * (A_DK ** -0.5)).astype(BF16)
    v_ref[0] = proj(2 * qk, A_WIDTH).astype(BF16)
    o_ref[0] = proj(2 * qk + A_WIDTH, A_WIDTH).astype(BF16)
    z_ref[0] = proj(2 * qk + 2 * A_WIDTH, A_WIDTH).astype(BF16)
    gates = _mm(h, wg_ref[...]) + bif_ref[...]
    lane = lax.broadcasted_iota(jnp.int32, gates.shape, 1)
    gate_ref[0] = jnp.where(lane < A_HEADS, gates, _log_sigmoid(gates))


def _a_in(x, g, w, wg, bif, tm):
    B, T, D = x.shape
    qk = A_HEADS * A_DK
    row = lambda width: pl.BlockSpec((1, tm, width), lambda b, t: (b, t, 0))
    return pl.pallas_call(
        _a_in_kernel,
        grid=(B, T // tm),
        in_specs=[row(D), _resident(g.shape), _resident(w.shape), _resident(wg.shape), _resident(bif.shape)],
        out_specs=[row(qk), row(qk), row(A_WIDTH), row(A_WIDTH), row(A_WIDTH), row(LANES)],
        out_shape=[jax.ShapeDtypeStruct((B, T, qk), BF16), jax.ShapeDtypeStruct((B, T, qk), BF16),
                   jax.ShapeDtypeStruct((B, T, A_WIDTH), BF16), jax.ShapeDtypeStruct((B, T, A_WIDTH), BF16),
                   jax.ShapeDtypeStruct((B, T, A_WIDTH), BF16), jax.ShapeDtypeStruct((B, T, LANES), F32)],
        compiler_params=_params("parallel", "parallel"),
        name="mlstm_in",
    )(x, g, w, wg, bif)


def _head_out(hh, ng, o, z):
    hn = hh * lax.rsqrt(jnp.mean(hh * hh, axis=-1, keepdims=True) + EPS) * ng
    return (hn * _sigmoid(o) * _silu(z)).astype(BF16)


def _mlstm_chunk_kernel(q_ref, k_ref, v_ref, o_ref, z_ref, gate_ref, ng_ref,
                        y_ref, c_out, n_out, m_out, c_s, n_s, m_s):
    chunk = pl.program_id(1)

    @pl.when(chunk == 0)
    def _():
        c_s[...] = jnp.zeros_like(c_s)
        n_s[...] = jnp.zeros_like(n_s)
        m_s[...] = jnp.zeros_like(m_s)

    L = q_ref.shape[1]
    gates = gate_ref[0]
    row = lax.broadcasted_iota(jnp.int32, gates.shape, 0)
    csum = gates
    step = 1
    while step < L:
        csum = csum + jnp.where(row >= step, pltpu.roll(csum, step, 0), 0.0)
        step *= 2
    gates_t = gates.T
    csum_t = csum.T
    ti = lax.broadcasted_iota(jnp.int32, (L, L), 0)
    si = lax.broadcasted_iota(jnp.int32, (L, L), 1)
    causal = ti >= si

    for h in range(A_HEADS):
        ks = slice(h * A_DK, (h + 1) * A_DK)
        vs = slice(h * A_DV, (h + 1) * A_DV)
        ig_col = gates[:, h:h + 1]
        b_col = csum[:, A_HEADS + h:A_HEADS + h + 1]
        ig_row = gates_t[h:h + 1, :]
        b_row = csum_t[A_HEADS + h:A_HEADS + h + 1, :]
        m_prev = m_s[h:h + 1, 0:1]
        qh = q_ref[0, :, ks]
        kh = k_ref[0, :, ks]
        vh = v_ref[0, :, vs]
        c_h = c_s[h]
        n_h = n_s[h:h + 1, :]

        log_d = jnp.where(causal, b_col + (ig_row - b_row), -jnp.inf)
        inter = b_col + m_prev
        m_t = jnp.maximum(inter, jnp.max(log_d, axis=-1, keepdims=True))
        dmat = jnp.exp(log_d - m_t)
        g = jnp.exp(inter - m_t)
        s = lax.dot_general(qh, kh, NT_DIMS, preferred_element_type=F32) * dmat
        num = _mm(s.astype(BF16), vh) + g * lax.dot_general(qh, c_h.astype(BF16), NT_DIMS,
                                                            preferred_element_type=F32)
        den = (jnp.sum(s, axis=-1, keepdims=True)
               + g * jnp.sum(qh.astype(F32) * n_h, axis=-1, keepdims=True))
        hh = num / jnp.maximum(jnp.abs(den), jnp.exp(-m_t))

        b_last = b_col[L - 1:L, :]
        a_col = b_last - b_col + ig_col
        m_new = jnp.maximum(b_last + m_prev, jnp.max(a_col, axis=0, keepdims=True))
        decay = jnp.exp(b_last + m_prev - m_new)
        w_col = jnp.exp(a_col - m_new)
        wv = (w_col * vh.astype(F32)).astype(BF16)
        c_s[h] = decay * c_h + lax.dot_general(wv, kh, TN_DIMS, preferred_element_type=F32)
        n_s[h:h + 1, :] = decay * n_h + jnp.sum(w_col * kh.astype(F32), axis=0, keepdims=True)
        m_s[h:h + 1, :] = jnp.broadcast_to(m_new, (1, LANES))

        y_ref[0, :, vs] = _head_out(hh, ng_ref[:, vs], o_ref[0, :, vs].astype(F32),
                                    z_ref[0, :, vs].astype(F32))

    @pl.when(chunk == pl.num_programs(1) - 1)
    def _():
        c_out[0] = c_s[...]
        n_out[0] = n_s[...]
        m_out[0] = m_s[...]


def _mlstm_prompt(q, k, v, o, z, gates, ng):
    B, T, _ = q.shape
    L = A_CHUNK
    qk = A_HEADS * A_DK
    row = lambda width: pl.BlockSpec((1, L, width), lambda b, c: (b, c, 0))
    state = lambda *shape: pl.BlockSpec((1,) + shape, lambda b, c: (b,) + (0,) * len(shape))
    return pl.pallas_call(
        _mlstm_chunk_kernel,
        grid=(B, T // L),
        in_specs=[row(qk), row(qk), row(A_WIDTH), row(A_WIDTH), row(A_WIDTH), row(LANES), _resident(ng.shape)],
        out_specs=[row(A_WIDTH), state(A_HEADS, A_DV, A_DK), state(A_HEADS, A_DK), state(A_HEADS, LANES)],
        out_shape=[jax.ShapeDtypeStruct((B, T, A_WIDTH), BF16),
                   jax.ShapeDtypeStruct((B, A_HEADS, A_DV, A_DK), F32),
                   jax.ShapeDtypeStruct((B, A_HEADS, A_DK), F32),
                   jax.ShapeDtypeStruct((B, A_HEADS, LANES), F32)],
        scratch_shapes=[pltpu.VMEM((A_HEADS, A_DV, A_DK), F32), pltpu.VMEM((A_HEADS, A_DK), F32),
                        pltpu.VMEM((A_HEADS, LANES), F32)],
        compiler_params=_params("parallel", "arbitrary"),
        name="mlstm_chunks",
    )(q, k, v, o, z, gates, ng)


def _mlstm_step_kernel(q_ref, k_ref, v_ref, o_ref, z_ref, gate_ref, ng_ref, c_ref, n_ref, m_ref,
                       y_ref, c_out, n_out, m_out):
    gates = gate_ref[0]
    row = lax.broadcasted_iota(jnp.int32, (LANES, A_DV), 0)
    for h in range(A_HEADS):
        ks = slice(h * A_DK, (h + 1) * A_DK)
        vs = slice(h * A_DV, (h + 1) * A_DV)
        ig = gates[:, h:h + 1]
        lf = gates[:, A_HEADS + h:A_HEADS + h + 1]
        m_prev = m_ref[0, :, h:h + 1]
        qh = q_ref[0, :, ks].astype(F32)
        kh = k_ref[0, :, ks].astype(F32)
        vh = v_ref[0, :, vs].astype(F32)
        c_h = c_ref[0, h]
        n_h = n_ref[0, h:h + 1, :]

        inter = lf + m_prev
        m_t = jnp.maximum(inter, ig)
        dm = jnp.exp(ig - m_t)
        g = jnp.exp(inter - m_t)
        s = jnp.sum(qh * kh, axis=-1, keepdims=True) * dm
        q8 = jnp.broadcast_to(qh, (8, A_DK)).astype(BF16)
        cq = lax.dot_general(q8, c_h.astype(BF16), NT_DIMS, preferred_element_type=F32)[0:1, :]
        num = s * vh + g * cq
        den = s + g * jnp.sum(qh * n_h, axis=-1, keepdims=True)
        hh = num / jnp.maximum(jnp.abs(den), jnp.exp(-m_t))

        v_pad = jnp.where(row == 0, jnp.broadcast_to(dm * vh, (LANES, A_DV)), 0.0).astype(BF16)
        k_pad = jnp.broadcast_to(kh, (LANES, A_DK)).astype(BF16)
        c_out[0, h] = g * c_h + lax.dot_general(v_pad, k_pad, TN_DIMS, preferred_element_type=F32)
        n_out[0, h:h + 1, :] = g * n_h + dm * kh
        m_out[0, :, h:h + 1] = m_t

        y_ref[0, :, vs] = _head_out(hh, ng_ref[:, vs], o_ref[0, :, vs].astype(F32),
                                    z_ref[0, :, vs].astype(F32))


def _mlstm_sample(q, k, v, o, z, gates, ng, c0, n0, m0):
    Bs = q.shape[0]
    qk = A_HEADS * A_DK
    row = lambda width: pl.BlockSpec((1, 1, width), lambda b: (b, 0, 0))
    c_spec = pl.BlockSpec((1, A_HEADS, A_DV, A_DK), lambda b: (b, 0, 0, 0))
    n_spec = pl.BlockSpec((1, A_HEADS, A_DK), lambda b: (b, 0, 0))
    m_spec = pl.BlockSpec((1, 1, A_HEADS), lambda b: (b, 0, 0))
    return pl.pallas_call(
        _mlstm_step_kernel,
        grid=(Bs,),
        in_specs=[row(qk), row(qk), row(A_WIDTH), row(A_WIDTH), row(A_WIDTH), row(LANES), _resident(ng.shape),
                  c_spec, n_spec, m_spec],
        out_specs=[row(A_WIDTH), c_spec, n_spec, m_spec],
        out_shape=[jax.ShapeDtypeStruct((Bs, 1, A_WIDTH), BF16),
                   jax.ShapeDtypeStruct(c0.shape, F32), jax.ShapeDtypeStruct(n0.shape, F32),
                   jax.ShapeDtypeStruct(m0.shape, F32)],
        compiler_params=_params("parallel"),
        name="mlstm_step",
    )(q, k, v, o, z, gates, ng, c0, n0, m0)


def _out_kernel(y_ref, x_ref, p_ref, wo_ref, pg_ref, pe_ref, xo_ref):
    x1 = x_ref[0] + _mm(y_ref[0], wo_ref[...])
    xo_ref[0] = _embed(x1, p_ref[0], pg_ref, pe_ref)


def _out_embed(y, x, p_all, layer, wo, pg, pe, tm):
    B, T, D = x.shape
    row = lambda width: pl.BlockSpec((1, tm, width), lambda b, t: (b, t, 0))
    p_spec = pl.BlockSpec((None, 1, tm, p_all.shape[-1]), lambda b, t: (layer, b, t, 0))
    return pl.pallas_call(
        _out_kernel,
        grid=(B, T // tm),
        in_specs=[row(y.shape[-1]), row(D), p_spec, _resident(wo.shape), _resident(pg.shape), _resident(pe.shape)],
        out_specs=row(D),
        out_shape=jax.ShapeDtypeStruct((B, T, D), F32),
        compiler_params=_params("parallel", "parallel"),
        name="out_embed",
    )(y, x, p_all, wo, pg, pe)


CONV_COLS = 512


def _conv_prompt_kernel(x_ref, p_ref, g_ref, win_ref, cw_ref, wo_ref, pg_ref, pe_ref,
                        xo_ref, st_ref, carry_s):
    E = wo_ref.shape[0]

    @pl.when(pl.program_id(1) == 0)
    def _():
        carry_s[...] = jnp.zeros_like(carry_s)

    x = x_ref[0]
    tm = x.shape[0]
    h = _rms(x, g_ref[...]).astype(BF16)
    row = lax.broadcasted_iota(jnp.int32, (tm, CONV_COLS), 0)
    acc = jnp.zeros(x.shape, F32)
    for c in range(E // CONV_COLS):
        cols = slice(c * CONV_COLS, (c + 1) * CONV_COLS)
        part = lambda i: _mm(h, win_ref[:, i * E + c * CONV_COLS:i * E + (c + 1) * CONV_COLS])
        bg, cg, xb, z = part(0), part(1), part(2), part(3)
        cx = cg * xb
        old = carry_s[0:1, cols]
        new = carry_s[1:2, cols]
        prev1 = jnp.where(row == 0, new, pltpu.roll(cx, 1, 0))
        prev2 = jnp.where(row == 0, old, jnp.where(row == 1, new, pltpu.roll(cx, 2, 0)))
        y = cw_ref[0:1, cols] * prev2 + cw_ref[1:2, cols] * prev1 + cw_ref[2:3, cols] * cx
        acc = acc + _mm((bg * y * _silu(z)).astype(BF16), wo_ref[cols, :])
        carry_s[0:2, cols] = cx[tm - 2:tm, :]
    xo_ref[0] = _embed(x + acc, p_ref[0], pg_ref, pe_ref)
    st_ref[0] = carry_s[0:2, :]


def _conv_prompt(x, p_all, layer, g, win, cw, wo, pg, pe, tm):
    B, T, D = x.shape
    E = wo.shape[0]
    row = lambda width: pl.BlockSpec((1, tm, width), lambda b, t: (b, t, 0))
    p_spec = pl.BlockSpec((None, 1, tm, p_all.shape[-1]), lambda b, t: (layer, b, t, 0))
    return pl.pallas_call(
        _conv_prompt_kernel,
        grid=(B, T // tm),
        in_specs=[row(D), p_spec, _resident(g.shape), _resident(win.shape), _resident(cw.shape),
                  _resident(wo.shape), _resident(pg.shape), _resident(pe.shape)],
        out_specs=[row(D), pl.BlockSpec((1, CONV_W - 1, E), lambda b, t: (b, 0, 0))],
        out_shape=[jax.ShapeDtypeStruct((B, T, D), F32), jax.ShapeDtypeStruct((B, CONV_W - 1, E), F32)],
        scratch_shapes=[pltpu.VMEM((8, E), F32)],
        compiler_params=_params("parallel", "arbitrary"),
        name="conv_prompt",
    )(x, p_all, g, win, cw, wo, pg, pe)


def _conv_sample_kernel(x_ref, p_ref, g_ref, win_ref, cw_ref, wo_ref, pg_ref, pe_ref, s0_ref, s1_ref,
                        xo_ref, cx_ref):
    E = wo_ref.shape[0]
    x = x_ref[0]
    h = _rms(x, g_ref[...]).astype(BF16)
    acc = jnp.zeros(x.shape, F32)
    for c in range(E // CONV_COLS):
        cols = slice(c * CONV_COLS, (c + 1) * CONV_COLS)
        part = lambda i: _mm(h, win_ref[:, i * E + c * CONV_COLS:i * E + (c + 1) * CONV_COLS])
        bg, cg, xb, z = part(0), part(1), part(2), part(3)
        cx = cg * xb
        y = cw_ref[0:1, cols] * s0_ref[:, cols] + cw_ref[1:2, cols] * s1_ref[:, cols] + cw_ref[2:3, cols] * cx
        acc = acc + _mm((bg * y * _silu(z)).astype(BF16), wo_ref[cols, :])
        cx_ref[:, cols] = cx
    xo_ref[0] = _embed(x + acc, p_ref[0], pg_ref, pe_ref)


def _conv_sample(x, p_all, layer, g, win, cw, wo, pg, pe, s0, s1):
    _, M, D = x.shape
    E = wo.shape[0]
    p_spec = pl.BlockSpec((None, 1, M, p_all.shape[-1]), lambda i: (layer, 0, 0, 0))
    return pl.pallas_call(
        _conv_sample_kernel,
        grid=(1,),
        in_specs=[_resident(x.shape), p_spec, _resident(g.shape), _resident(win.shape), _resident(cw.shape),
                  _resident(wo.shape), _resident(pg.shape), _resident(pe.shape),
                  _resident(s0.shape), _resident(s1.shape)],
        out_specs=[_whole(x.shape), _whole((M, E))],
        out_shape=[jax.ShapeDtypeStruct(x.shape, F32), jax.ShapeDtypeStruct((M, E), F32)],
        compiler_params=_params("arbitrary"),
        name="conv_sample",
    )(x, p_all, g, win, cw, wo, pg, pe, s0, s1)


def _rope(x, cos, sin_lo, sin_hi):
    half = ROPE_DIM // 2
    return x * cos + pltpu.roll(x, C_DH - half, 1) * sin_lo + pltpu.roll(x, half, 1) * sin_hi


def _c_in_kernel(x_ref, g_ref, w_ref, cos_ref, slo_ref, shi_ref, *refs, tails, dils):
    n_groups = len(C_GROUPS)
    qkv_refs = refs[:3 * n_groups]
    z_ref = refs[3 * n_groups]
    tail_refs = refs[3 * n_groups + 1:-1]
    rows_s = refs[-1]
    h = _rms(x_ref[0], g_ref[...]).astype(BF16)
    tm = h.shape[0]
    cos, slo, shi = cos_ref[...], slo_ref[...], shi_ref[...]
    tile_end = (pl.program_id(1) + 1) * tm

    def rope_all(u):
        return jnp.concatenate(
            [_rope(u[:, hd * C_DH:(hd + 1) * C_DH], cos, slo, shi) for hd in range(C_HEADS)], axis=1)

    for grp in range(n_groups):
        base = 3 * grp * C_WIDTH
        q = rope_all(_mm(h, w_ref[:, base:base + C_WIDTH])) * (C_DH ** -0.5)
        k = rope_all(_mm(h, w_ref[:, base + C_WIDTH:base + 2 * C_WIDTH]))
        v = _mm(h, w_ref[:, base + 2 * C_WIDTH:base + 3 * C_WIDTH])
        dil = dils[grp]
        for ref, val in zip(qkv_refs[3 * grp:3 * grp + 3], (q, k, v)):
            if dil == 1:
                ref[0, 0] = val.astype(BF16)
            else:
                for hd in range(C_HEADS):
                    rows_s[hd] = val[:, hd * C_DH:(hd + 1) * C_DH]
                for r in range(dil):
                    for hd in range(C_HEADS):
                        picked = rows_s[hd, pl.ds(r, tm // dil, stride=dil), :]
                        ref[0, r, :, hd * C_DH:(hd + 1) * C_DH] = picked.astype(BF16)
        first_row, rows = tails[grp]

        @pl.when(tile_end > first_row)
        def _(k=k, v=v, grp=grp, rows=rows):
            tail_refs[2 * grp][0] = k[tm - rows:, :]
            tail_refs[2 * grp + 1][0] = v[tm - rows:, :]

    zb = 3 * n_groups * C_WIDTH
    z_ref[0] = _mm(h, w_ref[:, zb:zb + C_WIDTH]).astype(BF16)


def _c_in(x, g, w, cos, slo, shi, tm, per_row=False):
    B, T, D = x.shape
    row = lambda width: pl.BlockSpec((1, tm, width), lambda b, t: (b, t, 0))
    table = pl.BlockSpec((tm, C_DH), lambda b, t: (t, 0))
    dils = tuple(1 if per_row else dil for _, dil in C_GROUPS)
    qkv_specs, qkv_shapes = [], []
    for dil in dils:
        qkv_specs += [pl.BlockSpec((1, dil, tm // dil, C_WIDTH), lambda b, t: (b, 0, t, 0))] * 3
        qkv_shapes += [jax.ShapeDtypeStruct((B, dil, T // dil, C_WIDTH), BF16)] * 3
    tails, tail_specs, tail_shapes = [], [], []
    for win, _ in C_GROUPS:
        keep = T if per_row else min(win, T)
        rows = min(tm, keep)
        first_row = T - keep
        tails.append((first_row, rows))

        def tail_index(b, t, first_row=first_row, rows=rows):
            return (b, jnp.maximum(((t + 1) * tm - first_row) // rows - 1, 0), 0)

        tail_specs += [pl.BlockSpec((1, rows, C_WIDTH), tail_index)] * 2
        tail_shapes += [jax.ShapeDtypeStruct((B, keep, C_WIDTH), F32)] * 2
    n_qkv = len(qkv_specs)
    outs = pl.pallas_call(
        functools.partial(_c_in_kernel, tails=tuple(tails), dils=dils),
        grid=(B, T // tm),
        in_specs=[row(D), _resident(g.shape), _resident(w.shape), table, table, table],
        out_specs=qkv_specs + [row(C_WIDTH)] + tail_specs,
        out_shape=qkv_shapes + [jax.ShapeDtypeStruct((B, T, C_WIDTH), BF16)] + tail_shapes,
        scratch_shapes=[pltpu.VMEM((C_HEADS, tm, C_DH), F32)],
        compiler_params=_params("parallel", "arbitrary"),
        name="attn_in",
    )(x, g, w, cos, slo, shi)
    return outs[:n_qkv], outs[n_qkv], outs[n_qkv + 1:]


ATTN_CHUNK = 512


def _attn_chunk_kernel(q_ref, kp_ref, kc_ref, vp_ref, vc_ref, acc_ref, st_ref, k_s, v_s, *, span):
    chunk = pl.program_id(2)
    n_blocks = q_ref.shape[2] // C_BLOCK
    k_s[0:C_BLOCK, :] = kp_ref[0, 0]
    k_s[C_BLOCK:, :] = kc_ref[0, 0]
    v_s[0:C_BLOCK, :] = vp_ref[0, 0]
    v_s[C_BLOCK:, :] = vc_ref[0, 0]
    qi = lax.broadcasted_iota(jnp.int32, (C_BLOCK, 2 * C_BLOCK), 0)
    kj = lax.broadcasted_iota(jnp.int32, (C_BLOCK, 2 * C_BLOCK), 1)
    dist = qi + C_BLOCK - kj
    band = (dist >= 0) & (dist <= span)
    lane = lax.broadcasted_iota(jnp.int32, (C_BLOCK, LANES), 1)

    def block(i, carry):
        lo = pl.multiple_of(i * C_BLOCK, C_BLOCK)
        first_key = jnp.where((chunk == 0) & (i == 0), C_BLOCK, 0)
        mask = band & (kj >= first_key)
        stats = jnp.zeros((C_BLOCK, LANES), F32)
        for hd in range(C_HEADS):
            hs = slice(hd * C_DH, (hd + 1) * C_DH)
            qh = q_ref[0, 0, pl.ds(lo, C_BLOCK), hs]
            kh = k_s[pl.ds(lo, 2 * C_BLOCK), hs]
            vh = v_s[pl.ds(lo, 2 * C_BLOCK), hs]
            s = jnp.where(mask, lax.dot_general(qh, kh, NT_DIMS, preferred_element_type=F32), -1e30)
            mx = jnp.max(s, axis=-1, keepdims=True)
            p = jnp.where(mask, jnp.exp(s - mx), 0.0)
            l = jnp.sum(p, axis=-1, keepdims=True)
            acc_ref[0, 0, pl.ds(lo, C_BLOCK), hs] = _mm(p.astype(BF16), vh).astype(BF16)
            stats = jnp.where(lane == hd, mx, jnp.where(lane == C_HEADS + hd, l, stats))
        st_ref[0, 0, pl.ds(lo, C_BLOCK), :] = stats
        return carry

    lax.fori_loop(0, n_blocks, block, 0)


def _attn_prompt_group(q, k, v, span):
    B, dil, n, _ = q.shape
    qc = min(n, ATTN_CHUNK)
    per_chunk = qc // C_BLOCK
    cur = lambda rows, width: pl.BlockSpec((1, 1, rows, width), lambda b, r, c: (b, r, c, 0))
    prev = pl.BlockSpec((1, 1, C_BLOCK, C_WIDTH), lambda b, r, c: (b, r, jnp.maximum(c * per_chunk - 1, 0), 0))
    return pl.pallas_call(
        functools.partial(_attn_chunk_kernel, span=span),
        grid=(B, dil, n // qc),
        in_specs=[cur(qc, C_WIDTH), prev, cur(qc, C_WIDTH), prev, cur(qc, C_WIDTH)],
        out_specs=[cur(qc, C_WIDTH), cur(qc, LANES)],
        out_shape=[jax.ShapeDtypeStruct((B, dil, n, C_WIDTH), BF16),
                   jax.ShapeDtypeStruct((B, dil, n, LANES), F32)],
        scratch_shapes=[pltpu.VMEM((C_BLOCK + qc, C_WIDTH), BF16), pltpu.VMEM((C_BLOCK + qc, C_WIDTH), BF16)],
        compiler_params=_params("parallel", "parallel", "arbitrary"),
        name="attn_chunks_d%d" % dil,
    )(q, k, k, v, v)


def _merge_out_kernel(a0_ref, a1_ref, a2_ref, s0_ref, s1_ref, s2_ref, z_ref, x_ref, p_ref,
                      wo_ref, pg_ref, pe_ref, xo_ref, y_s, acc_s, st_s):
    tm = x_ref.shape[1]
    for grp, (a_ref, s_ref) in enumerate(zip((a0_ref, a1_ref, a2_ref), (s0_ref, s1_ref, s2_ref))):
        dil = a_ref.shape[1]
        for r in range(dil):
            rows = pl.ds(r, tm // dil, stride=dil) if dil > 1 else slice(None)
            acc = a_ref[0, r].astype(F32)
            for hd in range(C_HEADS):
                acc_s[grp * C_HEADS + hd, rows, :] = acc[:, hd * C_DH:(hd + 1) * C_DH]
            st_s[grp, rows, :] = s_ref[0, r]
    mx = [st_s[grp, :, 0:C_HEADS] for grp in range(3)]
    ls = [st_s[grp, :, C_HEADS:2 * C_HEADS] for grp in range(3)]
    top = jnp.maximum(jnp.maximum(mx[0], mx[1]), mx[2])
    es = [jnp.exp(m - top) for m in mx]
    den = es[0] * ls[0] + es[1] * ls[1] + es[2] * ls[2]
    coef = [e / den for e in es]
    for hd in range(C_HEADS):
        hs = slice(hd * C_DH, (hd + 1) * C_DH)
        o = coef[0][:, hd:hd + 1] * acc_s[hd]
        for grp in (1, 2):
            o = o + coef[grp][:, hd:hd + 1] * acc_s[grp * C_HEADS + hd]
        y_s[:, hs] = (o * _silu(z_ref[0, :, hs].astype(F32))).astype(BF16)
    x1 = x_ref[0] + _mm(y_s[...], wo_ref[...])
    xo_ref[0] = _embed(x1, p_ref[0], pg_ref, pe_ref)


def _merge_out(accs, stats, z, x, p_all, layer, wo, pg, pe, tm):
    B, T, D = x.shape
    row = lambda width: pl.BlockSpec((1, tm, width), lambda b, t: (b, t, 0))
    classes = lambda a: pl.BlockSpec((1, a.shape[1], tm // a.shape[1], a.shape[3]), lambda b, t: (b, 0, t, 0))
    p_spec = pl.BlockSpec((None, 1, tm, p_all.shape[-1]), lambda b, t: (layer, b, t, 0))
    return pl.pallas_call(
        _merge_out_kernel,
        grid=(B, T // tm),
        in_specs=[classes(a) for a in accs] + [classes(s) for s in stats] + [row(C_WIDTH), row(D), p_spec,
                  _resident(wo.shape), _resident(pg.shape), _resident(pe.shape)],
        out_specs=row(D),
        out_shape=jax.ShapeDtypeStruct((B, T, D), F32),
        scratch_shapes=[pltpu.VMEM((tm, C_WIDTH), BF16), pltpu.VMEM((3 * C_HEADS, tm, C_DH), F32),
                        pltpu.VMEM((3, tm, LANES), F32)],
        compiler_params=_params("parallel", "parallel"),
        name="attn_merge_out",
    )(*accs, *stats, z, x, p_all, wo, pg, pe)


def _attn_sample_kernel(*refs):
    n_groups = len(C_GROUPS)
    z_ref, y_ref = refs[5 * n_groups], refs[5 * n_groups + 1]
    parts = []
    for grp in range(n_groups):
        q_ref, kn_ref, vn_ref, kc_ref, vc_ref = refs[5 * grp:5 * grp + 5]
        q = q_ref[0].astype(F32)
        s_old = jnp.sum(kc_ref[0] * q[None], axis=-1, keepdims=True)
        s_new = jnp.sum(kn_ref[0] * q, axis=-1, keepdims=True)
        mx = jnp.maximum(jnp.max(s_old, axis=0), s_new)
        p_old = jnp.exp(s_old - mx[None])
        p_new = jnp.exp(s_new - mx)
        l = jnp.sum(p_old, axis=0) + p_new
        acc = jnp.sum(p_old * vc_ref[0], axis=0) + p_new * vn_ref[0]
        parts.append((acc, mx, l))
    top = jnp.maximum(jnp.maximum(parts[0][1], parts[1][1]), parts[2][1])
    es = [jnp.exp(m - top) for _, m, _ in parts]
    num = es[0] * parts[0][0] + es[1] * parts[1][0] + es[2] * parts[2][0]
    den = es[0] * parts[0][2] + es[1] * parts[1][2] + es[2] * parts[2][2]
    y_ref[0] = (num / den * _silu(z_ref[0].astype(F32))).astype(BF16)


def _attn_sample(qkv, tails, caches, z):
    Bs = z.shape[1]
    heads = lambda a: a.reshape(Bs, C_HEADS, C_DH)
    row = pl.BlockSpec((1, C_HEADS, C_DH), lambda b: (b, 0, 0))
    args, specs = [], []
    for grp, (win, dil) in enumerate(C_GROUPS):
        kc, vc = caches[grp]
        n_buf = kc.shape[1]
        assert n_buf == win and win % dil == 0, "sample window must be fully cached"
        span = win // dil
        view = lambda a: a.reshape(Bs, span, dil, C_HEADS, C_DH)
        cache_spec = pl.BlockSpec((1, span, None, C_HEADS, C_DH), lambda b: (b, 0, 0, 0, 0))
        args += [heads(qkv[3 * grp]), heads(tails[2 * grp]), heads(tails[2 * grp + 1]), view(kc), view(vc)]
        specs += [row, row, row, cache_spec, cache_spec]
    return pl.pallas_call(
        _attn_sample_kernel,
        grid=(Bs,),
        in_specs=specs + [row],
        out_specs=row,
        out_shape=jax.ShapeDtypeStruct((Bs, C_HEADS, C_DH), BF16),
        compiler_params=_params("parallel"),
        name="attn_sample",
    )(*args, heads(z))


def _pool_tail(x, acc, p, pg_ref, pe_ref, fg_ref):
    return _rms(_embed(x + acc, p, pg_ref, pe_ref), fg_ref[...])


def _pool_prompt_kernel(x_ref, p_ref, g_ref, win_ref, wgrp_ref, sc_ref, wo_ref, pg_ref, pe_ref, fg_ref,
                        xo_ref, st_ref, pad_s):
    E = wo_ref.shape[0]
    G = E // len(D_WINDOWS)
    t = pl.program_id(1)

    @pl.when(t == 0)
    def _():
        pad_s[0:POOL_PAD, :] = jnp.zeros((POOL_PAD, E), F32)

    x = x_ref[0]
    tm = x.shape[0]
    h = _rms(x, g_ref[...]).astype(BF16)
    pos = t * tm + lax.broadcasted_iota(jnp.int32, (tm, 1), 0)
    acc = jnp.zeros(x.shape, F32)
    for grp, w in enumerate(D_WINDOWS):
        cols = slice(grp * G, (grp + 1) * G)
        xp = _mm(h, win_ref[:, cols])
        z = _mm(h, win_ref[:, E + grp * G:E + (grp + 1) * G])
        pad_s[POOL_PAD:, cols] = xp
        wsum = pad_s[:, cols]
        shift = 1
        while shift < w:
            wsum = wsum + pltpu.roll(wsum, shift, 0)
            shift *= 2
        cnt = jnp.minimum(w, pos + 1).astype(F32)
        r = wsum[POOL_PAD:, :] / cnt - xp
        y = _mm(r.astype(BF16), wgrp_ref[grp]) * sc_ref[:, cols]
        acc = acc + _mm((y * _silu(z)).astype(BF16), wo_ref[cols, :])
        pad_s[0:POOL_PAD, cols] = xp[tm - POOL_PAD:, :]
    xo_ref[0] = _pool_tail(x, acc, p_ref[0], pg_ref, pe_ref, fg_ref)
    st_ref[0] = pad_s[0:POOL_PAD, :]


def _pool_prompt(x, p_all, layer, g, win, wgrp, sc, wo, pg, pe, fg, tm):
    B, T, D = x.shape
    E = wo.shape[0]
    row = lambda width: pl.BlockSpec((1, tm, width), lambda b, t: (b, t, 0))
    p_spec = pl.BlockSpec((None, 1, tm, p_all.shape[-1]), lambda b, t: (layer, b, t, 0))
    return pl.pallas_call(
        _pool_prompt_kernel,
        grid=(B, T // tm),
        in_specs=[row(D), p_spec] + [_resident(a.shape) for a in (g, win, wgrp, sc, wo, pg, pe, fg)],
        out_specs=[row(D), pl.BlockSpec((1, POOL_PAD, E), lambda b, t: (b, 0, 0))],
        out_shape=[jax.ShapeDtypeStruct((B, T, D), F32), jax.ShapeDtypeStruct((B, POOL_PAD, E), F32)],
        scratch_shapes=[pltpu.VMEM((POOL_PAD + tm, E), F32)],
        compiler_params=_params("parallel", "arbitrary"),
        name="pool_prompt",
    )(x, p_all, g, win, wgrp, sc, wo, pg, pe, fg)


def _pool_sample_kernel(x_ref, p_ref, g_ref, win_ref, wgrp_ref, sc_ref, wo_ref, pg_ref, pe_ref, fg_ref, st_ref,
                        xo_ref, xp_ref, *, pos):
    E = wo_ref.shape[0]
    G = E // len(D_WINDOWS)
    x = x_ref[0]
    h = _rms(x, g_ref[...]).astype(BF16)
    acc = jnp.zeros(x.shape, F32)
    for grp, w in enumerate(D_WINDOWS):
        cols = slice(grp * G, (grp + 1) * G)
        xp = _mm(h, win_ref[:, cols])
        z = _mm(h, win_ref[:, E + grp * G:E + (grp + 1) * G])
        wsum = xp
        for back in range(1, w):
            lo = (POOL_STATE - back) * E + grp * G
            wsum = wsum + st_ref[:, lo:lo + G]
        r = wsum / float(min(w, pos + 1)) - xp
        y = _mm(r.astype(BF16), wgrp_ref[grp]) * sc_ref[:, cols]
        acc = acc + _mm((y * _silu(z)).astype(BF16), wo_ref[cols, :])
        xp_ref[:, cols] = xp
    xo_ref[0] = _pool_tail(x, acc, p_ref[0], pg_ref, pe_ref, fg_ref)


def _pool_sample(x, p_all, layer, g, win, wgrp, sc, wo, pg, pe, fg, st, pos):
    _, M, D = x.shape
    E = wo.shape[0]
    p_spec = pl.BlockSpec((None, 1, M, p_all.shape[-1]), lambda i: (layer, 0, 0, 0))
    return pl.pallas_call(
        functools.partial(_pool_sample_kernel, pos=pos),
        grid=(1,),
        in_specs=[_resident(x.shape), p_spec] + [_resident(a.shape) for a in (g, win, wgrp, sc, wo, pg, pe, fg, st)],
        out_specs=[_whole(x.shape), _whole((M, E))],
        out_shape=[jax.ShapeDtypeStruct(x.shape, F32), jax.ShapeDtypeStruct((M, E), F32)],
        compiler_params=_params("arbitrary"),
        name="pool_sample",
    )(x, p_all, g, win, wgrp, sc, wo, pg, pe, fg, st)


def _rope_tables(pos):
    half = ROPE_DIM // 2
    inv = ROPE_THETA ** (-jnp.arange(half, dtype=F32) / half)
    ang = pos.astype(F32)[:, None] * inv[None, :]
    cos, sin = jnp.cos(ang), jnp.sin(ang)
    rest = C_DH - ROPE_DIM
    n = pos.shape[0]
    cos_t = jnp.concatenate([cos, cos, jnp.ones((n, rest), F32)], axis=1)
    sin_lo = jnp.concatenate([-sin, jnp.zeros((n, half + rest), F32)], axis=1)
    sin_hi = jnp.concatenate([jnp.zeros((n, half), F32), sin, jnp.zeros((n, rest), F32)], axis=1)
    return cos_t, sin_lo, sin_hi


def _row(a):
    return a.reshape(1, -1).astype(F32)


def kernel(x_prompt, x_sample, state_mlstm_C, state_mlstm_n, state_mlstm_m, state_conv, cache_k_w128, cache_v_w128, cache_k_w512, cache_v_w512, cache_k_w2048, cache_v_w2048, state_pool, p_prompt, p_sample, norm_g, pe_w, pg_w, final_g, a_w_in, a_b_if, a_norm_g, a_w_out, b_w_in, b_conv_w, b_w_out, c_w_in, c_w_out, d_w_in, d_w_grp, d_scale, d_w_out):
    B, T, D = x_prompt.shape
    Bs, Ts, _ = x_sample.shape
    assert Ts == 1 and norm_g.shape[0] == 4 and a_w_in.shape[0] == 1
    assert T % 512 == 0 and all(T % (dil * C_BLOCK) == 0 for _, dil in C_GROUPS)
    assert Bs % 8 == 0

    bf = lambda a: a.astype(BF16)
    qk = A_HEADS * A_DK
    n_main = 2 * qk + 3 * A_WIDTH
    a_w = bf(a_w_in[0, :, :n_main])
    a_wg = bf(jnp.pad(a_w_in[0, :, n_main:], ((0, 0), (0, LANES - 2 * A_HEADS))))
    a_bif = jnp.pad(a_b_if[0], (0, LANES - 2 * A_HEADS)).reshape(1, LANES)
    a_ng = _row(a_norm_g[0])
    a_wo, b_wi, b_wo, c_wi, c_wo, d_wi, d_wg, d_wo = (
        bf(a_w_out[0]), bf(b_w_in[0]), bf(b_w_out[0]), bf(c_w_in[0]), bf(c_w_out[0]),
        bf(d_w_in[0]), bf(d_w_grp[0]), bf(d_w_out[0]))
    pg, pe = bf(pg_w), bf(pe_w)
    gs = [_row(norm_g[i]) for i in range(4)]
    fg = _row(final_g)
    b_cw = b_conv_w[0]
    d_sc = _row(d_scale[0])
    caches = [(cache_k_w128[0], cache_v_w128[0]), (cache_k_w512[0], cache_v_w512[0]),
              (cache_k_w2048[0], cache_v_w2048[0])]

    x = x_prompt
    q, k, v, o, z, gates = _a_in(x, gs[0], a_w, a_wg, a_bif, 512)
    y, c_p, n_p, m_p = _mlstm_prompt(q, k, v, o, z, gates, a_ng)
    x = _out_embed(y, x, p_prompt, 0, a_wo, pg[0], pe[0], 512)
    x, conv_p = _conv_prompt(x, p_prompt, 1, gs[1], b_wi, b_cw, b_wo, pg[1], pe[1], 512)
    qkv, z, tails_p = _c_in(x, gs[2], c_wi, *_rope_tables(jnp.arange(T)), 256)
    accs, stats = [], []
    for grp, (win, dil) in enumerate(C_GROUPS):
        acc, st = _attn_prompt_group(qkv[3 * grp], qkv[3 * grp + 1], qkv[3 * grp + 2], win // dil)
        accs.append(acc)
        stats.append(st)
    x = _merge_out(accs, stats, z, x, p_prompt, 2, c_wo, pg[2], pe[2], 512)
    y_prompt, pool_p = _pool_prompt(x, p_prompt, 3, gs[3], d_wi, d_wg, d_sc, d_wo, pg[3], pe[3], fg, 512)

    xs = x_sample.reshape(1, Bs, D)
    ps = p_sample.reshape(p_sample.shape[0], 1, Bs, p_sample.shape[-1])
    q, k, v, o, z, gates = _a_in(xs, gs[0], a_w, a_wg, a_bif, Bs)
    per_seq = lambda a: a.reshape(Bs, 1, a.shape[-1])
    y, c_s, n_s, m_s = _mlstm_sample(per_seq(q), per_seq(k), per_seq(v), per_seq(o), per_seq(z), per_seq(gates),
                                     a_ng, state_mlstm_C[0], state_mlstm_n[0],
                                     state_mlstm_m[0].reshape(Bs, 1, A_HEADS))
    xs = _out_embed(y.reshape(1, Bs, A_WIDTH), xs, ps, 0, a_wo, pg[0], pe[0], Bs)
    xs, cx = _conv_sample(xs, ps, 1, gs[1], b_wi, b_cw, b_wo, pg[1], pe[1],
                          state_conv[0, :, 0, :], state_conv[0, :, 1, :])
    conv_s = jnp.stack([state_conv[0, :, 1, :], cx], axis=1)
    tables = [jnp.broadcast_to(t, (Bs, C_DH)) for t in _rope_tables(PAST_LEN + jnp.arange(1))]
    qkv, z, tails_s = _c_in(xs, gs[2], c_wi, *tables, Bs, per_row=True)
    y = _attn_sample(qkv, tails_s, caches, z)
    xs = _out_embed(y.reshape(1, Bs, C_WIDTH), xs, ps, 2, c_wo, pg[2], pe[2], Bs)
    pool_flat = state_pool[0].reshape(Bs, POOL_STATE * state_pool.shape[-1])
    ys, xp = _pool_sample(xs, ps, 3, gs[3], d_wi, d_wg, d_sc, d_wo, pg[3], pe[3], fg, pool_flat, PAST_LEN)
    pool_s = jnp.concatenate([state_pool[0, :, 1:, :], xp[:, None, :]], axis=1)

    heads = lambda a, lead: a.reshape(1, lead, a.shape[1], C_HEADS, C_DH)
    kv_out = []
    for grp in range(len(C_GROUPS)):
        for j in range(2):
            kv_out += [heads(tails_p[2 * grp + j], B), tails_s[2 * grp + j].reshape(1, Bs, 1, C_HEADS, C_DH)]
    return (y_prompt, ys.reshape(Bs, 1, D),
            c_p[None], c_s[None], n_p[None], n_s[None], m_p[None, :, :, 0], m_s.reshape(1, Bs, A_HEADS),
            conv_p[None], conv_s[None],
            *kv_out,
            pool_p[None, :, 1:, :], pool_s[None])
```

```python
import functools

import jax
import jax.numpy as jnp
from jax import lax
from jax.experimental import pallas as pl
from jax.experimental.pallas import tpu as pltpu

F32 = jnp.float32
BF16 = jnp.bfloat16

EPS = 1e-6
PAST_LEN = 8192

A_HEADS = 8
A_DK = 128
A_DV = 256
A_WIDTH = A_HEADS * A_DV
A_CHUNK = 128

CONV_W = 3

C_HEADS = 8
C_DH = 128
C_WIDTH = C_HEADS * C_DH
C_GROUPS = ((128, 1), (512, 4), (2048, 16))
C_BLOCK = 128
ROPE_DIM = C_DH // 4
ROPE_THETA = 500000.0

D_WINDOWS = (2, 4, 8, 16)
POOL_STATE = max(D_WINDOWS) - 1
POOL_PAD = POOL_STATE + 1

LANES = 128
VMEM_LIMIT_BYTES = 56 * 2 ** 20

NT_DIMS = (((1,), (1,)), ((), ()))
TN_DIMS = (((0,), (0,)), ((), ()))


def _params(*semantics):
    return pltpu.CompilerParams(dimension_semantics=semantics, vmem_limit_bytes=VMEM_LIMIT_BYTES)


def _resident(shape):
    zeros = (0,) * len(shape)
    return pl.BlockSpec(shape, lambda *_: zeros, pipeline_mode=pl.Buffered(1))


def _whole(shape):
    zeros = (0,) * len(shape)
    return pl.BlockSpec(shape, lambda *_: zeros)


def _mm(a, b):
    return jnp.dot(a, b, preferred_element_type=F32)


def _rms(x, g):
    return x * lax.rsqrt(jnp.mean(x * x, axis=-1, keepdims=True) + EPS) * g


def _sigmoid(x):
    return 1.0 / (1.0 + jnp.exp(-x))


def _silu(x):
    return x * _sigmoid(x)


def _log_sigmoid(x):
    return jnp.minimum(x, 0.0) - jnp.log(1.0 + jnp.exp(-jnp.abs(x)))


def _embed(x1, p, pg_ref, pe_ref):
    gate = _sigmoid(_mm(x1.astype(BF16), pg_ref[...]))
    return x1 + gate * _mm(p.astype(BF16), pe_ref[...])


def _a_in_kernel(x_ref, g_ref, w_ref, wg_ref, bif_ref,
                 q_ref, k_ref, v_ref, o_ref, z_ref, gate_ref, *maybe_kt_ref):
    h = _rms(x_ref[0], g_ref[...]).astype(BF16)
    qk = A_HEADS * A_DK

    def proj(lo, width):
        return _mm(h, w_ref[:, lo:lo + width])

    q_ref[0] = proj(0, qk).astype(BF16)
    k = proj(qk, qk) * (A_DK ** -0.5)
    k_ref[0] = k.astype(BF16)
    for kt_ref in maybe_kt_ref:
        kt_ref[0] = k.T.astype(BF16)
    v_ref[0] = proj(2 * qk, A_WIDTH).astype(BF16)
    o_ref[0] = proj(2 * qk + A_WIDTH, A_WIDTH).astype(BF16)
    z_ref[0] = proj(2 * qk + 2 * A_WIDTH, A_WIDTH).astype(BF16)
    gates = _mm(h, wg_ref[...]) + bif_ref[...]
    lane = lax.broadcasted_iota(jnp.int32, gates.shape, 1)
    gate_ref[0] = jnp.where(lane < A_HEADS, gates, _log_sigmoid(gates))


def _a_in(x, g, w, wg, bif, tm, transposed_k=False):
    B, T, D = x.shape
    qk = A_HEADS * A_DK
    row = lambda width: pl.BlockSpec((1, tm, width), lambda b, t: (b, t, 0))
    out_specs = [row(qk), row(qk), row(A_WIDTH), row(A_WIDTH), row(A_WIDTH), row(LANES)]
    out_shape = [jax.ShapeDtypeStruct((B, T, qk), BF16), jax.ShapeDtypeStruct((B, T, qk), BF16),
                 jax.ShapeDtypeStruct((B, T, A_WIDTH), BF16), jax.ShapeDtypeStruct((B, T, A_WIDTH), BF16),
                 jax.ShapeDtypeStruct((B, T, A_WIDTH), BF16), jax.ShapeDtypeStruct((B, T, LANES), F32)]
    if transposed_k:
        out_specs.append(pl.BlockSpec((1, qk, tm), lambda b, t: (b, 0, t)))
        out_shape.append(jax.ShapeDtypeStruct((B, qk, T), BF16))
    return pl.pallas_call(
        _a_in_kernel,
        grid=(B, T // tm),
        in_specs=[row(D), _resident(g.shape), _resident(w.shape), _resident(wg.shape), _resident(bif.shape)],
        out_specs=out_specs,
        out_shape=out_shape,
        compiler_params=_params("parallel", "parallel"),
        name="mlstm_in",
    )(x, g, w, wg, bif)


def _mlstm_chunk_kernel(q_ref, k_ref, kt_ref, v_ref, gate_ref,
                        hh_ref, c_out, n_out, m_out, m_s, *state_s):
    chunk = pl.program_id(1)
    ct_s, nb_s = state_s[:A_HEADS], state_s[A_HEADS:]

    @pl.when(chunk == 0)
    def _():
        for ref in state_s:
            ref[...] = jnp.zeros_like(ref)
        m_s[...] = jnp.zeros_like(m_s)

    L = q_ref.shape[1]
    gates = gate_ref[0]
    row = lax.broadcasted_iota(jnp.int32, gates.shape, 0)

    def prefix(x, op, identity):
        step = 1
        while step < L:
            x = op(x, jnp.where(row >= step, pltpu.roll(x, step, 0), identity))
            step *= 2
        return x

    b = pltpu.roll(prefix(gates, jnp.add, 0.0), LANES - A_HEADS, 1)
    c = gates - b
    m_row = m_s[0:1, :]
    top = jnp.maximum(m_row, prefix(c, jnp.maximum, -jnp.inf))
    top_last = top[L - 1:L, :]
    g_all = jnp.exp(m_row - top)
    floor_all = jnp.exp(-(b + top))
    w_all = jnp.exp(c - top_last)
    decay_row = jnp.exp(m_row - top_last)
    m_s[...] = jnp.broadcast_to(b[L - 1:L, :] + top_last, m_s.shape)
    c_t = c.T
    ti = lax.broadcasted_iota(jnp.int32, (L, L), 0)
    si = lax.broadcasted_iota(jnp.int32, (L, L), 1)
    causal = ti >= si
    ones = jnp.ones((L, LANES), BF16)
    heads = range(A_HEADS)
    ks = [slice(h * A_DK, (h + 1) * A_DK) for h in heads]
    vs = [slice(h * A_DV, (h + 1) * A_DV) for h in heads]

    s_raw = [_mm(q_ref[0, :, ks[h]], kt_ref[0, ks[h], :]) for h in heads]
    upd = []
    for h in heads:
        w_b = jnp.broadcast_to(w_all[:, h:h + 1], (L, LANES))
        wv = jnp.concatenate([w_b, w_b], axis=1) * v_ref[0, :, vs[h]].astype(F32)
        upd.append(_mm(kt_ref[0, ks[h], :], jnp.concatenate([wv.astype(BF16), w_b.astype(BF16)], axis=1)))

    for h in heads:
        ct = ct_s[h][...]
        nb = nb_s[h][...]
        dmat = jnp.exp(jnp.where(causal, -top[:, h:h + 1] + c_t[h:h + 1, :], -jnp.inf))
        s = s_raw[h] * dmat
        gq = g_all[:, h:h + 1] * q_ref[0, :, ks[h]].astype(F32)
        lhs = jnp.concatenate([s.astype(BF16), gq.astype(BF16)], axis=1)
        num = _mm(lhs, jnp.concatenate([v_ref[0, :, vs[h]], ct.astype(BF16)], axis=0))
        den = _mm(lhs, jnp.concatenate([ones, nb.astype(BF16)], axis=0))
        scale = 1.0 / jnp.maximum(jnp.abs(den), floor_all[:, h:h + 1])
        hh_ref[0, :, vs[h]] = (num * jnp.concatenate([scale, scale], axis=1)).astype(BF16)
        decay = decay_row[:, h:h + 1]
        ct_s[h][...] = decay * ct + upd[h][:, :A_DV]
        nb_s[h][...] = decay * nb + upd[h][:, A_DV:]

    @pl.when(chunk == pl.num_programs(1) - 1)
    def _():
        for h in range(A_HEADS):
            c_out[0, h] = ct_s[h][...].T
            n_out[0, h:h + 1, :] = nb_s[h][...].T[0:1, :]
        m_out[0] = m_s[...]


def _mlstm_prompt(q, k, kt, v, gates):
    B, T, _ = q.shape
    L = A_CHUNK
    qk = A_HEADS * A_DK
    row = lambda width: pl.BlockSpec((1, L, width), lambda b, c: (b, c, 0))
    state = lambda *shape: pl.BlockSpec((1,) + shape, lambda b, c: (b,) + (0,) * len(shape))
    return pl.pallas_call(
        _mlstm_chunk_kernel,
        grid=(B, T // L),
        in_specs=[row(qk), row(qk), pl.BlockSpec((1, qk, L), lambda b, c: (b, 0, c)), row(A_WIDTH), row(LANES)],
        out_specs=[row(A_WIDTH), state(A_HEADS, A_DV, A_DK), state(A_HEADS, A_DK), state(A_HEADS, LANES)],
        out_shape=[jax.ShapeDtypeStruct((B, T, A_WIDTH), BF16),
                   jax.ShapeDtypeStruct((B, A_HEADS, A_DV, A_DK), F32),
                   jax.ShapeDtypeStruct((B, A_HEADS, A_DK), F32),
                   jax.ShapeDtypeStruct((B, A_HEADS, LANES), F32)],
        scratch_shapes=[pltpu.VMEM((A_HEADS, LANES), F32)] + [pltpu.VMEM((A_DK, A_DV), F32)] * A_HEADS
        + [pltpu.VMEM((A_DK, LANES), F32)] * A_HEADS,
        compiler_params=_params("parallel", "arbitrary"),
        name="mlstm_chunks",
    )(q, k, kt, v, gates)


def _mlstm_step_kernel(q_ref, k_ref, v_ref, gate_ref, c_ref, n_ref, m_ref,
                       hh_ref, c_out, n_out, m_out):
    gates = gate_ref[0]
    row = lax.broadcasted_iota(jnp.int32, (LANES, A_DV), 0)
    for h in range(A_HEADS):
        ks = slice(h * A_DK, (h + 1) * A_DK)
        vs = slice(h * A_DV, (h + 1) * A_DV)
        ig = gates[:, h:h + 1]
        lf = gates[:, A_HEADS + h:A_HEADS + h + 1]
        m_prev = m_ref[0, :, h:h + 1]
        qh = q_ref[0, :, ks].astype(F32)
        kh = k_ref[0, :, ks].astype(F32)
        vh = v_ref[0, :, vs].astype(F32)
        c_h = c_ref[0, h]
        n_h = n_ref[0, h:h + 1, :]

        inter = lf + m_prev
        m_t = jnp.maximum(inter, ig)
        dm = jnp.exp(ig - m_t)
        g = jnp.exp(inter - m_t)
        s = jnp.sum(qh * kh, axis=-1, keepdims=True) * dm
        q8 = jnp.broadcast_to(qh, (8, A_DK)).astype(BF16)
        cq = lax.dot_general(q8, c_h.astype(BF16), NT_DIMS, preferred_element_type=F32)[0:1, :]
        num = s * vh + g * cq
        den = s + g * jnp.sum(qh * n_h, axis=-1, keepdims=True)
        hh_ref[0, :, vs] = (num / jnp.maximum(jnp.abs(den), jnp.exp(-m_t))).astype(BF16)

        v_pad = jnp.where(row == 0, jnp.broadcast_to(dm * vh, (LANES, A_DV)), 0.0).astype(BF16)
        k_pad = jnp.broadcast_to(kh, (LANES, A_DK)).astype(BF16)
        c_out[0, h] = g * c_h + lax.dot_general(v_pad, k_pad, TN_DIMS, preferred_element_type=F32)
        n_out[0, h:h + 1, :] = g * n_h + dm * kh
        m_out[0, :, h:h + 1] = m_t


def _mlstm_sample(q, k, v, gates, c0, n0, m0):
    Bs = q.shape[0]
    qk = A_HEADS * A_DK
    row = lambda width: pl.BlockSpec((1, 1, width), lambda b: (b, 0, 0))
    c_spec = pl.BlockSpec((1, A_HEADS, A_DV, A_DK), lambda b: (b, 0, 0, 0))
    n_spec = pl.BlockSpec((1, A_HEADS, A_DK), lambda b: (b, 0, 0))
    m_spec = pl.BlockSpec((1, 1, A_HEADS), lambda b: (b, 0, 0))
    return pl.pallas_call(
        _mlstm_step_kernel,
        grid=(Bs,),
        in_specs=[row(qk), row(qk), row(A_WIDTH), row(LANES), c_spec, n_spec, m_spec],
        out_specs=[row(A_WIDTH), c_spec, n_spec, m_spec],
        out_shape=[jax.ShapeDtypeStruct((Bs, 1, A_WIDTH), BF16),
                   jax.ShapeDtypeStruct(c0.shape, F32), jax.ShapeDtypeStruct(n0.shape, F32),
                   jax.ShapeDtypeStruct(m0.shape, F32)],
        compiler_params=_params("parallel"),
        name="mlstm_step",
    )(q, k, v, gates, c0, n0, m0)


def _a_out_kernel(hh_ref, o_ref, z_ref, ng_ref, x_ref, p_ref, wo_ref, pg_ref, pe_ref, xo_ref, y_s):
    for h in range(A_HEADS):
        vs = slice(h * A_DV, (h + 1) * A_DV)
        hh = hh_ref[0, :, vs].astype(F32)
        hn = hh * lax.rsqrt(jnp.mean(hh * hh, axis=-1, keepdims=True) + EPS) * ng_ref[:, vs]
        y = hn * _sigmoid(o_ref[0, :, vs].astype(F32)) * _silu(z_ref[0, :, vs].astype(F32))
        y_s[:, vs] = y.astype(BF16)
    x1 = x_ref[0] + _mm(y_s[...], wo_ref[...])
    xo_ref[0] = _embed(x1, p_ref[0], pg_ref, pe_ref)


def _a_out(hh, o, z, ng, x, p_all, layer, wo, pg, pe, tm):
    B, T, D = x.shape
    row = lambda width: pl.BlockSpec((1, tm, width), lambda b, t: (b, t, 0))
    p_spec = pl.BlockSpec((None, 1, tm, p_all.shape[-1]), lambda b, t: (layer, b, t, 0))
    return pl.pallas_call(
        _a_out_kernel,
        grid=(B, T // tm),
        in_specs=[row(A_WIDTH), row(A_WIDTH), row(A_WIDTH), _resident(ng.shape), row(D), p_spec,
                  _resident(wo.shape), _resident(pg.shape), _resident(pe.shape)],
        out_specs=row(D),
        out_shape=jax.ShapeDtypeStruct((B, T, D), F32),
        scratch_shapes=[pltpu.VMEM((tm, A_WIDTH), BF16)],
        compiler_params=_params("parallel", "parallel"),
        name="mlstm_out",
    )(hh, o, z, ng, x, p_all, wo, pg, pe)


def _out_kernel(y_ref, x_ref, p_ref, wo_ref, pg_ref, pe_ref, xo_ref):
    x1 = x_ref[0] + _mm(y_ref[0], wo_ref[...])
    xo_ref[0] = _embed(x1, p_ref[0], pg_ref, pe_ref)


def _out_embed(y, x, p_all, layer, wo, pg, pe, tm):
    B, T, D = x.shape
    row = lambda width: pl.BlockSpec((1, tm, width), lambda b, t: (b, t, 0))
    p_spec = pl.BlockSpec((None, 1, tm, p_all.shape[-1]), lambda b, t: (layer, b, t, 0))
    return pl.pallas_call(
        _out_kernel,
        grid=(B, T // tm),
        in_specs=[row(y.shape[-1]), row(D), p_spec, _resident(wo.shape), _resident(pg.shape), _resident(pe.shape)],
        out_specs=row(D),
        out_shape=jax.ShapeDtypeStruct((B, T, D), F32),
        compiler_params=_params("parallel", "parallel"),
        name="out_embed",
    )(y, x, p_all, wo, pg, pe)


CONV_COLS = 512


def _conv_prompt_kernel(x_ref, p_ref, g_ref, win_ref, cw_ref, wo_ref, pg_ref, pe_ref,
                        xo_ref, st_ref, carry_s):
    E = wo_ref.shape[0]

    @pl.when(pl.program_id(1) == 0)
    def _():
        carry_s[...] = jnp.zeros_like(carry_s)

    x = x_ref[0]
    tm = x.shape[0]
    h = _rms(x, g_ref[...]).astype(BF16)
    row = lax.broadcasted_iota(jnp.int32, (tm, CONV_COLS), 0)
    acc = jnp.zeros(x.shape, F32)
    for c in range(E // CONV_COLS):
        cols = slice(c * CONV_COLS, (c + 1) * CONV_COLS)
        part = lambda i: _mm(h, win_ref[:, i * E + c * CONV_COLS:i * E + (c + 1) * CONV_COLS])
        bg, cg, xb, z = part(0), part(1), part(2), part(3)
        cx = cg * xb
        old = carry_s[0:1, cols]
        new = carry_s[1:2, cols]
        prev1 = jnp.where(row == 0, new, pltpu.roll(cx, 1, 0))
        prev2 = jnp.where(row == 0, old, jnp.where(row == 1, new, pltpu.roll(cx, 2, 0)))
        y = cw_ref[0:1, cols] * prev2 + cw_ref[1:2, cols] * prev1 + cw_ref[2:3, cols] * cx
        acc = acc + _mm((bg * y * _silu(z)).astype(BF16), wo_ref[cols, :])
        carry_s[0:2, cols] = cx[tm - 2:tm, :]
    xo_ref[0] = _embed(x + acc, p_ref[0], pg_ref, pe_ref)
    st_ref[0] = carry_s[0:2, :]


def _conv_prompt(x, p_all, layer, g, win, cw, wo, pg, pe, tm):
    B, T, D = x.shape
    E = wo.shape[0]
    row = lambda width: pl.BlockSpec((1, tm, width), lambda b, t: (b, t, 0))
    p_spec = pl.BlockSpec((None, 1, tm, p_all.shape[-1]), lambda b, t: (layer, b, t, 0))
    return pl.pallas_call(
        _conv_prompt_kernel,
        grid=(B, T // tm),
        in_specs=[row(D), p_spec, _resident(g.shape), _resident(win.shape), _resident(cw.shape),
                  _resident(wo.shape), _resident(pg.shape), _resident(pe.shape)],
        out_specs=[row(D), pl.BlockSpec((1, CONV_W - 1, E), lambda b, t: (b, 0, 0))],
        out_shape=[jax.ShapeDtypeStruct((B, T, D), F32), jax.ShapeDtypeStruct((B, CONV_W - 1, E), F32)],
        scratch_shapes=[pltpu.VMEM((8, E), F32)],
        compiler_params=_params("parallel", "arbitrary"),
        name="conv_prompt",
    )(x, p_all, g, win, cw, wo, pg, pe)


def _conv_sample_kernel(x_ref, p_ref, g_ref, win_ref, cw_ref, wo_ref, pg_ref, pe_ref, s0_ref, s1_ref,
                        xo_ref, cx_ref):
    E = wo_ref.shape[0]
    x = x_ref[0]
    h = _rms(x, g_ref[...]).astype(BF16)
    acc = jnp.zeros(x.shape, F32)
    for c in range(E // CONV_COLS):
        cols = slice(c * CONV_COLS, (c + 1) * CONV_COLS)
        part = lambda i: _mm(h, win_ref[:, i * E + c * CONV_COLS:i * E + (c + 1) * CONV_COLS])
        bg, cg, xb, z = part(0), part(1), part(2), part(3)
        cx = cg * xb
        y = cw_ref[0:1, cols] * s0_ref[:, cols] + cw_ref[1:2, cols] * s1_ref[:, cols] + cw_ref[2:3, cols] * cx
        acc = acc + _mm((bg * y * _silu(z)).astype(BF16), wo_ref[cols, :])
        cx_ref[:, cols] = cx
    xo_ref[0] = _embed(x + acc, p_ref[0], pg_ref, pe_ref)


def _conv_sample(x, p_all, layer, g, win, cw, wo, pg, pe, s0, s1):
    _, M, D = x.shape
    E = wo.shape[0]
    p_spec = pl.BlockSpec((None, 1, M, p_all.shape[-1]), lambda i: (layer, 0, 0, 0))
    return pl.pallas_call(
        _conv_sample_kernel,
        grid=(1,),
        in_specs=[_resident(x.shape), p_spec, _resident(g.shape), _resident(win.shape), _resident(cw.shape),
                  _resident(wo.shape), _resident(pg.shape), _resident(pe.shape),
                  _resident(s0.shape), _resident(s1.shape)],
        out_specs=[_whole(x.shape), _whole((M, E))],
        out_shape=[jax.ShapeDtypeStruct(x.shape, F32), jax.ShapeDtypeStruct((M, E), F32)],
        compiler_params=_params("arbitrary"),
        name="conv_sample",
    )(x, p_all, g, win, cw, wo, pg, pe, s0, s1)


def _rope(x, cos, sin_lo, sin_hi):
    half = ROPE_DIM // 2
    return x * cos + pltpu.roll(x, C_DH - half, 1) * sin_lo + pltpu.roll(x, half, 1) * sin_hi


def _c_in_kernel(x_ref, g_ref, w_ref, cos_ref, slo_ref, shi_ref, *refs, tails, dils):
    n_groups = len(C_GROUPS)
    qkv_refs = refs[:3 * n_groups]
    z_ref = refs[3 * n_groups]
    tail_refs = refs[3 * n_groups + 1:-1]
    rows_s = refs[-1]
    h = _rms(x_ref[0], g_ref[...]).astype(BF16)
    tm = h.shape[0]
    cos, slo, shi = cos_ref[...], slo_ref[...], shi_ref[...]
    tile_end = (pl.program_id(1) + 1) * tm

    def rope_all(u):
        return jnp.concatenate(
            [_rope(u[:, hd * C_DH:(hd + 1) * C_DH], cos, slo, shi) for hd in range(C_HEADS)], axis=1)

    for grp in range(n_groups):
        base = 3 * grp * C_WIDTH
        q = rope_all(_mm(h, w_ref[:, base:base + C_WIDTH])) * (C_DH ** -0.5)
        k = rope_all(_mm(h, w_ref[:, base + C_WIDTH:base + 2 * C_WIDTH]))
        v = _mm(h, w_ref[:, base + 2 * C_WIDTH:base + 3 * C_WIDTH])
        dil = dils[grp]
        for ref, val in zip(qkv_refs[3 * grp:3 * grp + 3], (q, k, v)):
            if dil == 1:
                ref[0, 0] = val.astype(BF16)
            else:
                for hd in range(C_HEADS):
                    rows_s[hd] = val[:, hd * C_DH:(hd + 1) * C_DH]
                for r in range(dil):
                    for hd in range(C_HEADS):
                        picked = rows_s[hd, pl.ds(r, tm // dil, stride=dil), :]
                        ref[0, r, :, hd * C_DH:(hd + 1) * C_DH] = picked.astype(BF16)
        first_row, rows = tails[grp]

        @pl.when(tile_end > first_row)
        def _(k=k, v=v, grp=grp, rows=rows):
            tail_refs[2 * grp][0] = k[tm - rows:, :]
            tail_refs[2 * grp + 1][0] = v[tm - rows:, :]

    zb = 3 * n_groups * C_WIDTH
    z_ref[0] = _mm(h, w_ref[:, zb:zb + C_WIDTH]).astype(BF16)


def _c_in(x, g, w, cos, slo, shi, tm, per_row=False):
    B, T, D = x.shape
    row = lambda width: pl.BlockSpec((1, tm, width), lambda b, t: (b, t, 0))
    table = pl.BlockSpec((tm, C_DH), lambda b, t: (t, 0))
    dils = tuple(1 if per_row else dil for _, dil in C_GROUPS)
    qkv_specs, qkv_shapes = [], []
    for dil in dils:
        qkv_specs += [pl.BlockSpec((1, dil, tm // dil, C_WIDTH), lambda b, t: (b, 0, t, 0))] * 3
        qkv_shapes += [jax.ShapeDtypeStruct((B, dil, T // dil, C_WIDTH), BF16)] * 3
    tails, tail_specs, tail_shapes = [], [], []
    for win, _ in C_GROUPS:
        keep = T if per_row else min(win, T)
        rows = min(tm, keep)
        first_row = T - keep
        tails.append((first_row, rows))

        def tail_index(b, t, first_row=first_row, rows=rows):
            return (b, jnp.maximum(((t + 1) * tm - first_row) // rows - 1, 0), 0)

        tail_specs += [pl.BlockSpec((1, rows, C_WIDTH), tail_index)] * 2
        tail_shapes += [jax.ShapeDtypeStruct((B, keep, C_WIDTH), F32)] * 2
    n_qkv = len(qkv_specs)
    outs = pl.pallas_call(
        functools.partial(_c_in_kernel, tails=tuple(tails), dils=dils),
        grid=(B, T // tm),
        in_specs=[row(D), _resident(g.shape), _resident(w.shape), table, table, table],
        out_specs=qkv_specs + [row(C_WIDTH)] + tail_specs,
        out_shape=qkv_shapes + [jax.ShapeDtypeStruct((B, T, C_WIDTH), BF16)] + tail_shapes,
        scratch_shapes=[pltpu.VMEM((C_HEADS, tm, C_DH), F32)],
        compiler_params=_params("parallel", "arbitrary"),
        name="attn_in",
    )(x, g, w, cos, slo, shi)
    return outs[:n_qkv], outs[n_qkv], outs[n_qkv + 1:]


ATTN_CHUNK = 512


def _attn_chunk_kernel(q_ref, kp_ref, kc_ref, vp_ref, vc_ref, acc_ref, st_ref, k_s, v_s, *, span):
    chunk = pl.program_id(2)
    n_blocks = q_ref.shape[2] // C_BLOCK
    k_s[0:C_BLOCK, :] = kp_ref[0, 0]
    k_s[C_BLOCK:, :] = kc_ref[0, 0]
    v_s[0:C_BLOCK, :] = vp_ref[0, 0]
    v_s[C_BLOCK:, :] = vc_ref[0, 0]
    qi = lax.broadcasted_iota(jnp.int32, (C_BLOCK, 2 * C_BLOCK), 0)
    kj = lax.broadcasted_iota(jnp.int32, (C_BLOCK, 2 * C_BLOCK), 1)
    dist = qi + C_BLOCK - kj
    band = (dist >= 0) & (dist <= span)
    lane = lax.broadcasted_iota(jnp.int32, (C_BLOCK, LANES), 1)
    ones = jnp.ones((2 * C_BLOCK, LANES), BF16)
    head_cols = [slice(hd * C_DH, (hd + 1) * C_DH) for hd in range(C_HEADS)]

    def block(i, carry):
        lo = pl.multiple_of(i * C_BLOCK, C_BLOCK)
        first_key = jnp.where((chunk == 0) & (i == 0), C_BLOCK, 0)
        bias = jnp.where(band & (kj >= first_key), 0.0, -1e30)

        def scores(hd):
            return lax.dot_general(q_ref[0, 0, pl.ds(lo, C_BLOCK), head_cols[hd]],
                                   k_s[pl.ds(lo, 2 * C_BLOCK), head_cols[hd]], NT_DIMS,
                                   preferred_element_type=F32)

        stats = jnp.zeros((C_BLOCK, LANES), F32)
        s_next = scores(0)
        for hd in range(C_HEADS):
            s = s_next + bias
            if hd + 1 < C_HEADS:
                s_next = scores(hd + 1)
            mx = jnp.max(s, axis=-1, keepdims=True)
            p = jnp.exp(s - mx).astype(BF16)
            both = _mm(p, jnp.concatenate([v_s[pl.ds(lo, 2 * C_BLOCK), head_cols[hd]], ones], axis=1))
            acc_ref[0, 0, pl.ds(lo, C_BLOCK), head_cols[hd]] = both[:, :C_DH].astype(BF16)
            stats = jnp.where(lane == hd, mx, jnp.where(lane == C_HEADS + hd, both[:, C_DH:], stats))
        st_ref[0, 0, pl.ds(lo, C_BLOCK), :] = stats
        return carry

    lax.fori_loop(0, n_blocks, block, 0)


def _attn_prompt_group(q, k, v, span):
    B, dil, n, _ = q.shape
    qc = min(n, ATTN_CHUNK)
    per_chunk = qc // C_BLOCK
    cur = lambda rows, width: pl.BlockSpec((1, 1, rows, width), lambda b, r, c: (b, r, c, 0))
    prev = pl.BlockSpec((1, 1, C_BLOCK, C_WIDTH), lambda b, r, c: (b, r, jnp.maximum(c * per_chunk - 1, 0), 0))
    return pl.pallas_call(
        functools.partial(_attn_chunk_kernel, span=span),
        grid=(B, dil, n // qc),
        in_specs=[cur(qc, C_WIDTH), prev, cur(qc, C_WIDTH), prev, cur(qc, C_WIDTH)],
        out_specs=[cur(qc, C_WIDTH), cur(qc, LANES)],
        out_shape=[jax.ShapeDtypeStruct((B, dil, n, C_WIDTH), BF16),
                   jax.ShapeDtypeStruct((B, dil, n, LANES), F32)],
        scratch_shapes=[pltpu.VMEM((C_BLOCK + qc, C_WIDTH), BF16), pltpu.VMEM((C_BLOCK + qc, C_WIDTH), BF16)],
        compiler_params=_params("parallel", "parallel", "arbitrary"),
        name="attn_chunks_d%d" % dil,
    )(q, k, k, v, v)


def _merge_out_kernel(a0_ref, a1_ref, a2_ref, s0_ref, s1_ref, s2_ref, z_ref, x_ref, p_ref,
                      wo_ref, pg_ref, pe_ref, xo_ref, y_s, acc_s, st_s):
    tm = x_ref.shape[1]
    for grp, (a_ref, s_ref) in enumerate(zip((a0_ref, a1_ref, a2_ref), (s0_ref, s1_ref, s2_ref))):
        dil = a_ref.shape[1]
        for r in range(dil):
            rows = pl.ds(r, tm // dil, stride=dil) if dil > 1 else slice(None)
            acc = a_ref[0, r].astype(F32)
            for hd in range(C_HEADS):
                acc_s[grp * C_HEADS + hd, rows, :] = acc[:, hd * C_DH:(hd + 1) * C_DH]
            st_s[grp, rows, :] = s_ref[0, r]
    mx = [st_s[grp, :, 0:C_HEADS] for grp in range(3)]
    ls = [st_s[grp, :, C_HEADS:2 * C_HEADS] for grp in range(3)]
    top = jnp.maximum(jnp.maximum(mx[0], mx[1]), mx[2])
    es = [jnp.exp(m - top) for m in mx]
    den = es[0] * ls[0] + es[1] * ls[1] + es[2] * ls[2]
    coef = [e / den for e in es]
    for hd in range(C_HEADS):
        hs = slice(hd * C_DH, (hd + 1) * C_DH)
        o = coef[0][:, hd:hd + 1] * acc_s[hd]
        for grp in (1, 2):
            o = o + coef[grp][:, hd:hd + 1] * acc_s[grp * C_HEADS + hd]
        y_s[:, hs] = (o * _silu(z_ref[0, :, hs].astype(F32))).astype(BF16)
    x1 = x_ref[0] + _mm(y_s[...], wo_ref[...])
    xo_ref[0] = _embed(x1, p_ref[0], pg_ref, pe_ref)


def _merge_out(accs, stats, z, x, p_all, layer, wo, pg, pe, tm):
    B, T, D = x.shape
    row = lambda width: pl.BlockSpec((1, tm, width), lambda b, t: (b, t, 0))
    classes = lambda a: pl.BlockSpec((1, a.shape[1], tm // a.shape[1], a.shape[3]), lambda b, t: (b, 0, t, 0))
    p_spec = pl.BlockSpec((None, 1, tm, p_all.shape[-1]), lambda b, t: (layer, b, t, 0))
    return pl.pallas_call(
        _merge_out_kernel,
        grid=(B, T // tm),
        in_specs=[classes(a) for a in accs] + [classes(s) for s in stats] + [row(C_WIDTH), row(D), p_spec,
                  _resident(wo.shape), _resident(pg.shape), _resident(pe.shape)],
        out_specs=row(D),
        out_shape=jax.ShapeDtypeStruct((B, T, D), F32),
        scratch_shapes=[pltpu.VMEM((tm, C_WIDTH), BF16), pltpu.VMEM((3 * C_HEADS, tm, C_DH), F32),
                        pltpu.VMEM((3, tm, LANES), F32)],
        compiler_params=_params("parallel", "parallel"),
        name="attn_merge_out",
    )(*accs, *stats, z, x, p_all, wo, pg, pe)


def _attn_sample_kernel(*refs):
    n_groups = len(C_GROUPS)
    z_ref, y_ref = refs[5 * n_groups], refs[5 * n_groups + 1]
    parts = []
    for grp in range(n_groups):
        q_ref, kn_ref, vn_ref, kc_ref, vc_ref = refs[5 * grp:5 * grp + 5]
        q = q_ref[0].astype(F32)
        s_old = jnp.sum(kc_ref[0] * q[None], axis=-1, keepdims=True)
        s_new = jnp.sum(kn_ref[0] * q, axis=-1, keepdims=True)
        mx = jnp.maximum(jnp.max(s_old, axis=0), s_new)
        p_old = jnp.exp(s_old - mx[None])
        p_new = jnp.exp(s_new - mx)
        l = jnp.sum(p_old, axis=0) + p_new
        acc = jnp.sum(p_old * vc_ref[0], axis=0) + p_new * vn_ref[0]
        parts.append((acc, mx, l))
    top = jnp.maximum(jnp.maximum(parts[0][1], parts[1][1]), parts[2][1])
    es = [jnp.exp(m - top) for _, m, _ in parts]
    num = es[0] * parts[0][0] + es[1] * parts[1][0] + es[2] * parts[2][0]
    den = es[0] * parts[0][2] + es[1] * parts[1][2] + es[2] * parts[2][2]
    y_ref[0] = (num / den * _silu(z_ref[0].astype(F32))).astype(BF16)


def _attn_sample(qkv, tails, caches, z):
    Bs = z.shape[1]
    heads = lambda a: a.reshape(Bs, C_HEADS, C_DH)
    row = pl.BlockSpec((1, C_HEADS, C_DH), lambda b: (b, 0, 0))
    args, specs = [], []
    for grp, (win, dil) in enumerate(C_GROUPS):
        kc, vc = caches[grp]
        n_buf = kc.shape[1]
        assert n_buf == win and win % dil == 0, "sample window must be fully cached"
        span = win // dil
        view = lambda a: a.reshape(Bs, span, dil, C_HEADS, C_DH)
        cache_spec = pl.BlockSpec((1, span, None, C_HEADS, C_DH), lambda b: (b, 0, 0, 0, 0))
        args += [heads(qkv[3 * grp]), heads(tails[2 * grp]), heads(tails[2 * grp + 1]), view(kc), view(vc)]
        specs += [row, row, row, cache_spec, cache_spec]
    return pl.pallas_call(
        _attn_sample_kernel,
        grid=(Bs,),
        in_specs=specs + [row],
        out_specs=row,
        out_shape=jax.ShapeDtypeStruct((Bs, C_HEADS, C_DH), BF16),
        compiler_params=_params("parallel"),
        name="attn_sample",
    )(*args, heads(z))


def _pool_tail(x, acc, p, pg_ref, pe_ref, fg_ref):
    return _rms(_embed(x + acc, p, pg_ref, pe_ref), fg_ref[...])


def _pool_prompt_kernel(x_ref, p_ref, g_ref, win_ref, wgrp_ref, sc_ref, wo_ref, pg_ref, pe_ref, fg_ref,
                        xo_ref, st_ref, pad_s):
    E = wo_ref.shape[0]
    G = E // len(D_WINDOWS)
    t = pl.program_id(1)

    @pl.when(t == 0)
    def _():
        pad_s[0:POOL_PAD, :] = jnp.zeros((POOL_PAD, E), F32)

    x = x_ref[0]
    tm = x.shape[0]
    h = _rms(x, g_ref[...]).astype(BF16)
    pos = t * tm + lax.broadcasted_iota(jnp.int32, (tm, 1), 0)
    acc = jnp.zeros(x.shape, F32)
    for grp, w in enumerate(D_WINDOWS):
        cols = slice(grp * G, (grp + 1) * G)
        xp = _mm(h, win_ref[:, cols])
        z = _mm(h, win_ref[:, E + grp * G:E + (grp + 1) * G])
        pad_s[POOL_PAD:, cols] = xp
        wsum = pad_s[:, cols]
        shift = 1
        while shift < w:
            wsum = wsum + pltpu.roll(wsum, shift, 0)
            shift *= 2
        cnt = jnp.minimum(w, pos + 1).astype(F32)
        r = wsum[POOL_PAD:, :] / cnt - xp
        y = _mm(r.astype(BF16), wgrp_ref[grp]) * sc_ref[:, cols]
        acc = acc + _mm((y * _silu(z)).astype(BF16), wo_ref[cols, :])
        pad_s[0:POOL_PAD, cols] = xp[tm - POOL_PAD:, :]
    xo_ref[0] = _pool_tail(x, acc, p_ref[0], pg_ref, pe_ref, fg_ref)
    st_ref[0] = pad_s[0:POOL_PAD, :]


def _pool_prompt(x, p_all, layer, g, win, wgrp, sc, wo, pg, pe, fg, tm):
    B, T, D = x.shape
    E = wo.shape[0]
    row = lambda width: pl.BlockSpec((1, tm, width), lambda b, t: (b, t, 0))
    p_spec = pl.BlockSpec((None, 1, tm, p_all.shape[-1]), lambda b, t: (layer, b, t, 0))
    return pl.pallas_call(
        _pool_prompt_kernel,
        grid=(B, T // tm),
        in_specs=[row(D), p_spec] + [_resident(a.shape) for a in (g, win, wgrp, sc, wo, pg, pe, fg)],
        out_specs=[row(D), pl.BlockSpec((1, POOL_PAD, E), lambda b, t: (b, 0, 0))],
        out_shape=[jax.ShapeDtypeStruct((B, T, D), F32), jax.ShapeDtypeStruct((B, POOL_PAD, E), F32)],
        scratch_shapes=[pltpu.VMEM((POOL_PAD + tm, E), F32)],
        compiler_params=_params("parallel", "arbitrary"),
        name="pool_prompt",
    )(x, p_all, g, win, wgrp, sc, wo, pg, pe, fg)


def _pool_sample_kernel(x_ref, p_ref, g_ref, win_ref, wgrp_ref, sc_ref, wo_ref, pg_ref, pe_ref, fg_ref, st_ref,
                        xo_ref, xp_ref, *, pos):
    E = wo_ref.shape[0]
    G = E // len(D_WINDOWS)
    x = x_ref[0]
    h = _rms(x, g_ref[...]).astype(BF16)
    acc = jnp.zeros(x.shape, F32)
    for grp, w in enumerate(D_WINDOWS):
        cols = slice(grp * G, (grp + 1) * G)
        xp = _mm(h, win_ref[:, cols])
        z = _mm(h, win_ref[:, E + grp * G:E + (grp + 1) * G])
        wsum = xp
        for back in range(1, w):
            lo = (POOL_STATE - back) * E + grp * G
            wsum = wsum + st_ref[:, lo:lo + G]
        r = wsum / float(min(w, pos + 1)) - xp
        y = _mm(r.astype(BF16), wgrp_ref[grp]) * sc_ref[:, cols]
        acc = acc + _mm((y * _silu(z)).astype(BF16), wo_ref[cols, :])
        xp_ref[:, cols] = xp
    xo_ref[0] = _pool_tail(x, acc, p_ref[0], pg_ref, pe_ref, fg_ref)


def _pool_sample(x, p_all, layer, g, win, wgrp, sc, wo, pg, pe, fg, st, pos):
    _, M, D = x.shape
    E = wo.shape[0]
    p_spec = pl.BlockSpec((None, 1, M, p_all.shape[-1]), lambda i: (layer, 0, 0, 0))
    return pl.pallas_call(
        functools.partial(_pool_sample_kernel, pos=pos),
        grid=(1,),
        in_specs=[_resident(x.shape), p_spec] + [_resident(a.shape) for a in (g, win, wgrp, sc, wo, pg, pe, fg, st)],
        out_specs=[_whole(x.shape), _whole((M, E))],
        out_shape=[jax.ShapeDtypeStruct(x.shape, F32), jax.ShapeDtypeStruct((M, E), F32)],
        compiler_params=_params("arbitrary"),
        name="pool_sample",
    )(x, p_all, g, win, wgrp, sc, wo, pg, pe, fg, st)


def _rope_tables(pos):
    half = ROPE_DIM // 2
    inv = ROPE_THETA ** (-jnp.arange(half, dtype=F32) / half)
    ang = pos.astype(F32)[:, None] * inv[None, :]
    cos, sin = jnp.cos(ang), jnp.sin(ang)
    rest = C_DH - ROPE_DIM
    n = pos.shape[0]
    cos_t = jnp.concatenate([cos, cos, jnp.ones((n, rest), F32)], axis=1)
    sin_lo = jnp.concatenate([-sin, jnp.zeros((n, half + rest), F32)], axis=1)
    sin_hi = jnp.concatenate([jnp.zeros((n, half), F32), sin, jnp.zeros((n, rest), F32)], axis=1)
    return cos_t, sin_lo, sin_hi


def _row(a):
    return a.reshape(1, -1).astype(F32)


def kernel(x_prompt, x_sample, state_mlstm_C, state_mlstm_n, state_mlstm_m, state_conv, cache_k_w128, cache_v_w128, cache_k_w512, cache_v_w512, cache_k_w2048, cache_v_w2048, state_pool, p_prompt, p_sample, norm_g, pe_w, pg_w, final_g, a_w_in, a_b_if, a_norm_g, a_w_out, b_w_in, b_conv_w, b_w_out, c_w_in, c_w_out, d_w_in, d_w_grp, d_scale, d_w_out):
    B, T, D = x_prompt.shape
    Bs, Ts, _ = x_sample.shape
    assert Ts == 1 and norm_g.shape[0] == 4 and a_w_in.shape[0] == 1
    assert T % 512 == 0 and all(T % (dil * C_BLOCK) == 0 for _, dil in C_GROUPS)
    assert Bs % 8 == 0

    bf = lambda a: a.astype(BF16)
    qk = A_HEADS * A_DK
    n_main = 2 * qk + 3 * A_WIDTH
    a_w = bf(a_w_in[0, :, :n_main])
    a_wg = bf(jnp.pad(a_w_in[0, :, n_main:], ((0, 0), (0, LANES - 2 * A_HEADS))))
    a_bif = jnp.pad(a_b_if[0], (0, LANES - 2 * A_HEADS)).reshape(1, LANES)
    a_ng = _row(a_norm_g[0])
    a_wo, b_wi, b_wo, c_wi, c_wo, d_wi, d_wg, d_wo = (
        bf(a_w_out[0]), bf(b_w_in[0]), bf(b_w_out[0]), bf(c_w_in[0]), bf(c_w_out[0]),
        bf(d_w_in[0]), bf(d_w_grp[0]), bf(d_w_out[0]))
    pg, pe = bf(pg_w), bf(pe_w)
    gs = [_row(norm_g[i]) for i in range(4)]
    fg = _row(final_g)
    b_cw = b_conv_w[0]
    d_sc = _row(d_scale[0])
    caches = [(cache_k_w128[0], cache_v_w128[0]), (cache_k_w512[0], cache_v_w512[0]),
              (cache_k_w2048[0], cache_v_w2048[0])]

    x = x_prompt
    q, k, v, o, z, gates, kt = _a_in(x, gs[0], a_w, a_wg, a_bif, 512, transposed_k=True)
    hh, c_p, n_p, m_p = _mlstm_prompt(q, k, kt, v, gates)
    x = _a_out(hh, o, z, a_ng, x, p_prompt, 0, a_wo, pg[0], pe[0], 512)
    x, conv_p = _conv_prompt(x, p_prompt, 1, gs[1], b_wi, b_cw, b_wo, pg[1], pe[1], 512)
    qkv, z, tails_p = _c_in(x, gs[2], c_wi, *_rope_tables(jnp.arange(T)), 256)
    accs, stats = [], []
    for grp, (win, dil) in enumerate(C_GROUPS):
        acc, st = _attn_prompt_group(qkv[3 * grp], qkv[3 * grp + 1], qkv[3 * grp + 2], win // dil)
        accs.append(acc)
        stats.append(st)
    x = _merge_out(accs, stats, z, x, p_prompt, 2, c_wo, pg[2], pe[2], 512)
    y_prompt, pool_p = _pool_prompt(x, p_prompt, 3, gs[3], d_wi, d_wg, d_sc, d_wo, pg[3], pe[3], fg, 512)

    xs = x_sample.reshape(1, Bs, D)
    ps = p_sample.reshape(p_sample.shape[0], 1, Bs, p_sample.shape[-1])
    q, k, v, o, z, gates = _a_in(xs, gs[0], a_w, a_wg, a_bif, Bs)
    per_seq = lambda a: a.reshape(Bs, 1, a.shape[-1])
    hh, c_s, n_s, m_s = _mlstm_sample(per_seq(q), per_seq(k), per_seq(v), per_seq(gates),
                                      state_mlstm_C[0], state_mlstm_n[0], state_mlstm_m[0].reshape(Bs, 1, A_HEADS))
    xs = _a_out(hh.reshape(1, Bs, A_WIDTH), o, z, a_ng, xs, ps, 0, a_wo, pg[0], pe[0], Bs)
    xs, cx = _conv_sample(xs, ps, 1, gs[1], b_wi, b_cw, b_wo, pg[1], pe[1],
                          state_conv[0, :, 0, :], state_conv[0, :, 1, :])
    conv_s = jnp.stack([state_conv[0, :, 1, :], cx], axis=1)
    tables = [jnp.broadcast_to(t, (Bs, C_DH)) for t in _rope_tables(PAST_LEN + jnp.arange(1))]
    qkv, z, tails_s = _c_in(xs, gs[2], c_wi, *tables, Bs, per_row=True)
    y = _attn_sample(qkv, tails_s, caches, z)
    xs = _out_embed(y.reshape(1, Bs, C_WIDTH), xs, ps, 2, c_wo, pg[2], pe[2], Bs)
    pool_flat = state_pool[0].reshape(Bs, POOL_STATE * state_pool.shape[-1])
    ys, xp = _pool_sample(xs, ps, 3, gs[3], d_wi, d_wg, d_sc, d_wo, pg[3], pe[3], fg, pool_flat, PAST_LEN)
    pool_s = jnp.concatenate([state_pool[0, :, 1:, :], xp[:, None, :]], axis=1)

    heads = lambda a, lead: a.reshape(1, lead, a.shape[1], C_HEADS, C_DH)
    kv_out = []
    for grp in range(len(C_GROUPS)):
        for j in range(2):
            kv_out += [heads(tails_p[2 * grp + j], B), tails_s[2 * grp + j].reshape(1, Bs, 1, C_HEADS, C_DH)]
    return (y_prompt, ys.reshape(Bs, 1, D),
            c_p[None], c_s[None], n_p[None], n_s[None], m_p[None, :, 0, :A_HEADS], m_s.reshape(1, Bs, A_HEADS),
            conv_p[None], conv_s[None],
            *kv_out,
            pool_p[None, :, 1:, :], pool_s[None])
```

```python
import functools

import jax
import jax.numpy as jnp
from jax import lax
from jax.experimental import pallas as pl
from jax.experimental.pallas import tpu as pltpu

F32 = jnp.float32
BF16 = jnp.bfloat16

EPS = 1e-6
PAST_LEN = 8192

A_HEADS = 8
A_DK = 128
A_DV = 256
A_WIDTH = A_HEADS * A_DV
A_CHUNK = 128

CONV_W = 3

C_HEADS = 8
C_DH = 128
C_WIDTH = C_HEADS * C_DH
C_GROUPS = ((128, 1), (512, 4), (2048, 16))
C_BLOCK = 128
ROPE_DIM = C_DH // 4
ROPE_THETA = 500000.0

D_WINDOWS = (2, 4, 8, 16)
POOL_STATE = max(D_WINDOWS) - 1
POOL_PAD = POOL_STATE + 1

LANES = 128
VMEM_LIMIT_BYTES = 56 * 2 ** 20

NT_DIMS = (((1,), (1,)), ((), ()))
TN_DIMS = (((0,), (0,)), ((), ()))


def _params(*semantics):
    return pltpu.CompilerParams(dimension_semantics=semantics, vmem_limit_bytes=VMEM_LIMIT_BYTES)


def _resident(shape):
    zeros = (0,) * len(shape)
    return pl.BlockSpec(shape, lambda *_: zeros, pipeline_mode=pl.Buffered(1))


def _whole(shape):
    zeros = (0,) * len(shape)
    return pl.BlockSpec(shape, lambda *_: zeros)


def _mm(a, b):
    return jnp.dot(a, b, preferred_element_type=F32)


def _rms(x, g):
    return x * lax.rsqrt(jnp.mean(x * x, axis=-1, keepdims=True) + EPS) * g


def _sigmoid(x):
    return 0.5 * jnp.tanh(0.5 * x) + 0.5


def _silu(x):
    return x * _sigmoid(x)


def _log_sigmoid(x):
    return jnp.minimum(x, 0.0) - jnp.log(1.0 + jnp.exp(-jnp.abs(x)))


def _embed(x1, p, pg_ref, pe_ref):
    gate = _sigmoid(_mm(x1.astype(BF16), pg_ref[...]))
    return x1 + gate * _mm(p.astype(BF16), pe_ref[...])


def _a_in_kernel(x_ref, g_ref, w_ref, wg_ref, bif_ref,
                 q_ref, k_ref, v_ref, o_ref, z_ref, gate_ref, *maybe_kt_ref):
    h = _rms(x_ref[0], g_ref[...]).astype(BF16)
    qk = A_HEADS * A_DK

    def proj(lo, width):
        return _mm(h, w_ref[:, lo:lo + width])

    q_ref[0] = proj(0, qk).astype(BF16)
    k = proj(qk, qk) * (A_DK ** -0.5)
    k_ref[0] = k.astype(BF16)
    for kt_ref in maybe_kt_ref:
        kt_ref[0] = k.T.astype(BF16)
    v_ref[0] = proj(2 * qk, A_WIDTH).astype(BF16)
    o_ref[0] = proj(2 * qk + A_WIDTH, A_WIDTH).astype(BF16)
    z_ref[0] = proj(2 * qk + 2 * A_WIDTH, A_WIDTH).astype(BF16)
    gates = _mm(h, wg_ref[...]) + bif_ref[...]
    lane = lax.broadcasted_iota(jnp.int32, gates.shape, 1)
    gate_ref[0] = jnp.where(lane < A_HEADS, gates, _log_sigmoid(gates))


def _a_in(x, g, w, wg, bif, tm, transposed_k=False):
    B, T, D = x.shape
    qk = A_HEADS * A_DK
    row = lambda width: pl.BlockSpec((1, tm, width), lambda b, t: (b, t, 0))
    out_specs = [row(qk), row(qk), row(A_WIDTH), row(A_WIDTH), row(A_WIDTH), row(LANES)]
    out_shape = [jax.ShapeDtypeStruct((B, T, qk), BF16), jax.ShapeDtypeStruct((B, T, qk), BF16),
                 jax.ShapeDtypeStruct((B, T, A_WIDTH), BF16), jax.ShapeDtypeStruct((B, T, A_WIDTH), BF16),
                 jax.ShapeDtypeStruct((B, T, A_WIDTH), BF16), jax.ShapeDtypeStruct((B, T, LANES), F32)]
    if transposed_k:
        out_specs.append(pl.BlockSpec((1, qk, tm), lambda b, t: (b, 0, t)))
        out_shape.append(jax.ShapeDtypeStruct((B, qk, T), BF16))
    return pl.pallas_call(
        _a_in_kernel,
        grid=(B, T // tm),
        in_specs=[row(D), _resident(g.shape), _resident(w.shape), _resident(wg.shape), _resident(bif.shape)],
        out_specs=out_specs,
        out_shape=out_shape,
        compiler_params=_params("parallel", "parallel"),
        name="mlstm_in",
    )(x, g, w, wg, bif)


def _mlstm_chunk_kernel(q_ref, k_ref, kt_ref, v_ref, gate_ref,
                        hh_ref, c_out, n_out, m_out, m_s, *state_s):
    chunk = pl.program_id(1)
    ct_s, nb_s = state_s[:A_HEADS], state_s[A_HEADS:]

    @pl.when(chunk == 0)
    def _():
        for ref in state_s:
            ref[...] = jnp.zeros_like(ref)
        m_s[...] = jnp.zeros_like(m_s)

    L = q_ref.shape[1]
    gates = gate_ref[0]
    row = lax.broadcasted_iota(jnp.int32, gates.shape, 0)

    def prefix(x, op, identity):
        step = 1
        while step < L:
            x = op(x, jnp.where(row >= step, pltpu.roll(x, step, 0), identity))
            step *= 2
        return x

    b = pltpu.roll(prefix(gates, jnp.add, 0.0), LANES - A_HEADS, 1)
    c = gates - b
    m_row = m_s[0:1, :]
    top = jnp.maximum(m_row, prefix(c, jnp.maximum, -jnp.inf))
    top_last = top[L - 1:L, :]
    g_all = jnp.exp(m_row - top)
    floor_all = jnp.exp(-(b + top))
    w_all = jnp.exp(c - top_last)
    decay_row = jnp.exp(m_row - top_last)
    m_s[...] = jnp.broadcast_to(b[L - 1:L, :] + top_last, m_s.shape)
    c_t = c.T
    ti = lax.broadcasted_iota(jnp.int32, (L, L), 0)
    si = lax.broadcasted_iota(jnp.int32, (L, L), 1)
    causal = ti >= si
    ones = jnp.ones((L, LANES), BF16)
    heads = range(A_HEADS)
    ks = [slice(h * A_DK, (h + 1) * A_DK) for h in heads]
    vs = [slice(h * A_DV, (h + 1) * A_DV) for h in heads]

    s_raw = [_mm(q_ref[0, :, ks[h]], kt_ref[0, ks[h], :]) for h in heads]
    upd = []
    for h in heads:
        w_b = jnp.broadcast_to(w_all[:, h:h + 1], (L, LANES))
        wv = jnp.concatenate([w_b, w_b], axis=1) * v_ref[0, :, vs[h]].astype(F32)
        upd.append(_mm(kt_ref[0, ks[h], :], jnp.concatenate([wv.astype(BF16), w_b.astype(BF16)], axis=1)))

    for h in heads:
        ct = ct_s[h][...]
        nb = nb_s[h][...]
        dmat = jnp.exp(jnp.where(causal, -top[:, h:h + 1] + c_t[h:h + 1, :], -jnp.inf))
        s = s_raw[h] * dmat
        gq = g_all[:, h:h + 1] * q_ref[0, :, ks[h]].astype(F32)
        lhs = jnp.concatenate([s.astype(BF16), gq.astype(BF16)], axis=1)
        num = _mm(lhs, jnp.concatenate([v_ref[0, :, vs[h]], ct.astype(BF16)], axis=0))
        den = _mm(lhs, jnp.concatenate([ones, nb.astype(BF16)], axis=0))
        scale = 1.0 / jnp.maximum(jnp.abs(den), floor_all[:, h:h + 1])
        hh_ref[0, :, vs[h]] = (num * jnp.concatenate([scale, scale], axis=1)).astype(BF16)
        decay = decay_row[:, h:h + 1]
        ct_s[h][...] = decay * ct + upd[h][:, :A_DV]
        nb_s[h][...] = decay * nb + upd[h][:, A_DV:]

    @pl.when(chunk == pl.num_programs(1) - 1)
    def _():
        for h in range(A_HEADS):
            c_out[0, h] = ct_s[h][...].T
            n_out[0, h:h + 1, :] = nb_s[h][...].T[0:1, :]
        m_out[0] = m_s[...]


def _mlstm_prompt(q, k, kt, v, gates):
    B, T, _ = q.shape
    L = A_CHUNK
    qk = A_HEADS * A_DK
    row = lambda width: pl.BlockSpec((1, L, width), lambda b, c: (b, c, 0))
    state = lambda *shape: pl.BlockSpec((1,) + shape, lambda b, c: (b,) + (0,) * len(shape))
    return pl.pallas_call(
        _mlstm_chunk_kernel,
        grid=(B, T // L),
        in_specs=[row(qk), row(qk), pl.BlockSpec((1, qk, L), lambda b, c: (b, 0, c)), row(A_WIDTH), row(LANES)],
        out_specs=[row(A_WIDTH), state(A_HEADS, A_DV, A_DK), state(A_HEADS, A_DK), state(A_HEADS, LANES)],
        out_shape=[jax.ShapeDtypeStruct((B, T, A_WIDTH), BF16),
                   jax.ShapeDtypeStruct((B, A_HEADS, A_DV, A_DK), F32),
                   jax.ShapeDtypeStruct((B, A_HEADS, A_DK), F32),
                   jax.ShapeDtypeStruct((B, A_HEADS, LANES), F32)],
        scratch_shapes=[pltpu.VMEM((A_HEADS, LANES), F32)] + [pltpu.VMEM((A_DK, A_DV), F32)] * A_HEADS
        + [pltpu.VMEM((A_DK, LANES), F32)] * A_HEADS,
        compiler_params=_params("parallel", "arbitrary"),
        name="mlstm_chunks",
    )(q, k, kt, v, gates)


def _mlstm_step_kernel(q_ref, k_ref, v_ref, gate_ref, c_ref, n_ref, m_ref,
                       hh_ref, c_out, n_out, m_out):
    gates = gate_ref[0]
    row = lax.broadcasted_iota(jnp.int32, (LANES, A_DV), 0)
    for h in range(A_HEADS):
        ks = slice(h * A_DK, (h + 1) * A_DK)
        vs = slice(h * A_DV, (h + 1) * A_DV)
        ig = gates[:, h:h + 1]
        lf = gates[:, A_HEADS + h:A_HEADS + h + 1]
        m_prev = m_ref[0, :, h:h + 1]
        qh = q_ref[0, :, ks].astype(F32)
        kh = k_ref[0, :, ks].astype(F32)
        vh = v_ref[0, :, vs].astype(F32)
        c_h = c_ref[0, h]
        n_h = n_ref[0, h:h + 1, :]

        inter = lf + m_prev
        m_t = jnp.maximum(inter, ig)
        dm = jnp.exp(ig - m_t)
        g = jnp.exp(inter - m_t)
        s = jnp.sum(qh * kh, axis=-1, keepdims=True) * dm
        q8 = jnp.broadcast_to(qh, (8, A_DK)).astype(BF16)
        cq = lax.dot_general(q8, c_h.astype(BF16), NT_DIMS, preferred_element_type=F32)[0:1, :]
        num = s * vh + g * cq
        den = s + g * jnp.sum(qh * n_h, axis=-1, keepdims=True)
        hh_ref[0, :, vs] = (num / jnp.maximum(jnp.abs(den), jnp.exp(-m_t))).astype(BF16)

        v_pad = jnp.where(row == 0, jnp.broadcast_to(dm * vh, (LANES, A_DV)), 0.0).astype(BF16)
        k_pad = jnp.broadcast_to(kh, (LANES, A_DK)).astype(BF16)
        c_out[0, h] = g * c_h + lax.dot_general(v_pad, k_pad, TN_DIMS, preferred_element_type=F32)
        n_out[0, h:h + 1, :] = g * n_h + dm * kh
        m_out[0, :, h:h + 1] = m_t


def _mlstm_sample(q, k, v, gates, c0, n0, m0):
    Bs = q.shape[0]
    qk = A_HEADS * A_DK
    row = lambda width: pl.BlockSpec((1, 1, width), lambda b: (b, 0, 0))
    c_spec = pl.BlockSpec((1, A_HEADS, A_DV, A_DK), lambda b: (b, 0, 0, 0))
    n_spec = pl.BlockSpec((1, A_HEADS, A_DK), lambda b: (b, 0, 0))
    m_spec = pl.BlockSpec((1, 1, A_HEADS), lambda b: (b, 0, 0))
    return pl.pallas_call(
        _mlstm_step_kernel,
        grid=(Bs,),
        in_specs=[row(qk), row(qk), row(A_WIDTH), row(LANES), c_spec, n_spec, m_spec],
        out_specs=[row(A_WIDTH), c_spec, n_spec, m_spec],
        out_shape=[jax.ShapeDtypeStruct((Bs, 1, A_WIDTH), BF16),
                   jax.ShapeDtypeStruct(c0.shape, F32), jax.ShapeDtypeStruct(n0.shape, F32),
                   jax.ShapeDtypeStruct(m0.shape, F32)],
        compiler_params=_params("parallel"),
        name="mlstm_step",
    )(q, k, v, gates, c0, n0, m0)


def _a_out_kernel(hh_ref, o_ref, z_ref, ng_ref, x_ref, p_ref, wo_ref, pg_ref, pe_ref, xo_ref):
    x1 = x_ref[0]
    for h in range(A_HEADS):
        vs = slice(h * A_DV, (h + 1) * A_DV)
        hh = hh_ref[0, :, vs].astype(F32)
        hn = hh * lax.rsqrt(jnp.mean(hh * hh, axis=-1, keepdims=True) + EPS) * ng_ref[:, vs]
        y = hn * _sigmoid(o_ref[0, :, vs].astype(F32)) * _silu(z_ref[0, :, vs].astype(F32))
        x1 = x1 + _mm(y.astype(BF16), wo_ref[vs, :])
    xo_ref[0] = _embed(x1, p_ref[0], pg_ref, pe_ref)


def _a_out(hh, o, z, ng, x, p_all, layer, wo, pg, pe, tm):
    B, T, D = x.shape
    row = lambda width: pl.BlockSpec((1, tm, width), lambda b, t: (b, t, 0))
    p_spec = pl.BlockSpec((None, 1, tm, p_all.shape[-1]), lambda b, t: (layer, b, t, 0))
    return pl.pallas_call(
        _a_out_kernel,
        grid=(B, T // tm),
        in_specs=[row(A_WIDTH), row(A_WIDTH), row(A_WIDTH), _resident(ng.shape), row(D), p_spec,
                  _resident(wo.shape), _resident(pg.shape), _resident(pe.shape)],
        out_specs=row(D),
        out_shape=jax.ShapeDtypeStruct((B, T, D), F32),
        compiler_params=_params("parallel", "parallel"),
        name="mlstm_out",
    )(hh, o, z, ng, x, p_all, wo, pg, pe)


def _out_kernel(y_ref, x_ref, p_ref, wo_ref, pg_ref, pe_ref, xo_ref):
    x1 = x_ref[0] + _mm(y_ref[0], wo_ref[...])
    xo_ref[0] = _embed(x1, p_ref[0], pg_ref, pe_ref)


def _out_embed(y, x, p_all, layer, wo, pg, pe, tm):
    B, T, D = x.shape
    row = lambda width: pl.BlockSpec((1, tm, width), lambda b, t: (b, t, 0))
    p_spec = pl.BlockSpec((None, 1, tm, p_all.shape[-1]), lambda b, t: (layer, b, t, 0))
    return pl.pallas_call(
        _out_kernel,
        grid=(B, T // tm),
        in_specs=[row(y.shape[-1]), row(D), p_spec, _resident(wo.shape), _resident(pg.shape), _resident(pe.shape)],
        out_specs=row(D),
        out_shape=jax.ShapeDtypeStruct((B, T, D), F32),
        compiler_params=_params("parallel", "parallel"),
        name="out_embed",
    )(y, x, p_all, wo, pg, pe)


CONV_COLS = 512


def _conv_prompt_kernel(x_ref, p_ref, g_ref, win_ref, cw_ref, wo_ref, pg_ref, pe_ref,
                        xo_ref, st_ref, carry_s):
    E = wo_ref.shape[0]

    @pl.when(pl.program_id(1) == 0)
    def _():
        carry_s[...] = jnp.zeros_like(carry_s)

    x = x_ref[0]
    tm = x.shape[0]
    h = _rms(x, g_ref[...]).astype(BF16)
    row = lax.broadcasted_iota(jnp.int32, (tm, CONV_COLS), 0)
    acc = jnp.zeros(x.shape, F32)
    for c in range(E // CONV_COLS):
        cols = slice(c * CONV_COLS, (c + 1) * CONV_COLS)
        part = lambda i: _mm(h, win_ref[:, i * E + c * CONV_COLS:i * E + (c + 1) * CONV_COLS])
        bg, cg, xb, z = part(0), part(1), part(2), part(3)
        cx = cg * xb
        old = carry_s[0:1, cols]
        new = carry_s[1:2, cols]
        prev1 = jnp.where(row == 0, new, pltpu.roll(cx, 1, 0))
        prev2 = jnp.where(row == 0, old, jnp.where(row == 1, new, pltpu.roll(cx, 2, 0)))
        y = cw_ref[0:1, cols] * prev2 + cw_ref[1:2, cols] * prev1 + cw_ref[2:3, cols] * cx
        acc = acc + _mm((bg * y * _silu(z)).astype(BF16), wo_ref[cols, :])
        carry_s[0:2, cols] = cx[tm - 2:tm, :]
    xo_ref[0] = _embed(x + acc, p_ref[0], pg_ref, pe_ref)
    st_ref[0] = carry_s[0:2, :]


def _conv_prompt(x, p_all, layer, g, win, cw, wo, pg, pe, tm):
    B, T, D = x.shape
    E = wo.shape[0]
    row = lambda width: pl.BlockSpec((1, tm, width), lambda b, t: (b, t, 0))
    p_spec = pl.BlockSpec((None, 1, tm, p_all.shape[-1]), lambda b, t: (layer, b, t, 0))
    return pl.pallas_call(
        _conv_prompt_kernel,
        grid=(B, T // tm),
        in_specs=[row(D), p_spec, _resident(g.shape), _resident(win.shape), _resident(cw.shape),
                  _resident(wo.shape), _resident(pg.shape), _resident(pe.shape)],
        out_specs=[row(D), pl.BlockSpec((1, CONV_W - 1, E), lambda b, t: (b, 0, 0))],
        out_shape=[jax.ShapeDtypeStruct((B, T, D), F32), jax.ShapeDtypeStruct((B, CONV_W - 1, E), F32)],
        scratch_shapes=[pltpu.VMEM((8, E), F32)],
        compiler_params=_params("parallel", "arbitrary"),
        name="conv_prompt",
    )(x, p_all, g, win, cw, wo, pg, pe)


def _conv_sample_kernel(x_ref, p_ref, g_ref, win_ref, cw_ref, wo_ref, pg_ref, pe_ref, s0_ref, s1_ref,
                        xo_ref, cx_ref):
    E = wo_ref.shape[0]
    x = x_ref[0]
    h = _rms(x, g_ref[...]).astype(BF16)
    acc = jnp.zeros(x.shape, F32)
    for c in range(E // CONV_COLS):
        cols = slice(c * CONV_COLS, (c + 1) * CONV_COLS)
        part = lambda i: _mm(h, win_ref[:, i * E + c * CONV_COLS:i * E + (c + 1) * CONV_COLS])
        bg, cg, xb, z = part(0), part(1), part(2), part(3)
        cx = cg * xb
        y = cw_ref[0:1, cols] * s0_ref[:, cols] + cw_ref[1:2, cols] * s1_ref[:, cols] + cw_ref[2:3, cols] * cx
        acc = acc + _mm((bg * y * _silu(z)).astype(BF16), wo_ref[cols, :])
        cx_ref[:, cols] = cx
    xo_ref[0] = _embed(x + acc, p_ref[0], pg_ref, pe_ref)


def _conv_sample(x, p_all, layer, g, win, cw, wo, pg, pe, s0, s1):
    _, M, D = x.shape
    E = wo.shape[0]
    p_spec = pl.BlockSpec((None, 1, M, p_all.shape[-1]), lambda i: (layer, 0, 0, 0))
    return pl.pallas_call(
        _conv_sample_kernel,
        grid=(1,),
        in_specs=[_resident(x.shape), p_spec, _resident(g.shape), _resident(win.shape), _resident(cw.shape),
                  _resident(wo.shape), _resident(pg.shape), _resident(pe.shape),
                  _resident(s0.shape), _resident(s1.shape)],
        out_specs=[_whole(x.shape), _whole((M, E))],
        out_shape=[jax.ShapeDtypeStruct(x.shape, F32), jax.ShapeDtypeStruct((M, E), F32)],
        compiler_params=_params("arbitrary"),
        name="conv_sample",
    )(x, p_all, g, win, cw, wo, pg, pe, s0, s1)


def _rope(x, cos, sin_lo, sin_hi):
    half = ROPE_DIM // 2
    return x * cos + pltpu.roll(x, C_DH - half, 1) * sin_lo + pltpu.roll(x, half, 1) * sin_hi


def _c_in_kernel(x_ref, g_ref, w_ref, cos_ref, slo_ref, shi_ref, *refs, tails, dils):
    n_groups = len(C_GROUPS)
    qkv_refs = refs[:3 * n_groups]
    z_ref = refs[3 * n_groups]
    tail_refs = refs[3 * n_groups + 1:-1]
    rows_s = refs[-1]
    h = _rms(x_ref[0], g_ref[...]).astype(BF16)
    tm = h.shape[0]
    cos, slo, shi = cos_ref[...], slo_ref[...], shi_ref[...]
    tile_end = (pl.program_id(1) + 1) * tm

    def rope_all(u):
        return jnp.concatenate(
            [_rope(u[:, hd * C_DH:(hd + 1) * C_DH], cos, slo, shi) for hd in range(C_HEADS)], axis=1)

    for grp in range(n_groups):
        base = 3 * grp * C_WIDTH
        q = rope_all(_mm(h, w_ref[:, base:base + C_WIDTH])) * (C_DH ** -0.5)
        k = rope_all(_mm(h, w_ref[:, base + C_WIDTH:base + 2 * C_WIDTH]))
        v = _mm(h, w_ref[:, base + 2 * C_WIDTH:base + 3 * C_WIDTH])
        dil = dils[grp]
        for ref, val in zip(qkv_refs[3 * grp:3 * grp + 3], (q, k, v)):
            if dil == 1:
                ref[0, 0] = val.astype(BF16)
            else:
                for hd in range(C_HEADS):
                    rows_s[hd] = val[:, hd * C_DH:(hd + 1) * C_DH]
                for r in range(dil):
                    for hd in range(C_HEADS):
                        picked = rows_s[hd, pl.ds(r, tm // dil, stride=dil), :]
                        ref[0, r, :, hd * C_DH:(hd + 1) * C_DH] = picked.astype(BF16)
        first_row, rows = tails[grp]

        @pl.when(tile_end > first_row)
        def _(k=k, v=v, grp=grp, rows=rows):
            tail_refs[2 * grp][0] = k[tm - rows:, :]
            tail_refs[2 * grp + 1][0] = v[tm - rows:, :]

    zb = 3 * n_groups * C_WIDTH
    z_ref[0] = _mm(h, w_ref[:, zb:zb + C_WIDTH]).astype(BF16)


def _c_in(x, g, w, cos, slo, shi, tm, per_row=False):
    B, T, D = x.shape
    row = lambda width: pl.BlockSpec((1, tm, width), lambda b, t: (b, t, 0))
    table = pl.BlockSpec((tm, C_DH), lambda b, t: (t, 0))
    dils = tuple(1 if per_row else dil for _, dil in C_GROUPS)
    qkv_specs, qkv_shapes = [], []
    for dil in dils:
        qkv_specs += [pl.BlockSpec((1, dil, tm // dil, C_WIDTH), lambda b, t: (b, 0, t, 0))] * 3
        qkv_shapes += [jax.ShapeDtypeStruct((B, dil, T // dil, C_WIDTH), BF16)] * 3
    tails, tail_specs, tail_shapes = [], [], []
    for win, _ in C_GROUPS:
        keep = T if per_row else min(win, T)
        rows = min(tm, keep)
        first_row = T - keep
        tails.append((first_row, rows))

        def tail_index(b, t, first_row=first_row, rows=rows):
            return (b, jnp.maximum(((t + 1) * tm - first_row) // rows - 1, 0), 0)

        tail_specs += [pl.BlockSpec((1, rows, C_WIDTH), tail_index)] * 2
        tail_shapes += [jax.ShapeDtypeStruct((B, keep, C_WIDTH), F32)] * 2
    n_qkv = len(qkv_specs)
    outs = pl.pallas_call(
        functools.partial(_c_in_kernel, tails=tuple(tails), dils=dils),
        grid=(B, T // tm),
        in_specs=[row(D), _resident(g.shape), _resident(w.shape), table, table, table],
        out_specs=qkv_specs + [row(C_WIDTH)] + tail_specs,
        out_shape=qkv_shapes + [jax.ShapeDtypeStruct((B, T, C_WIDTH), BF16)] + tail_shapes,
        scratch_shapes=[pltpu.VMEM((C_HEADS, tm, C_DH), F32)],
        compiler_params=_params("parallel", "arbitrary"),
        name="attn_in",
    )(x, g, w, cos, slo, shi)
    return outs[:n_qkv], outs[n_qkv], outs[n_qkv + 1:]


ATTN_CHUNK = 512


def _attn_chunk_kernel(q_ref, kp_ref, kc_ref, vp_ref, vc_ref, acc_ref, st_ref, k_s, v_s, *, span):
    chunk = pl.program_id(2)
    n_blocks = q_ref.shape[2] // C_BLOCK
    k_s[0:C_BLOCK, :] = kp_ref[0, 0]
    k_s[C_BLOCK:, :] = kc_ref[0, 0]
    v_s[0:C_BLOCK, :] = vp_ref[0, 0]
    v_s[C_BLOCK:, :] = vc_ref[0, 0]
    qi = lax.broadcasted_iota(jnp.int32, (C_BLOCK, 2 * C_BLOCK), 0)
    kj = lax.broadcasted_iota(jnp.int32, (C_BLOCK, 2 * C_BLOCK), 1)
    dist = qi + C_BLOCK - kj
    band = (dist >= 0) & (dist <= span)
    lane = lax.broadcasted_iota(jnp.int32, (C_BLOCK, LANES), 1)
    ones = jnp.ones((2 * C_BLOCK, LANES), BF16)
    head_cols = [slice(hd * C_DH, (hd + 1) * C_DH) for hd in range(C_HEADS)]

    def block(i, carry):
        lo = pl.multiple_of(i * C_BLOCK, C_BLOCK)
        first_key = jnp.where((chunk == 0) & (i == 0), C_BLOCK, 0)
        bias = jnp.where(band & (kj >= first_key), 0.0, -1e30)

        def scores(hd):
            return lax.dot_general(q_ref[0, 0, pl.ds(lo, C_BLOCK), head_cols[hd]],
                                   k_s[pl.ds(lo, 2 * C_BLOCK), head_cols[hd]], NT_DIMS,
                                   preferred_element_type=F32)

        stats = jnp.zeros((C_BLOCK, LANES), F32)
        s_next = scores(0)
        for hd in range(C_HEADS):
            s = s_next + bias
            if hd + 1 < C_HEADS:
                s_next = scores(hd + 1)
            mx = jnp.max(s, axis=-1, keepdims=True)
            p = jnp.exp(s - mx).astype(BF16)
            both = _mm(p, jnp.concatenate([v_s[pl.ds(lo, 2 * C_BLOCK), head_cols[hd]], ones], axis=1))
            acc_ref[0, 0, pl.ds(lo, C_BLOCK), head_cols[hd]] = both[:, :C_DH].astype(BF16)
            stats = jnp.where(lane == hd, mx, jnp.where(lane == C_HEADS + hd, both[:, C_DH:], stats))
        st_ref[0, 0, pl.ds(lo, C_BLOCK), :] = stats
        return carry

    lax.fori_loop(0, n_blocks, block, 0)


def _attn_prompt_group(q, k, v, span):
    B, dil, n, _ = q.shape
    qc = min(n, ATTN_CHUNK)
    per_chunk = qc // C_BLOCK
    cur = lambda rows, width: pl.BlockSpec((1, 1, rows, width), lambda b, r, c: (b, r, c, 0))
    prev = pl.BlockSpec((1, 1, C_BLOCK, C_WIDTH), lambda b, r, c: (b, r, jnp.maximum(c * per_chunk - 1, 0), 0))
    return pl.pallas_call(
        functools.partial(_attn_chunk_kernel, span=span),
        grid=(B, dil, n // qc),
        in_specs=[cur(qc, C_WIDTH), prev, cur(qc, C_WIDTH), prev, cur(qc, C_WIDTH)],
        out_specs=[cur(qc, C_WIDTH), cur(qc, LANES)],
        out_shape=[jax.ShapeDtypeStruct((B, dil, n, C_WIDTH), BF16),
                   jax.ShapeDtypeStruct((B, dil, n, LANES), F32)],
        scratch_shapes=[pltpu.VMEM((C_BLOCK + qc, C_WIDTH), BF16), pltpu.VMEM((C_BLOCK + qc, C_WIDTH), BF16)],
        compiler_params=_params("parallel", "parallel", "arbitrary"),
        name="attn_chunks_d%d" % dil,
    )(q, k, k, v, v)


def _merge_out_kernel(a0_ref, a1_ref, a2_ref, s0_ref, s1_ref, s2_ref, z_ref, x_ref, p_ref,
                      wo_ref, pg_ref, pe_ref, xo_ref, acc_s, st_s):
    tm = x_ref.shape[1]
    for grp, (a_ref, s_ref) in enumerate(zip((a0_ref, a1_ref, a2_ref), (s0_ref, s1_ref, s2_ref))):
        dil = a_ref.shape[1]
        for r in range(dil):
            rows = pl.ds(r, tm // dil, stride=dil) if dil > 1 else slice(None)
            acc = a_ref[0, r].astype(F32)
            for hd in range(C_HEADS):
                acc_s[grp * C_HEADS + hd, rows, :] = acc[:, hd * C_DH:(hd + 1) * C_DH]
            st_s[grp, rows, :] = s_ref[0, r]
    mx = [st_s[grp, :, 0:C_HEADS] for grp in range(3)]
    ls = [st_s[grp, :, C_HEADS:2 * C_HEADS] for grp in range(3)]
    top = jnp.maximum(jnp.maximum(mx[0], mx[1]), mx[2])
    es = [jnp.exp(m - top) for m in mx]
    den = es[0] * ls[0] + es[1] * ls[1] + es[2] * ls[2]
    coef = [e / den for e in es]
    x1 = x_ref[0]
    for pair in range(C_HEADS // 2):
        ys = []
        for hd in (2 * pair, 2 * pair + 1):
            hs = slice(hd * C_DH, (hd + 1) * C_DH)
            o = coef[0][:, hd:hd + 1] * acc_s[hd]
            for grp in (1, 2):
                o = o + coef[grp][:, hd:hd + 1] * acc_s[grp * C_HEADS + hd]
            ys.append((o * _silu(z_ref[0, :, hs].astype(F32))).astype(BF16))
        x1 = x1 + _mm(jnp.concatenate(ys, axis=1), wo_ref[2 * pair * C_DH:(2 * pair + 2) * C_DH, :])
    xo_ref[0] = _embed(x1, p_ref[0], pg_ref, pe_ref)


def _merge_out(accs, stats, z, x, p_all, layer, wo, pg, pe, tm):
    B, T, D = x.shape
    row = lambda width: pl.BlockSpec((1, tm, width), lambda b, t: (b, t, 0))
    classes = lambda a: pl.BlockSpec((1, a.shape[1], tm // a.shape[1], a.shape[3]), lambda b, t: (b, 0, t, 0))
    p_spec = pl.BlockSpec((None, 1, tm, p_all.shape[-1]), lambda b, t: (layer, b, t, 0))
    return pl.pallas_call(
        _merge_out_kernel,
        grid=(B, T // tm),
        in_specs=[classes(a) for a in accs] + [classes(s) for s in stats] + [row(C_WIDTH), row(D), p_spec,
                  _resident(wo.shape), _resident(pg.shape), _resident(pe.shape)],
        out_specs=row(D),
        out_shape=jax.ShapeDtypeStruct((B, T, D), F32),
        scratch_shapes=[pltpu.VMEM((3 * C_HEADS, tm, C_DH), F32), pltpu.VMEM((3, tm, LANES), F32)],
        compiler_params=_params("parallel", "parallel"),
        name="attn_merge_out",
    )(*accs, *stats, z, x, p_all, wo, pg, pe)


def _attn_sample_kernel(*refs):
    n_groups = len(C_GROUPS)
    z_ref, y_ref = refs[5 * n_groups], refs[5 * n_groups + 1]
    parts = []
    for grp in range(n_groups):
        q_ref, kn_ref, vn_ref, kc_ref, vc_ref = refs[5 * grp:5 * grp + 5]
        q = q_ref[0].astype(F32)
        s_old = jnp.sum(kc_ref[0] * q[None], axis=-1, keepdims=True)
        s_new = jnp.sum(kn_ref[0] * q, axis=-1, keepdims=True)
        mx = jnp.maximum(jnp.max(s_old, axis=0), s_new)
        p_old = jnp.exp(s_old - mx[None])
        p_new = jnp.exp(s_new - mx)
        l = jnp.sum(p_old, axis=0) + p_new
        acc = jnp.sum(p_old * vc_ref[0], axis=0) + p_new * vn_ref[0]
        parts.append((acc, mx, l))
    top = jnp.maximum(jnp.maximum(parts[0][1], parts[1][1]), parts[2][1])
    es = [jnp.exp(m - top) for _, m, _ in parts]
    num = es[0] * parts[0][0] + es[1] * parts[1][0] + es[2] * parts[2][0]
    den = es[0] * parts[0][2] + es[1] * parts[1][2] + es[2] * parts[2][2]
    y_ref[0] = (num / den * _silu(z_ref[0].astype(F32))).astype(BF16)


def _attn_sample(qkv, tails, caches, z):
    Bs = z.shape[1]
    heads = lambda a: a.reshape(Bs, C_HEADS, C_DH)
    row = pl.BlockSpec((1, C_HEADS, C_DH), lambda b: (b, 0, 0))
    args, specs = [], []
    for grp, (win, dil) in enumerate(C_GROUPS):
        kc, vc = caches[grp]
        n_buf = kc.shape[1]
        assert n_buf == win and win % dil == 0, "sample window must be fully cached"
        span = win // dil
        view = lambda a: a.reshape(Bs, span, dil, C_HEADS, C_DH)
        cache_spec = pl.BlockSpec((1, span, None, C_HEADS, C_DH), lambda b: (b, 0, 0, 0, 0))
        args += [heads(qkv[3 * grp]), heads(tails[2 * grp]), heads(tails[2 * grp + 1]), view(kc), view(vc)]
        specs += [row, row, row, cache_spec, cache_spec]
    return pl.pallas_call(
        _attn_sample_kernel,
        grid=(Bs,),
        in_specs=specs + [row],
        out_specs=row,
        out_shape=jax.ShapeDtypeStruct((Bs, C_HEADS, C_DH), BF16),
        compiler_params=_params("parallel"),
        name="attn_sample",
    )(*args, heads(z))


def _pool_tail(x, acc, p, pg_ref, pe_ref, fg_ref):
    return _rms(_embed(x + acc, p, pg_ref, pe_ref), fg_ref[...])


def _pool_prompt_kernel(x_ref, p_ref, g_ref, win_ref, wgrp_ref, sc_ref, wo_ref, pg_ref, pe_ref, fg_ref,
                        xo_ref, st_ref, pad_s):
    E = wo_ref.shape[0]
    G = E // len(D_WINDOWS)
    t = pl.program_id(1)

    @pl.when(t == 0)
    def _():
        pad_s[0:POOL_PAD, :] = jnp.zeros((POOL_PAD, E), F32)

    x = x_ref[0]
    tm = x.shape[0]
    h = _rms(x, g_ref[...]).astype(BF16)
    pos = t * tm + lax.broadcasted_iota(jnp.int32, (tm, 1), 0)
    acc = jnp.zeros(x.shape, F32)
    for grp, w in enumerate(D_WINDOWS):
        cols = slice(grp * G, (grp + 1) * G)
        xp = _mm(h, win_ref[:, cols])
        z = _mm(h, win_ref[:, E + grp * G:E + (grp + 1) * G])
        pad_s[POOL_PAD:, cols] = xp
        wsum = pad_s[:, cols]
        shift = 1
        while shift < w:
            wsum = wsum + pltpu.roll(wsum, shift, 0)
            shift *= 2
        cnt = jnp.minimum(w, pos + 1).astype(F32)
        r = wsum[POOL_PAD:, :] / cnt - xp
        y = _mm(r.astype(BF16), wgrp_ref[grp]) * sc_ref[:, cols]
        acc = acc + _mm((y * _silu(z)).astype(BF16), wo_ref[cols, :])
        pad_s[0:POOL_PAD, cols] = xp[tm - POOL_PAD:, :]
    xo_ref[0] = _pool_tail(x, acc, p_ref[0], pg_ref, pe_ref, fg_ref)
    st_ref[0] = pad_s[0:POOL_PAD, :]


def _pool_prompt(x, p_all, layer, g, win, wgrp, sc, wo, pg, pe, fg, tm):
    B, T, D = x.shape
    E = wo.shape[0]
    row = lambda width: pl.BlockSpec((1, tm, width), lambda b, t: (b, t, 0))
    p_spec = pl.BlockSpec((None, 1, tm, p_all.shape[-1]), lambda b, t: (layer, b, t, 0))
    return pl.pallas_call(
        _pool_prompt_kernel,
        grid=(B, T // tm),
        in_specs=[row(D), p_spec] + [_resident(a.shape) for a in (g, win, wgrp, sc, wo, pg, pe, fg)],
        out_specs=[row(D), pl.BlockSpec((1, POOL_PAD, E), lambda b, t: (b, 0, 0))],
        out_shape=[jax.ShapeDtypeStruct((B, T, D), F32), jax.ShapeDtypeStruct((B, POOL_PAD, E), F32)],
        scratch_shapes=[pltpu.VMEM((POOL_PAD + tm, E), F32)],
        compiler_params=_params("parallel", "arbitrary"),
        name="pool_prompt",
    )(x, p_all, g, win, wgrp, sc, wo, pg, pe, fg)


def _pool_sample_kernel(x_ref, p_ref, g_ref, win_ref, wgrp_ref, sc_ref, wo_ref, pg_ref, pe_ref, fg_ref, st_ref,
                        xo_ref, xp_ref, *, pos):
    E = wo_ref.shape[0]
    G = E // len(D_WINDOWS)
    x = x_ref[0]
    h = _rms(x, g_ref[...]).astype(BF16)
    acc = jnp.zeros(x.shape, F32)
    for grp, w in enumerate(D_WINDOWS):
        cols = slice(grp * G, (grp + 1) * G)
        xp = _mm(h, win_ref[:, cols])
        z = _mm(h, win_ref[:, E + grp * G:E + (grp + 1) * G])
        wsum = xp
        for back in range(1, w):
            lo = (POOL_STATE - back) * E + grp * G
            wsum = wsum + st_ref[:, lo:lo + G]
        r = wsum / float(min(w, pos + 1)) - xp
        y = _mm(r.astype(BF16), wgrp_ref[grp]) * sc_ref[:, cols]
        acc = acc + _mm((y * _silu(z)).astype(BF16), wo_ref[cols, :])
        xp_ref[:, cols] = xp
    xo_ref[0] = _pool_tail(x, acc, p_ref[0], pg_ref, pe_ref, fg_ref)


def _pool_sample(x, p_all, layer, g, win, wgrp, sc, wo, pg, pe, fg, st, pos):
    _, M, D = x.shape
    E = wo.shape[0]
    p_spec = pl.BlockSpec((None, 1, M, p_all.shape[-1]), lambda i: (layer, 0, 0, 0))
    return pl.pallas_call(
        functools.partial(_pool_sample_kernel, pos=pos),
        grid=(1,),
        in_specs=[_resident(x.shape), p_spec] + [_resident(a.shape) for a in (g, win, wgrp, sc, wo, pg, pe, fg, st)],
        out_specs=[_whole(x.shape), _whole((M, E))],
        out_shape=[jax.ShapeDtypeStruct(x.shape, F32), jax.ShapeDtypeStruct((M, E), F32)],
        compiler_params=_params("arbitrary"),
        name="pool_sample",
    )(x, p_all, g, win, wgrp, sc, wo, pg, pe, fg, st)


def _rope_tables(pos):
    half = ROPE_DIM // 2
    inv = ROPE_THETA ** (-jnp.arange(half, dtype=F32) / half)
    ang = pos.astype(F32)[:, None] * inv[None, :]
    cos, sin = jnp.cos(ang), jnp.sin(ang)
    rest = C_DH - ROPE_DIM
    n = pos.shape[0]
    cos_t = jnp.concatenate([cos, cos, jnp.ones((n, rest), F32)], axis=1)
    sin_lo = jnp.concatenate([-sin, jnp.zeros((n, half + rest), F32)], axis=1)
    sin_hi = jnp.concatenate([jnp.zeros((n, half), F32), sin, jnp.zeros((n, rest), F32)], axis=1)
    return cos_t, sin_lo, sin_hi


def _row(a):
    return a.reshape(1, -1).astype(F32)


def kernel(x_prompt, x_sample, state_mlstm_C, state_mlstm_n, state_mlstm_m, state_conv, cache_k_w128, cache_v_w128, cache_k_w512, cache_v_w512, cache_k_w2048, cache_v_w2048, state_pool, p_prompt, p_sample, norm_g, pe_w, pg_w, final_g, a_w_in, a_b_if, a_norm_g, a_w_out, b_w_in, b_conv_w, b_w_out, c_w_in, c_w_out, d_w_in, d_w_grp, d_scale, d_w_out):
    B, T, D = x_prompt.shape
    Bs, Ts, _ = x_sample.shape
    assert Ts == 1 and norm_g.shape[0] == 4 and a_w_in.shape[0] == 1
    assert T % 512 == 0 and all(T % (dil * C_BLOCK) == 0 for _, dil in C_GROUPS)
    assert Bs % 8 == 0

    bf = lambda a: a.astype(BF16)
    qk = A_HEADS * A_DK
    n_main = 2 * qk + 3 * A_WIDTH
    a_w = bf(a_w_in[0, :, :n_main])
    a_wg = bf(jnp.pad(a_w_in[0, :, n_main:], ((0, 0), (0, LANES - 2 * A_HEADS))))
    a_bif = jnp.pad(a_b_if[0], (0, LANES - 2 * A_HEADS)).reshape(1, LANES)
    a_ng = _row(a_norm_g[0])
    a_wo, b_wi, b_wo, c_wi, c_wo, d_wi, d_wg, d_wo = (
        bf(a_w_out[0]), bf(b_w_in[0]), bf(b_w_out[0]), bf(c_w_in[0]), bf(c_w_out[0]),
        bf(d_w_in[0]), bf(d_w_grp[0]), bf(d_w_out[0]))
    pg, pe = bf(pg_w), bf(pe_w)
    gs = [_row(norm_g[i]) for i in range(4)]
    fg = _row(final_g)
    b_cw = b_conv_w[0]
    d_sc = _row(d_scale[0])
    caches = [(cache_k_w128[0], cache_v_w128[0]), (cache_k_w512[0], cache_v_w512[0]),
              (cache_k_w2048[0], cache_v_w2048[0])]

    x = x_prompt
    q, k, v, o, z, gates, kt = _a_in(x, gs[0], a_w, a_wg, a_bif, 512, transposed_k=True)
    hh, c_p, n_p, m_p = _mlstm_prompt(q, k, kt, v, gates)
    x = _a_out(hh, o, z, a_ng, x, p_prompt, 0, a_wo, pg[0], pe[0], 512)
    x, conv_p = _conv_prompt(x, p_prompt, 1, gs[1], b_wi, b_cw, b_wo, pg[1], pe[1], 512)
    qkv, z, tails_p = _c_in(x, gs[2], c_wi, *_rope_tables(jnp.arange(T)), 256)
    accs, stats = [], []
    for grp, (win, dil) in enumerate(C_GROUPS):
        acc, st = _attn_prompt_group(qkv[3 * grp], qkv[3 * grp + 1], qkv[3 * grp + 2], win // dil)
        accs.append(acc)
        stats.append(st)
    x = _merge_out(accs, stats, z, x, p_prompt, 2, c_wo, pg[2], pe[2], 512)
    y_prompt, pool_p = _pool_prompt(x, p_prompt, 3, gs[3], d_wi, d_wg, d_sc, d_wo, pg[3], pe[3], fg, 512)

    xs = x_sample.reshape(1, Bs, D)
    ps = p_sample.reshape(p_sample.shape[0], 1, Bs, p_sample.shape[-1])
    q, k, v, o, z, gates = _a_in(xs, gs[0], a_w, a_wg, a_bif, Bs)
    per_seq = lambda a: a.reshape(Bs, 1, a.shape[-1])
    hh, c_s, n_s, m_s = _mlstm_sample(per_seq(q), per_seq(k), per_seq(v), per_seq(gates),
                                      state_mlstm_C[0], state_mlstm_n[0], state_mlstm_m[0].reshape(Bs, 1, A_HEADS))
    xs = _a_out(hh.reshape(1, Bs, A_WIDTH), o, z, a_ng, xs, ps, 0, a_wo, pg[0], pe[0], Bs)
    xs, cx = _conv_sample(xs, ps, 1, gs[1], b_wi, b_cw, b_wo, pg[1], pe[1],
                          state_conv[0, :, 0, :], state_conv[0, :, 1, :])
    conv_s = jnp.stack([state_conv[0, :, 1, :], cx], axis=1)
    tables = [jnp.broadcast_to(t, (Bs, C_DH)) for t in _rope_tables(PAST_LEN + jnp.arange(1))]
    qkv, z, tails_s = _c_in(xs, gs[2], c_wi, *tables, Bs, per_row=True)
    y = _attn_sample(qkv, tails_s, caches, z)
    xs = _out_embed(y.reshape(1, Bs, C_WIDTH), xs, ps, 2, c_wo, pg[2], pe[2], Bs)
    pool_flat = state_pool[0].reshape(Bs, POOL_STATE * state_pool.shape[-1])
    ys, xp = _pool_sample(xs, ps, 3, gs[3], d_wi, d_wg, d_sc, d_wo, pg[3], pe[3], fg, pool_flat, PAST_LEN)
    pool_s = jnp.concatenate([state_pool[0, :, 1:, :], xp[:, None, :]], axis=1)

    heads = lambda a, lead: a.reshape(1, lead, a.shape[1], C_HEADS, C_DH)
    kv_out = []
    for grp in range(len(C_GROUPS)):
        for j in range(2):
            kv_out += [heads(tails_p[2 * grp + j], B), tails_s[2 * grp + j].reshape(1, Bs, 1, C_HEADS, C_DH)]
    return (y_prompt, ys.reshape(Bs, 1, D),
            c_p[None], c_s[None], n_p[None], n_s[None], m_p[None, :, 0, :A_HEADS], m_s.reshape(1, Bs, A_HEADS),
            conv_p[None], conv_s[None],
            *kv_out,
            pool_p[None, :, 1:, :], pool_s[None])
```

```python
import functools

import jax
import jax.numpy as jnp
from jax import lax
from jax.experimental import pallas as pl
from jax.experimental.pallas import tpu as pltpu

F32 = jnp.float32
BF16 = jnp.bfloat16

EPS = 1e-6
PAST_LEN = 8192

A_HEADS = 8
A_DK = 128
A_DV = 256
A_WIDTH = A_HEADS * A_DV
A_CHUNK = 128
A_CHUNKS_PER_STEP = 2

CONV_W = 3

C_HEADS = 8
C_DH = 128
C_WIDTH = C_HEADS * C_DH
C_GROUPS = ((128, 1), (512, 4), (2048, 16))
C_BLOCK = 128
ROPE_DIM = C_DH // 4
ROPE_THETA = 500000.0
LOG2_E = 1.4426950408889634
Q_SCALE = C_DH ** -0.5 * LOG2_E

D_WINDOWS = (2, 4, 8, 16)
POOL_STATE = max(D_WINDOWS) - 1
POOL_PAD = POOL_STATE + 1

LANES = 128
VMEM_LIMIT_BYTES = 56 * 2 ** 20

NT_DIMS = (((1,), (1,)), ((), ()))
TN_DIMS = (((0,), (0,)), ((), ()))


def _params(*semantics):
    return pltpu.CompilerParams(dimension_semantics=semantics, vmem_limit_bytes=VMEM_LIMIT_BYTES)


def _resident(shape):
    zeros = (0,) * len(shape)
    return pl.BlockSpec(shape, lambda *_: zeros, pipeline_mode=pl.Buffered(1))


def _whole(shape):
    zeros = (0,) * len(shape)
    return pl.BlockSpec(shape, lambda *_: zeros)


def _mm(a, b):
    return jnp.dot(a, b, preferred_element_type=F32)


def _rms(x, g):
    return x * lax.rsqrt(jnp.mean(x * x, axis=-1, keepdims=True) + EPS) * g


def _sigmoid(x):
    return 0.5 * jnp.tanh(0.5 * x) + 0.5


def _silu(x):
    return x * _sigmoid(x)


def _log_sigmoid(x):
    return jnp.minimum(x, 0.0) - jnp.log(1.0 + jnp.exp(-jnp.abs(x)))


def _embed(x1, p, pg_ref, pe_ref):
    gate = _sigmoid(_mm(x1.astype(BF16), pg_ref[...]))
    return x1 + gate * _mm(p.astype(BF16), pe_ref[...])


def _a_in_kernel(x_ref, g_ref, w_ref, wg_ref, bif_ref,
                 q_ref, k_ref, v_ref, o_ref, z_ref, gate_ref, *maybe_kt_ref):
    h = _rms(x_ref[0], g_ref[...]).astype(BF16)
    qk = A_HEADS * A_DK

    def proj(lo, width):
        return _mm(h, w_ref[:, lo:lo + width])

    q_ref[0] = proj(0, qk).astype(BF16)
    k = proj(qk, qk) * (A_DK ** -0.5)
    k_ref[0] = k.astype(BF16)
    for kt_ref in maybe_kt_ref:
        kt_ref[0] = k.T.astype(BF16)
    v_ref[0] = proj(2 * qk, A_WIDTH).astype(BF16)
    o_ref[0] = proj(2 * qk + A_WIDTH, A_WIDTH).astype(BF16)
    z_ref[0] = proj(2 * qk + 2 * A_WIDTH, A_WIDTH).astype(BF16)
    gates = _mm(h, wg_ref[...]) + bif_ref[...]
    lane = lax.broadcasted_iota(jnp.int32, gates.shape, 1)
    gate_ref[0] = jnp.where(lane < A_HEADS, gates, _log_sigmoid(gates))


def _a_in(x, g, w, wg, bif, tm, transposed_k=False):
    B, T, D = x.shape
    qk = A_HEADS * A_DK
    row = lambda width: pl.BlockSpec((1, tm, width), lambda b, t: (b, t, 0))
    out_specs = [row(qk), row(qk), row(A_WIDTH), row(A_WIDTH), row(A_WIDTH), row(LANES)]
    out_shape = [jax.ShapeDtypeStruct((B, T, qk), BF16), jax.ShapeDtypeStruct((B, T, qk), BF16),
                 jax.ShapeDtypeStruct((B, T, A_WIDTH), BF16), jax.ShapeDtypeStruct((B, T, A_WIDTH), BF16),
                 jax.ShapeDtypeStruct((B, T, A_WIDTH), BF16), jax.ShapeDtypeStruct((B, T, LANES), F32)]
    if transposed_k:
        out_specs.append(pl.BlockSpec((1, qk, tm), lambda b, t: (b, 0, t)))
        out_shape.append(jax.ShapeDtypeStruct((B, qk, T), BF16))
    return pl.pallas_call(
        _a_in_kernel,
        grid=(B, T // tm),
        in_specs=[row(D), _resident(g.shape), _resident(w.shape), _resident(wg.shape), _resident(bif.shape)],
        out_specs=out_specs,
        out_shape=out_shape,
        compiler_params=_params("parallel", "parallel"),
        name="mlstm_in",
    )(x, g, w, wg, bif)


def _mlstm_chunk_kernel(q_ref, k_ref, kt_ref, v_ref, gate_ref,
                        hh_ref, c_out, n_out, m_out, m_s, *state_s):
    chunk = pl.program_id(1)
    ct_s, nb_s = state_s[:A_HEADS], state_s[A_HEADS:]

    @pl.when(chunk == 0)
    def _():
        for ref in state_s:
            ref[...] = jnp.zeros_like(ref)
        m_s[...] = jnp.zeros_like(m_s)

    L = A_CHUNK
    row = lax.broadcasted_iota(jnp.int32, (L, LANES), 0)
    ti = lax.broadcasted_iota(jnp.int32, (L, L), 0)
    si = lax.broadcasted_iota(jnp.int32, (L, L), 1)
    causal = ti >= si
    ones = jnp.ones((L, LANES), BF16)
    heads = range(A_HEADS)
    ks = [slice(h * A_DK, (h + 1) * A_DK) for h in heads]
    vs = [slice(h * A_DV, (h + 1) * A_DV) for h in heads]

    def prefix(x, op, identity):
        step = 1
        while step < L:
            x = op(x, jnp.where(row >= step, pltpu.roll(x, step, 0), identity))
            step *= 2
        return x

    def one_chunk(rows):
        gates = gate_ref[0, rows, :]
        b = pltpu.roll(prefix(gates, jnp.add, 0.0), LANES - A_HEADS, 1)
        c = gates - b
        m_row = m_s[0:1, :]
        top = jnp.maximum(m_row, prefix(c, jnp.maximum, -jnp.inf))
        top_last = top[L - 1:L, :]
        g_all = jnp.exp(m_row - top)
        floor_all = jnp.exp(-(b + top))
        w_all = jnp.exp(c - top_last)
        decay_row = jnp.exp(m_row - top_last)
        m_s[...] = jnp.broadcast_to(b[L - 1:L, :] + top_last, m_s.shape)
        c_t = c.T

        s_raw = [_mm(q_ref[0, rows, ks[h]], kt_ref[0, ks[h], rows]) for h in heads]
        upd = []
        for h in heads:
            w_b = jnp.broadcast_to(w_all[:, h:h + 1], (L, LANES))
            wv = jnp.concatenate([w_b, w_b], axis=1) * v_ref[0, rows, vs[h]].astype(F32)
            upd.append(_mm(kt_ref[0, ks[h], rows], jnp.concatenate([wv.astype(BF16), w_b.astype(BF16)], axis=1)))

        for h in heads:
            ct = ct_s[h][...]
            nb = nb_s[h][...]
            dmat = jnp.exp(jnp.where(causal, -top[:, h:h + 1] + c_t[h:h + 1, :], -jnp.inf))
            s = s_raw[h] * dmat
            gq = g_all[:, h:h + 1] * q_ref[0, rows, ks[h]].astype(F32)
            lhs = jnp.concatenate([s.astype(BF16), gq.astype(BF16)], axis=1)
            num = _mm(lhs, jnp.concatenate([v_ref[0, rows, vs[h]], ct.astype(BF16)], axis=0))
            den = _mm(lhs, jnp.concatenate([ones, nb.astype(BF16)], axis=0))
            scale = 1.0 / jnp.maximum(jnp.abs(den), floor_all[:, h:h + 1])
            hh_ref[0, rows, vs[h]] = (num * jnp.concatenate([scale, scale], axis=1)).astype(BF16)
            decay = decay_row[:, h:h + 1]
            ct_s[h][...] = decay * ct + upd[h][:, :A_DV]
            nb_s[h][...] = decay * nb + upd[h][:, A_DV:]

    for sub in range(q_ref.shape[1] // L):
        one_chunk(slice(sub * L, (sub + 1) * L))

    @pl.when(chunk == pl.num_programs(1) - 1)
    def _():
        for h in range(A_HEADS):
            c_out[0, h] = ct_s[h][...].T
            n_out[0, h:h + 1, :] = nb_s[h][...].T[0:1, :]
        m_out[0] = m_s[...]


def _mlstm_prompt(q, k, kt, v, gates):
    B, T, _ = q.shape
    L = A_CHUNK * A_CHUNKS_PER_STEP
    qk = A_HEADS * A_DK
    row = lambda width: pl.BlockSpec((1, L, width), lambda b, c: (b, c, 0))
    state = lambda *shape: pl.BlockSpec((1,) + shape, lambda b, c: (b,) + (0,) * len(shape))
    return pl.pallas_call(
        _mlstm_chunk_kernel,
        grid=(B, T // L),
        in_specs=[row(qk), row(qk), pl.BlockSpec((1, qk, L), lambda b, c: (b, 0, c)), row(A_WIDTH), row(LANES)],
        out_specs=[row(A_WIDTH), state(A_HEADS, A_DV, A_DK), state(A_HEADS, A_DK), state(A_HEADS, LANES)],
        out_shape=[jax.ShapeDtypeStruct((B, T, A_WIDTH), BF16),
                   jax.ShapeDtypeStruct((B, A_HEADS, A_DV, A_DK), F32),
                   jax.ShapeDtypeStruct((B, A_HEADS, A_DK), F32),
                   jax.ShapeDtypeStruct((B, A_HEADS, LANES), F32)],
        scratch_shapes=[pltpu.VMEM((A_HEADS, LANES), F32)] + [pltpu.VMEM((A_DK, A_DV), F32)] * A_HEADS
        + [pltpu.VMEM((A_DK, LANES), F32)] * A_HEADS,
        compiler_params=_params("parallel", "arbitrary"),
        name="mlstm_chunks",
    )(q, k, kt, v, gates)


def _mlstm_step_kernel(q_ref, k_ref, v_ref, gate_ref, c_ref, n_ref, m_ref,
                       hh_ref, c_out, n_out, m_out):
    gates = gate_ref[0]
    row = lax.broadcasted_iota(jnp.int32, (LANES, A_DV), 0)
    for h in range(A_HEADS):
        ks = slice(h * A_DK, (h + 1) * A_DK)
        vs = slice(h * A_DV, (h + 1) * A_DV)
        ig = gates[:, h:h + 1]
        lf = gates[:, A_HEADS + h:A_HEADS + h + 1]
        m_prev = m_ref[0, :, h:h + 1]
        qh = q_ref[0, :, ks].astype(F32)
        kh = k_ref[0, :, ks].astype(F32)
        vh = v_ref[0, :, vs].astype(F32)
        c_h = c_ref[0, h]
        n_h = n_ref[0, h:h + 1, :]

        inter = lf + m_prev
        m_t = jnp.maximum(inter, ig)
        dm = jnp.exp(ig - m_t)
        g = jnp.exp(inter - m_t)
        s = jnp.sum(qh * kh, axis=-1, keepdims=True) * dm
        q8 = jnp.broadcast_to(qh, (8, A_DK)).astype(BF16)
        cq = lax.dot_general(q8, c_h.astype(BF16), NT_DIMS, preferred_element_type=F32)[0:1, :]
        num = s * vh + g * cq
        den = s + g * jnp.sum(qh * n_h, axis=-1, keepdims=True)
        hh_ref[0, :, vs] = (num / jnp.maximum(jnp.abs(den), jnp.exp(-m_t))).astype(BF16)

        v_pad = jnp.where(row == 0, jnp.broadcast_to(dm * vh, (LANES, A_DV)), 0.0).astype(BF16)
        k_pad = jnp.broadcast_to(kh, (LANES, A_DK)).astype(BF16)
        c_out[0, h] = g * c_h + lax.dot_general(v_pad, k_pad, TN_DIMS, preferred_element_type=F32)
        n_out[0, h:h + 1, :] = g * n_h + dm * kh
        m_out[0, :, h:h + 1] = m_t


def _mlstm_sample(q, k, v, gates, c0, n0, m0):
    Bs = q.shape[0]
    qk = A_HEADS * A_DK
    row = lambda width: pl.BlockSpec((1, 1, width), lambda b: (b, 0, 0))
    c_spec = pl.BlockSpec((1, A_HEADS, A_DV, A_DK), lambda b: (b, 0, 0, 0))
    n_spec = pl.BlockSpec((1, A_HEADS, A_DK), lambda b: (b, 0, 0))
    m_spec = pl.BlockSpec((1, 1, A_HEADS), lambda b: (b, 0, 0))
    return pl.pallas_call(
        _mlstm_step_kernel,
        grid=(Bs,),
        in_specs=[row(qk), row(qk), row(A_WIDTH), row(LANES), c_spec, n_spec, m_spec],
        out_specs=[row(A_WIDTH), c_spec, n_spec, m_spec],
        out_shape=[jax.ShapeDtypeStruct((Bs, 1, A_WIDTH), BF16),
                   jax.ShapeDtypeStruct(c0.shape, F32), jax.ShapeDtypeStruct(n0.shape, F32),
                   jax.ShapeDtypeStruct(m0.shape, F32)],
        compiler_params=_params("parallel"),
        name="mlstm_step",
    )(q, k, v, gates, c0, n0, m0)


def _a_out_kernel(hh_ref, o_ref, z_ref, ng_ref, x_ref, p_ref, wo_ref, pg_ref, pe_ref, xo_ref):
    x1 = x_ref[0]
    for h in range(A_HEADS):
        vs = slice(h * A_DV, (h + 1) * A_DV)
        hh = hh_ref[0, :, vs].astype(F32)
        hn = hh * lax.rsqrt(jnp.mean(hh * hh, axis=-1, keepdims=True) + EPS) * ng_ref[:, vs]
        y = hn * _sigmoid(o_ref[0, :, vs].astype(F32)) * _silu(z_ref[0, :, vs].astype(F32))
        x1 = x1 + _mm(y.astype(BF16), wo_ref[vs, :])
    xo_ref[0] = _embed(x1, p_ref[0], pg_ref, pe_ref)


def _a_out(hh, o, z, ng, x, p_all, layer, wo, pg, pe, tm):
    B, T, D = x.shape
    row = lambda width: pl.BlockSpec((1, tm, width), lambda b, t: (b, t, 0))
    p_spec = pl.BlockSpec((None, 1, tm, p_all.shape[-1]), lambda b, t: (layer, b, t, 0))
    return pl.pallas_call(
        _a_out_kernel,
        grid=(B, T // tm),
        in_specs=[row(A_WIDTH), row(A_WIDTH), row(A_WIDTH), _resident(ng.shape), row(D), p_spec,
                  _resident(wo.shape), _resident(pg.shape), _resident(pe.shape)],
        out_specs=row(D),
        out_shape=jax.ShapeDtypeStruct((B, T, D), F32),
        compiler_params=_params("parallel", "parallel"),
        name="mlstm_out",
    )(hh, o, z, ng, x, p_all, wo, pg, pe)


def _out_kernel(y_ref, x_ref, p_ref, wo_ref, pg_ref, pe_ref, xo_ref):
    x1 = x_ref[0] + _mm(y_ref[0], wo_ref[...])
    xo_ref[0] = _embed(x1, p_ref[0], pg_ref, pe_ref)


def _out_embed(y, x, p_all, layer, wo, pg, pe, tm):
    B, T, D = x.shape
    row = lambda width: pl.BlockSpec((1, tm, width), lambda b, t: (b, t, 0))
    p_spec = pl.BlockSpec((None, 1, tm, p_all.shape[-1]), lambda b, t: (layer, b, t, 0))
    return pl.pallas_call(
        _out_kernel,
        grid=(B, T // tm),
        in_specs=[row(y.shape[-1]), row(D), p_spec, _resident(wo.shape), _resident(pg.shape), _resident(pe.shape)],
        out_specs=row(D),
        out_shape=jax.ShapeDtypeStruct((B, T, D), F32),
        compiler_params=_params("parallel", "parallel"),
        name="out_embed",
    )(y, x, p_all, wo, pg, pe)


CONV_COLS = 512


def _conv_prompt_kernel(x_ref, p_ref, g_ref, win_ref, cw_ref, wo_ref, pg_ref, pe_ref,
                        xo_ref, st_ref, carry_s):
    E = wo_ref.shape[0]

    @pl.when(pl.program_id(1) == 0)
    def _():
        carry_s[...] = jnp.zeros_like(carry_s)

    x = x_ref[0]
    tm = x.shape[0]
    h = _rms(x, g_ref[...]).astype(BF16)
    row = lax.broadcasted_iota(jnp.int32, (tm, CONV_COLS), 0)
    acc = jnp.zeros(x.shape, F32)
    for c in range(E // CONV_COLS):
        cols = slice(c * CONV_COLS, (c + 1) * CONV_COLS)
        part = lambda i: _mm(h, win_ref[:, i * E + c * CONV_COLS:i * E + (c + 1) * CONV_COLS])
        bg, cg, xb, z = part(0), part(1), part(2), part(3)
        cx = cg * xb
        old = carry_s[0:1, cols]
        new = carry_s[1:2, cols]
        prev1 = jnp.where(row == 0, new, pltpu.roll(cx, 1, 0))
        prev2 = jnp.where(row == 0, old, jnp.where(row == 1, new, pltpu.roll(cx, 2, 0)))
        y = cw_ref[0:1, cols] * prev2 + cw_ref[1:2, cols] * prev1 + cw_ref[2:3, cols] * cx
        acc = acc + _mm((bg * y * _silu(z)).astype(BF16), wo_ref[cols, :])
        carry_s[0:2, cols] = cx[tm - 2:tm, :]
    xo_ref[0] = _embed(x + acc, p_ref[0], pg_ref, pe_ref)
    st_ref[0] = carry_s[0:2, :]


def _conv_prompt(x, p_all, layer, g, win, cw, wo, pg, pe, tm):
    B, T, D = x.shape
    E = wo.shape[0]
    row = lambda width: pl.BlockSpec((1, tm, width), lambda b, t: (b, t, 0))
    p_spec = pl.BlockSpec((None, 1, tm, p_all.shape[-1]), lambda b, t: (layer, b, t, 0))
    return pl.pallas_call(
        _conv_prompt_kernel,
        grid=(B, T // tm),
        in_specs=[row(D), p_spec, _resident(g.shape), _resident(win.shape), _resident(cw.shape),
                  _resident(wo.shape), _resident(pg.shape), _resident(pe.shape)],
        out_specs=[row(D), pl.BlockSpec((1, CONV_W - 1, E), lambda b, t: (b, 0, 0))],
        out_shape=[jax.ShapeDtypeStruct((B, T, D), F32), jax.ShapeDtypeStruct((B, CONV_W - 1, E), F32)],
        scratch_shapes=[pltpu.VMEM((8, E), F32)],
        compiler_params=_params("parallel", "arbitrary"),
        name="conv_prompt",
    )(x, p_all, g, win, cw, wo, pg, pe)


def _conv_sample_kernel(x_ref, p_ref, g_ref, win_ref, cw_ref, wo_ref, pg_ref, pe_ref, s0_ref, s1_ref,
                        xo_ref, cx_ref):
    E = wo_ref.shape[0]
    x = x_ref[0]
    h = _rms(x, g_ref[...]).astype(BF16)
    acc = jnp.zeros(x.shape, F32)
    for c in range(E // CONV_COLS):
        cols = slice(c * CONV_COLS, (c + 1) * CONV_COLS)
        part = lambda i: _mm(h, win_ref[:, i * E + c * CONV_COLS:i * E + (c + 1) * CONV_COLS])
        bg, cg, xb, z = part(0), part(1), part(2), part(3)
        cx = cg * xb
        y = cw_ref[0:1, cols] * s0_ref[:, cols] + cw_ref[1:2, cols] * s1_ref[:, cols] + cw_ref[2:3, cols] * cx
        acc = acc + _mm((bg * y * _silu(z)).astype(BF16), wo_ref[cols, :])
        cx_ref[:, cols] = cx
    xo_ref[0] = _embed(x + acc, p_ref[0], pg_ref, pe_ref)


def _conv_sample(x, p_all, layer, g, win, cw, wo, pg, pe, s0, s1):
    _, M, D = x.shape
    E = wo.shape[0]
    p_spec = pl.BlockSpec((None, 1, M, p_all.shape[-1]), lambda i: (layer, 0, 0, 0))
    return pl.pallas_call(
        _conv_sample_kernel,
        grid=(1,),
        in_specs=[_resident(x.shape), p_spec, _resident(g.shape), _resident(win.shape), _resident(cw.shape),
                  _resident(wo.shape), _resident(pg.shape), _resident(pe.shape),
                  _resident(s0.shape), _resident(s1.shape)],
        out_specs=[_whole(x.shape), _whole((M, E))],
        out_shape=[jax.ShapeDtypeStruct(x.shape, F32), jax.ShapeDtypeStruct((M, E), F32)],
        compiler_params=_params("arbitrary"),
        name="conv_sample",
    )(x, p_all, g, win, cw, wo, pg, pe, s0, s1)


def _rope(x, cos, sin_lo, sin_hi):
    half = ROPE_DIM // 2
    return x * cos + pltpu.roll(x, C_DH - half, 1) * sin_lo + pltpu.roll(x, half, 1) * sin_hi


def _c_in_kernel(x_ref, g_ref, w_ref, cos_ref, slo_ref, shi_ref, *refs, tails, dils):
    n_groups = len(C_GROUPS)
    qkv_refs = refs[:3 * n_groups]
    z_ref = refs[3 * n_groups]
    tail_refs = refs[3 * n_groups + 1:-3]
    slabs = refs[-3:]
    h = _rms(x_ref[0], g_ref[...]).astype(BF16)
    tm = h.shape[0]
    cos, slo, shi = cos_ref[...], slo_ref[...], shi_ref[...]

    for grp in range(n_groups):
        base = 3 * grp * C_WIDTH
        uq = _mm(h, w_ref[:, base:base + C_WIDTH])
        uk = _mm(h, w_ref[:, base + C_WIDTH:base + 2 * C_WIDTH])
        uv = _mm(h, w_ref[:, base + 2 * C_WIDTH:base + 3 * C_WIDTH])
        dil = dils[grp]
        rows = tails[grp]
        out_refs = qkv_refs[3 * grp:3 * grp + 3]
        for hd in range(C_HEADS):
            hs = slice(hd * C_DH, (hd + 1) * C_DH)
            qh = _rope(uq[:, hs], cos, slo, shi) * Q_SCALE
            kh = _rope(uk[:, hs], cos, slo, shi)
            vh = uv[:, hs]
            tail_refs[2 * grp][0, :, hs] = kh[tm - rows:, :]
            tail_refs[2 * grp + 1][0, :, hs] = vh[tm - rows:, :]
            for ref, slab, val in zip(out_refs, slabs, (qh, kh, vh)):
                if dil == 1:
                    ref[0, 0, :, hs] = val.astype(BF16)
                else:
                    slab[hd] = val
        if dil > 1:
            for r in range(dil):
                for hd in range(C_HEADS):
                    hs = slice(hd * C_DH, (hd + 1) * C_DH)
                    for ref, slab in zip(out_refs, slabs):
                        ref[0, r, :, hs] = slab[hd, pl.ds(r, tm // dil, stride=dil), :].astype(BF16)

    zb = 3 * n_groups * C_WIDTH
    z_ref[0] = _mm(h, w_ref[:, zb:zb + C_WIDTH]).astype(BF16)


def _c_in(x, g, w, cos, slo, shi, tm, per_row=False):
    B, T, D = x.shape
    row = lambda width: pl.BlockSpec((1, tm, width), lambda b, t: (b, t, 0))
    table = pl.BlockSpec((tm, C_DH), lambda b, t: (t, 0))
    dils = tuple(1 if per_row else dil for _, dil in C_GROUPS)
    qkv_specs, qkv_shapes = [], []
    for dil in dils:
        qkv_specs += [pl.BlockSpec((1, dil, tm // dil, C_WIDTH), lambda b, t: (b, 0, t, 0))] * 3
        qkv_shapes += [jax.ShapeDtypeStruct((B, dil, T // dil, C_WIDTH), BF16)] * 3
    tails, tail_specs, tail_shapes = [], [], []
    for win, _ in C_GROUPS:
        keep = T if per_row else min(win, T)
        rows = min(tm, keep)
        first_row = T - keep
        tails.append(rows)

        def tail_index(b, t, first_row=first_row, rows=rows):
            return (b, jnp.maximum(((t + 1) * tm - first_row) // rows - 1, 0), 0)

        tail_specs += [pl.BlockSpec((1, rows, C_WIDTH), tail_index)] * 2
        tail_shapes += [jax.ShapeDtypeStruct((B, keep, C_WIDTH), F32)] * 2
    n_qkv = len(qkv_specs)
    outs = pl.pallas_call(
        functools.partial(_c_in_kernel, tails=tuple(tails), dils=dils),
        grid=(B, T // tm),
        in_specs=[row(D), _resident(g.shape), _resident(w.shape), table, table, table],
        out_specs=qkv_specs + [row(C_WIDTH)] + tail_specs,
        out_shape=qkv_shapes + [jax.ShapeDtypeStruct((B, T, C_WIDTH), BF16)] + tail_shapes,
        scratch_shapes=[pltpu.VMEM((C_HEADS, tm, C_DH), F32)] * 3,
        compiler_params=_params("parallel", "arbitrary"),
        name="attn_in",
    )(x, g, w, cos, slo, shi)
    return outs[:n_qkv], outs[n_qkv], outs[n_qkv + 1:]


ATTN_CHUNK = 512


def _attn_chunk_kernel(q_ref, kp_ref, kc_ref, vp_ref, vc_ref, acc_ref, st_ref, *, span):
    chunk = pl.program_id(2)
    qi = lax.broadcasted_iota(jnp.int32, (C_BLOCK, 2 * C_BLOCK), 0)
    kj = lax.broadcasted_iota(jnp.int32, (C_BLOCK, 2 * C_BLOCK), 1)
    dist = qi + C_BLOCK - kj
    band = (dist >= 0) & (dist <= span)
    bias_inner = jnp.where(band, 0.0, -1e30)
    bias_first = jnp.where(band & (kj >= jnp.where(chunk == 0, C_BLOCK, 0)), 0.0, -1e30)
    lane = lax.broadcasted_iota(jnp.int32, (C_BLOCK, LANES), 1)
    ones = jnp.ones((2 * C_BLOCK, LANES), BF16)

    for i in range(q_ref.shape[2] // C_BLOCK):
        rows = slice(i * C_BLOCK, (i + 1) * C_BLOCK)

        def keys(prev_ref, cur_ref, cols):
            if i == 0:
                return jnp.concatenate([prev_ref[0, 0, :, cols], cur_ref[0, 0, rows, cols]], axis=0)
            return cur_ref[0, 0, (i - 1) * C_BLOCK:(i + 1) * C_BLOCK, cols]

        head_cols = [slice(hd * C_DH, (hd + 1) * C_DH) for hd in range(C_HEADS)]
        s_all = [lax.dot_general(q_ref[0, 0, rows, cols], keys(kp_ref, kc_ref, cols), NT_DIMS,
                                 preferred_element_type=F32) for cols in head_cols]
        stats = jnp.zeros((C_BLOCK, LANES), F32)
        for hd, cols in enumerate(head_cols):
            s = s_all[hd] + (bias_first if i == 0 else bias_inner)
            mx = jnp.max(s, axis=-1, keepdims=True)
            p = jnp.exp2(s - mx).astype(BF16)
            both = _mm(p, jnp.concatenate([keys(vp_ref, vc_ref, cols), ones], axis=1))
            acc_ref[0, 0, rows, cols] = both[:, :C_DH].astype(BF16)
            stats = jnp.where(lane == hd, mx, jnp.where(lane == C_HEADS + hd, both[:, C_DH:], stats))
        st_ref[0, 0, rows, :] = stats


def _attn_prompt_group(q, k, v, span):
    B, dil, n, _ = q.shape
    qc = min(n, ATTN_CHUNK)
    per_chunk = qc // C_BLOCK
    cur = lambda rows, width: pl.BlockSpec((1, 1, rows, width), lambda b, r, c: (b, r, c, 0))
    prev = pl.BlockSpec((1, 1, C_BLOCK, C_WIDTH), lambda b, r, c: (b, r, jnp.maximum(c * per_chunk - 1, 0), 0))
    return pl.pallas_call(
        functools.partial(_attn_chunk_kernel, span=span),
        grid=(B, dil, n // qc),
        in_specs=[cur(qc, C_WIDTH), prev, cur(qc, C_WIDTH), prev, cur(qc, C_WIDTH)],
        out_specs=[cur(qc, C_WIDTH), cur(qc, LANES)],
        out_shape=[jax.ShapeDtypeStruct((B, dil, n, C_WIDTH), BF16),
                   jax.ShapeDtypeStruct((B, dil, n, LANES), F32)],
        compiler_params=_params("parallel", "parallel", "arbitrary"),
        name="attn_chunks_d%d" % dil,
    )(q, k, k, v, v)


def _merge_out_kernel(a0_ref, a1_ref, a2_ref, s0_ref, s1_ref, s2_ref, z_ref, x_ref, p_ref,
                      wo_ref, pg_ref, pe_ref, xo_ref, acc_s, st_s):
    tm = x_ref.shape[1]
    for grp, (a_ref, s_ref) in enumerate(zip((a0_ref, a1_ref, a2_ref), (s0_ref, s1_ref, s2_ref))):
        dil = a_ref.shape[1]
        for r in range(dil):
            rows = pl.ds(r, tm // dil, stride=dil) if dil > 1 else slice(None)
            acc = a_ref[0, r].astype(F32)
            for hd in range(C_HEADS):
                acc_s[grp * C_HEADS + hd, rows, :] = acc[:, hd * C_DH:(hd + 1) * C_DH]
            st_s[grp, rows, :] = s_ref[0, r]
    mx = [st_s[grp, :, 0:C_HEADS] for grp in range(3)]
    ls = [st_s[grp, :, C_HEADS:2 * C_HEADS] for grp in range(3)]
    top = jnp.maximum(jnp.maximum(mx[0], mx[1]), mx[2])
    es = [jnp.exp2(m - top) for m in mx]
    den = es[0] * ls[0] + es[1] * ls[1] + es[2] * ls[2]
    coef = [e / den for e in es]
    x1 = x_ref[0]
    for pair in range(C_HEADS // 2):
        ys = []
        for hd in (2 * pair, 2 * pair + 1):
            hs = slice(hd * C_DH, (hd + 1) * C_DH)
            o = coef[0][:, hd:hd + 1] * acc_s[hd]
            for grp in (1, 2):
                o = o + coef[grp][:, hd:hd + 1] * acc_s[grp * C_HEADS + hd]
            ys.append((o * _silu(z_ref[0, :, hs].astype(F32))).astype(BF16))
        x1 = x1 + _mm(jnp.concatenate(ys, axis=1), wo_ref[2 * pair * C_DH:(2 * pair + 2) * C_DH, :])
    xo_ref[0] = _embed(x1, p_ref[0], pg_ref, pe_ref)


def _merge_out(accs, stats, z, x, p_all, layer, wo, pg, pe, tm):
    B, T, D = x.shape
    row = lambda width: pl.BlockSpec((1, tm, width), lambda b, t: (b, t, 0))
    classes = lambda a: pl.BlockSpec((1, a.shape[1], tm // a.shape[1], a.shape[3]), lambda b, t: (b, 0, t, 0))
    p_spec = pl.BlockSpec((None, 1, tm, p_all.shape[-1]), lambda b, t: (layer, b, t, 0))
    return pl.pallas_call(
        _merge_out_kernel,
        grid=(B, T // tm),
        in_specs=[classes(a) for a in accs] + [classes(s) for s in stats] + [row(C_WIDTH), row(D), p_spec,
                  _resident(wo.shape), _resident(pg.shape), _resident(pe.shape)],
        out_specs=row(D),
        out_shape=jax.ShapeDtypeStruct((B, T, D), F32),
        scratch_shapes=[pltpu.VMEM((3 * C_HEADS, tm, C_DH), F32), pltpu.VMEM((3, tm, LANES), F32)],
        compiler_params=_params("parallel", "parallel"),
        name="attn_merge_out",
    )(*accs, *stats, z, x, p_all, wo, pg, pe)


def _attn_sample_kernel(*refs):
    n_groups = len(C_GROUPS)
    z_ref, y_ref = refs[5 * n_groups], refs[5 * n_groups + 1]
    parts = []
    for grp in range(n_groups):
        q_ref, kn_ref, vn_ref, kc_ref, vc_ref = refs[5 * grp:5 * grp + 5]
        q = q_ref[0].astype(F32)
        s_old = jnp.sum(kc_ref[0] * q[None], axis=-1, keepdims=True)
        s_new = jnp.sum(kn_ref[0] * q, axis=-1, keepdims=True)
        mx = jnp.maximum(jnp.max(s_old, axis=0), s_new)
        p_old = jnp.exp2(s_old - mx[None])
        p_new = jnp.exp2(s_new - mx)
        l = jnp.sum(p_old, axis=0) + p_new
        acc = jnp.sum(p_old * vc_ref[0], axis=0) + p_new * vn_ref[0]
        parts.append((acc, mx, l))
    top = jnp.maximum(jnp.maximum(parts[0][1], parts[1][1]), parts[2][1])
    es = [jnp.exp2(m - top) for _, m, _ in parts]
    num = es[0] * parts[0][0] + es[1] * parts[1][0] + es[2] * parts[2][0]
    den = es[0] * parts[0][2] + es[1] * parts[1][2] + es[2] * parts[2][2]
    y_ref[0] = (num / den * _silu(z_ref[0].astype(F32))).astype(BF16)


def _attn_sample(qkv, tails, caches, z):
    Bs = z.shape[1]
    heads = lambda a: a.reshape(Bs, C_HEADS, C_DH)
    row = pl.BlockSpec((1, C_HEADS, C_DH), lambda b: (b, 0, 0))
    args, specs = [], []
    for grp, (win, dil) in enumerate(C_GROUPS):
        kc, vc = caches[grp]
        n_buf = kc.shape[1]
        assert n_buf == win and win % dil == 0, "sample window must be fully cached"
        span = win // dil
        view = lambda a: a.reshape(Bs, span, dil, C_HEADS, C_DH)
        cache_spec = pl.BlockSpec((1, span, None, C_HEADS, C_DH), lambda b: (b, 0, 0, 0, 0))
        args += [heads(qkv[3 * grp]), heads(tails[2 * grp]), heads(tails[2 * grp + 1]), view(kc), view(vc)]
        specs += [row, row, row, cache_spec, cache_spec]
    return pl.pallas_call(
        _attn_sample_kernel,
        grid=(Bs,),
        in_specs=specs + [row],
        out_specs=row,
        out_shape=jax.ShapeDtypeStruct((Bs, C_HEADS, C_DH), BF16),
        compiler_params=_params("parallel"),
        name="attn_sample",
    )(*args, heads(z))


def _pool_tail(x, acc, p, pg_ref, pe_ref, fg_ref):
    return _rms(_embed(x + acc, p, pg_ref, pe_ref), fg_ref[...])


def _pool_prompt_kernel(x_ref, p_ref, g_ref, win_ref, wgrp_ref, sc_ref, wo_ref, pg_ref, pe_ref, fg_ref,
                        xo_ref, st_ref, pad_s):
    E = wo_ref.shape[0]
    G = E // len(D_WINDOWS)
    t = pl.program_id(1)

    @pl.when(t == 0)
    def _():
        pad_s[0:POOL_PAD, :] = jnp.zeros((POOL_PAD, E), F32)

    x = x_ref[0]
    tm = x.shape[0]
    h = _rms(x, g_ref[...]).astype(BF16)
    pos = t * tm + lax.broadcasted_iota(jnp.int32, (tm, 1), 0)
    acc = jnp.zeros(x.shape, F32)
    for grp, w in enumerate(D_WINDOWS):
        cols = slice(grp * G, (grp + 1) * G)
        xp = _mm(h, win_ref[:, cols])
        z = _mm(h, win_ref[:, E + grp * G:E + (grp + 1) * G])
        pad_s[POOL_PAD:, cols] = xp
        wsum = pad_s[:, cols]
        shift = 1
        while shift < w:
            wsum = wsum + pltpu.roll(wsum, shift, 0)
            shift *= 2
        cnt = jnp.minimum(w, pos + 1).astype(F32)
        r = wsum[POOL_PAD:, :] / cnt - xp
        y = _mm(r.astype(BF16), wgrp_ref[grp]) * sc_ref[:, cols]
        acc = acc + _mm((y * _silu(z)).astype(BF16), wo_ref[cols, :])
        pad_s[0:POOL_PAD, cols] = xp[tm - POOL_PAD:, :]
    xo_ref[0] = _pool_tail(x, acc, p_ref[0], pg_ref, pe_ref, fg_ref)
    st_ref[0] = pad_s[0:POOL_PAD, :]


def _pool_prompt(x, p_all, layer, g, win, wgrp, sc, wo, pg, pe, fg, tm):
    B, T, D = x.shape
    E = wo.shape[0]
    row = lambda width: pl.BlockSpec((1, tm, width), lambda b, t: (b, t, 0))
    p_spec = pl.BlockSpec((None, 1, tm, p_all.shape[-1]), lambda b, t: (layer, b, t, 0))
    return pl.pallas_call(
        _pool_prompt_kernel,
        grid=(B, T // tm),
        in_specs=[row(D), p_spec] + [_resident(a.shape) for a in (g, win, wgrp, sc, wo, pg, pe, fg)],
        out_specs=[row(D), pl.BlockSpec((1, POOL_PAD, E), lambda b, t: (b, 0, 0))],
        out_shape=[jax.ShapeDtypeStruct((B, T, D), F32), jax.ShapeDtypeStruct((B, POOL_PAD, E), F32)],
        scratch_shapes=[pltpu.VMEM((POOL_PAD + tm, E), F32)],
        compiler_params=_params("parallel", "arbitrary"),
        name="pool_prompt",
    )(x, p_all, g, win, wgrp, sc, wo, pg, pe, fg)


def _pool_sample_kernel(x_ref, p_ref, g_ref, win_ref, wgrp_ref, sc_ref, wo_ref, pg_ref, pe_ref, fg_ref, st_ref,
                        xo_ref, xp_ref, *, pos):
    E = wo_ref.shape[0]
    G = E // len(D_WINDOWS)
    x = x_ref[0]
    h = _rms(x, g_ref[...]).astype(BF16)
    acc = jnp.zeros(x.shape, F32)
    for grp, w in enumerate(D_WINDOWS):
        cols = slice(grp * G, (grp + 1) * G)
        xp = _mm(h, win_ref[:, cols])
        z = _mm(h, win_ref[:, E + grp * G:E + (grp + 1) * G])
        wsum = xp
        for back in range(1, w):
            lo = (POOL_STATE - back) * E + grp * G
            wsum = wsum + st_ref[:, lo:lo + G]
        r = wsum / float(min(w, pos + 1)) - xp
        y = _mm(r.astype(BF16), wgrp_ref[grp]) * sc_ref[:, cols]
        acc = acc + _mm((y * _silu(z)).astype(BF16), wo_ref[cols, :])
        xp_ref[:, cols] = xp
    xo_ref[0] = _pool_tail(x, acc, p_ref[0], pg_ref, pe_ref, fg_ref)


def _pool_sample(x, p_all, layer, g, win, wgrp, sc, wo, pg, pe, fg, st, pos):
    _, M, D = x.shape
    E = wo.shape[0]
    p_spec = pl.BlockSpec((None, 1, M, p_all.shape[-1]), lambda i: (layer, 0, 0, 0))
    return pl.pallas_call(
        functools.partial(_pool_sample_kernel, pos=pos),
        grid=(1,),
        in_specs=[_resident(x.shape), p_spec] + [_resident(a.shape) for a in (g, win, wgrp, sc, wo, pg, pe, fg, st)],
        out_specs=[_whole(x.shape), _whole((M, E))],
        out_shape=[jax.ShapeDtypeStruct(x.shape, F32), jax.ShapeDtypeStruct((M, E), F32)],
        compiler_params=_params("arbitrary"),
        name="pool_sample",
    )(x, p_all, g, win, wgrp, sc, wo, pg, pe, fg, st)


def _rope_tables(pos):
    half = ROPE_DIM // 2
    inv = ROPE_THETA ** (-jnp.arange(half, dtype=F32) / half)
    ang = pos.astype(F32)[:, None] * inv[None, :]
    cos, sin = jnp.cos(ang), jnp.sin(ang)
    rest = C_DH - ROPE_DIM
    n = pos.shape[0]
    cos_t = jnp.concatenate([cos, cos, jnp.ones((n, rest), F32)], axis=1)
    sin_lo = jnp.concatenate([-sin, jnp.zeros((n, half + rest), F32)], axis=1)
    sin_hi = jnp.concatenate([jnp.zeros((n, half), F32), sin, jnp.zeros((n, rest), F32)], axis=1)
    return cos_t, sin_lo, sin_hi


def _row(a):
    return a.reshape(1, -1).astype(F32)


def kernel(x_prompt, x_sample, state_mlstm_C, state_mlstm_n, state_mlstm_m, state_conv, cache_k_w128, cache_v_w128, cache_k_w512, cache_v_w512, cache_k_w2048, cache_v_w2048, state_pool, p_prompt, p_sample, norm_g, pe_w, pg_w, final_g, a_w_in, a_b_if, a_norm_g, a_w_out, b_w_in, b_conv_w, b_w_out, c_w_in, c_w_out, d_w_in, d_w_grp, d_scale, d_w_out):
    B, T, D = x_prompt.shape
    Bs, Ts, _ = x_sample.shape
    assert Ts == 1 and norm_g.shape[0] == 4 and a_w_in.shape[0] == 1
    assert T % 512 == 0 and all(T % (dil * C_BLOCK) == 0 for _, dil in C_GROUPS)
    assert Bs % 8 == 0

    bf = lambda a: a.astype(BF16)
    qk = A_HEADS * A_DK
    n_main = 2 * qk + 3 * A_WIDTH
    a_w = bf(a_w_in[0, :, :n_main])
    a_wg = bf(jnp.pad(a_w_in[0, :, n_main:], ((0, 0), (0, LANES - 2 * A_HEADS))))
    a_bif = jnp.pad(a_b_if[0], (0, LANES - 2 * A_HEADS)).reshape(1, LANES)
    a_ng = _row(a_norm_g[0])
    a_wo, b_wi, b_wo, c_wi, c_wo, d_wi, d_wg, d_wo = (
        bf(a_w_out[0]), bf(b_w_in[0]), bf(b_w_out[0]), bf(c_w_in[0]), bf(c_w_out[0]),
        bf(d_w_in[0]), bf(d_w_grp[0]), bf(d_w_out[0]))
    pg, pe = bf(pg_w), bf(pe_w)
    gs = [_row(norm_g[i]) for i in range(4)]
    fg = _row(final_g)
    b_cw = b_conv_w[0]
    d_sc = _row(d_scale[0])
    caches = [(cache_k_w128[0], cache_v_w128[0]), (cache_k_w512[0], cache_v_w512[0]),
              (cache_k_w2048[0], cache_v_w2048[0])]

    x = x_prompt
    q, k, v, o, z, gates, kt = _a_in(x, gs[0], a_w, a_wg, a_bif, 512, transposed_k=True)
    hh, c_p, n_p, m_p = _mlstm_prompt(q, k, kt, v, gates)
    x = _a_out(hh, o, z, a_ng, x, p_prompt, 0, a_wo, pg[0], pe[0], 512)
    x, conv_p = _conv_prompt(x, p_prompt, 1, gs[1], b_wi, b_cw, b_wo, pg[1], pe[1], 512)
    qkv, z, tails_p = _c_in(x, gs[2], c_wi, *_rope_tables(jnp.arange(T)), 256)
    accs, stats = [], []
    for grp, (win, dil) in enumerate(C_GROUPS):
        acc, st = _attn_prompt_group(qkv[3 * grp], qkv[3 * grp + 1], qkv[3 * grp + 2], win // dil)
        accs.append(acc)
        stats.append(st)
    x = _merge_out(accs, stats, z, x, p_prompt, 2, c_wo, pg[2], pe[2], 512)
    y_prompt, pool_p = _pool_prompt(x, p_prompt, 3, gs[3], d_wi, d_wg, d_sc, d_wo, pg[3], pe[3], fg, 512)

    xs = x_sample.reshape(1, Bs, D)
    ps = p_sample.reshape(p_sample.shape[0], 1, Bs, p_sample.shape[-1])
    q, k, v, o, z, gates = _a_in(xs, gs[0], a_w, a_wg, a_bif, Bs)
    per_seq = lambda a: a.reshape(Bs, 1, a.shape[-1])
    hh, c_s, n_s, m_s = _mlstm_sample(per_seq(q), per_seq(k), per_seq(v), per_seq(gates),
                                      state_mlstm_C[0], state_mlstm_n[0], state_mlstm_m[0].reshape(Bs, 1, A_HEADS))
    xs = _a_out(hh.reshape(1, Bs, A_WIDTH), o, z, a_ng, xs, ps, 0, a_wo, pg[0], pe[0], Bs)
    xs, cx = _conv_sample(xs, ps, 1, gs[1], b_wi, b_cw, b_wo, pg[1], pe[1],
                          state_conv[0, :, 0, :], state_conv[0, :, 1, :])
    conv_s = jnp.stack([state_conv[0, :, 1, :], cx], axis=1)
    tables = [jnp.broadcast_to(t, (Bs, C_DH)) for t in _rope_tables(PAST_LEN + jnp.arange(1))]
    qkv, z, tails_s = _c_in(xs, gs[2], c_wi, *tables, Bs, per_row=True)
    y = _attn_sample(qkv, tails_s, caches, z)
    xs = _out_embed(y.reshape(1, Bs, C_WIDTH), xs, ps, 2, c_wo, pg[2], pe[2], Bs)
    pool_flat = state_pool[0].reshape(Bs, POOL_STATE * state_pool.shape[-1])
    ys, xp = _pool_sample(xs, ps, 3, gs[3], d_wi, d_wg, d_sc, d_wo, pg[3], pe[3], fg, pool_flat, PAST_LEN)
    pool_s = jnp.concatenate([state_pool[0, :, 1:, :], xp[:, None, :]], axis=1)

    heads = lambda a, lead: a.reshape(1, lead, a.shape[1], C_HEADS, C_DH)
    kv_out = []
    for grp in range(len(C_GROUPS)):
        for j in range(2):
            kv_out += [heads(tails_p[2 * grp + j], B), tails_s[2 * grp + j].reshape(1, Bs, 1, C_HEADS, C_DH)]
    return (y_prompt, ys.reshape(Bs, 1, D),
            c_p[None], c_s[None], n_p[None], n_s[None], m_p[None, :, 0, :A_HEADS], m_s.reshape(1, Bs, A_HEADS),
            conv_p[None], conv_s[None],
            *kv_out,
            pool_p[None, :, 1:, :], pool_s[None])
```

```python
import functools

import jax
import jax.numpy as jnp
from jax import lax
from jax.experimental import pallas as pl
from jax.experimental.pallas import tpu as pltpu

F32 = jnp.float32
BF16 = jnp.bfloat16

EPS = 1e-6
PAST_LEN = 8192

A_HEADS = 8
A_DK = 128
A_DV = 256
A_WIDTH = A_HEADS * A_DV
A_CHUNK = 128
A_CHUNKS_PER_STEP = 2

CONV_W = 3

C_HEADS = 8
C_DH = 128
C_WIDTH = C_HEADS * C_DH
C_GROUPS = ((128, 1), (512, 4), (2048, 16))
C_BLOCK = 128
ROPE_DIM = C_DH // 4
ROPE_THETA = 500000.0
LOG2_E = 1.4426950408889634
Q_SCALE = C_DH ** -0.5 * LOG2_E

D_WINDOWS = (2, 4, 8, 16)
POOL_STATE = max(D_WINDOWS) - 1
POOL_PAD = POOL_STATE + 1

LANES = 128
VMEM_LIMIT_BYTES = 56 * 2 ** 20

NT_DIMS = (((1,), (1,)), ((), ()))
TN_DIMS = (((0,), (0,)), ((), ()))


def _params(*semantics):
    return pltpu.CompilerParams(dimension_semantics=semantics, vmem_limit_bytes=VMEM_LIMIT_BYTES)


def _resident(shape):
    zeros = (0,) * len(shape)
    return pl.BlockSpec(shape, lambda *_: zeros, pipeline_mode=pl.Buffered(1))


def _whole(shape):
    zeros = (0,) * len(shape)
    return pl.BlockSpec(shape, lambda *_: zeros)


def _mm(a, b):
    return jnp.dot(a, b, preferred_element_type=F32)


def _rms(x, g):
    return x * lax.rsqrt(jnp.mean(x * x, axis=-1, keepdims=True) + EPS) * g


def _sigmoid(x):
    return 0.5 * jnp.tanh(0.5 * x) + 0.5


def _silu(x):
    return x * _sigmoid(x)


def _log_sigmoid(x):
    return jnp.minimum(x, 0.0) - jnp.log(1.0 + jnp.exp(-jnp.abs(x)))


def _embed(x1, p, pg_ref, pe_ref):
    gate = _sigmoid(_mm(x1.astype(BF16), pg_ref[...]))
    return x1 + gate * _mm(p.astype(BF16), pe_ref[...])


def _a_in_kernel(x_ref, g_ref, w_ref, wg_ref, bif_ref,
                 q_ref, k_ref, v_ref, oz_ref, gate_ref, *maybe_kt_ref):
    h = _rms(x_ref[0], g_ref[...]).astype(BF16)
    qk = A_HEADS * A_DK

    def proj(lo, width):
        return _mm(h, w_ref[:, lo:lo + width])

    q_ref[0] = proj(0, qk).astype(BF16)
    k = proj(qk, qk) * (A_DK ** -0.5)
    k_ref[0] = k.astype(BF16)
    for kt_ref in maybe_kt_ref:
        kt_ref[0] = k.T.astype(BF16)
    v_ref[0] = proj(2 * qk, A_WIDTH).astype(BF16)
    oz_ref[0] = (_sigmoid(proj(2 * qk + A_WIDTH, A_WIDTH))
                 * _silu(proj(2 * qk + 2 * A_WIDTH, A_WIDTH))).astype(BF16)
    gates = _mm(h, wg_ref[...]) + bif_ref[...]
    lane = lax.broadcasted_iota(jnp.int32, gates.shape, 1)
    gate_ref[0] = jnp.where(lane < A_HEADS, gates, _log_sigmoid(gates))


def _a_in(x, g, w, wg, bif, tm, transposed_k=False):
    B, T, D = x.shape
    qk = A_HEADS * A_DK
    row = lambda width: pl.BlockSpec((1, tm, width), lambda b, t: (b, t, 0))
    out_specs = [row(qk), row(qk), row(A_WIDTH), row(A_WIDTH), row(LANES)]
    out_shape = [jax.ShapeDtypeStruct((B, T, qk), BF16), jax.ShapeDtypeStruct((B, T, qk), BF16),
                 jax.ShapeDtypeStruct((B, T, A_WIDTH), BF16), jax.ShapeDtypeStruct((B, T, A_WIDTH), BF16),
                 jax.ShapeDtypeStruct((B, T, LANES), F32)]
    if transposed_k:
        out_specs.append(pl.BlockSpec((1, qk, tm), lambda b, t: (b, 0, t)))
        out_shape.append(jax.ShapeDtypeStruct((B, qk, T), BF16))
    return pl.pallas_call(
        _a_in_kernel,
        grid=(B, T // tm),
        in_specs=[row(D), _resident(g.shape), _resident(w.shape), _resident(wg.shape), _resident(bif.shape)],
        out_specs=out_specs,
        out_shape=out_shape,
        compiler_params=_params("parallel", "parallel"),
        name="mlstm_in",
    )(x, g, w, wg, bif)


def _mlstm_chunk_kernel(q_ref, k_ref, kt_ref, v_ref, gate_ref,
                        hh_ref, c_out, n_out, m_out, m_s, *state_s):
    chunk = pl.program_id(1)
    ct_s, nb_s = state_s[:A_HEADS], state_s[A_HEADS:]

    @pl.when(chunk == 0)
    def _():
        for ref in state_s:
            ref[...] = jnp.zeros_like(ref)
        m_s[...] = jnp.zeros_like(m_s)

    L = A_CHUNK
    row = lax.broadcasted_iota(jnp.int32, (L, LANES), 0)
    ti = lax.broadcasted_iota(jnp.int32, (L, L), 0)
    si = lax.broadcasted_iota(jnp.int32, (L, L), 1)
    causal = ti >= si
    ones = jnp.ones((L, LANES), BF16)
    heads = range(A_HEADS)
    ks = [slice(h * A_DK, (h + 1) * A_DK) for h in heads]
    vs = [slice(h * A_DV, (h + 1) * A_DV) for h in heads]

    def prefix(x, op, identity):
        step = 1
        while step < L:
            x = op(x, jnp.where(row >= step, pltpu.roll(x, step, 0), identity))
            step *= 2
        return x

    def one_chunk(rows):
        gates = gate_ref[0, rows, :]
        b = pltpu.roll(prefix(gates, jnp.add, 0.0), LANES - A_HEADS, 1)
        c = gates - b
        m_row = m_s[0:1, :]
        top = jnp.maximum(m_row, prefix(c, jnp.maximum, -jnp.inf))
        top_last = top[L - 1:L, :]
        g_all = jnp.exp(m_row - top)
        floor_all = jnp.exp(-(b + top))
        w_all = jnp.exp(c - top_last)
        decay_row = jnp.exp(m_row - top_last)
        m_s[...] = jnp.broadcast_to(b[L - 1:L, :] + top_last, m_s.shape)
        c_t = c.T

        s_raw = [_mm(q_ref[0, rows, ks[h]], kt_ref[0, ks[h], rows]) for h in heads]
        upd = []
        for h in heads:
            w_b = jnp.broadcast_to(w_all[:, h:h + 1], (L, LANES))
            wv = jnp.concatenate([w_b, w_b], axis=1) * v_ref[0, rows, vs[h]].astype(F32)
            upd.append(_mm(kt_ref[0, ks[h], rows], jnp.concatenate([wv.astype(BF16), w_b.astype(BF16)], axis=1)))

        for h in heads:
            ct = ct_s[h][...]
            nb = nb_s[h][...]
            dmat = jnp.exp(jnp.where(causal, -top[:, h:h + 1] + c_t[h:h + 1, :], -jnp.inf))
            s = s_raw[h] * dmat
            gq = g_all[:, h:h + 1] * q_ref[0, rows, ks[h]].astype(F32)
            lhs = jnp.concatenate([s.astype(BF16), gq.astype(BF16)], axis=1)
            num = _mm(lhs, jnp.concatenate([v_ref[0, rows, vs[h]], ct.astype(BF16)], axis=0))
            den = _mm(lhs, jnp.concatenate([ones, nb.astype(BF16)], axis=0))
            scale = 1.0 / jnp.maximum(jnp.abs(den), floor_all[:, h:h + 1])
            hh_ref[0, rows, vs[h]] = (num * jnp.concatenate([scale, scale], axis=1)).astype(BF16)
            decay = decay_row[:, h:h + 1]
            ct_s[h][...] = decay * ct + upd[h][:, :A_DV]
            nb_s[h][...] = decay * nb + upd[h][:, A_DV:]

    for sub in range(q_ref.shape[1] // L):
        one_chunk(slice(sub * L, (sub + 1) * L))

    @pl.when(chunk == pl.num_programs(1) - 1)
    def _():
        for h in range(A_HEADS):
            c_out[0, h] = ct_s[h][...].T
            n_out[0, h:h + 1, :] = nb_s[h][...].T[0:1, :]
        m_out[0] = m_s[...]


def _mlstm_prompt(q, k, kt, v, gates):
    B, T, _ = q.shape
    L = A_CHUNK * A_CHUNKS_PER_STEP
    qk = A_HEADS * A_DK
    row = lambda width: pl.BlockSpec((1, L, width), lambda b, c: (b, c, 0))
    state = lambda *shape: pl.BlockSpec((1,) + shape, lambda b, c: (b,) + (0,) * len(shape))
    return pl.pallas_call(
        _mlstm_chunk_kernel,
        grid=(B, T // L),
        in_specs=[row(qk), row(qk), pl.BlockSpec((1, qk, L), lambda b, c: (b, 0, c)), row(A_WIDTH), row(LANES)],
        out_specs=[row(A_WIDTH), state(A_HEADS, A_DV, A_DK), state(A_HEADS, A_DK), state(A_HEADS, LANES)],
        out_shape=[jax.ShapeDtypeStruct((B, T, A_WIDTH), BF16),
                   jax.ShapeDtypeStruct((B, A_HEADS, A_DV, A_DK), F32),
                   jax.ShapeDtypeStruct((B, A_HEADS, A_DK), F32),
                   jax.ShapeDtypeStruct((B, A_HEADS, LANES), F32)],
        scratch_shapes=[pltpu.VMEM((A_HEADS, LANES), F32)] + [pltpu.VMEM((A_DK, A_DV), F32)] * A_HEADS
        + [pltpu.VMEM((A_DK, LANES), F32)] * A_HEADS,
        compiler_params=_params("parallel", "arbitrary"),
        name="mlstm_chunks",
    )(q, k, kt, v, gates)


def _mlstm_step_kernel(q_ref, k_ref, v_ref, gate_ref, m_ref, c_ref, n_ref,
                       hh_ref, c_out, n_out, m_out, hht_s):
    head = pl.program_id(0)
    Bs = q_ref.shape[0]
    q = q_ref[...].astype(F32)
    k = k_ref[...].astype(F32)
    n = n_ref[...]
    lane = lax.broadcasted_iota(jnp.int32, (Bs, LANES), 1)
    pick = lambda a, idx: jnp.sum(jnp.where(lane == idx, a, 0.0), axis=-1, keepdims=True)
    ig = pick(gate_ref[...], head)
    lf = pick(gate_ref[...], A_HEADS + head)
    m_prev = pick(m_ref[...], head)

    inter = lf + m_prev
    m_t = jnp.maximum(inter, ig)
    dm = jnp.exp(ig - m_t)
    g = jnp.exp(inter - m_t)
    s = jnp.sum(q * k, axis=-1, keepdims=True) * dm
    den = s + g * jnp.sum(q * n, axis=-1, keepdims=True)
    scale = 1.0 / jnp.maximum(jnp.abs(den), jnp.exp(-m_t))
    n_out[...] = g * n + dm * k
    m_out[...] = jnp.broadcast_to(m_t, (Bs, LANES))

    v_pad = jnp.concatenate([v_ref[...].astype(F32), jnp.zeros((LANES - Bs, A_DV), F32)], axis=0)
    vt = v_pad.T
    hht_s[...] = jnp.zeros_like(hht_s)
    for b in range(Bs):
        c_b = c_ref[b]
        one = slice(b, b + 1)
        v_col = vt[:, one]
        cq = jnp.sum(c_b * q[one, :], axis=-1, keepdims=True)
        hht_s[:, one] = (s[one, :] * v_col + g[one, :] * cq) * scale[one, :]
        c_out[b] = g[one, :] * c_b + (dm[one, :] * v_col) * k[one, :]
    hh_ref[...] = hht_s[...].T[0:Bs, :].astype(BF16)


def _mlstm_sample(q, k, v, gates, c0, n0, m0):
    Bs = q.shape[0]
    assert Bs <= LANES
    cols = lambda width: pl.BlockSpec((Bs, width), lambda h: (0, h))
    full = pl.BlockSpec((Bs, LANES), lambda h: (0, 0))
    c_spec = pl.BlockSpec((Bs, None, A_DV, A_DK), lambda h: (0, h, 0, 0))
    hh, c1, n1, m1 = pl.pallas_call(
        _mlstm_step_kernel,
        grid=(A_HEADS,),
        in_specs=[cols(A_DK), cols(A_DK), cols(A_DV), full, full, c_spec, cols(A_DK)],
        out_specs=[cols(A_DV), c_spec, cols(A_DK), pl.BlockSpec((None, Bs, LANES), lambda h: (h, 0, 0))],
        out_shape=[jax.ShapeDtypeStruct((Bs, A_WIDTH), BF16),
                   jax.ShapeDtypeStruct(c0.shape, F32), jax.ShapeDtypeStruct((Bs, A_HEADS * A_DK), F32),
                   jax.ShapeDtypeStruct((A_HEADS, Bs, LANES), F32)],
        scratch_shapes=[pltpu.VMEM((A_DV, LANES), F32)],
        compiler_params=_params("parallel"),
        name="mlstm_step",
    )(q, k, v, gates, m0, c0, n0.reshape(Bs, A_HEADS * A_DK))
    return hh, c1, n1.reshape(n0.shape), m1


def _a_out_kernel(hh_ref, oz_ref, ng_ref, x_ref, p_ref, wo_ref, pg_ref, pe_ref, xo_ref):
    x1 = x_ref[0]
    for h in range(A_HEADS):
        vs = slice(h * A_DV, (h + 1) * A_DV)
        hh = hh_ref[0, :, vs].astype(F32)
        hn = hh * lax.rsqrt(jnp.mean(hh * hh, axis=-1, keepdims=True) + EPS) * ng_ref[:, vs]
        y = hn * oz_ref[0, :, vs].astype(F32)
        x1 = x1 + _mm(y.astype(BF16), wo_ref[vs, :])
    xo_ref[0] = _embed(x1, p_ref[0], pg_ref, pe_ref)


def _a_out(hh, oz, ng, x, p_all, layer, wo, pg, pe, tm):
    B, T, D = x.shape
    row = lambda width: pl.BlockSpec((1, tm, width), lambda b, t: (b, t, 0))
    p_spec = pl.BlockSpec((None, 1, tm, p_all.shape[-1]), lambda b, t: (layer, b, t, 0))
    return pl.pallas_call(
        _a_out_kernel,
        grid=(B, T // tm),
        in_specs=[row(A_WIDTH), row(A_WIDTH), _resident(ng.shape), row(D), p_spec,
                  _resident(wo.shape), _resident(pg.shape), _resident(pe.shape)],
        out_specs=row(D),
        out_shape=jax.ShapeDtypeStruct((B, T, D), F32),
        compiler_params=_params("parallel", "parallel"),
        name="mlstm_out",
    )(hh, oz, ng, x, p_all, wo, pg, pe)


def _out_kernel(y_ref, x_ref, p_ref, wo_ref, pg_ref, pe_ref, xo_ref):
    x1 = x_ref[0] + _mm(y_ref[0], wo_ref[...])
    xo_ref[0] = _embed(x1, p_ref[0], pg_ref, pe_ref)


def _out_embed(y, x, p_all, layer, wo, pg, pe, tm):
    B, T, D = x.shape
    row = lambda width: pl.BlockSpec((1, tm, width), lambda b, t: (b, t, 0))
    p_spec = pl.BlockSpec((None, 1, tm, p_all.shape[-1]), lambda b, t: (layer, b, t, 0))
    return pl.pallas_call(
        _out_kernel,
        grid=(B, T // tm),
        in_specs=[row(y.shape[-1]), row(D), p_spec, _resident(wo.shape), _resident(pg.shape), _resident(pe.shape)],
        out_specs=row(D),
        out_shape=jax.ShapeDtypeStruct((B, T, D), F32),
        compiler_params=_params("parallel", "parallel"),
        name="out_embed",
    )(y, x, p_all, wo, pg, pe)


CONV_COLS = 512


def _conv_prompt_kernel(x_ref, p_ref, g_ref, win_ref, cw_ref, wo_ref, pg_ref, pe_ref,
                        xo_ref, st_ref, carry_s):
    E = wo_ref.shape[0]

    @pl.when(pl.program_id(1) == 0)
    def _():
        carry_s[...] = jnp.zeros_like(carry_s)

    x = x_ref[0]
    tm = x.shape[0]
    h = _rms(x, g_ref[...]).astype(BF16)
    row = lax.broadcasted_iota(jnp.int32, (tm, CONV_COLS), 0)
    acc = jnp.zeros(x.shape, F32)
    for c in range(E // CONV_COLS):
        cols = slice(c * CONV_COLS, (c + 1) * CONV_COLS)
        part = lambda i: _mm(h, win_ref[:, i * E + c * CONV_COLS:i * E + (c + 1) * CONV_COLS])
        bg, cg, xb, z = part(0), part(1), part(2), part(3)
        cx = cg * xb
        old = carry_s[0:1, cols]
        new = carry_s[1:2, cols]
        prev1 = jnp.where(row == 0, new, pltpu.roll(cx, 1, 0))
        prev2 = jnp.where(row == 0, old, jnp.where(row == 1, new, pltpu.roll(cx, 2, 0)))
        y = cw_ref[0:1, cols] * prev2 + cw_ref[1:2, cols] * prev1 + cw_ref[2:3, cols] * cx
        acc = acc + _mm((bg * y * _silu(z)).astype(BF16), wo_ref[cols, :])
        carry_s[0:2, cols] = cx[tm - 2:tm, :]
    xo_ref[0] = _embed(x + acc, p_ref[0], pg_ref, pe_ref)
    st_ref[0] = carry_s[0:2, :]


def _conv_prompt(x, p_all, layer, g, win, cw, wo, pg, pe, tm):
    B, T, D = x.shape
    E = wo.shape[0]
    row = lambda width: pl.BlockSpec((1, tm, width), lambda b, t: (b, t, 0))
    p_spec = pl.BlockSpec((None, 1, tm, p_all.shape[-1]), lambda b, t: (layer, b, t, 0))
    return pl.pallas_call(
        _conv_prompt_kernel,
        grid=(B, T // tm),
        in_specs=[row(D), p_spec, _resident(g.shape), _resident(win.shape), _resident(cw.shape),
                  _resident(wo.shape), _resident(pg.shape), _resident(pe.shape)],
        out_specs=[row(D), pl.BlockSpec((1, CONV_W - 1, E), lambda b, t: (b, 0, 0))],
        out_shape=[jax.ShapeDtypeStruct((B, T, D), F32), jax.ShapeDtypeStruct((B, CONV_W - 1, E), F32)],
        scratch_shapes=[pltpu.VMEM((8, E), F32)],
        compiler_params=_params("parallel", "arbitrary"),
        name="conv_prompt",
    )(x, p_all, g, win, cw, wo, pg, pe)


def _conv_sample_kernel(x_ref, p_ref, g_ref, win_ref, cw_ref, wo_ref, pg_ref, pe_ref, s0_ref, s1_ref,
                        xo_ref, cx_ref):
    E = wo_ref.shape[0]
    x = x_ref[0]
    h = _rms(x, g_ref[...]).astype(BF16)
    acc = jnp.zeros(x.shape, F32)
    for c in range(E // CONV_COLS):
        cols = slice(c * CONV_COLS, (c + 1) * CONV_COLS)
        part = lambda i: _mm(h, win_ref[:, i * E + c * CONV_COLS:i * E + (c + 1) * CONV_COLS])
        bg, cg, xb, z = part(0), part(1), part(2), part(3)
        cx = cg * xb
        y = cw_ref[0:1, cols] * s0_ref[:, cols] + cw_ref[1:2, cols] * s1_ref[:, cols] + cw_ref[2:3, cols] * cx
        acc = acc + _mm((bg * y * _silu(z)).astype(BF16), wo_ref[cols, :])
        cx_ref[:, cols] = cx
    xo_ref[0] = _embed(x + acc, p_ref[0], pg_ref, pe_ref)


def _conv_sample(x, p_all, layer, g, win, cw, wo, pg, pe, s0, s1):
    _, M, D = x.shape
    E = wo.shape[0]
    p_spec = pl.BlockSpec((None, 1, M, p_all.shape[-1]), lambda i: (layer, 0, 0, 0))
    return pl.pallas_call(
        _conv_sample_kernel,
        grid=(1,),
        in_specs=[_resident(x.shape), p_spec, _resident(g.shape), _resident(win.shape), _resident(cw.shape),
                  _resident(wo.shape), _resident(pg.shape), _resident(pe.shape),
                  _resident(s0.shape), _resident(s1.shape)],
        out_specs=[_whole(x.shape), _whole((M, E))],
        out_shape=[jax.ShapeDtypeStruct(x.shape, F32), jax.ShapeDtypeStruct((M, E), F32)],
        compiler_params=_params("arbitrary"),
        name="conv_sample",
    )(x, p_all, g, win, cw, wo, pg, pe, s0, s1)


def _rope(x, cos, sin_lo, sin_hi):
    half = ROPE_DIM // 2
    return x * cos + pltpu.roll(x, C_DH - half, 1) * sin_lo + pltpu.roll(x, half, 1) * sin_hi


def _c_in_kernel(x_ref, g_ref, w_ref, cos_ref, slo_ref, shi_ref, *refs, tails, dils):
    n_groups = len(C_GROUPS)
    qkv_refs = refs[:3 * n_groups]
    z_ref = refs[3 * n_groups]
    tail_refs = refs[3 * n_groups + 1:-3]
    slabs = refs[-3:]
    h = _rms(x_ref[0], g_ref[...]).astype(BF16)
    tm = h.shape[0]
    cos, slo, shi = cos_ref[...], slo_ref[...], shi_ref[...]

    for grp in range(n_groups):
        base = 3 * grp * C_WIDTH
        uq = _mm(h, w_ref[:, base:base + C_WIDTH])
        uk = _mm(h, w_ref[:, base + C_WIDTH:base + 2 * C_WIDTH])
        uv = _mm(h, w_ref[:, base + 2 * C_WIDTH:base + 3 * C_WIDTH])
        dil = dils[grp]
        rows = tails[grp]
        out_refs = qkv_refs[3 * grp:3 * grp + 3]
        for hd in range(C_HEADS):
            hs = slice(hd * C_DH, (hd + 1) * C_DH)
            qh = _rope(uq[:, hs], cos, slo, shi) * Q_SCALE
            kh = _rope(uk[:, hs], cos, slo, shi)
            vh = uv[:, hs]
            tail_refs[2 * grp][0, :, hs] = kh[tm - rows:, :]
            tail_refs[2 * grp + 1][0, :, hs] = vh[tm - rows:, :]
            for ref, slab, val in zip(out_refs, slabs, (qh, kh, vh)):
                if dil == 1:
                    ref[0, 0, :, hs] = val.astype(BF16)
                else:
                    slab[hd] = val
        if dil > 1:
            for r in range(dil):
                for hd in range(C_HEADS):
                    hs = slice(hd * C_DH, (hd + 1) * C_DH)
                    for ref, slab in zip(out_refs, slabs):
                        ref[0, r, :, hs] = slab[hd, pl.ds(r, tm // dil, stride=dil), :].astype(BF16)

    zb = 3 * n_groups * C_WIDTH
    z_ref[0] = _mm(h, w_ref[:, zb:zb + C_WIDTH]).astype(BF16)


def _c_in(x, g, w, cos, slo, shi, tm, per_row=False):
    B, T, D = x.shape
    row = lambda width: pl.BlockSpec((1, tm, width), lambda b, t: (b, t, 0))
    table = pl.BlockSpec((tm, C_DH), lambda b, t: (t, 0))
    dils = tuple(1 if per_row else dil for _, dil in C_GROUPS)
    qkv_specs, qkv_shapes = [], []
    for dil in dils:
        qkv_specs += [pl.BlockSpec((1, dil, tm // dil, C_WIDTH), lambda b, t: (b, 0, t, 0))] * 3
        qkv_shapes += [jax.ShapeDtypeStruct((B, dil, T // dil, C_WIDTH), BF16)] * 3
    tails, tail_specs, tail_shapes = [], [], []
    for win, _ in C_GROUPS:
        keep = T if per_row else min(win, T)
        rows = min(tm, keep)
        first_row = T - keep
        tails.append(rows)

        def tail_index(b, t, first_row=first_row, rows=rows):
            return (b, jnp.maximum(((t + 1) * tm - first_row) // rows - 1, 0), 0)

        tail_specs += [pl.BlockSpec((1, rows, C_WIDTH), tail_index)] * 2
        tail_shapes += [jax.ShapeDtypeStruct((B, keep, C_WIDTH), F32)] * 2
    n_qkv = len(qkv_specs)
    outs = pl.pallas_call(
        functools.partial(_c_in_kernel, tails=tuple(tails), dils=dils),
        grid=(B, T // tm),
        in_specs=[row(D), _resident(g.shape), _resident(w.shape), table, table, table],
        out_specs=qkv_specs + [row(C_WIDTH)] + tail_specs,
        out_shape=qkv_shapes + [jax.ShapeDtypeStruct((B, T, C_WIDTH), BF16)] + tail_shapes,
        scratch_shapes=[pltpu.VMEM((C_HEADS, tm, C_DH), F32)] * 3,
        compiler_params=_params("parallel", "arbitrary"),
        name="attn_in",
    )(x, g, w, cos, slo, shi)
    return outs[:n_qkv], outs[n_qkv], outs[n_qkv + 1:]


ATTN_CHUNK = 512


def _attn_chunk_kernel(q_ref, kp_ref, kc_ref, vp_ref, vc_ref, acc_ref, st_ref, *, span):
    chunk = pl.program_id(2)
    qi = lax.broadcasted_iota(jnp.int32, (C_BLOCK, 2 * C_BLOCK), 0)
    kj = lax.broadcasted_iota(jnp.int32, (C_BLOCK, 2 * C_BLOCK), 1)
    dist = qi + C_BLOCK - kj
    band = (dist >= 0) & (dist <= span)
    bias_inner = jnp.where(band, 0.0, -1e30)
    bias_first = jnp.where(band & (kj >= jnp.where(chunk == 0, C_BLOCK, 0)), 0.0, -1e30)
    lane = lax.broadcasted_iota(jnp.int32, (C_BLOCK, LANES), 1)
    ones = jnp.ones((2 * C_BLOCK, LANES), BF16)

    for i in range(q_ref.shape[2] // C_BLOCK):
        rows = slice(i * C_BLOCK, (i + 1) * C_BLOCK)

        def keys(prev_ref, cur_ref, cols):
            if i == 0:
                return jnp.concatenate([prev_ref[0, 0, :, cols], cur_ref[0, 0, rows, cols]], axis=0)
            return cur_ref[0, 0, (i - 1) * C_BLOCK:(i + 1) * C_BLOCK, cols]

        head_cols = [slice(hd * C_DH, (hd + 1) * C_DH) for hd in range(C_HEADS)]
        s_all = [lax.dot_general(q_ref[0, 0, rows, cols], keys(kp_ref, kc_ref, cols), NT_DIMS,
                                 preferred_element_type=F32) for cols in head_cols]
        stats = jnp.zeros((C_BLOCK, LANES), F32)
        for hd, cols in enumerate(head_cols):
            s = s_all[hd] + (bias_first if i == 0 else bias_inner)
            mx = jnp.max(s, axis=-1, keepdims=True)
            p = jnp.exp2(s - mx).astype(BF16)
            both = _mm(p, jnp.concatenate([keys(vp_ref, vc_ref, cols), ones], axis=1))
            acc_ref[0, 0, rows, cols] = both[:, :C_DH].astype(BF16)
            stats = jnp.where(lane == hd, mx, jnp.where(lane == C_HEADS + hd, both[:, C_DH:], stats))
        st_ref[0, 0, rows, :] = stats


def _attn_prompt_group(q, k, v, span):
    B, dil, n, _ = q.shape
    qc = min(n, ATTN_CHUNK)
    per_chunk = qc // C_BLOCK
    cur = lambda rows, width: pl.BlockSpec((1, 1, rows, width), lambda b, r, c: (b, r, c, 0))
    prev = pl.BlockSpec((1, 1, C_BLOCK, C_WIDTH), lambda b, r, c: (b, r, jnp.maximum(c * per_chunk - 1, 0), 0))
    return pl.pallas_call(
        functools.partial(_attn_chunk_kernel, span=span),
        grid=(B, dil, n // qc),
        in_specs=[cur(qc, C_WIDTH), prev, cur(qc, C_WIDTH), prev, cur(qc, C_WIDTH)],
        out_specs=[cur(qc, C_WIDTH), cur(qc, LANES)],
        out_shape=[jax.ShapeDtypeStruct((B, dil, n, C_WIDTH), BF16),
                   jax.ShapeDtypeStruct((B, dil, n, LANES), F32)],
        compiler_params=_params("parallel", "parallel", "arbitrary"),
        name="attn_chunks_d%d" % dil,
    )(q, k, k, v, v)


def _merge_out_kernel(a0_ref, a1_ref, a2_ref, s0_ref, s1_ref, s2_ref, z_ref, x_ref, p_ref,
                      wo_ref, pg_ref, pe_ref, xo_ref, acc_s, st_s):
    tm = x_ref.shape[1]
    for grp, (a_ref, s_ref) in enumerate(zip((a0_ref, a1_ref, a2_ref), (s0_ref, s1_ref, s2_ref))):
        dil = a_ref.shape[1]
        for r in range(dil):
            rows = pl.ds(r, tm // dil, stride=dil) if dil > 1 else slice(None)
            acc = a_ref[0, r].astype(F32)
            for hd in range(C_HEADS):
                acc_s[grp * C_HEADS + hd, rows, :] = acc[:, hd * C_DH:(hd + 1) * C_DH]
            st_s[grp, rows, :] = s_ref[0, r]
    mx = [st_s[grp, :, 0:C_HEADS] for grp in range(3)]
    ls = [st_s[grp, :, C_HEADS:2 * C_HEADS] for grp in range(3)]
    top = jnp.maximum(jnp.maximum(mx[0], mx[1]), mx[2])
    es = [jnp.exp2(m - top) for m in mx]
    den = es[0] * ls[0] + es[1] * ls[1] + es[2] * ls[2]
    coef = [e / den for e in es]
    x1 = x_ref[0]
    for pair in range(C_HEADS // 2):
        ys = []
        for hd in (2 * pair, 2 * pair + 1):
            hs = slice(hd * C_DH, (hd + 1) * C_DH)
            o = coef[0][:, hd:hd + 1] * acc_s[hd]
            for grp in (1, 2):
                o = o + coef[grp][:, hd:hd + 1] * acc_s[grp * C_HEADS + hd]
            ys.append((o * _silu(z_ref[0, :, hs].astype(F32))).astype(BF16))
        x1 = x1 + _mm(jnp.concatenate(ys, axis=1), wo_ref[2 * pair * C_DH:(2 * pair + 2) * C_DH, :])
    xo_ref[0] = _embed(x1, p_ref[0], pg_ref, pe_ref)


def _merge_out(accs, stats, z, x, p_all, layer, wo, pg, pe, tm):
    B, T, D = x.shape
    row = lambda width: pl.BlockSpec((1, tm, width), lambda b, t: (b, t, 0))
    classes = lambda a: pl.BlockSpec((1, a.shape[1], tm // a.shape[1], a.shape[3]), lambda b, t: (b, 0, t, 0))
    p_spec = pl.BlockSpec((None, 1, tm, p_all.shape[-1]), lambda b, t: (layer, b, t, 0))
    return pl.pallas_call(
        _merge_out_kernel,
        grid=(B, T // tm),
        in_specs=[classes(a) for a in accs] + [classes(s) for s in stats] + [row(C_WIDTH), row(D), p_spec,
                  _resident(wo.shape), _resident(pg.shape), _resident(pe.shape)],
        out_specs=row(D),
        out_shape=jax.ShapeDtypeStruct((B, T, D), F32),
        scratch_shapes=[pltpu.VMEM((3 * C_HEADS, tm, C_DH), F32), pltpu.VMEM((3, tm, LANES), F32)],
        compiler_params=_params("parallel", "parallel"),
        name="attn_merge_out",
    )(*accs, *stats, z, x, p_all, wo, pg, pe)


def _attn_sample_kernel(*refs):
    n_groups = len(C_GROUPS)
    z_ref, y_ref = refs[5 * n_groups], refs[5 * n_groups + 1]
    parts = []
    for grp in range(n_groups):
        q_ref, kn_ref, vn_ref, kc_ref, vc_ref = refs[5 * grp:5 * grp + 5]
        q = q_ref[0].astype(F32)
        s_old = jnp.sum(kc_ref[0] * q[None], axis=-1, keepdims=True)
        s_new = jnp.sum(kn_ref[0] * q, axis=-1, keepdims=True)
        mx = jnp.maximum(jnp.max(s_old, axis=0), s_new)
        p_old = jnp.exp2(s_old - mx[None])
        p_new = jnp.exp2(s_new - mx)
        l = jnp.sum(p_old, axis=0) + p_new
        acc = jnp.sum(p_old * vc_ref[0], axis=0) + p_new * vn_ref[0]
        parts.append((acc, mx, l))
    top = jnp.maximum(jnp.maximum(parts[0][1], parts[1][1]), parts[2][1])
    es = [jnp.exp2(m - top) for _, m, _ in parts]
    num = es[0] * parts[0][0] + es[1] * parts[1][0] + es[2] * parts[2][0]
    den = es[0] * parts[0][2] + es[1] * parts[1][2] + es[2] * parts[2][2]
    y_ref[0] = (num / den * _silu(z_ref[0].astype(F32))).astype(BF16)


def _attn_sample(qkv, tails, caches, z):
    Bs = z.shape[1]
    heads = lambda a: a.reshape(Bs, C_HEADS, C_DH)
    row = pl.BlockSpec((1, C_HEADS, C_DH), lambda b: (b, 0, 0))
    args, specs = [], []
    for grp, (win, dil) in enumerate(C_GROUPS):
        kc, vc = caches[grp]
        n_buf = kc.shape[1]
        assert n_buf == win and win % dil == 0, "sample window must be fully cached"
        span = win // dil
        view = lambda a: a.reshape(Bs, span, dil, C_HEADS, C_DH)
        cache_spec = pl.BlockSpec((1, span, None, C_HEADS, C_DH), lambda b: (b, 0, 0, 0, 0))
        args += [heads(qkv[3 * grp]), heads(tails[2 * grp]), heads(tails[2 * grp + 1]), view(kc), view(vc)]
        specs += [row, row, row, cache_spec, cache_spec]
    return pl.pallas_call(
        _attn_sample_kernel,
        grid=(Bs,),
        in_specs=specs + [row],
        out_specs=row,
        out_shape=jax.ShapeDtypeStruct((Bs, C_HEADS, C_DH), BF16),
        compiler_params=_params("parallel"),
        name="attn_sample",
    )(*args, heads(z))


def _pool_tail(x, acc, p, pg_ref, pe_ref, fg_ref):
    return _rms(_embed(x + acc, p, pg_ref, pe_ref), fg_ref[...])


def _pool_prompt_kernel(x_ref, p_ref, g_ref, win_ref, wgrp_ref, sc_ref, wo_ref, pg_ref, pe_ref, fg_ref,
                        xo_ref, st_ref, pad_s):
    E = wo_ref.shape[0]
    G = E // len(D_WINDOWS)
    t = pl.program_id(1)

    @pl.when(t == 0)
    def _():
        pad_s[0:POOL_PAD, :] = jnp.zeros((POOL_PAD, E), F32)

    x = x_ref[0]
    tm = x.shape[0]
    h = _rms(x, g_ref[...]).astype(BF16)
    pos = t * tm + lax.broadcasted_iota(jnp.int32, (tm, 1), 0)
    acc = jnp.zeros(x.shape, F32)
    for grp, w in enumerate(D_WINDOWS):
        cols = slice(grp * G, (grp + 1) * G)
        xp = _mm(h, win_ref[:, cols])
        z = _mm(h, win_ref[:, E + grp * G:E + (grp + 1) * G])
        pad_s[POOL_PAD:, cols] = xp
        wsum = pad_s[:, cols]
        shift = 1
        while shift < w:
            wsum = wsum + pltpu.roll(wsum, shift, 0)
            shift *= 2
        cnt = jnp.minimum(w, pos + 1).astype(F32)
        r = wsum[POOL_PAD:, :] / cnt - xp
        y = _mm(r.astype(BF16), wgrp_ref[grp]) * sc_ref[:, cols]
        acc = acc + _mm((y * _silu(z)).astype(BF16), wo_ref[cols, :])
        pad_s[0:POOL_PAD, cols] = xp[tm - POOL_PAD:, :]
    xo_ref[0] = _pool_tail(x, acc, p_ref[0], pg_ref, pe_ref, fg_ref)
    st_ref[0] = pad_s[0:POOL_PAD, :]


def _pool_prompt(x, p_all, layer, g, win, wgrp, sc, wo, pg, pe, fg, tm):
    B, T, D = x.shape
    E = wo.shape[0]
    row = lambda width: pl.BlockSpec((1, tm, width), lambda b, t: (b, t, 0))
    p_spec = pl.BlockSpec((None, 1, tm, p_all.shape[-1]), lambda b, t: (layer, b, t, 0))
    return pl.pallas_call(
        _pool_prompt_kernel,
        grid=(B, T // tm),
        in_specs=[row(D), p_spec] + [_resident(a.shape) for a in (g, win, wgrp, sc, wo, pg, pe, fg)],
        out_specs=[row(D), pl.BlockSpec((1, POOL_PAD, E), lambda b, t: (b, 0, 0))],
        out_shape=[jax.ShapeDtypeStruct((B, T, D), F32), jax.ShapeDtypeStruct((B, POOL_PAD, E), F32)],
        scratch_shapes=[pltpu.VMEM((POOL_PAD + tm, E), F32)],
        compiler_params=_params("parallel", "arbitrary"),
        name="pool_prompt",
    )(x, p_all, g, win, wgrp, sc, wo, pg, pe, fg)


def _pool_sample_kernel(x_ref, p_ref, g_ref, win_ref, wgrp_ref, sc_ref, wo_ref, pg_ref, pe_ref, fg_ref, st_ref,
                        xo_ref, xp_ref, *, pos):
    E = wo_ref.shape[0]
    G = E // len(D_WINDOWS)
    x = x_ref[0]
    h = _rms(x, g_ref[...]).astype(BF16)
    acc = jnp.zeros(x.shape, F32)
    for grp, w in enumerate(D_WINDOWS):
        cols = slice(grp * G, (grp + 1) * G)
        xp = _mm(h, win_ref[:, cols])
        z = _mm(h, win_ref[:, E + grp * G:E + (grp + 1) * G])
        wsum = xp
        for back in range(1, w):
            lo = (POOL_STATE - back) * E + grp * G
            wsum = wsum + st_ref[:, lo:lo + G]
        r = wsum / float(min(w, pos + 1)) - xp
        y = _mm(r.astype(BF16), wgrp_ref[grp]) * sc_ref[:, cols]
        acc = acc + _mm((y * _silu(z)).astype(BF16), wo_ref[cols, :])
        xp_ref[:, cols] = xp
    xo_ref[0] = _pool_tail(x, acc, p_ref[0], pg_ref, pe_ref, fg_ref)


def _pool_sample(x, p_all, layer, g, win, wgrp, sc, wo, pg, pe, fg, st, pos):
    _, M, D = x.shape
    E = wo.shape[0]
    p_spec = pl.BlockSpec((None, 1, M, p_all.shape[-1]), lambda i: (layer, 0, 0, 0))
    return pl.pallas_call(
        functools.partial(_pool_sample_kernel, pos=pos),
        grid=(1,),
        in_specs=[_resident(x.shape), p_spec] + [_resident(a.shape) for a in (g, win, wgrp, sc, wo, pg, pe, fg, st)],
        out_specs=[_whole(x.shape), _whole((M, E))],
        out_shape=[jax.ShapeDtypeStruct(x.shape, F32), jax.ShapeDtypeStruct((M, E), F32)],
        compiler_params=_params("arbitrary"),
        name="pool_sample",
    )(x, p_all, g, win, wgrp, sc, wo, pg, pe, fg, st)


def _rope_tables(pos):
    half = ROPE_DIM // 2
    inv = ROPE_THETA ** (-jnp.arange(half, dtype=F32) / half)
    ang = pos.astype(F32)[:, None] * inv[None, :]
    cos, sin = jnp.cos(ang), jnp.sin(ang)
    rest = C_DH - ROPE_DIM
    n = pos.shape[0]
    cos_t = jnp.concatenate([cos, cos, jnp.ones((n, rest), F32)], axis=1)
    sin_lo = jnp.concatenate([-sin, jnp.zeros((n, half + rest), F32)], axis=1)
    sin_hi = jnp.concatenate([jnp.zeros((n, half), F32), sin, jnp.zeros((n, rest), F32)], axis=1)
    return cos_t, sin_lo, sin_hi


def _row(a):
    return a.reshape(1, -1).astype(F32)


def kernel(x_prompt, x_sample, state_mlstm_C, state_mlstm_n, state_mlstm_m, state_conv, cache_k_w128, cache_v_w128, cache_k_w512, cache_v_w512, cache_k_w2048, cache_v_w2048, state_pool, p_prompt, p_sample, norm_g, pe_w, pg_w, final_g, a_w_in, a_b_if, a_norm_g, a_w_out, b_w_in, b_conv_w, b_w_out, c_w_in, c_w_out, d_w_in, d_w_grp, d_scale, d_w_out):
    B, T, D = x_prompt.shape
    Bs, Ts, _ = x_sample.shape
    assert Ts == 1 and norm_g.shape[0] == 4 and a_w_in.shape[0] == 1
    assert T % 512 == 0 and all(T % (dil * C_BLOCK) == 0 for _, dil in C_GROUPS)
    assert Bs % 8 == 0

    bf = lambda a: a.astype(BF16)
    qk = A_HEADS * A_DK
    n_main = 2 * qk + 3 * A_WIDTH
    a_w = bf(a_w_in[0])
    a_wg = bf(jnp.pad(a_w_in[0, :, n_main:], ((0, 0), (0, LANES - 2 * A_HEADS))))
    a_bif = jnp.pad(a_b_if[0], (0, LANES - 2 * A_HEADS)).reshape(1, LANES)
    a_ng = _row(a_norm_g[0])
    a_wo, b_wi, b_wo, c_wi, c_wo, d_wi, d_wg, d_wo = (
        bf(a_w_out[0]), bf(b_w_in[0]), bf(b_w_out[0]), bf(c_w_in[0]), bf(c_w_out[0]),
        bf(d_w_in[0]), bf(d_w_grp[0]), bf(d_w_out[0]))
    pg, pe = bf(pg_w), bf(pe_w)
    gs = [_row(norm_g[i]) for i in range(4)]
    fg = _row(final_g)
    b_cw = b_conv_w[0]
    d_sc = _row(d_scale[0])
    caches = [(cache_k_w128[0], cache_v_w128[0]), (cache_k_w512[0], cache_v_w512[0]),
              (cache_k_w2048[0], cache_v_w2048[0])]

    x = x_prompt
    q, k, v, oz, gates, kt = _a_in(x, gs[0], a_w, a_wg, a_bif, 512, transposed_k=True)
    hh, c_p, n_p, m_p = _mlstm_prompt(q, k, kt, v, gates)
    x = _a_out(hh, oz, a_ng, x, p_prompt, 0, a_wo, pg[0], pe[0], 512)
    x, conv_p = _conv_prompt(x, p_prompt, 1, gs[1], b_wi, b_cw, b_wo, pg[1], pe[1], 512)
    qkv, z, tails_p = _c_in(x, gs[2], c_wi, *_rope_tables(jnp.arange(T)), 256)
    accs, stats = [], []
    for grp, (win, dil) in enumerate(C_GROUPS):
        acc, st = _attn_prompt_group(qkv[3 * grp], qkv[3 * grp + 1], qkv[3 * grp + 2], win // dil)
        accs.append(acc)
        stats.append(st)
    x = _merge_out(accs, stats, z, x, p_prompt, 2, c_wo, pg[2], pe[2], 512)
    y_prompt, pool_p = _pool_prompt(x, p_prompt, 3, gs[3], d_wi, d_wg, d_sc, d_wo, pg[3], pe[3], fg, 512)

    xs = x_sample.reshape(1, Bs, D)
    ps = p_sample.reshape(p_sample.shape[0], 1, Bs, p_sample.shape[-1])
    q, k, v, oz, gates = _a_in(xs, gs[0], a_w, a_wg, a_bif, Bs)
    m_lanes = jnp.pad(state_mlstm_m[0], ((0, 0), (0, LANES - A_HEADS)))
    hh, c_s, n_s, m_s = _mlstm_sample(q[0], k[0], v[0], gates[0], state_mlstm_C[0], state_mlstm_n[0], m_lanes)
    m_s = m_s[:, :, 0].T
    xs = _a_out(hh.reshape(1, Bs, A_WIDTH), oz, a_ng, xs, ps, 0, a_wo, pg[0], pe[0], Bs)
    xs, cx = _conv_sample(xs, ps, 1, gs[1], b_wi, b_cw, b_wo, pg[1], pe[1],
                          state_conv[0, :, 0, :], state_conv[0, :, 1, :])
    conv_s = jnp.stack([state_conv[0, :, 1, :], cx], axis=1)
    tables = [jnp.broadcast_to(t, (Bs, C_DH)) for t in _rope_tables(PAST_LEN + jnp.arange(1))]
    qkv, z, tails_s = _c_in(xs, gs[2], c_wi, *tables, Bs, per_row=True)
    y = _attn_sample(qkv, tails_s, caches, z)
    xs = _out_embed(y.reshape(1, Bs, C_WIDTH), xs, ps, 2, c_wo, pg[2], pe[2], Bs)
    pool_flat = state_pool[0].reshape(Bs, POOL_STATE * state_pool.shape[-1])
    ys, xp = _pool_sample(xs, ps, 3, gs[3], d_wi, d_wg, d_sc, d_wo, pg[3], pe[3], fg, pool_flat, PAST_LEN)
    pool_s = jnp.concatenate([state_pool[0, :, 1:, :], xp[:, None, :]], axis=1)

    heads = lambda a, lead: a.reshape(1, lead, a.shape[1], C_HEADS, C_DH)
    kv_out = []
    for grp in range(len(C_GROUPS)):
        for j in range(2):
            kv_out += [heads(tails_p[2 * grp + j], B), tails_s[2 * grp + j].reshape(1, Bs, 1, C_HEADS, C_DH)]
    return (y_prompt, ys.reshape(Bs, 1, D),
            c_p[None], c_s[None], n_p[None], n_s[None], m_p[None, :, 0, :A_HEADS], m_s.reshape(1, Bs, A_HEADS),
            conv_p[None], conv_s[None],
            *kv_out,
            pool_p[None, :, 1:, :], pool_s[None])
```

```python
import functools

import jax
import jax.numpy as jnp
from jax import lax
from jax.experimental import pallas as pl
from jax.experimental.pallas import tpu as pltpu

F32 = jnp.float32
BF16 = jnp.bfloat16

EPS = 1e-6
PAST_LEN = 8192

A_HEADS = 8
A_DK = 128
A_DV = 256
A_WIDTH = A_HEADS * A_DV
A_CHUNK = 128
A_CHUNKS_PER_STEP = 2

CONV_W = 3

C_HEADS = 8
C_DH = 128
C_WIDTH = C_HEADS * C_DH
C_GROUPS = ((128, 1), (512, 4), (2048, 16))
C_BLOCK = 128
ROPE_DIM = C_DH // 4
ROPE_THETA = 500000.0
LOG2_E = 1.4426950408889634
Q_SCALE = C_DH ** -0.5 * LOG2_E

D_WINDOWS = (2, 4, 8, 16)
POOL_STATE = max(D_WINDOWS) - 1
POOL_PAD = POOL_STATE + 1

LANES = 128
VMEM_LIMIT_BYTES = 56 * 2 ** 20

NT_DIMS = (((1,), (1,)), ((), ()))
TN_DIMS = (((0,), (0,)), ((), ()))


def _params(*semantics):
    return pltpu.CompilerParams(dimension_semantics=semantics, vmem_limit_bytes=VMEM_LIMIT_BYTES)


def _resident(shape):
    zeros = (0,) * len(shape)
    return pl.BlockSpec(shape, lambda *_: zeros, pipeline_mode=pl.Buffered(1))


def _whole(shape):
    zeros = (0,) * len(shape)
    return pl.BlockSpec(shape, lambda *_: zeros)


def _mm(a, b):
    return jnp.dot(a, b, preferred_element_type=F32)


def _rms(x, g):
    return x * lax.rsqrt(jnp.mean(x * x, axis=-1, keepdims=True) + EPS) * g


def _sigmoid(x):
    return 0.5 * jnp.tanh(0.5 * x) + 0.5


def _silu(x):
    return x * _sigmoid(x)


def _log_sigmoid(x):
    return jnp.minimum(x, 0.0) - jnp.log(1.0 + jnp.exp(-jnp.abs(x)))


def _embed(x1, p, pg_ref, pe_ref):
    gate = _sigmoid(_mm(x1.astype(BF16), pg_ref[...]))
    return x1 + gate * _mm(p.astype(BF16), pe_ref[...])


def _a_in_kernel(x_ref, g_ref, w_ref, wg_ref, bif_ref,
                 q_ref, k_ref, v_ref, oz_ref, gate_ref, *maybe_kt_ref):
    h = _rms(x_ref[0], g_ref[...]).astype(BF16)
    qk = A_HEADS * A_DK

    def proj(lo, width):
        return _mm(h, w_ref[:, lo:lo + width])

    q_ref[0] = proj(0, qk).astype(BF16)
    k = proj(qk, qk) * (A_DK ** -0.5)
    k_ref[0] = k.astype(BF16)
    for kt_ref in maybe_kt_ref:
        kt_ref[0] = k.T.astype(BF16)
    v_ref[0] = proj(2 * qk, A_WIDTH).astype(BF16)
    oz_ref[0] = (_sigmoid(proj(2 * qk + A_WIDTH, A_WIDTH))
                 * _silu(proj(2 * qk + 2 * A_WIDTH, A_WIDTH))).astype(BF16)
    gates = _mm(h, wg_ref[...]) + bif_ref[...]
    lane = lax.broadcasted_iota(jnp.int32, gates.shape, 1)
    gate_ref[0] = jnp.where(lane < A_HEADS, gates, _log_sigmoid(gates))


def _a_in(x, g, w, wg, bif, tm, transposed_k=False):
    B, T, D = x.shape
    qk = A_HEADS * A_DK
    row = lambda width: pl.BlockSpec((1, tm, width), lambda b, t: (b, t, 0))
    out_specs = [row(qk), row(qk), row(A_WIDTH), row(A_WIDTH), row(LANES)]
    out_shape = [jax.ShapeDtypeStruct((B, T, qk), BF16), jax.ShapeDtypeStruct((B, T, qk), BF16),
                 jax.ShapeDtypeStruct((B, T, A_WIDTH), BF16), jax.ShapeDtypeStruct((B, T, A_WIDTH), BF16),
                 jax.ShapeDtypeStruct((B, T, LANES), F32)]
    if transposed_k:
        out_specs.append(pl.BlockSpec((1, qk, tm), lambda b, t: (b, 0, t)))
        out_shape.append(jax.ShapeDtypeStruct((B, qk, T), BF16))
    return pl.pallas_call(
        _a_in_kernel,
        grid=(B, T // tm),
        in_specs=[row(D), _resident(g.shape), _resident(w.shape), _resident(wg.shape), _resident(bif.shape)],
        out_specs=out_specs,
        out_shape=out_shape,
        compiler_params=_params("parallel", "parallel"),
        name="mlstm_in",
    )(x, g, w, wg, bif)


def _mlstm_chunk_kernel(q_ref, k_ref, kt_ref, v_ref, gate_ref,
                        hh_ref, c_out, n_out, m_out, m_s, *state_s):
    chunk = pl.program_id(1)
    ct_s, nb_s = state_s[:A_HEADS], state_s[A_HEADS:]

    @pl.when(chunk == 0)
    def _():
        for ref in state_s:
            ref[...] = jnp.zeros_like(ref)
        m_s[...] = jnp.zeros_like(m_s)

    L = A_CHUNK
    row = lax.broadcasted_iota(jnp.int32, (L, LANES), 0)
    ti = lax.broadcasted_iota(jnp.int32, (L, L), 0)
    si = lax.broadcasted_iota(jnp.int32, (L, L), 1)
    causal = ti >= si
    ones = jnp.ones((L, LANES), BF16)
    heads = range(A_HEADS)
    ks = [slice(h * A_DK, (h + 1) * A_DK) for h in heads]
    vs = [slice(h * A_DV, (h + 1) * A_DV) for h in heads]

    def prefix(x, op, identity):
        step = 1
        while step < L:
            x = op(x, jnp.where(row >= step, pltpu.roll(x, step, 0), identity))
            step *= 2
        return x

    def one_chunk(rows):
        gates = gate_ref[0, rows, :]
        b = pltpu.roll(prefix(gates, jnp.add, 0.0), LANES - A_HEADS, 1)
        c = gates - b
        m_row = m_s[0:1, :]
        top = jnp.maximum(m_row, prefix(c, jnp.maximum, -jnp.inf))
        top_last = top[L - 1:L, :]
        g_all = jnp.exp(m_row - top)
        floor_all = jnp.exp(-(b + top))
        w_all = jnp.exp(c - top_last)
        decay_row = jnp.exp(m_row - top_last)
        m_s[...] = jnp.broadcast_to(b[L - 1:L, :] + top_last, m_s.shape)
        c_t = c.T

        s_raw = [_mm(q_ref[0, rows, ks[h]], kt_ref[0, ks[h], rows]) for h in heads]
        upd = []
        for h in heads:
            w_b = jnp.broadcast_to(w_all[:, h:h + 1], (L, LANES))
            wv = jnp.concatenate([w_b, w_b], axis=1) * v_ref[0, rows, vs[h]].astype(F32)
            upd.append(_mm(kt_ref[0, ks[h], rows], jnp.concatenate([wv.astype(BF16), w_b.astype(BF16)], axis=1)))

        for h in heads:
            ct = ct_s[h][...]
            nb = nb_s[h][...]
            dmat = jnp.exp(jnp.where(causal, -top[:, h:h + 1] + c_t[h:h + 1, :], -jnp.inf))
            s = s_raw[h] * dmat
            gq = g_all[:, h:h + 1] * q_ref[0, rows, ks[h]].astype(F32)
            lhs = jnp.concatenate([s.astype(BF16), gq.astype(BF16)], axis=1)
            num = _mm(lhs, jnp.concatenate([v_ref[0, rows, vs[h]], ct.astype(BF16)], axis=0))
            den = _mm(lhs, jnp.concatenate([ones, nb.astype(BF16)], axis=0))
            scale = 1.0 / jnp.maximum(jnp.abs(den), floor_all[:, h:h + 1])
            hh_ref[0, rows, vs[h]] = (num * jnp.concatenate([scale, scale], axis=1)).astype(BF16)
            decay = decay_row[:, h:h + 1]
            ct_s[h][...] = decay * ct + upd[h][:, :A_DV]
            nb_s[h][...] = decay * nb + upd[h][:, A_DV:]

    for sub in range(q_ref.shape[1] // L):
        one_chunk(slice(sub * L, (sub + 1) * L))

    @pl.when(chunk == pl.num_programs(1) - 1)
    def _():
        for h in range(A_HEADS):
            c_out[0, h] = ct_s[h][...].T
            n_out[0, h:h + 1, :] = nb_s[h][...].T[0:1, :]
        m_out[0] = m_s[...]


def _mlstm_prompt(q, k, kt, v, gates):
    B, T, _ = q.shape
    L = A_CHUNK * A_CHUNKS_PER_STEP
    qk = A_HEADS * A_DK
    row = lambda width: pl.BlockSpec((1, L, width), lambda b, c: (b, c, 0))
    state = lambda *shape: pl.BlockSpec((1,) + shape, lambda b, c: (b,) + (0,) * len(shape))
    return pl.pallas_call(
        _mlstm_chunk_kernel,
        grid=(B, T // L),
        in_specs=[row(qk), row(qk), pl.BlockSpec((1, qk, L), lambda b, c: (b, 0, c)), row(A_WIDTH), row(LANES)],
        out_specs=[row(A_WIDTH), state(A_HEADS, A_DV, A_DK), state(A_HEADS, A_DK), state(A_HEADS, LANES)],
        out_shape=[jax.ShapeDtypeStruct((B, T, A_WIDTH), BF16),
                   jax.ShapeDtypeStruct((B, A_HEADS, A_DV, A_DK), F32),
                   jax.ShapeDtypeStruct((B, A_HEADS, A_DK), F32),
                   jax.ShapeDtypeStruct((B, A_HEADS, LANES), F32)],
        scratch_shapes=[pltpu.VMEM((A_HEADS, LANES), F32)] + [pltpu.VMEM((A_DK, A_DV), F32)] * A_HEADS
        + [pltpu.VMEM((A_DK, LANES), F32)] * A_HEADS,
        compiler_params=_params("parallel", "arbitrary"),
        name="mlstm_chunks",
    )(q, k, kt, v, gates)


def _mlstm_step_kernel(q_ref, k_ref, v_ref, gate_ref, m_ref, c_ref, n_ref,
                       hh_ref, c_out, n_out, m_out, hht_s):
    head = pl.program_id(0)
    Bs = q_ref.shape[0]
    q = q_ref[...].astype(F32)
    k = k_ref[...].astype(F32)
    n = n_ref[...]
    lane = lax.broadcasted_iota(jnp.int32, (Bs, LANES), 1)
    pick = lambda a, idx: jnp.sum(jnp.where(lane == idx, a, 0.0), axis=-1, keepdims=True)
    ig = pick(gate_ref[...], head)
    lf = pick(gate_ref[...], A_HEADS + head)
    m_prev = pick(m_ref[...], head)

    inter = lf + m_prev
    m_t = jnp.maximum(inter, ig)
    dm = jnp.exp(ig - m_t)
    g = jnp.exp(inter - m_t)
    s = jnp.sum(q * k, axis=-1, keepdims=True) * dm
    den = s + g * jnp.sum(q * n, axis=-1, keepdims=True)
    scale = 1.0 / jnp.maximum(jnp.abs(den), jnp.exp(-m_t))
    n_out[...] = g * n + dm * k
    m_out[...] = jnp.broadcast_to(m_t, (Bs, LANES))

    v_pad = jnp.concatenate([v_ref[...].astype(F32), jnp.zeros((LANES - Bs, A_DV), F32)], axis=0)
    vt = v_pad.T
    hht_s[...] = jnp.zeros_like(hht_s)
    for b in range(Bs):
        c_b = c_ref[b]
        one = slice(b, b + 1)
        v_col = vt[:, one]
        cq = jnp.sum(c_b * q[one, :], axis=-1, keepdims=True)
        hht_s[:, one] = (s[one, :] * v_col + g[one, :] * cq) * scale[one, :]
        c_out[b] = g[one, :] * c_b + (dm[one, :] * v_col) * k[one, :]
    hh_ref[...] = hht_s[...].T[0:Bs, :].astype(BF16)


def _mlstm_sample(q, k, v, gates, c0, n0, m0):
    Bs = q.shape[0]
    assert Bs <= LANES
    cols = lambda width: pl.BlockSpec((Bs, width), lambda h: (0, h))
    full = pl.BlockSpec((Bs, LANES), lambda h: (0, 0))
    c_spec = pl.BlockSpec((Bs, None, A_DV, A_DK), lambda h: (0, h, 0, 0))
    hh, c1, n1, m1 = pl.pallas_call(
        _mlstm_step_kernel,
        grid=(A_HEADS,),
        in_specs=[cols(A_DK), cols(A_DK), cols(A_DV), full, full, c_spec, cols(A_DK)],
        out_specs=[cols(A_DV), c_spec, cols(A_DK), pl.BlockSpec((None, Bs, LANES), lambda h: (h, 0, 0))],
        out_shape=[jax.ShapeDtypeStruct((Bs, A_WIDTH), BF16),
                   jax.ShapeDtypeStruct(c0.shape, F32), jax.ShapeDtypeStruct((Bs, A_HEADS * A_DK), F32),
                   jax.ShapeDtypeStruct((A_HEADS, Bs, LANES), F32)],
        scratch_shapes=[pltpu.VMEM((A_DV, LANES), F32)],
        compiler_params=_params("parallel"),
        name="mlstm_step",
    )(q, k, v, gates, m0, c0, n0.reshape(Bs, A_HEADS * A_DK))
    return hh, c1, n1.reshape(n0.shape), m1


def _a_out_kernel(hh_ref, oz_ref, ng_ref, x_ref, p_ref, wo_ref, pg_ref, pe_ref, xo_ref):
    x1 = x_ref[0]
    for h in range(A_HEADS):
        vs = slice(h * A_DV, (h + 1) * A_DV)
        hh = hh_ref[0, :, vs].astype(F32)
        hn = hh * lax.rsqrt(jnp.mean(hh * hh, axis=-1, keepdims=True) + EPS) * ng_ref[:, vs]
        y = hn * oz_ref[0, :, vs].astype(F32)
        x1 = x1 + _mm(y.astype(BF16), wo_ref[vs, :])
    xo_ref[0] = _embed(x1, p_ref[0], pg_ref, pe_ref)


def _a_out(hh, oz, ng, x, p_all, layer, wo, pg, pe, tm):
    B, T, D = x.shape
    row = lambda width: pl.BlockSpec((1, tm, width), lambda b, t: (b, t, 0))
    p_spec = pl.BlockSpec((None, 1, tm, p_all.shape[-1]), lambda b, t: (layer, b, t, 0))
    return pl.pallas_call(
        _a_out_kernel,
        grid=(B, T // tm),
        in_specs=[row(A_WIDTH), row(A_WIDTH), _resident(ng.shape), row(D), p_spec,
                  _resident(wo.shape), _resident(pg.shape), _resident(pe.shape)],
        out_specs=row(D),
        out_shape=jax.ShapeDtypeStruct((B, T, D), F32),
        compiler_params=_params("parallel", "parallel"),
        name="mlstm_out",
    )(hh, oz, ng, x, p_all, wo, pg, pe)


def _out_kernel(y_ref, x_ref, p_ref, wo_ref, pg_ref, pe_ref, xo_ref):
    x1 = x_ref[0] + _mm(y_ref[0], wo_ref[...])
    xo_ref[0] = _embed(x1, p_ref[0], pg_ref, pe_ref)


def _out_embed(y, x, p_all, layer, wo, pg, pe, tm):
    B, T, D = x.shape
    row = lambda width: pl.BlockSpec((1, tm, width), lambda b, t: (b, t, 0))
    p_spec = pl.BlockSpec((None, 1, tm, p_all.shape[-1]), lambda b, t: (layer, b, t, 0))
    return pl.pallas_call(
        _out_kernel,
        grid=(B, T // tm),
        in_specs=[row(y.shape[-1]), row(D), p_spec, _resident(wo.shape), _resident(pg.shape), _resident(pe.shape)],
        out_specs=row(D),
        out_shape=jax.ShapeDtypeStruct((B, T, D), F32),
        compiler_params=_params("parallel", "parallel"),
        name="out_embed",
    )(y, x, p_all, wo, pg, pe)


CONV_COLS = 512


def _conv_prompt_kernel(x_ref, p_ref, g_ref, win_ref, cw_ref, wo_ref, pg_ref, pe_ref,
                        xo_ref, st_ref, carry_s):
    E = wo_ref.shape[0]

    @pl.when(pl.program_id(1) == 0)
    def _():
        carry_s[...] = jnp.zeros_like(carry_s)

    x = x_ref[0]
    tm = x.shape[0]
    h = _rms(x, g_ref[...]).astype(BF16)
    row = lax.broadcasted_iota(jnp.int32, (tm, CONV_COLS), 0)
    acc = jnp.zeros(x.shape, F32)
    for c in range(E // CONV_COLS):
        cols = slice(c * CONV_COLS, (c + 1) * CONV_COLS)
        part = lambda i: _mm(h, win_ref[:, i * E + c * CONV_COLS:i * E + (c + 1) * CONV_COLS])
        bg, cg, xb, z = part(0), part(1), part(2), part(3)
        cx = cg * xb
        old = carry_s[0:1, cols]
        new = carry_s[1:2, cols]
        prev1 = jnp.where(row == 0, new, pltpu.roll(cx, 1, 0))
        prev2 = jnp.where(row == 0, old, jnp.where(row == 1, new, pltpu.roll(cx, 2, 0)))
        y = cw_ref[0:1, cols] * prev2 + cw_ref[1:2, cols] * prev1 + cw_ref[2:3, cols] * cx
        acc = acc + _mm((bg * y * _silu(z)).astype(BF16), wo_ref[cols, :])
        carry_s[0:2, cols] = cx[tm - 2:tm, :]
    xo_ref[0] = _embed(x + acc, p_ref[0], pg_ref, pe_ref)
    st_ref[0] = carry_s[0:2, :]


def _conv_prompt(x, p_all, layer, g, win, cw, wo, pg, pe, tm):
    B, T, D = x.shape
    E = wo.shape[0]
    row = lambda width: pl.BlockSpec((1, tm, width), lambda b, t: (b, t, 0))
    p_spec = pl.BlockSpec((None, 1, tm, p_all.shape[-1]), lambda b, t: (layer, b, t, 0))
    return pl.pallas_call(
        _conv_prompt_kernel,
        grid=(B, T // tm),
        in_specs=[row(D), p_spec, _resident(g.shape), _resident(win.shape), _resident(cw.shape),
                  _resident(wo.shape), _resident(pg.shape), _resident(pe.shape)],
        out_specs=[row(D), pl.BlockSpec((1, CONV_W - 1, E), lambda b, t: (b, 0, 0))],
        out_shape=[jax.ShapeDtypeStruct((B, T, D), F32), jax.ShapeDtypeStruct((B, CONV_W - 1, E), F32)],
        scratch_shapes=[pltpu.VMEM((8, E), F32)],
        compiler_params=_params("parallel", "arbitrary"),
        name="conv_prompt",
    )(x, p_all, g, win, cw, wo, pg, pe)


def _conv_sample_kernel(x_ref, p_ref, g_ref, win_ref, cw_ref, wo_ref, pg_ref, pe_ref, s0_ref, s1_ref,
                        xo_ref, cx_ref):
    E = wo_ref.shape[0]
    x = x_ref[0]
    h = _rms(x, g_ref[...]).astype(BF16)
    acc = jnp.zeros(x.shape, F32)
    for c in range(E // CONV_COLS):
        cols = slice(c * CONV_COLS, (c + 1) * CONV_COLS)
        part = lambda i: _mm(h, win_ref[:, i * E + c * CONV_COLS:i * E + (c + 1) * CONV_COLS])
        bg, cg, xb, z = part(0), part(1), part(2), part(3)
        cx = cg * xb
        y = cw_ref[0:1, cols] * s0_ref[:, cols] + cw_ref[1:2, cols] * s1_ref[:, cols] + cw_ref[2:3, cols] * cx
        acc = acc + _mm((bg * y * _silu(z)).astype(BF16), wo_ref[cols, :])
        cx_ref[:, cols] = cx
    xo_ref[0] = _embed(x + acc, p_ref[0], pg_ref, pe_ref)


def _conv_sample(x, p_all, layer, g, win, cw, wo, pg, pe, s0, s1):
    _, M, D = x.shape
    E = wo.shape[0]
    p_spec = pl.BlockSpec((None, 1, M, p_all.shape[-1]), lambda i: (layer, 0, 0, 0))
    return pl.pallas_call(
        _conv_sample_kernel,
        grid=(1,),
        in_specs=[_resident(x.shape), p_spec, _resident(g.shape), _resident(win.shape), _resident(cw.shape),
                  _resident(wo.shape), _resident(pg.shape), _resident(pe.shape),
                  _resident(s0.shape), _resident(s1.shape)],
        out_specs=[_whole(x.shape), _whole((M, E))],
        out_shape=[jax.ShapeDtypeStruct(x.shape, F32), jax.ShapeDtypeStruct((M, E), F32)],
        compiler_params=_params("arbitrary"),
        name="conv_sample",
    )(x, p_all, g, win, cw, wo, pg, pe, s0, s1)


def _rope(x, cos, sin_lo, sin_hi):
    half = ROPE_DIM // 2
    return x * cos + pltpu.roll(x, C_DH - half, 1) * sin_lo + pltpu.roll(x, half, 1) * sin_hi


def _c_in_kernel(x_ref, g_ref, w_ref, cos_ref, slo_ref, shi_ref, *refs, tails, dils):
    n_groups = len(C_GROUPS)
    qkv_refs = refs[:3 * n_groups]
    z_ref = refs[3 * n_groups]
    tail_refs = refs[3 * n_groups + 1:-6]
    slabs, halves = refs[-6:-3], refs[-3:]
    h = _rms(x_ref[0], g_ref[...]).astype(BF16)
    tm = h.shape[0]
    cos, slo, shi = cos_ref[...], slo_ref[...], shi_ref[...]

    for grp in range(n_groups):
        base = 3 * grp * C_WIDTH
        uq = _mm(h, w_ref[:, base:base + C_WIDTH])
        uk = _mm(h, w_ref[:, base + C_WIDTH:base + 2 * C_WIDTH])
        uv = _mm(h, w_ref[:, base + 2 * C_WIDTH:base + 3 * C_WIDTH])
        dil = dils[grp]
        rows = tails[grp]
        out_refs = qkv_refs[3 * grp:3 * grp + 3]
        for hd in range(C_HEADS):
            hs = slice(hd * C_DH, (hd + 1) * C_DH)
            qh = _rope(uq[:, hs], cos, slo, shi) * Q_SCALE
            kh = _rope(uk[:, hs], cos, slo, shi)
            vh = uv[:, hs]
            tail_refs[2 * grp][0, :, hs] = kh[tm - rows:, :]
            tail_refs[2 * grp + 1][0, :, hs] = vh[tm - rows:, :]
            for ref, slab, val in zip(out_refs, slabs, (qh, kh, vh)):
                if dil == 1:
                    ref[0, 0, :, hs] = val.astype(BF16)
                else:
                    slab[hd] = val
        if 1 < dil <= REGROUP_STRIDE:
            for r in range(dil):
                for hd in range(C_HEADS):
                    hs = slice(hd * C_DH, (hd + 1) * C_DH)
                    for ref, slab in zip(out_refs, slabs):
                        ref[0, r, :, hs] = slab[hd, pl.ds(r, tm // dil, stride=dil), :].astype(BF16)
        elif dil > REGROUP_STRIDE:
            first, second = REGROUP_STRIDE, dil // REGROUP_STRIDE
            part = tm // first
            for hd in range(C_HEADS):
                hs = slice(hd * C_DH, (hd + 1) * C_DH)
                for ref, slab, half in zip(out_refs, slabs, halves):
                    for r1 in range(first):
                        half[hd, r1 * part:(r1 + 1) * part, :] = slab[hd, pl.ds(r1, part, stride=first), :]
                    for r1 in range(first):
                        for r2 in range(second):
                            picked = half[hd, pl.ds(r1 * part + r2, tm // dil, stride=second), :]
                            ref[0, r1 + first * r2, :, hs] = picked.astype(BF16)

    zb = 3 * n_groups * C_WIDTH
    z_ref[0] = _silu(_mm(h, w_ref[:, zb:zb + C_WIDTH])).astype(BF16)


def _c_in(x, g, w, cos, slo, shi, tm, per_row=False):
    B, T, D = x.shape
    row = lambda width: pl.BlockSpec((1, tm, width), lambda b, t: (b, t, 0))
    table = pl.BlockSpec((tm, C_DH), lambda b, t: (t, 0))
    dils = tuple(1 if per_row else dil for _, dil in C_GROUPS)
    qkv_specs, qkv_shapes = [], []
    for dil in dils:
        qkv_specs += [pl.BlockSpec((1, dil, tm // dil, C_WIDTH), lambda b, t: (b, 0, t, 0))] * 3
        qkv_shapes += [jax.ShapeDtypeStruct((B, dil, T // dil, C_WIDTH), BF16)] * 3
    tails, tail_specs, tail_shapes = [], [], []
    for win, _ in C_GROUPS:
        keep = T if per_row else min(win, T)
        rows = min(tm, keep)
        first_row = T - keep
        tails.append(rows)

        def tail_index(b, t, first_row=first_row, rows=rows):
            return (b, jnp.maximum(((t + 1) * tm - first_row) // rows - 1, 0), 0)

        tail_specs += [pl.BlockSpec((1, rows, C_WIDTH), tail_index)] * 2
        tail_shapes += [jax.ShapeDtypeStruct((B, keep, C_WIDTH), F32)] * 2
    n_qkv = len(qkv_specs)
    outs = pl.pallas_call(
        functools.partial(_c_in_kernel, tails=tuple(tails), dils=dils),
        grid=(B, T // tm),
        in_specs=[row(D), _resident(g.shape), _resident(w.shape), table, table, table],
        out_specs=qkv_specs + [row(C_WIDTH)] + tail_specs,
        out_shape=qkv_shapes + [jax.ShapeDtypeStruct((B, T, C_WIDTH), BF16)] + tail_shapes,
        scratch_shapes=[pltpu.VMEM((C_HEADS, tm, C_DH), F32)] * 6,
        compiler_params=_params("parallel", "arbitrary"),
        name="attn_in",
    )(x, g, w, cos, slo, shi)
    return outs[:n_qkv], outs[n_qkv], outs[n_qkv + 1:]


ATTN_CHUNK = 512
REGROUP_STRIDE = 4


def _attn_chunk_kernel(q_ref, kc_ref, vc_ref, *refs, span):
    acc_ref, st_ref = refs[-2:]
    kp_ref, vp_ref = refs[:-2] if len(refs) == 4 else (None, None)
    chunk = pl.program_id(2)
    qi = lax.broadcasted_iota(jnp.int32, (C_BLOCK, 2 * C_BLOCK), 0)
    kj = lax.broadcasted_iota(jnp.int32, (C_BLOCK, 2 * C_BLOCK), 1)
    dist = qi + C_BLOCK - kj
    band = (dist >= 0) & (dist <= span)
    bias_inner = jnp.where(band, 0.0, -1e30)
    bias_first = jnp.where(band & (kj >= jnp.where(chunk == 0, C_BLOCK, 0)), 0.0, -1e30)
    lane = lax.broadcasted_iota(jnp.int32, (C_BLOCK, LANES), 1)
    ones = jnp.ones((2 * C_BLOCK, LANES), BF16)

    for i in range(q_ref.shape[2] // C_BLOCK):
        rows = slice(i * C_BLOCK, (i + 1) * C_BLOCK)

        def keys(prev_ref, cur_ref, cols):
            if i == 0:
                before = cur_ref[0, 0, rows, cols] if prev_ref is None else prev_ref[0, 0, :, cols]
                return jnp.concatenate([before, cur_ref[0, 0, rows, cols]], axis=0)
            return cur_ref[0, 0, (i - 1) * C_BLOCK:(i + 1) * C_BLOCK, cols]

        head_cols = [slice(hd * C_DH, (hd + 1) * C_DH) for hd in range(C_HEADS)]
        s_all = [lax.dot_general(q_ref[0, 0, rows, cols], keys(kp_ref, kc_ref, cols), NT_DIMS,
                                 preferred_element_type=F32) for cols in head_cols]
        stats = jnp.zeros((C_BLOCK, LANES), F32)
        for hd, cols in enumerate(head_cols):
            s = s_all[hd] + (bias_first if i == 0 else bias_inner)
            mx = jnp.max(s, axis=-1, keepdims=True)
            p = jnp.exp2(s - mx).astype(BF16)
            both = _mm(p, jnp.concatenate([keys(vp_ref, vc_ref, cols), ones], axis=1))
            acc_ref[0, 0, rows, cols] = both[:, :C_DH].astype(BF16)
            stats = jnp.where(lane == hd, mx, jnp.where(lane == C_HEADS + hd, both[:, C_DH:], stats))
        st_ref[0, 0, rows, :] = stats


def _attn_prompt_group(q, k, v, span):
    B, dil, n, _ = q.shape
    qc = min(n, ATTN_CHUNK)
    per_chunk = qc // C_BLOCK
    cur = lambda rows, width: pl.BlockSpec((1, 1, rows, width), lambda b, r, c: (b, r, c, 0))
    prev = pl.BlockSpec((1, 1, C_BLOCK, C_WIDTH), lambda b, r, c: (b, r, jnp.maximum(c * per_chunk - 1, 0), 0))
    before = [(k, prev), (v, prev)] if n > qc else []
    return pl.pallas_call(
        functools.partial(_attn_chunk_kernel, span=span),
        grid=(B, dil, n // qc),
        in_specs=[cur(qc, C_WIDTH)] * 3 + [spec for _, spec in before],
        out_specs=[cur(qc, C_WIDTH), cur(qc, LANES)],
        out_shape=[jax.ShapeDtypeStruct((B, dil, n, C_WIDTH), BF16),
                   jax.ShapeDtypeStruct((B, dil, n, LANES), F32)],
        compiler_params=_params("parallel", "parallel", "arbitrary"),
        name="attn_chunks_d%d" % dil,
    )(q, k, v, *[a for a, _ in before])


def _merge_out_kernel(a0_ref, a1_ref, a2_ref, s0_ref, s1_ref, s2_ref, z_ref, x_ref, p_ref,
                      wo_ref, pg_ref, pe_ref, xo_ref, acc_s, st_s, half_s):
    tm = x_ref.shape[1]
    a_refs, s_refs = (a0_ref, a1_ref, a2_ref), (s0_ref, s1_ref, s2_ref)
    dils = [a_ref.shape[1] for a_ref in a_refs]
    for grp, dil in enumerate(dils):
        if dil == 1:
            continue
        two_steps = dil > REGROUP_STRIDE
        first, second = REGROUP_STRIDE, dil // REGROUP_STRIDE
        part = tm // first
        for r in range(dil):
            rows = pl.ds(r, tm // dil, stride=dil)
            acc = a_refs[grp][0, r].astype(F32)
            for hd in range(C_HEADS):
                piece = acc[:, hd * C_DH:(hd + 1) * C_DH]
                if two_steps:
                    half_s[hd, pl.ds((r % first) * part + r // first, tm // dil, stride=second), :] = piece
                else:
                    acc_s[grp * C_HEADS + hd, rows, :] = piece
            st_s[grp, rows, :] = s_refs[grp][0, r]
        if two_steps:
            for hd in range(C_HEADS):
                for r1 in range(first):
                    acc_s[grp * C_HEADS + hd, pl.ds(r1, part, stride=first), :] = (
                        half_s[hd, r1 * part:(r1 + 1) * part, :])

    def acc_of(grp, hd):
        if dils[grp] == 1:
            return a_refs[grp][0, 0, :, hd * C_DH:(hd + 1) * C_DH].astype(F32)
        return acc_s[grp * C_HEADS + hd]

    stats = [s_refs[grp][0, 0] if dils[grp] == 1 else st_s[grp] for grp in range(3)]
    mx = [st[:, 0:C_HEADS] for st in stats]
    ls = [st[:, C_HEADS:2 * C_HEADS] for st in stats]
    top = jnp.maximum(jnp.maximum(mx[0], mx[1]), mx[2])
    es = [jnp.exp2(m - top) for m in mx]
    den = es[0] * ls[0] + es[1] * ls[1] + es[2] * ls[2]
    coef = [e / den for e in es]
    x1 = x_ref[0]
    for pair in range(C_HEADS // 2):
        ys = []
        for hd in (2 * pair, 2 * pair + 1):
            hs = slice(hd * C_DH, (hd + 1) * C_DH)
            o = coef[0][:, hd:hd + 1] * acc_of(0, hd)
            for grp in (1, 2):
                o = o + coef[grp][:, hd:hd + 1] * acc_of(grp, hd)
            ys.append((o * z_ref[0, :, hs].astype(F32)).astype(BF16))
        x1 = x1 + _mm(jnp.concatenate(ys, axis=1), wo_ref[2 * pair * C_DH:(2 * pair + 2) * C_DH, :])
    xo_ref[0] = _embed(x1, p_ref[0], pg_ref, pe_ref)


def _merge_out(accs, stats, z, x, p_all, layer, wo, pg, pe, tm):
    B, T, D = x.shape
    row = lambda width: pl.BlockSpec((1, tm, width), lambda b, t: (b, t, 0))
    classes = lambda a: pl.BlockSpec((1, a.shape[1], tm // a.shape[1], a.shape[3]), lambda b, t: (b, 0, t, 0))
    p_spec = pl.BlockSpec((None, 1, tm, p_all.shape[-1]), lambda b, t: (layer, b, t, 0))
    return pl.pallas_call(
        _merge_out_kernel,
        grid=(B, T // tm),
        in_specs=[classes(a) for a in accs] + [classes(s) for s in stats] + [row(C_WIDTH), row(D), p_spec,
                  _resident(wo.shape), _resident(pg.shape), _resident(pe.shape)],
        out_specs=row(D),
        out_shape=jax.ShapeDtypeStruct((B, T, D), F32),
        scratch_shapes=[pltpu.VMEM((3 * C_HEADS, tm, C_DH), F32), pltpu.VMEM((3, tm, LANES), F32),
                        pltpu.VMEM((C_HEADS, tm, C_DH), F32)],
        compiler_params=_params("parallel", "parallel"),
        name="attn_merge_out",
    )(*accs, *stats, z, x, p_all, wo, pg, pe)


def _attn_sample_kernel(*refs):
    n_groups = len(C_GROUPS)
    z_ref, y_ref = refs[5 * n_groups], refs[5 * n_groups + 1]
    parts = []
    for grp in range(n_groups):
        q_ref, kn_ref, vn_ref, kc_ref, vc_ref = refs[5 * grp:5 * grp + 5]
        q = q_ref[0].astype(F32)
        s_old = jnp.sum(kc_ref[0] * q[None], axis=-1, keepdims=True)
        s_new = jnp.sum(kn_ref[0] * q, axis=-1, keepdims=True)
        mx = jnp.maximum(jnp.max(s_old, axis=0), s_new)
        p_old = jnp.exp2(s_old - mx[None])
        p_new = jnp.exp2(s_new - mx)
        l = jnp.sum(p_old, axis=0) + p_new
        acc = jnp.sum(p_old * vc_ref[0], axis=0) + p_new * vn_ref[0]
        parts.append((acc, mx, l))
    top = jnp.maximum(jnp.maximum(parts[0][1], parts[1][1]), parts[2][1])
    es = [jnp.exp2(m - top) for _, m, _ in parts]
    num = es[0] * parts[0][0] + es[1] * parts[1][0] + es[2] * parts[2][0]
    den = es[0] * parts[0][2] + es[1] * parts[1][2] + es[2] * parts[2][2]
    y_ref[0] = (num / den * z_ref[0].astype(F32)).astype(BF16)


def _attn_sample(qkv, tails, caches, z):
    Bs = z.shape[1]
    heads = lambda a: a.reshape(Bs, C_HEADS, C_DH)
    row = pl.BlockSpec((1, C_HEADS, C_DH), lambda b: (b, 0, 0))
    args, specs = [], []
    for grp, (win, dil) in enumerate(C_GROUPS):
        kc, vc = caches[grp]
        n_buf = kc.shape[1]
        assert n_buf == win and win % dil == 0, "sample window must be fully cached"
        span = win // dil
        view = lambda a: a.reshape(Bs, span, dil, C_HEADS, C_DH)
        cache_spec = pl.BlockSpec((1, span, None, C_HEADS, C_DH), lambda b: (b, 0, 0, 0, 0))
        args += [heads(qkv[3 * grp]), heads(tails[2 * grp]), heads(tails[2 * grp + 1]), view(kc), view(vc)]
        specs += [row, row, row, cache_spec, cache_spec]
    return pl.pallas_call(
        _attn_sample_kernel,
        grid=(Bs,),
        in_specs=specs + [row],
        out_specs=row,
        out_shape=jax.ShapeDtypeStruct((Bs, C_HEADS, C_DH), BF16),
        compiler_params=_params("parallel"),
        name="attn_sample",
    )(*args, heads(z))


def _pool_tail(x, acc, p, pg_ref, pe_ref, fg_ref):
    return _rms(_embed(x + acc, p, pg_ref, pe_ref), fg_ref[...])


def _pool_prompt_kernel(x_ref, p_ref, g_ref, win_ref, wgrp_ref, sc_ref, wo_ref, pg_ref, pe_ref, fg_ref,
                        xo_ref, st_ref, pad_s):
    E = wo_ref.shape[0]
    G = E // len(D_WINDOWS)
    t = pl.program_id(1)

    @pl.when(t == 0)
    def _():
        pad_s[0:POOL_PAD, :] = jnp.zeros((POOL_PAD, E), F32)

    x = x_ref[0]
    tm = x.shape[0]
    h = _rms(x, g_ref[...]).astype(BF16)
    pos = t * tm + lax.broadcasted_iota(jnp.int32, (tm, 1), 0)
    acc = jnp.zeros(x.shape, F32)
    for grp, w in enumerate(D_WINDOWS):
        cols = slice(grp * G, (grp + 1) * G)
        xp = _mm(h, win_ref[:, cols])
        z = _mm(h, win_ref[:, E + grp * G:E + (grp + 1) * G])
        pad_s[POOL_PAD:, cols] = xp
        wsum = pad_s[:, cols]
        shift = 1
        while shift < w:
            wsum = wsum + pltpu.roll(wsum, shift, 0)
            shift *= 2
        cnt = jnp.minimum(w, pos + 1).astype(F32)
        r = wsum[POOL_PAD:, :] / cnt - xp
        y = _mm(r.astype(BF16), wgrp_ref[grp]) * sc_ref[:, cols]
        acc = acc + _mm((y * _silu(z)).astype(BF16), wo_ref[cols, :])
        pad_s[0:POOL_PAD, cols] = xp[tm - POOL_PAD:, :]
    xo_ref[0] = _pool_tail(x, acc, p_ref[0], pg_ref, pe_ref, fg_ref)
    st_ref[0] = pad_s[0:POOL_PAD, :]


def _pool_prompt(x, p_all, layer, g, win, wgrp, sc, wo, pg, pe, fg, tm):
    B, T, D = x.shape
    E = wo.shape[0]
    row = lambda width: pl.BlockSpec((1, tm, width), lambda b, t: (b, t, 0))
    p_spec = pl.BlockSpec((None, 1, tm, p_all.shape[-1]), lambda b, t: (layer, b, t, 0))
    return pl.pallas_call(
        _pool_prompt_kernel,
        grid=(B, T // tm),
        in_specs=[row(D), p_spec] + [_resident(a.shape) for a in (g, win, wgrp, sc, wo, pg, pe, fg)],
        out_specs=[row(D), pl.BlockSpec((1, POOL_PAD, E), lambda b, t: (b, 0, 0))],
        out_shape=[jax.ShapeDtypeStruct((B, T, D), F32), jax.ShapeDtypeStruct((B, POOL_PAD, E), F32)],
        scratch_shapes=[pltpu.VMEM((POOL_PAD + tm, E), F32)],
        compiler_params=_params("parallel", "arbitrary"),
        name="pool_prompt",
    )(x, p_all, g, win, wgrp, sc, wo, pg, pe, fg)


def _pool_sample_kernel(x_ref, p_ref, g_ref, win_ref, wgrp_ref, sc_ref, wo_ref, pg_ref, pe_ref, fg_ref, st_ref,
                        xo_ref, xp_ref, *, pos):
    E = wo_ref.shape[0]
    G = E // len(D_WINDOWS)
    x = x_ref[0]
    h = _rms(x, g_ref[...]).astype(BF16)
    acc = jnp.zeros(x.shape, F32)
    for grp, w in enumerate(D_WINDOWS):
        cols = slice(grp * G, (grp + 1) * G)
        xp = _mm(h, win_ref[:, cols])
        z = _mm(h, win_ref[:, E + grp * G:E + (grp + 1) * G])
        wsum = xp
        for back in range(1, w):
            lo = (POOL_STATE - back) * E + grp * G
            wsum = wsum + st_ref[:, lo:lo + G]
        r = wsum / float(min(w, pos + 1)) - xp
        y = _mm(r.astype(BF16), wgrp_ref[grp]) * sc_ref[:, cols]
        acc = acc + _mm((y * _silu(z)).astype(BF16), wo_ref[cols, :])
        xp_ref[:, cols] = xp
    xo_ref[0] = _pool_tail(x, acc, p_ref[0], pg_ref, pe_ref, fg_ref)


def _pool_sample(x, p_all, layer, g, win, wgrp, sc, wo, pg, pe, fg, st, pos):
    _, M, D = x.shape
    E = wo.shape[0]
    p_spec = pl.BlockSpec((None, 1, M, p_all.shape[-1]), lambda i: (layer, 0, 0, 0))
    return pl.pallas_call(
        functools.partial(_pool_sample_kernel, pos=pos),
        grid=(1,),
        in_specs=[_resident(x.shape), p_spec] + [_resident(a.shape) for a in (g, win, wgrp, sc, wo, pg, pe, fg, st)],
        out_specs=[_whole(x.shape), _whole((M, E))],
        out_shape=[jax.ShapeDtypeStruct(x.shape, F32), jax.ShapeDtypeStruct((M, E), F32)],
        compiler_params=_params("arbitrary"),
        name="pool_sample",
    )(x, p_all, g, win, wgrp, sc, wo, pg, pe, fg, st)


def _rope_tables(pos):
    half = ROPE_DIM // 2
    inv = ROPE_THETA ** (-jnp.arange(half, dtype=F32) / half)
    ang = pos.astype(F32)[:, None] * inv[None, :]
    cos, sin = jnp.cos(ang), jnp.sin(ang)
    rest = C_DH - ROPE_DIM
    n = pos.shape[0]
    cos_t = jnp.concatenate([cos, cos, jnp.ones((n, rest), F32)], axis=1)
    sin_lo = jnp.concatenate([-sin, jnp.zeros((n, half + rest), F32)], axis=1)
    sin_hi = jnp.concatenate([jnp.zeros((n, half), F32), sin, jnp.zeros((n, rest), F32)], axis=1)
    return cos_t, sin_lo, sin_hi


def _row(a):
    return a.reshape(1, -1).astype(F32)


def kernel(x_prompt, x_sample, state_mlstm_C, state_mlstm_n, state_mlstm_m, state_conv, cache_k_w128, cache_v_w128, cache_k_w512, cache_v_w512, cache_k_w2048, cache_v_w2048, state_pool, p_prompt, p_sample, norm_g, pe_w, pg_w, final_g, a_w_in, a_b_if, a_norm_g, a_w_out, b_w_in, b_conv_w, b_w_out, c_w_in, c_w_out, d_w_in, d_w_grp, d_scale, d_w_out):
    B, T, D = x_prompt.shape
    Bs, Ts, _ = x_sample.shape
    assert Ts == 1 and norm_g.shape[0] == 4 and a_w_in.shape[0] == 1
    assert T % 512 == 0 and all(T % (dil * C_BLOCK) == 0 for _, dil in C_GROUPS)
    assert Bs % 8 == 0

    bf = lambda a: a.astype(BF16)
    qk = A_HEADS * A_DK
    n_main = 2 * qk + 3 * A_WIDTH
    a_w = bf(a_w_in[0])
    a_wg = bf(jnp.pad(a_w_in[0, :, n_main:], ((0, 0), (0, LANES - 2 * A_HEADS))))
    a_bif = jnp.pad(a_b_if[0], (0, LANES - 2 * A_HEADS)).reshape(1, LANES)
    a_ng = _row(a_norm_g[0])
    a_wo, b_wi, b_wo, c_wi, c_wo, d_wi, d_wg, d_wo = (
        bf(a_w_out[0]), bf(b_w_in[0]), bf(b_w_out[0]), bf(c_w_in[0]), bf(c_w_out[0]),
        bf(d_w_in[0]), bf(d_w_grp[0]), bf(d_w_out[0]))
    pg, pe = bf(pg_w), bf(pe_w)
    gs = [_row(norm_g[i]) for i in range(4)]
    fg = _row(final_g)
    b_cw = b_conv_w[0]
    d_sc = _row(d_scale[0])
    caches = [(cache_k_w128[0], cache_v_w128[0]), (cache_k_w512[0], cache_v_w512[0]),
              (cache_k_w2048[0], cache_v_w2048[0])]

    x = x_prompt
    q, k, v, oz, gates, kt = _a_in(x, gs[0], a_w, a_wg, a_bif, 512, transposed_k=True)
    hh, c_p, n_p, m_p = _mlstm_prompt(q, k, kt, v, gates)
    x = _a_out(hh, oz, a_ng, x, p_prompt, 0, a_wo, pg[0], pe[0], 512)
    x, conv_p = _conv_prompt(x, p_prompt, 1, gs[1], b_wi, b_cw, b_wo, pg[1], pe[1], 512)
    qkv, z, tails_p = _c_in(x, gs[2], c_wi, *_rope_tables(jnp.arange(T)), 256)
    accs, stats = [], []
    for grp, (win, dil) in enumerate(C_GROUPS):
        acc, st = _attn_prompt_group(qkv[3 * grp], qkv[3 * grp + 1], qkv[3 * grp + 2], win // dil)
        accs.append(acc)
        stats.append(st)
    x = _merge_out(accs, stats, z, x, p_prompt, 2, c_wo, pg[2], pe[2], 512)
    y_prompt, pool_p = _pool_prompt(x, p_prompt, 3, gs[3], d_wi, d_wg, d_sc, d_wo, pg[3], pe[3], fg, 512)

    xs = x_sample.reshape(1, Bs, D)
    ps = p_sample.reshape(p_sample.shape[0], 1, Bs, p_sample.shape[-1])
    q, k, v, oz, gates = _a_in(xs, gs[0], a_w, a_wg, a_bif, Bs)
    m_lanes = jnp.pad(state_mlstm_m[0], ((0, 0), (0, LANES - A_HEADS)))
    hh, c_s, n_s, m_s = _mlstm_sample(q[0], k[0], v[0], gates[0], state_mlstm_C[0], state_mlstm_n[0], m_lanes)
    m_s = m_s[:, :, 0].T
    xs = _a_out(hh.reshape(1, Bs, A_WIDTH), oz, a_ng, xs, ps, 0, a_wo, pg[0], pe[0], Bs)
    xs, cx = _conv_sample(xs, ps, 1, gs[1], b_wi, b_cw, b_wo, pg[1], pe[1],
                          state_conv[0, :, 0, :], state_conv[0, :, 1, :])
    conv_s = jnp.stack([state_conv[0, :, 1, :], cx], axis=1)
    tables = [jnp.broadcast_to(t, (Bs, C_DH)) for t in _rope_tables(PAST_LEN + jnp.arange(1))]
    qkv, z, tails_s = _c_in(xs, gs[2], c_wi, *tables, Bs, per_row=True)
    y = _attn_sample(qkv, tails_s, caches, z)
    xs = _out_embed(y.reshape(1, Bs, C_WIDTH), xs, ps, 2, c_wo, pg[2], pe[2], Bs)
    pool_flat = state_pool[0].reshape(Bs, POOL_STATE * state_pool.shape[-1])
    ys, xp = _pool_sample(xs, ps, 3, gs[3], d_wi, d_wg, d_sc, d_wo, pg[3], pe[3], fg, pool_flat, PAST_LEN)
    pool_s = jnp.concatenate([state_pool[0, :, 1:, :], xp[:, None, :]], axis=1)

    heads = lambda a, lead: a.reshape(1, lead, a.shape[1], C_HEADS, C_DH)
    kv_out = []
    for grp in range(len(C_GROUPS)):
        for j in range(2):
            kv_out += [heads(tails_p[2 * grp + j], B), tails_s[2 * grp + j].reshape(1, Bs, 1, C_HEADS, C_DH)]
    return (y_prompt, ys.reshape(Bs, 1, D),
            c_p[None], c_s[None], n_p[None], n_s[None], m_p[None, :, 0, :A_HEADS], m_s.reshape(1, Bs, A_HEADS),
            conv_p[None], conv_s[None],
            *kv_out,
            pool_p[None, :, 1:, :], pool_s[None])
```

```python
import functools

import jax
import jax.numpy as jnp
from jax import lax
from jax.experimental import pallas as pl
from jax.experimental.pallas import tpu as pltpu

F32 = jnp.float32
BF16 = jnp.bfloat16

EPS = 1e-6
PAST_LEN = 8192

A_HEADS = 8
A_DK = 128
A_DV = 256
A_WIDTH = A_HEADS * A_DV
A_CHUNK = 128
A_CHUNKS_PER_STEP = 2

CONV_W = 3

C_HEADS = 8
C_DH = 128
C_WIDTH = C_HEADS * C_DH
C_GROUPS = ((128, 1), (512, 4), (2048, 16))
C_BLOCK = 128
ROPE_DIM = C_DH // 4
ROPE_THETA = 500000.0
LOG2_E = 1.4426950408889634
Q_SCALE = C_DH ** -0.5 * LOG2_E

D_WINDOWS = (2, 4, 8, 16)
POOL_STATE = max(D_WINDOWS) - 1
POOL_PAD = POOL_STATE + 1

LANES = 128
VMEM_LIMIT_BYTES = 56 * 2 ** 20

NT_DIMS = (((1,), (1,)), ((), ()))
TN_DIMS = (((0,), (0,)), ((), ()))


def _params(*semantics):
    return pltpu.CompilerParams(dimension_semantics=semantics, vmem_limit_bytes=VMEM_LIMIT_BYTES)


def _resident(shape):
    zeros = (0,) * len(shape)
    return pl.BlockSpec(shape, lambda *_: zeros, pipeline_mode=pl.Buffered(1))


def _whole(shape):
    zeros = (0,) * len(shape)
    return pl.BlockSpec(shape, lambda *_: zeros)


def _mm(a, b):
    return jnp.dot(a, b, preferred_element_type=F32)


def _rms(x, g):
    return x * lax.rsqrt(jnp.mean(x * x, axis=-1, keepdims=True) + EPS) * g


def _sigmoid(x):
    return 0.5 * jnp.tanh(0.5 * x) + 0.5


def _silu(x):
    return x * _sigmoid(x)


def _log_sigmoid(x):
    return jnp.minimum(x, 0.0) - jnp.log(1.0 + jnp.exp(-jnp.abs(x)))


def _embed(x1, p, pg_ref, pe_ref):
    gate = _sigmoid(_mm(x1.astype(BF16), pg_ref[...]))
    return x1 + gate * _mm(p.astype(BF16), pe_ref[...])


def _a_in_kernel(x_ref, g_ref, w_ref, wg_ref, bif_ref,
                 q_ref, k_ref, v_ref, oz_ref, gate_ref, *maybe_kt_ref):
    h = _rms(x_ref[0], g_ref[...]).astype(BF16)
    qk = A_HEADS * A_DK

    def proj(lo, width):
        return _mm(h, w_ref[:, lo:lo + width])

    q_ref[0] = proj(0, qk).astype(BF16)
    k = proj(qk, qk) * (A_DK ** -0.5)
    k_ref[0] = k.astype(BF16)
    for kt_ref in maybe_kt_ref:
        kt_ref[0] = k.T.astype(BF16)
    v_ref[0] = proj(2 * qk, A_WIDTH).astype(BF16)
    oz_ref[0] = (_sigmoid(proj(2 * qk + A_WIDTH, A_WIDTH))
                 * _silu(proj(2 * qk + 2 * A_WIDTH, A_WIDTH))).astype(BF16)
    gates = _mm(h, wg_ref[...]) + bif_ref[...]
    lane = lax.broadcasted_iota(jnp.int32, gates.shape, 1)
    gate_ref[0] = jnp.where(lane < A_HEADS, gates, _log_sigmoid(gates))


def _a_in(x, g, w, wg, bif, tm, transposed_k=False):
    B, T, D = x.shape
    qk = A_HEADS * A_DK
    row = lambda width: pl.BlockSpec((1, tm, width), lambda b, t: (b, t, 0))
    out_specs = [row(qk), row(qk), row(A_WIDTH), row(A_WIDTH), row(LANES)]
    out_shape = [jax.ShapeDtypeStruct((B, T, qk), BF16), jax.ShapeDtypeStruct((B, T, qk), BF16),
                 jax.ShapeDtypeStruct((B, T, A_WIDTH), BF16), jax.ShapeDtypeStruct((B, T, A_WIDTH), BF16),
                 jax.ShapeDtypeStruct((B, T, LANES), F32)]
    if transposed_k:
        out_specs.append(pl.BlockSpec((1, qk, tm), lambda b, t: (b, 0, t)))
        out_shape.append(jax.ShapeDtypeStruct((B, qk, T), BF16))
    return pl.pallas_call(
        _a_in_kernel,
        grid=(B, T // tm),
        in_specs=[row(D), _resident(g.shape), _resident(w.shape), _resident(wg.shape), _resident(bif.shape)],
        out_specs=out_specs,
        out_shape=out_shape,
        compiler_params=_params("parallel", "parallel"),
        name="mlstm_in",
    )(x, g, w, wg, bif)


def _mlstm_chunk_kernel(q_ref, k_ref, kt_ref, v_ref, gate_ref,
                        hh_ref, c_out, n_out, m_out, m_s, *state_s):
    chunk = pl.program_id(1)
    ct_s, nb_s = state_s[:A_HEADS], state_s[A_HEADS:]

    @pl.when(chunk == 0)
    def _():
        for ref in state_s:
            ref[...] = jnp.zeros_like(ref)
        m_s[...] = jnp.zeros_like(m_s)

    L = A_CHUNK
    row = lax.broadcasted_iota(jnp.int32, (L, LANES), 0)
    ti = lax.broadcasted_iota(jnp.int32, (L, L), 0)
    si = lax.broadcasted_iota(jnp.int32, (L, L), 1)
    causal = ti >= si
    ones = jnp.ones((L, LANES), BF16)
    heads = range(A_HEADS)
    ks = [slice(h * A_DK, (h + 1) * A_DK) for h in heads]
    vs = [slice(h * A_DV, (h + 1) * A_DV) for h in heads]

    def prefix(x, op, identity):
        step = 1
        while step < L:
            x = op(x, jnp.where(row >= step, pltpu.roll(x, step, 0), identity))
            step *= 2
        return x

    def one_chunk(rows):
        gates = gate_ref[0, rows, :]
        b = pltpu.roll(prefix(gates, jnp.add, 0.0), LANES - A_HEADS, 1)
        c = gates - b
        m_row = m_s[0:1, :]
        top = jnp.maximum(m_row, prefix(c, jnp.maximum, -jnp.inf))
        top_last = top[L - 1:L, :]
        g_all = jnp.exp(m_row - top)
        floor_all = jnp.exp(-(b + top))
        w_all = jnp.exp(c - top_last)
        decay_row = jnp.exp(m_row - top_last)
        m_s[...] = jnp.broadcast_to(b[L - 1:L, :] + top_last, m_s.shape)
        c_t = c.T

        s_raw = [_mm(q_ref[0, rows, ks[h]], kt_ref[0, ks[h], rows]) for h in heads]
        upd = []
        for h in heads:
            w_b = jnp.broadcast_to(w_all[:, h:h + 1], (L, LANES))
            wv = jnp.concatenate([w_b, w_b], axis=1) * v_ref[0, rows, vs[h]].astype(F32)
            upd.append(_mm(kt_ref[0, ks[h], rows], jnp.concatenate([wv.astype(BF16), w_b.astype(BF16)], axis=1)))

        for h in heads:
            ct = ct_s[h][...]
            nb = nb_s[h][...]
            dmat = jnp.exp(jnp.where(causal, -top[:, h:h + 1] + c_t[h:h + 1, :], -jnp.inf))
            s = s_raw[h] * dmat
            gq = g_all[:, h:h + 1] * q_ref[0, rows, ks[h]].astype(F32)
            lhs = jnp.concatenate([s.astype(BF16), gq.astype(BF16)], axis=1)
            num = _mm(lhs, jnp.concatenate([v_ref[0, rows, vs[h]], ct.astype(BF16)], axis=0))
            den = _mm(lhs, jnp.concatenate([ones, nb.astype(BF16)], axis=0))
            scale = 1.0 / jnp.maximum(jnp.abs(den), floor_all[:, h:h + 1])
            hh_ref[0, rows, vs[h]] = (num * jnp.concatenate([scale, scale], axis=1)).astype(BF16)
            decay = decay_row[:, h:h + 1]
            ct_s[h][...] = decay * ct + upd[h][:, :A_DV]
            nb_s[h][...] = decay * nb + upd[h][:, A_DV:]

    for sub in range(q_ref.shape[1] // L):
        one_chunk(slice(sub * L, (sub + 1) * L))

    @pl.when(chunk == pl.num_programs(1) - 1)
    def _():
        for h in range(A_HEADS):
            c_out[0, h] = ct_s[h][...].T
            n_out[0, h:h + 1, :] = nb_s[h][...].T[0:1, :]
        m_out[0] = m_s[...]


def _mlstm_prompt(q, k, kt, v, gates):
    B, T, _ = q.shape
    L = A_CHUNK * A_CHUNKS_PER_STEP
    qk = A_HEADS * A_DK
    row = lambda width: pl.BlockSpec((1, L, width), lambda b, c: (b, c, 0))
    state = lambda *shape: pl.BlockSpec((1,) + shape, lambda b, c: (b,) + (0,) * len(shape))
    return pl.pallas_call(
        _mlstm_chunk_kernel,
        grid=(B, T // L),
        in_specs=[row(qk), row(qk), pl.BlockSpec((1, qk, L), lambda b, c: (b, 0, c)), row(A_WIDTH), row(LANES)],
        out_specs=[row(A_WIDTH), state(A_HEADS, A_DV, A_DK), state(A_HEADS, A_DK), state(A_HEADS, LANES)],
        out_shape=[jax.ShapeDtypeStruct((B, T, A_WIDTH), BF16),
                   jax.ShapeDtypeStruct((B, A_HEADS, A_DV, A_DK), F32),
                   jax.ShapeDtypeStruct((B, A_HEADS, A_DK), F32),
                   jax.ShapeDtypeStruct((B, A_HEADS, LANES), F32)],
        scratch_shapes=[pltpu.VMEM((A_HEADS, LANES), F32)] + [pltpu.VMEM((A_DK, A_DV), F32)] * A_HEADS
        + [pltpu.VMEM((A_DK, LANES), F32)] * A_HEADS,
        compiler_params=_params("parallel", "arbitrary"),
        name="mlstm_chunks",
    )(q, k, kt, v, gates)


def _mlstm_step_kernel(q_ref, k_ref, v_ref, gate_ref, m_ref, c_ref, n_ref,
                       hh_ref, c_out, n_out, m_out, hht_s):
    head = pl.program_id(0)
    Bs = q_ref.shape[0]
    q = q_ref[...].astype(F32)
    k = k_ref[...].astype(F32)
    n = n_ref[...]
    lane = lax.broadcasted_iota(jnp.int32, (Bs, LANES), 1)
    pick = lambda a, idx: jnp.sum(jnp.where(lane == idx, a, 0.0), axis=-1, keepdims=True)
    ig = pick(gate_ref[...], head)
    lf = pick(gate_ref[...], A_HEADS + head)
    m_prev = pick(m_ref[...], head)

    inter = lf + m_prev
    m_t = jnp.maximum(inter, ig)
    dm = jnp.exp(ig - m_t)
    g = jnp.exp(inter - m_t)
    s = jnp.sum(q * k, axis=-1, keepdims=True) * dm
    den = s + g * jnp.sum(q * n, axis=-1, keepdims=True)
    scale = 1.0 / jnp.maximum(jnp.abs(den), jnp.exp(-m_t))
    n_out[...] = g * n + dm * k
    m_out[...] = jnp.broadcast_to(m_t, (Bs, LANES))

    v_pad = jnp.concatenate([v_ref[...].astype(F32), jnp.zeros((LANES - Bs, A_DV), F32)], axis=0)
    vt = v_pad.T
    hht_s[...] = jnp.zeros_like(hht_s)
    for b in range(Bs):
        c_b = c_ref[b]
        one = slice(b, b + 1)
        v_col = vt[:, one]
        cq = jnp.sum(c_b * q[one, :], axis=-1, keepdims=True)
        hht_s[:, one] = (s[one, :] * v_col + g[one, :] * cq) * scale[one, :]
        c_out[b] = g[one, :] * c_b + (dm[one, :] * v_col) * k[one, :]
    hh_ref[...] = hht_s[...].T[0:Bs, :].astype(BF16)


def _mlstm_sample(q, k, v, gates, c0, n0, m0):
    Bs = q.shape[0]
    assert Bs <= LANES
    cols = lambda width: pl.BlockSpec((Bs, width), lambda h: (0, h))
    full = pl.BlockSpec((Bs, LANES), lambda h: (0, 0))
    c_spec = pl.BlockSpec((Bs, None, A_DV, A_DK), lambda h: (0, h, 0, 0))
    hh, c1, n1, m1 = pl.pallas_call(
        _mlstm_step_kernel,
        grid=(A_HEADS,),
        in_specs=[cols(A_DK), cols(A_DK), cols(A_DV), full, full, c_spec, cols(A_DK)],
        out_specs=[cols(A_DV), c_spec, cols(A_DK), pl.BlockSpec((None, Bs, LANES), lambda h: (h, 0, 0))],
        out_shape=[jax.ShapeDtypeStruct((Bs, A_WIDTH), BF16),
                   jax.ShapeDtypeStruct(c0.shape, F32), jax.ShapeDtypeStruct((Bs, A_HEADS * A_DK), F32),
                   jax.ShapeDtypeStruct((A_HEADS, Bs, LANES), F32)],
        scratch_shapes=[pltpu.VMEM((A_DV, LANES), F32)],
        compiler_params=_params("parallel"),
        name="mlstm_step",
    )(q, k, v, gates, m0, c0, n0.reshape(Bs, A_HEADS * A_DK))
    return hh, c1, n1.reshape(n0.shape), m1


def _a_out_kernel(hh_ref, oz_ref, ng_ref, x_ref, p_ref, wo_ref, pg_ref, pe_ref, xo_ref):
    x1 = x_ref[0]
    for h in range(A_HEADS):
        vs = slice(h * A_DV, (h + 1) * A_DV)
        hh = hh_ref[0, :, vs].astype(F32)
        hn = hh * lax.rsqrt(jnp.mean(hh * hh, axis=-1, keepdims=True) + EPS) * ng_ref[:, vs]
        y = hn * oz_ref[0, :, vs].astype(F32)
        x1 = x1 + _mm(y.astype(BF16), wo_ref[vs, :])
    xo_ref[0] = _embed(x1, p_ref[0], pg_ref, pe_ref)


def _a_out(hh, oz, ng, x, p_all, layer, wo, pg, pe, tm):
    B, T, D = x.shape
    row = lambda width: pl.BlockSpec((1, tm, width), lambda b, t: (b, t, 0))
    p_spec = pl.BlockSpec((None, 1, tm, p_all.shape[-1]), lambda b, t: (layer, b, t, 0))
    return pl.pallas_call(
        _a_out_kernel,
        grid=(B, T // tm),
        in_specs=[row(A_WIDTH), row(A_WIDTH), _resident(ng.shape), row(D), p_spec,
                  _resident(wo.shape), _resident(pg.shape), _resident(pe.shape)],
        out_specs=row(D),
        out_shape=jax.ShapeDtypeStruct((B, T, D), F32),
        compiler_params=_params("parallel", "parallel"),
        name="mlstm_out",
    )(hh, oz, ng, x, p_all, wo, pg, pe)


def _out_kernel(y_ref, x_ref, p_ref, wo_ref, pg_ref, pe_ref, xo_ref):
    x1 = x_ref[0] + _mm(y_ref[0], wo_ref[...])
    xo_ref[0] = _embed(x1, p_ref[0], pg_ref, pe_ref)


def _out_embed(y, x, p_all, layer, wo, pg, pe, tm):
    B, T, D = x.shape
    row = lambda width: pl.BlockSpec((1, tm, width), lambda b, t: (b, t, 0))
    p_spec = pl.BlockSpec((None, 1, tm, p_all.shape[-1]), lambda b, t: (layer, b, t, 0))
    return pl.pallas_call(
        _out_kernel,
        grid=(B, T // tm),
        in_specs=[row(y.shape[-1]), row(D), p_spec, _resident(wo.shape), _resident(pg.shape), _resident(pe.shape)],
        out_specs=row(D),
        out_shape=jax.ShapeDtypeStruct((B, T, D), F32),
        compiler_params=_params("parallel", "parallel"),
        name="out_embed",
    )(y, x, p_all, wo, pg, pe)


CONV_COLS = 512


def _conv_prompt_kernel(x_ref, p_ref, g_ref, win_ref, cw_ref, wo_ref, pg_ref, pe_ref,
                        xo_ref, st_ref, carry_s):
    E = wo_ref.shape[0]

    @pl.when(pl.program_id(1) == 0)
    def _():
        carry_s[...] = jnp.zeros_like(carry_s)

    x = x_ref[0]
    tm = x.shape[0]
    h = _rms(x, g_ref[...]).astype(BF16)
    row = lax.broadcasted_iota(jnp.int32, (tm, CONV_COLS), 0)
    acc = jnp.zeros(x.shape, F32)
    for c in range(E // CONV_COLS):
        cols = slice(c * CONV_COLS, (c + 1) * CONV_COLS)
        part = lambda i: _mm(h, win_ref[:, i * E + c * CONV_COLS:i * E + (c + 1) * CONV_COLS])
        bg, cg, xb, z = part(0), part(1), part(2), part(3)
        cx = cg * xb
        old = carry_s[0:1, cols]
        new = carry_s[1:2, cols]
        prev1 = jnp.where(row == 0, new, pltpu.roll(cx, 1, 0))
        prev2 = jnp.where(row == 0, old, jnp.where(row == 1, new, pltpu.roll(cx, 2, 0)))
        y = cw_ref[0:1, cols] * prev2 + cw_ref[1:2, cols] * prev1 + cw_ref[2:3, cols] * cx
        acc = acc + _mm((bg * y * _silu(z)).astype(BF16), wo_ref[cols, :])
        carry_s[0:2, cols] = cx[tm - 2:tm, :]
    xo_ref[0] = _embed(x + acc, p_ref[0], pg_ref, pe_ref)
    st_ref[0] = carry_s[0:2, :]


def _conv_prompt(x, p_all, layer, g, win, cw, wo, pg, pe, tm):
    B, T, D = x.shape
    E = wo.shape[0]
    row = lambda width: pl.BlockSpec((1, tm, width), lambda b, t: (b, t, 0))
    p_spec = pl.BlockSpec((None, 1, tm, p_all.shape[-1]), lambda b, t: (layer, b, t, 0))
    return pl.pallas_call(
        _conv_prompt_kernel,
        grid=(B, T // tm),
        in_specs=[row(D), p_spec, _resident(g.shape), _resident(win.shape), _resident(cw.shape),
                  _resident(wo.shape), _resident(pg.shape), _resident(pe.shape)],
        out_specs=[row(D), pl.BlockSpec((1, CONV_W - 1, E), lambda b, t: (b, 0, 0))],
        out_shape=[jax.ShapeDtypeStruct((B, T, D), F32), jax.ShapeDtypeStruct((B, CONV_W - 1, E), F32)],
        scratch_shapes=[pltpu.VMEM((8, E), F32)],
        compiler_params=_params("parallel", "arbitrary"),
        name="conv_prompt",
    )(x, p_all, g, win, cw, wo, pg, pe)


def _conv_sample_kernel(x_ref, p_ref, g_ref, win_ref, cw_ref, wo_ref, pg_ref, pe_ref, s0_ref, s1_ref,
                        xo_ref, cx_ref):
    E = wo_ref.shape[0]
    x = x_ref[0]
    h = _rms(x, g_ref[...]).astype(BF16)
    acc = jnp.zeros(x.shape, F32)
    for c in range(E // CONV_COLS):
        cols = slice(c * CONV_COLS, (c + 1) * CONV_COLS)
        part = lambda i: _mm(h, win_ref[:, i * E + c * CONV_COLS:i * E + (c + 1) * CONV_COLS])
        bg, cg, xb, z = part(0), part(1), part(2), part(3)
        cx = cg * xb
        y = cw_ref[0:1, cols] * s0_ref[:, cols] + cw_ref[1:2, cols] * s1_ref[:, cols] + cw_ref[2:3, cols] * cx
        acc = acc + _mm((bg * y * _silu(z)).astype(BF16), wo_ref[cols, :])
        cx_ref[:, cols] = cx
    xo_ref[0] = _embed(x + acc, p_ref[0], pg_ref, pe_ref)


def _conv_sample(x, p_all, layer, g, win, cw, wo, pg, pe, s0, s1):
    _, M, D = x.shape
    E = wo.shape[0]
    p_spec = pl.BlockSpec((None, 1, M, p_all.shape[-1]), lambda i: (layer, 0, 0, 0))
    return pl.pallas_call(
        _conv_sample_kernel,
        grid=(1,),
        in_specs=[_resident(x.shape), p_spec, _resident(g.shape), _resident(win.shape), _resident(cw.shape),
                  _resident(wo.shape), _resident(pg.shape), _resident(pe.shape),
                  _resident(s0.shape), _resident(s1.shape)],
        out_specs=[_whole(x.shape), _whole((M, E))],
        out_shape=[jax.ShapeDtypeStruct(x.shape, F32), jax.ShapeDtypeStruct((M, E), F32)],
        compiler_params=_params("arbitrary"),
        name="conv_sample",
    )(x, p_all, g, win, cw, wo, pg, pe, s0, s1)


def _rope(x, cos, sin_lo, sin_hi):
    half = ROPE_DIM // 2
    return x * cos + pltpu.roll(x, C_DH - half, 1) * sin_lo + pltpu.roll(x, half, 1) * sin_hi


def _c_in_kernel(x_ref, g_ref, w_ref, cos_ref, slo_ref, shi_ref, *refs, tails, dils):
    n_groups = len(C_GROUPS)
    perm_dils = sorted({d for d in dils if d > 1})
    if perm_dils:
        perm_ref, refs = refs[0], refs[1:]
    qkv_refs = refs[:3 * n_groups]
    z_ref = refs[3 * n_groups]
    tail_refs = refs[3 * n_groups + 1:]
    h = _rms(x_ref[0], g_ref[...]).astype(BF16)
    tm = h.shape[0]
    cos, slo, shi = cos_ref[...], slo_ref[...], shi_ref[...]

    for grp in range(n_groups):
        base = 3 * grp * C_WIDTH
        uq = _mm(h, w_ref[:, base:base + C_WIDTH])
        uk = _mm(h, w_ref[:, base + C_WIDTH:base + 2 * C_WIDTH])
        uv = _mm(h, w_ref[:, base + 2 * C_WIDTH:base + 3 * C_WIDTH])
        dil = dils[grp]
        rows = tails[grp]
        out_refs = qkv_refs[3 * grp:3 * grp + 3]
        pieces = ([], [], [])
        for hd in range(C_HEADS):
            hs = slice(hd * C_DH, (hd + 1) * C_DH)
            qh = _rope(uq[:, hs], cos, slo, shi) * Q_SCALE
            kh = _rope(uk[:, hs], cos, slo, shi)
            vh = uv[:, hs]
            tail_refs[2 * grp][0, :, hs] = kh[tm - rows:, :]
            tail_refs[2 * grp + 1][0, :, hs] = vh[tm - rows:, :]
            for ref, done, val in zip(out_refs, pieces, (qh, kh, vh)):
                if dil == 1:
                    ref[0, 0, :, hs] = val.astype(BF16)
                else:
                    done.append(val.astype(BF16))
        if dil > 1:
            perm = perm_ref[perm_dils.index(dil)]
            part = tm // dil
            for ref, done in zip(out_refs, pieces):
                for pair in range(C_HEADS // 2):
                    cols = slice(2 * pair * C_DH, (2 * pair + 2) * C_DH)
                    moved = _mm(perm, jnp.concatenate(done[2 * pair:2 * pair + 2], axis=1)).astype(BF16)
                    for r in range(dil):
                        ref[0, r, :, cols] = moved[r * part:(r + 1) * part, :]

    zb = 3 * n_groups * C_WIDTH
    z_ref[0] = _silu(_mm(h, w_ref[:, zb:zb + C_WIDTH])).astype(BF16)


def _c_in(x, g, w, cos, slo, shi, tm, per_row=False):
    B, T, D = x.shape
    row = lambda width: pl.BlockSpec((1, tm, width), lambda b, t: (b, t, 0))
    table = pl.BlockSpec((tm, C_DH), lambda b, t: (t, 0))
    dils = tuple(1 if per_row else dil for _, dil in C_GROUPS)
    qkv_specs, qkv_shapes = [], []
    for dil in dils:
        qkv_specs += [pl.BlockSpec((1, dil, tm // dil, C_WIDTH), lambda b, t: (b, 0, t, 0))] * 3
        qkv_shapes += [jax.ShapeDtypeStruct((B, dil, T // dil, C_WIDTH), BF16)] * 3
    tails, tail_specs, tail_shapes = [], [], []
    for win, _ in C_GROUPS:
        keep = T if per_row else min(win, T)
        rows = min(tm, keep)
        first_row = T - keep
        tails.append(rows)

        def tail_index(b, t, first_row=first_row, rows=rows):
            return (b, jnp.maximum(((t + 1) * tm - first_row) // rows - 1, 0), 0)

        tail_specs += [pl.BlockSpec((1, rows, C_WIDTH), tail_index)] * 2
        tail_shapes += [jax.ShapeDtypeStruct((B, keep, C_WIDTH), F32)] * 2
    n_qkv = len(qkv_specs)
    perms = []
    for dil in sorted({d for d in dils if d > 1}):
        out_row = jnp.arange(tm)
        src = (out_row % (tm // dil)) * dil + out_row // (tm // dil)
        perms.append(src[:, None] == jnp.arange(tm)[None, :])
    extra = [jnp.stack(perms).astype(BF16)] if perms else []
    outs = pl.pallas_call(
        functools.partial(_c_in_kernel, tails=tuple(tails), dils=dils),
        grid=(B, T // tm),
        in_specs=[row(D), _resident(g.shape), _resident(w.shape), table, table, table]
        + [_resident(a.shape) for a in extra],
        out_specs=qkv_specs + [row(C_WIDTH)] + tail_specs,
        out_shape=qkv_shapes + [jax.ShapeDtypeStruct((B, T, C_WIDTH), BF16)] + tail_shapes,
        compiler_params=_params("parallel", "arbitrary"),
        name="attn_in",
    )(x, g, w, cos, slo, shi, *extra)
    return outs[:n_qkv], outs[n_qkv], outs[n_qkv + 1:]


ATTN_CHUNK = 512
REGROUP_STRIDE = 4


def _attn_chunk_kernel(q_ref, kc_ref, vc_ref, *refs, span):
    acc_ref, st_ref = refs[-2:]
    kp_ref, vp_ref = refs[:-2] if len(refs) == 4 else (None, None)
    chunk = pl.program_id(2)
    qi = lax.broadcasted_iota(jnp.int32, (C_BLOCK, 2 * C_BLOCK), 0)
    kj = lax.broadcasted_iota(jnp.int32, (C_BLOCK, 2 * C_BLOCK), 1)
    dist = qi + C_BLOCK - kj
    band = (dist >= 0) & (dist <= span)
    bias_inner = jnp.where(band, 0.0, -1e30)
    bias_first = jnp.where(band & (kj >= jnp.where(chunk == 0, C_BLOCK, 0)), 0.0, -1e30)
    lane = lax.broadcasted_iota(jnp.int32, (C_BLOCK, LANES), 1)
    ones = jnp.ones((2 * C_BLOCK, LANES), BF16)

    for i in range(q_ref.shape[2] // C_BLOCK):
        rows = slice(i * C_BLOCK, (i + 1) * C_BLOCK)

        def keys(prev_ref, cur_ref, cols):
            if i == 0:
                before = cur_ref[0, 0, rows, cols] if prev_ref is None else prev_ref[0, 0, :, cols]
                return jnp.concatenate([before, cur_ref[0, 0, rows, cols]], axis=0)
            return cur_ref[0, 0, (i - 1) * C_BLOCK:(i + 1) * C_BLOCK, cols]

        head_cols = [slice(hd * C_DH, (hd + 1) * C_DH) for hd in range(C_HEADS)]
        s_all = [lax.dot_general(q_ref[0, 0, rows, cols], keys(kp_ref, kc_ref, cols), NT_DIMS,
                                 preferred_element_type=F32) for cols in head_cols]
        stats = jnp.zeros((C_BLOCK, LANES), F32)
        for hd, cols in enumerate(head_cols):
            s = s_all[hd] + (bias_first if i == 0 else bias_inner)
            mx = jnp.max(s, axis=-1, keepdims=True)
            p = jnp.exp2(s - mx).astype(BF16)
            both = _mm(p, jnp.concatenate([keys(vp_ref, vc_ref, cols), ones], axis=1))
            acc_ref[0, 0, rows, cols] = both[:, :C_DH].astype(BF16)
            stats = jnp.where(lane == hd, mx, jnp.where(lane == C_HEADS + hd, both[:, C_DH:], stats))
        st_ref[0, 0, rows, :] = stats


def _attn_prompt_group(q, k, v, span):
    B, dil, n, _ = q.shape
    qc = min(n, ATTN_CHUNK)
    per_chunk = qc // C_BLOCK
    cur = lambda rows, width: pl.BlockSpec((1, 1, rows, width), lambda b, r, c: (b, r, c, 0))
    prev = pl.BlockSpec((1, 1, C_BLOCK, C_WIDTH), lambda b, r, c: (b, r, jnp.maximum(c * per_chunk - 1, 0), 0))
    before = [(k, prev), (v, prev)] if n > qc else []
    return pl.pallas_call(
        functools.partial(_attn_chunk_kernel, span=span),
        grid=(B, dil, n // qc),
        in_specs=[cur(qc, C_WIDTH)] * 3 + [spec for _, spec in before],
        out_specs=[cur(qc, C_WIDTH), cur(qc, LANES)],
        out_shape=[jax.ShapeDtypeStruct((B, dil, n, C_WIDTH), BF16),
                   jax.ShapeDtypeStruct((B, dil, n, LANES), F32)],
        compiler_params=_params("parallel", "parallel", "arbitrary"),
        name="attn_chunks_d%d" % dil,
    )(q, k, v, *[a for a, _ in before])


def _merge_out_kernel(a0_ref, a1_ref, a2_ref, s0_ref, s1_ref, s2_ref, z_ref, x_ref, p_ref,
                      wo_ref, pg_ref, pe_ref, perm_ref, xo_ref, st_s):
    tm = x_ref.shape[1]
    a_refs, s_refs = (a0_ref, a1_ref, a2_ref), (s0_ref, s1_ref, s2_ref)
    dils = [a_ref.shape[1] for a_ref in a_refs]
    perm_dils = sorted({d for d in dils if d > 1})
    for grp, dil in enumerate(dils):
        for r in range(dil if dil > 1 else 0):
            st_s[grp, pl.ds(r, tm // dil, stride=dil), :] = s_refs[grp][0, r]

    def acc_of(grp, pair):
        cols = slice(2 * pair * C_DH, (2 * pair + 2) * C_DH)
        dil = dils[grp]
        if dil == 1:
            return a_refs[grp][0, 0, :, cols].astype(F32)
        by_class = jnp.concatenate([a_refs[grp][0, r, :, cols] for r in range(dil)], axis=0)
        return _mm(perm_ref[perm_dils.index(dil)], by_class)

    stats = [s_refs[grp][0, 0] if dils[grp] == 1 else st_s[grp] for grp in range(3)]
    mx = [st[:, 0:C_HEADS] for st in stats]
    ls = [st[:, C_HEADS:2 * C_HEADS] for st in stats]
    top = jnp.maximum(jnp.maximum(mx[0], mx[1]), mx[2])
    es = [jnp.exp2(m - top) for m in mx]
    den = es[0] * ls[0] + es[1] * ls[1] + es[2] * ls[2]
    coef = [e / den for e in es]
    x1 = x_ref[0]
    for pair in range(C_HEADS // 2):
        accs = [acc_of(grp, pair) for grp in range(3)]
        ys = []
        for half, hd in enumerate((2 * pair, 2 * pair + 1)):
            hs = slice(hd * C_DH, (hd + 1) * C_DH)
            inner = slice(half * C_DH, (half + 1) * C_DH)
            o = coef[0][:, hd:hd + 1] * accs[0][:, inner]
            for grp in (1, 2):
                o = o + coef[grp][:, hd:hd + 1] * accs[grp][:, inner]
            ys.append((o * z_ref[0, :, hs].astype(F32)).astype(BF16))
        x1 = x1 + _mm(jnp.concatenate(ys, axis=1), wo_ref[2 * pair * C_DH:(2 * pair + 2) * C_DH, :])
    xo_ref[0] = _embed(x1, p_ref[0], pg_ref, pe_ref)


def _merge_out(accs, stats, z, x, p_all, layer, wo, pg, pe, tm):
    B, T, D = x.shape
    row = lambda width: pl.BlockSpec((1, tm, width), lambda b, t: (b, t, 0))
    classes = lambda a: pl.BlockSpec((1, a.shape[1], tm // a.shape[1], a.shape[3]), lambda b, t: (b, 0, t, 0))
    p_spec = pl.BlockSpec((None, 1, tm, p_all.shape[-1]), lambda b, t: (layer, b, t, 0))
    perms = []
    for dil in sorted({a.shape[1] for a in accs if a.shape[1] > 1}):
        seq_row = jnp.arange(tm)
        src = (seq_row % dil) * (tm // dil) + seq_row // dil
        perms.append(src[:, None] == jnp.arange(tm)[None, :])
    perm = jnp.stack(perms).astype(BF16)
    return pl.pallas_call(
        _merge_out_kernel,
        grid=(B, T // tm),
        in_specs=[classes(a) for a in accs] + [classes(s) for s in stats] + [row(C_WIDTH), row(D), p_spec,
                  _resident(wo.shape), _resident(pg.shape), _resident(pe.shape), _resident(perm.shape)],
        out_specs=row(D),
        out_shape=jax.ShapeDtypeStruct((B, T, D), F32),
        scratch_shapes=[pltpu.VMEM((3, tm, LANES), F32)],
        compiler_params=_params("parallel", "parallel"),
        name="attn_merge_out",
    )(*accs, *stats, z, x, p_all, wo, pg, pe, perm)


def _attn_sample_kernel(*refs):
    n_groups = len(C_GROUPS)
    z_ref, y_ref = refs[5 * n_groups], refs[5 * n_groups + 1]
    parts = []
    for grp in range(n_groups):
        q_ref, kn_ref, vn_ref, kc_ref, vc_ref = refs[5 * grp:5 * grp + 5]
        q = q_ref[0].astype(F32)
        s_old = jnp.sum(kc_ref[0] * q[None], axis=-1, keepdims=True)
        s_new = jnp.sum(kn_ref[0] * q, axis=-1, keepdims=True)
        mx = jnp.maximum(jnp.max(s_old, axis=0), s_new)
        p_old = jnp.exp2(s_old - mx[None])
        p_new = jnp.exp2(s_new - mx)
        l = jnp.sum(p_old, axis=0) + p_new
        acc = jnp.sum(p_old * vc_ref[0], axis=0) + p_new * vn_ref[0]
        parts.append((acc, mx, l))
    top = jnp.maximum(jnp.maximum(parts[0][1], parts[1][1]), parts[2][1])
    es = [jnp.exp2(m - top) for _, m, _ in parts]
    num = es[0] * parts[0][0] + es[1] * parts[1][0] + es[2] * parts[2][0]
    den = es[0] * parts[0][2] + es[1] * parts[1][2] + es[2] * parts[2][2]
    y_ref[0] = (num / den * z_ref[0].astype(F32)).astype(BF16)


def _attn_sample(qkv, tails, caches, z):
    Bs = z.shape[1]
    heads = lambda a: a.reshape(Bs, C_HEADS, C_DH)
    row = pl.BlockSpec((1, C_HEADS, C_DH), lambda b: (b, 0, 0))
    args, specs = [], []
    for grp, (win, dil) in enumerate(C_GROUPS):
        kc, vc = caches[grp]
        n_buf = kc.shape[1]
        assert n_buf == win and win % dil == 0, "sample window must be fully cached"
        span = win // dil
        view = lambda a: a.reshape(Bs, span, dil, C_HEADS, C_DH)
        cache_spec = pl.BlockSpec((1, span, None, C_HEADS, C_DH), lambda b: (b, 0, 0, 0, 0))
        args += [heads(qkv[3 * grp]), heads(tails[2 * grp]), heads(tails[2 * grp + 1]), view(kc), view(vc)]
        specs += [row, row, row, cache_spec, cache_spec]
    return pl.pallas_call(
        _attn_sample_kernel,
        grid=(Bs,),
        in_specs=specs + [row],
        out_specs=row,
        out_shape=jax.ShapeDtypeStruct((Bs, C_HEADS, C_DH), BF16),
        compiler_params=_params("parallel"),
        name="attn_sample",
    )(*args, heads(z))


def _pool_tail(x, acc, p, pg_ref, pe_ref, fg_ref):
    return _rms(_embed(x + acc, p, pg_ref, pe_ref), fg_ref[...])


def _pool_prompt_kernel(x_ref, p_ref, g_ref, win_ref, wgrp_ref, sc_ref, wo_ref, pg_ref, pe_ref, fg_ref,
                        xo_ref, st_ref, pad_s):
    E = wo_ref.shape[0]
    G = E // len(D_WINDOWS)
    t = pl.program_id(1)

    @pl.when(t == 0)
    def _():
        pad_s[0:POOL_PAD, :] = jnp.zeros((POOL_PAD, E), F32)

    x = x_ref[0]
    tm = x.shape[0]
    h = _rms(x, g_ref[...]).astype(BF16)
    pos = t * tm + lax.broadcasted_iota(jnp.int32, (tm, 1), 0)
    acc = jnp.zeros(x.shape, F32)
    for grp, w in enumerate(D_WINDOWS):
        cols = slice(grp * G, (grp + 1) * G)
        xp = _mm(h, win_ref[:, cols])
        z = _mm(h, win_ref[:, E + grp * G:E + (grp + 1) * G])
        pad_s[POOL_PAD:, cols] = xp
        wsum = pad_s[:, cols]
        shift = 1
        while shift < w:
            wsum = wsum + pltpu.roll(wsum, shift, 0)
            shift *= 2
        cnt = jnp.minimum(w, pos + 1).astype(F32)
        r = wsum[POOL_PAD:, :] / cnt - xp
        y = _mm(r.astype(BF16), wgrp_ref[grp]) * sc_ref[:, cols]
        acc = acc + _mm((y * _silu(z)).astype(BF16), wo_ref[cols, :])
        pad_s[0:POOL_PAD, cols] = xp[tm - POOL_PAD:, :]
    xo_ref[0] = _pool_tail(x, acc, p_ref[0], pg_ref, pe_ref, fg_ref)
    st_ref[0] = pad_s[0:POOL_PAD, :]


def _pool_prompt(x, p_all, layer, g, win, wgrp, sc, wo, pg, pe, fg, tm):
    B, T, D = x.shape
    E = wo.shape[0]
    row = lambda width: pl.BlockSpec((1, tm, width), lambda b, t: (b, t, 0))
    p_spec = pl.BlockSpec((None, 1, tm, p_all.shape[-1]), lambda b, t: (layer, b, t, 0))
    return pl.pallas_call(
        _pool_prompt_kernel,
        grid=(B, T // tm),
        in_specs=[row(D), p_spec] + [_resident(a.shape) for a in (g, win, wgrp, sc, wo, pg, pe, fg)],
        out_specs=[row(D), pl.BlockSpec((1, POOL_PAD, E), lambda b, t: (b, 0, 0))],
        out_shape=[jax.ShapeDtypeStruct((B, T, D), F32), jax.ShapeDtypeStruct((B, POOL_PAD, E), F32)],
        scratch_shapes=[pltpu.VMEM((POOL_PAD + tm, E), F32)],
        compiler_params=_params("parallel", "arbitrary"),
        name="pool_prompt",
    )(x, p_all, g, win, wgrp, sc, wo, pg, pe, fg)


def _pool_sample_kernel(x_ref, p_ref, g_ref, win_ref, wgrp_ref, sc_ref, wo_ref, pg_ref, pe_ref, fg_ref, st_ref,
                        xo_ref, xp_ref, *, pos):
    E = wo_ref.shape[0]
    G = E // len(D_WINDOWS)
    x = x_ref[0]
    h = _rms(x, g_ref[...]).astype(BF16)
    acc = jnp.zeros(x.shape, F32)
    for grp, w in enumerate(D_WINDOWS):
        cols = slice(grp * G, (grp + 1) * G)
        xp = _mm(h, win_ref[:, cols])
        z = _mm(h, win_ref[:, E + grp * G:E + (grp + 1) * G])
        wsum = xp
        for back in range(1, w):
            lo = (POOL_STATE - back) * E + grp * G
            wsum = wsum + st_ref[:, lo:lo + G]
        r = wsum / float(min(w, pos + 1)) - xp
        y = _mm(r.astype(BF16), wgrp_ref[grp]) * sc_ref[:, cols]
        acc = acc + _mm((y * _silu(z)).astype(BF16), wo_ref[cols, :])
        xp_ref[:, cols] = xp
    xo_ref[0] = _pool_tail(x, acc, p_ref[0], pg_ref, pe_ref, fg_ref)


def _pool_sample(x, p_all, layer, g, win, wgrp, sc, wo, pg, pe, fg, st, pos):
    _, M, D = x.shape
    E = wo.shape[0]
    p_spec = pl.BlockSpec((None, 1, M, p_all.shape[-1]), lambda i: (layer, 0, 0, 0))
    return pl.pallas_call(
        functools.partial(_pool_sample_kernel, pos=pos),
        grid=(1,),
        in_specs=[_resident(x.shape), p_spec] + [_resident(a.shape) for a in (g, win, wgrp, sc, wo, pg, pe, fg, st)],
        out_specs=[_whole(x.shape), _whole((M, E))],
        out_shape=[jax.ShapeDtypeStruct(x.shape, F32), jax.ShapeDtypeStruct((M, E), F32)],
        compiler_params=_params("arbitrary"),
        name="pool_sample",
    )(x, p_all, g, win, wgrp, sc, wo, pg, pe, fg, st)


def _rope_tables(pos):
    half = ROPE_DIM // 2
    inv = ROPE_THETA ** (-jnp.arange(half, dtype=F32) / half)
    ang = pos.astype(F32)[:, None] * inv[None, :]
    cos, sin = jnp.cos(ang), jnp.sin(ang)
    rest = C_DH - ROPE_DIM
    n = pos.shape[0]
    cos_t = jnp.concatenate([cos, cos, jnp.ones((n, rest), F32)], axis=1)
    sin_lo = jnp.concatenate([-sin, jnp.zeros((n, half + rest), F32)], axis=1)
    sin_hi = jnp.concatenate([jnp.zeros((n, half), F32), sin, jnp.zeros((n, rest), F32)], axis=1)
    return cos_t, sin_lo, sin_hi


def _row(a):
    return a.reshape(1, -1).astype(F32)


def kernel(x_prompt, x_sample, state_mlstm_C, state_mlstm_n, state_mlstm_m, state_conv, cache_k_w128, cache_v_w128, cache_k_w512, cache_v_w512, cache_k_w2048, cache_v_w2048, state_pool, p_prompt, p_sample, norm_g, pe_w, pg_w, final_g, a_w_in, a_b_if, a_norm_g, a_w_out, b_w_in, b_conv_w, b_w_out, c_w_in, c_w_out, d_w_in, d_w_grp, d_scale, d_w_out):
    B, T, D = x_prompt.shape
    Bs, Ts, _ = x_sample.shape
    assert Ts == 1 and norm_g.shape[0] == 4 and a_w_in.shape[0] == 1
    assert T % 512 == 0 and all(T % (dil * C_BLOCK) == 0 for _, dil in C_GROUPS)
    assert Bs % 8 == 0

    bf = lambda a: a.astype(BF16)
    qk = A_HEADS * A_DK
    n_main = 2 * qk + 3 * A_WIDTH
    a_w = bf(a_w_in[0])
    a_wg = bf(jnp.pad(a_w_in[0, :, n_main:], ((0, 0), (0, LANES - 2 * A_HEADS))))
    a_bif = jnp.pad(a_b_if[0], (0, LANES - 2 * A_HEADS)).reshape(1, LANES)
    a_ng = _row(a_norm_g[0])
    a_wo, b_wi, b_wo, c_wi, c_wo, d_wi, d_wg, d_wo = (
        bf(a_w_out[0]), bf(b_w_in[0]), bf(b_w_out[0]), bf(c_w_in[0]), bf(c_w_out[0]),
        bf(d_w_in[0]), bf(d_w_grp[0]), bf(d_w_out[0]))
    pg, pe = bf(pg_w), bf(pe_w)
    gs = [_row(norm_g[i]) for i in range(4)]
    fg = _row(final_g)
    b_cw = b_conv_w[0]
    d_sc = _row(d_scale[0])
    caches = [(cache_k_w128[0], cache_v_w128[0]), (cache_k_w512[0], cache_v_w512[0]),
              (cache_k_w2048[0], cache_v_w2048[0])]

    x = x_prompt
    q, k, v, oz, gates, kt = _a_in(x, gs[0], a_w, a_wg, a_bif, 512, transposed_k=True)
    hh, c_p, n_p, m_p = _mlstm_prompt(q, k, kt, v, gates)
    x = _a_out(hh, oz, a_ng, x, p_prompt, 0, a_wo, pg[0], pe[0], 512)
    x, conv_p = _conv_prompt(x, p_prompt, 1, gs[1], b_wi, b_cw, b_wo, pg[1], pe[1], 512)
    qkv, z, tails_p = _c_in(x, gs[2], c_wi, *_rope_tables(jnp.arange(T)), 256)
    accs, stats = [], []
    for grp, (win, dil) in enumerate(C_GROUPS):
        acc, st = _attn_prompt_group(qkv[3 * grp], qkv[3 * grp + 1], qkv[3 * grp + 2], win // dil)
        accs.append(acc)
        stats.append(st)
    x = _merge_out(accs, stats, z, x, p_prompt, 2, c_wo, pg[2], pe[2], 512)
    y_prompt, pool_p = _pool_prompt(x, p_prompt, 3, gs[3], d_wi, d_wg, d_sc, d_wo, pg[3], pe[3], fg, 512)

    xs = x_sample.reshape(1, Bs, D)
    ps = p_sample.reshape(p_sample.shape[0], 1, Bs, p_sample.shape[-1])
    q, k, v, oz, gates = _a_in(xs, gs[0], a_w, a_wg, a_bif, Bs)
    m_lanes = jnp.pad(state_mlstm_m[0], ((0, 0), (0, LANES - A_HEADS)))
    hh, c_s, n_s, m_s = _mlstm_sample(q[0], k[0], v[0], gates[0], state_mlstm_C[0], state_mlstm_n[0], m_lanes)
    m_s = m_s[:, :, 0].T
    xs = _a_out(hh.reshape(1, Bs, A_WIDTH), oz, a_ng, xs, ps, 0, a_wo, pg[0], pe[0], Bs)
    xs, cx = _conv_sample(xs, ps, 1, gs[1], b_wi, b_cw, b_wo, pg[1], pe[1],
                          state_conv[0, :, 0, :], state_conv[0, :, 1, :])
    conv_s = jnp.stack([state_conv[0, :, 1, :], cx], axis=1)
    tables = [jnp.broadcast_to(t, (Bs, C_DH)) for t in _rope_tables(PAST_LEN + jnp.arange(1))]
    qkv, z, tails_s = _c_in(xs, gs[2], c_wi, *tables, Bs, per_row=True)
    y = _attn_sample(qkv, tails_s, caches, z)
    xs = _out_embed(y.reshape(1, Bs, C_WIDTH), xs, ps, 2, c_wo, pg[2], pe[2], Bs)
    pool_flat = state_pool[0].reshape(Bs, POOL_STATE * state_pool.shape[-1])
    ys, xp = _pool_sample(xs, ps, 3, gs[3], d_wi, d_wg, d_sc, d_wo, pg[3], pe[3], fg, pool_flat, PAST_LEN)
    pool_s = jnp.concatenate([state_pool[0, :, 1:, :], xp[:, None, :]], axis=1)

    heads = lambda a, lead: a.reshape(1, lead, a.shape[1], C_HEADS, C_DH)
    kv_out = []
    for grp in range(len(C_GROUPS)):
        for j in range(2):
            kv_out += [heads(tails_p[2 * grp + j], B), tails_s[2 * grp + j].reshape(1, Bs, 1, C_HEADS, C_DH)]
    return (y_prompt, ys.reshape(Bs, 1, D),
            c_p[None], c_s[None], n_p[None], n_s[None], m_p[None, :, 0, :A_HEADS], m_s.reshape(1, Bs, A_HEADS),
            conv_p[None], conv_s[None],
            *kv_out,
            pool_p[None, :, 1:, :], pool_s[None])
```

```python
import functools

import jax
import jax.numpy as jnp
from jax import lax
from jax.experimental import pallas as pl
from jax.experimental.pallas import tpu as pltpu

F32 = jnp.float32
BF16 = jnp.bfloat16

EPS = 1e-6
PAST_LEN = 8192

A_HEADS = 8
A_DK = 128
A_DV = 256
A_WIDTH = A_HEADS * A_DV
A_CHUNK = 128
A_CHUNKS_PER_STEP = 2

CONV_W = 3

C_HEADS = 8
C_DH = 128
C_WIDTH = C_HEADS * C_DH
C_GROUPS = ((128, 1), (512, 4), (2048, 16))
C_BLOCK = 128
ROPE_DIM = C_DH // 4
ROPE_THETA = 500000.0
LOG2_E = 1.4426950408889634
Q_SCALE = C_DH ** -0.5 * LOG2_E

D_WINDOWS = (2, 4, 8, 16)
POOL_STATE = max(D_WINDOWS) - 1
POOL_PAD = POOL_STATE + 1

LANES = 128
VMEM_LIMIT_BYTES = 56 * 2 ** 20

NT_DIMS = (((1,), (1,)), ((), ()))
TN_DIMS = (((0,), (0,)), ((), ()))


def _params(*semantics):
    return pltpu.CompilerParams(dimension_semantics=semantics, vmem_limit_bytes=VMEM_LIMIT_BYTES)


def _resident(shape):
    zeros = (0,) * len(shape)
    return pl.BlockSpec(shape, lambda *_: zeros, pipeline_mode=pl.Buffered(1))


def _whole(shape):
    zeros = (0,) * len(shape)
    return pl.BlockSpec(shape, lambda *_: zeros)


def _mm(a, b):
    return jnp.dot(a, b, preferred_element_type=F32)


def _rms(x, g):
    return x * lax.rsqrt(jnp.mean(x * x, axis=-1, keepdims=True) + EPS) * g


def _sigmoid(x):
    return 0.5 * jnp.tanh(0.5 * x) + 0.5


def _silu(x):
    return x * _sigmoid(x)


def _log_sigmoid(x):
    return jnp.minimum(x, 0.0) - jnp.log(1.0 + jnp.exp(-jnp.abs(x)))


def _embed(x1, p, pg_ref, pe_ref):
    gate = _sigmoid(_mm(x1.astype(BF16), pg_ref[...]))
    return x1 + gate * _mm(p.astype(BF16), pe_ref[...])


def _a_in_kernel(x_ref, g_ref, w_ref, wg_ref, bif_ref,
                 q_ref, k_ref, v_ref, oz_ref, gate_ref, *maybe_kt_ref):
    h = _rms(x_ref[0], g_ref[...]).astype(BF16)
    qk = A_HEADS * A_DK

    def proj(lo, width):
        return _mm(h, w_ref[:, lo:lo + width])

    q_ref[0] = proj(0, qk).astype(BF16)
    k = proj(qk, qk) * (A_DK ** -0.5)
    k_ref[0] = k.astype(BF16)
    for kt_ref in maybe_kt_ref:
        kt_ref[0] = k.T.astype(BF16)
    v_ref[0] = proj(2 * qk, A_WIDTH).astype(BF16)
    oz_ref[0] = (_sigmoid(proj(2 * qk + A_WIDTH, A_WIDTH))
                 * _silu(proj(2 * qk + 2 * A_WIDTH, A_WIDTH))).astype(BF16)
    gates = _mm(h, wg_ref[...]) + bif_ref[...]
    lane = lax.broadcasted_iota(jnp.int32, gates.shape, 1)
    gate_ref[0] = jnp.where(lane < A_HEADS, gates, _log_sigmoid(gates))


def _a_in(x, g, w, wg, bif, tm, transposed_k=False):
    B, T, D = x.shape
    qk = A_HEADS * A_DK
    row = lambda width: pl.BlockSpec((1, tm, width), lambda b, t: (b, t, 0))
    out_specs = [row(qk), row(qk), row(A_WIDTH), row(A_WIDTH), row(LANES)]
    out_shape = [jax.ShapeDtypeStruct((B, T, qk), BF16), jax.ShapeDtypeStruct((B, T, qk), BF16),
                 jax.ShapeDtypeStruct((B, T, A_WIDTH), BF16), jax.ShapeDtypeStruct((B, T, A_WIDTH), BF16),
                 jax.ShapeDtypeStruct((B, T, LANES), F32)]
    if transposed_k:
        out_specs.append(pl.BlockSpec((1, qk, tm), lambda b, t: (b, 0, t)))
        out_shape.append(jax.ShapeDtypeStruct((B, qk, T), BF16))
    return pl.pallas_call(
        _a_in_kernel,
        grid=(B, T // tm),
        in_specs=[row(D), _resident(g.shape), _resident(w.shape), _resident(wg.shape), _resident(bif.shape)],
        out_specs=out_specs,
        out_shape=out_shape,
        compiler_params=_params("parallel", "parallel"),
        name="mlstm_in",
    )(x, g, w, wg, bif)


def _mlstm_chunk_kernel(q_ref, k_ref, kt_ref, v_ref, gate_ref,
                        hh_ref, c_out, n_out, m_out, m_s, *state_s):
    chunk = pl.program_id(1)
    ct_s, nb_s = state_s[:A_HEADS], state_s[A_HEADS:]

    @pl.when(chunk == 0)
    def _():
        for ref in state_s:
            ref[...] = jnp.zeros_like(ref)
        m_s[...] = jnp.zeros_like(m_s)

    L = A_CHUNK
    row = lax.broadcasted_iota(jnp.int32, (L, LANES), 0)
    ti = lax.broadcasted_iota(jnp.int32, (L, L), 0)
    si = lax.broadcasted_iota(jnp.int32, (L, L), 1)
    causal = ti >= si
    ones = jnp.ones((L, LANES), BF16)
    heads = range(A_HEADS)
    ks = [slice(h * A_DK, (h + 1) * A_DK) for h in heads]
    vs = [slice(h * A_DV, (h + 1) * A_DV) for h in heads]

    def prefix(x, op, identity):
        step = 1
        while step < L:
            x = op(x, jnp.where(row >= step, pltpu.roll(x, step, 0), identity))
            step *= 2
        return x

    def one_chunk(rows):
        gates = gate_ref[0, rows, :]
        b = pltpu.roll(prefix(gates, jnp.add, 0.0), LANES - A_HEADS, 1)
        c = gates - b
        m_row = m_s[0:1, :]
        top = jnp.maximum(m_row, prefix(c, jnp.maximum, -jnp.inf))
        top_last = top[L - 1:L, :]
        g_all = jnp.exp(m_row - top)
        floor_all = jnp.exp(-(b + top))
        w_all = jnp.exp(c - top_last)
        decay_row = jnp.exp(m_row - top_last)
        m_s[...] = jnp.broadcast_to(b[L - 1:L, :] + top_last, m_s.shape)
        c_t = c.T

        s_raw = [_mm(q_ref[0, rows, ks[h]], kt_ref[0, ks[h], rows]) for h in heads]
        upd = []
        for h in heads:
            w_b = jnp.broadcast_to(w_all[:, h:h + 1], (L, LANES))
            wv = jnp.concatenate([w_b, w_b], axis=1) * v_ref[0, rows, vs[h]].astype(F32)
            upd.append(_mm(kt_ref[0, ks[h], rows], jnp.concatenate([wv.astype(BF16), w_b.astype(BF16)], axis=1)))

        for h in heads:
            ct = ct_s[h][...]
            nb = nb_s[h][...]
            dmat = jnp.exp(jnp.where(causal, -top[:, h:h + 1] + c_t[h:h + 1, :], -jnp.inf))
            s = s_raw[h] * dmat
            gq = g_all[:, h:h + 1] * q_ref[0, rows, ks[h]].astype(F32)
            lhs = jnp.concatenate([s.astype(BF16), gq.astype(BF16)], axis=1)
            num = _mm(lhs, jnp.concatenate([v_ref[0, rows, vs[h]], ct.astype(BF16)], axis=0))
            den = _mm(lhs, jnp.concatenate([ones, nb.astype(BF16)], axis=0))
            scale = 1.0 / jnp.maximum(jnp.abs(den), floor_all[:, h:h + 1])
            hh_ref[0, rows, vs[h]] = (num * jnp.concatenate([scale, scale], axis=1)).astype(BF16)
            decay = decay_row[:, h:h + 1]
            ct_s[h][...] = decay * ct + upd[h][:, :A_DV]
            nb_s[h][...] = decay * nb + upd[h][:, A_DV:]

    for sub in range(q_ref.shape[1] // L):
        one_chunk(slice(sub * L, (sub + 1) * L))

    @pl.when(chunk == pl.num_programs(1) - 1)
    def _():
        for h in range(A_HEADS):
            c_out[0, h] = ct_s[h][...].T
            n_out[0, h:h + 1, :] = nb_s[h][...].T[0:1, :]
        m_out[0] = m_s[...]


def _mlstm_prompt(q, k, kt, v, gates):
    B, T, _ = q.shape
    L = A_CHUNK * A_CHUNKS_PER_STEP
    qk = A_HEADS * A_DK
    row = lambda width: pl.BlockSpec((1, L, width), lambda b, c: (b, c, 0))
    state = lambda *shape: pl.BlockSpec((1,) + shape, lambda b, c: (b,) + (0,) * len(shape))
    return pl.pallas_call(
        _mlstm_chunk_kernel,
        grid=(B, T // L),
        in_specs=[row(qk), row(qk), pl.BlockSpec((1, qk, L), lambda b, c: (b, 0, c)), row(A_WIDTH), row(LANES)],
        out_specs=[row(A_WIDTH), state(A_HEADS, A_DV, A_DK), state(A_HEADS, A_DK), state(A_HEADS, LANES)],
        out_shape=[jax.ShapeDtypeStruct((B, T, A_WIDTH), BF16),
                   jax.ShapeDtypeStruct((B, A_HEADS, A_DV, A_DK), F32),
                   jax.ShapeDtypeStruct((B, A_HEADS, A_DK), F32),
                   jax.ShapeDtypeStruct((B, A_HEADS, LANES), F32)],
        scratch_shapes=[pltpu.VMEM((A_HEADS, LANES), F32)] + [pltpu.VMEM((A_DK, A_DV), F32)] * A_HEADS
        + [pltpu.VMEM((A_DK, LANES), F32)] * A_HEADS,
        compiler_params=_params("parallel", "arbitrary"),
        name="mlstm_chunks",
    )(q, k, kt, v, gates)


def _mlstm_step_kernel(q_ref, k_ref, v_ref, gate_ref, m_ref, c_ref, n_ref,
                       hh_ref, c_out, n_out, m_out, hht_s):
    head = pl.program_id(0)
    Bs = q_ref.shape[0]
    q = q_ref[...].astype(F32)
    k = k_ref[...].astype(F32)
    n = n_ref[...]
    lane = lax.broadcasted_iota(jnp.int32, (Bs, LANES), 1)
    pick = lambda a, idx: jnp.sum(jnp.where(lane == idx, a, 0.0), axis=-1, keepdims=True)
    ig = pick(gate_ref[...], head)
    lf = pick(gate_ref[...], A_HEADS + head)
    m_prev = pick(m_ref[...], head)

    inter = lf + m_prev
    m_t = jnp.maximum(inter, ig)
    dm = jnp.exp(ig - m_t)
    g = jnp.exp(inter - m_t)
    s = jnp.sum(q * k, axis=-1, keepdims=True) * dm
    den = s + g * jnp.sum(q * n, axis=-1, keepdims=True)
    scale = 1.0 / jnp.maximum(jnp.abs(den), jnp.exp(-m_t))
    n_out[...] = g * n + dm * k
    m_out[...] = jnp.broadcast_to(m_t, (Bs, LANES))

    v_pad = jnp.concatenate([v_ref[...].astype(F32), jnp.zeros((LANES - Bs, A_DV), F32)], axis=0)
    vt = v_pad.T
    hht_s[...] = jnp.zeros_like(hht_s)
    for b in range(Bs):
        c_b = c_ref[b]
        one = slice(b, b + 1)
        v_col = vt[:, one]
        cq = jnp.sum(c_b * q[one, :], axis=-1, keepdims=True)
        hht_s[:, one] = (s[one, :] * v_col + g[one, :] * cq) * scale[one, :]
        c_out[b] = g[one, :] * c_b + (dm[one, :] * v_col) * k[one, :]
    hh_ref[...] = hht_s[...].T[0:Bs, :].astype(BF16)


def _mlstm_sample(q, k, v, gates, c0, n0, m0):
    Bs = q.shape[0]
    assert Bs <= LANES
    cols = lambda width: pl.BlockSpec((Bs, width), lambda h: (0, h))
    full = pl.BlockSpec((Bs, LANES), lambda h: (0, 0))
    c_spec = pl.BlockSpec((Bs, None, A_DV, A_DK), lambda h: (0, h, 0, 0))
    hh, c1, n1, m1 = pl.pallas_call(
        _mlstm_step_kernel,
        grid=(A_HEADS,),
        in_specs=[cols(A_DK), cols(A_DK), cols(A_DV), full, full, c_spec, cols(A_DK)],
        out_specs=[cols(A_DV), c_spec, cols(A_DK), pl.BlockSpec((None, Bs, LANES), lambda h: (h, 0, 0))],
        out_shape=[jax.ShapeDtypeStruct((Bs, A_WIDTH), BF16),
                   jax.ShapeDtypeStruct(c0.shape, F32), jax.ShapeDtypeStruct((Bs, A_HEADS * A_DK), F32),
                   jax.ShapeDtypeStruct((A_HEADS, Bs, LANES), F32)],
        scratch_shapes=[pltpu.VMEM((A_DV, LANES), F32)],
        compiler_params=_params("parallel"),
        name="mlstm_step",
    )(q, k, v, gates, m0, c0, n0.reshape(Bs, A_HEADS * A_DK))
    return hh, c1, n1.reshape(n0.shape), m1


def _a_out_kernel(hh_ref, oz_ref, ng_ref, x_ref, p_ref, wo_ref, pg_ref, pe_ref, xo_ref):
    x1 = x_ref[0]
    for h in range(A_HEADS):
        vs = slice(h * A_DV, (h + 1) * A_DV)
        hh = hh_ref[0, :, vs].astype(F32)
        hn = hh * lax.rsqrt(jnp.mean(hh * hh, axis=-1, keepdims=True) + EPS) * ng_ref[:, vs]
        y = hn * oz_ref[0, :, vs].astype(F32)
        x1 = x1 + _mm(y.astype(BF16), wo_ref[vs, :])
    xo_ref[0] = _embed(x1, p_ref[0], pg_ref, pe_ref)


def _a_out(hh, oz, ng, x, p_all, layer, wo, pg, pe, tm):
    B, T, D = x.shape
    row = lambda width: pl.BlockSpec((1, tm, width), lambda b, t: (b, t, 0))
    p_spec = pl.BlockSpec((None, 1, tm, p_all.shape[-1]), lambda b, t: (layer, b, t, 0))
    return pl.pallas_call(
        _a_out_kernel,
        grid=(B, T // tm),
        in_specs=[row(A_WIDTH), row(A_WIDTH), _resident(ng.shape), row(D), p_spec,
                  _resident(wo.shape), _resident(pg.shape), _resident(pe.shape)],
        out_specs=row(D),
        out_shape=jax.ShapeDtypeStruct((B, T, D), F32),
        compiler_params=_params("parallel", "parallel"),
        name="mlstm_out",
    )(hh, oz, ng, x, p_all, wo, pg, pe)


def _out_kernel(y_ref, x_ref, p_ref, wo_ref, pg_ref, pe_ref, xo_ref):
    x1 = x_ref[0] + _mm(y_ref[0], wo_ref[...])
    xo_ref[0] = _embed(x1, p_ref[0], pg_ref, pe_ref)


def _out_embed(y, x, p_all, layer, wo, pg, pe, tm):
    B, T, D = x.shape
    row = lambda width: pl.BlockSpec((1, tm, width), lambda b, t: (b, t, 0))
    p_spec = pl.BlockSpec((None, 1, tm, p_all.shape[-1]), lambda b, t: (layer, b, t, 0))
    return pl.pallas_call(
        _out_kernel,
        grid=(B, T // tm),
        in_specs=[row(y.shape[-1]), row(D), p_spec, _resident(wo.shape), _resident(pg.shape), _resident(pe.shape)],
        out_specs=row(D),
        out_shape=jax.ShapeDtypeStruct((B, T, D), F32),
        compiler_params=_params("parallel", "parallel"),
        name="out_embed",
    )(y, x, p_all, wo, pg, pe)


CONV_COLS = 512


def _conv_prompt_kernel(x_ref, p_ref, g_ref, win_ref, cw_ref, wo_ref, pg_ref, pe_ref,
                        xo_ref, st_ref, carry_s):
    E = wo_ref.shape[0]

    @pl.when(pl.program_id(1) == 0)
    def _():
        carry_s[...] = jnp.zeros_like(carry_s)

    x = x_ref[0]
    tm = x.shape[0]
    h = _rms(x, g_ref[...]).astype(BF16)
    row = lax.broadcasted_iota(jnp.int32, (tm, CONV_COLS), 0)
    acc = jnp.zeros(x.shape, F32)
    for c in range(E // CONV_COLS):
        cols = slice(c * CONV_COLS, (c + 1) * CONV_COLS)
        part = lambda i: _mm(h, win_ref[:, i * E + c * CONV_COLS:i * E + (c + 1) * CONV_COLS])
        bg, cg, xb, z = part(0), part(1), part(2), part(3)
        cx = cg * xb
        old = carry_s[0:1, cols]
        new = carry_s[1:2, cols]
        prev1 = jnp.where(row == 0, new, pltpu.roll(cx, 1, 0))
        prev2 = jnp.where(row == 0, old, jnp.where(row == 1, new, pltpu.roll(cx, 2, 0)))
        y = cw_ref[0:1, cols] * prev2 + cw_ref[1:2, cols] * prev1 + cw_ref[2:3, cols] * cx
        acc = acc + _mm((bg * y * _silu(z)).astype(BF16), wo_ref[cols, :])
        carry_s[0:2, cols] = cx[tm - 2:tm, :]
    xo_ref[0] = _embed(x + acc, p_ref[0], pg_ref, pe_ref)
    st_ref[0] = carry_s[0:2, :]


def _conv_prompt(x, p_all, layer, g, win, cw, wo, pg, pe, tm):
    B, T, D = x.shape
    E = wo.shape[0]
    row = lambda width: pl.BlockSpec((1, tm, width), lambda b, t: (b, t, 0))
    p_spec = pl.BlockSpec((None, 1, tm, p_all.shape[-1]), lambda b, t: (layer, b, t, 0))
    return pl.pallas_call(
        _conv_prompt_kernel,
        grid=(B, T // tm),
        in_specs=[row(D), p_spec, _resident(g.shape), _resident(win.shape), _resident(cw.shape),
                  _resident(wo.shape), _resident(pg.shape), _resident(pe.shape)],
        out_specs=[row(D), pl.BlockSpec((1, CONV_W - 1, E), lambda b, t: (b, 0, 0))],
        out_shape=[jax.ShapeDtypeStruct((B, T, D), F32), jax.ShapeDtypeStruct((B, CONV_W - 1, E), F32)],
        scratch_shapes=[pltpu.VMEM((8, E), F32)],
        compiler_params=_params("parallel", "arbitrary"),
        name="conv_prompt",
    )(x, p_all, g, win, cw, wo, pg, pe)


def _conv_sample_kernel(x_ref, p_ref, g_ref, win_ref, cw_ref, wo_ref, pg_ref, pe_ref, s0_ref, s1_ref,
                        xo_ref, cx_ref):
    E = wo_ref.shape[0]
    x = x_ref[0]
    h = _rms(x, g_ref[...]).astype(BF16)
    acc = jnp.zeros(x.shape, F32)
    for c in range(E // CONV_COLS):
        cols = slice(c * CONV_COLS, (c + 1) * CONV_COLS)
        part = lambda i: _mm(h, win_ref[:, i * E + c * CONV_COLS:i * E + (c + 1) * CONV_COLS])
        bg, cg, xb, z = part(0), part(1), part(2), part(3)
        cx = cg * xb
        y = cw_ref[0:1, cols] * s0_ref[:, cols] + cw_ref[1:2, cols] * s1_ref[:, cols] + cw_ref[2:3, cols] * cx
        acc = acc + _mm((bg * y * _silu(z)).astype(BF16), wo_ref[cols, :])
        cx_ref[:, cols] = cx
    xo_ref[0] = _embed(x + acc, p_ref[0], pg_ref, pe_ref)


def _conv_sample(x, p_all, layer, g, win, cw, wo, pg, pe, s0, s1):
    _, M, D = x.shape
    E = wo.shape[0]
    p_spec = pl.BlockSpec((None, 1, M, p_all.shape[-1]), lambda i: (layer, 0, 0, 0))
    return pl.pallas_call(
        _conv_sample_kernel,
        grid=(1,),
        in_specs=[_resident(x.shape), p_spec, _resident(g.shape), _resident(win.shape), _resident(cw.shape),
                  _resident(wo.shape), _resident(pg.shape), _resident(pe.shape),
                  _resident(s0.shape), _resident(s1.shape)],
        out_specs=[_whole(x.shape), _whole((M, E))],
        out_shape=[jax.ShapeDtypeStruct(x.shape, F32), jax.ShapeDtypeStruct((M, E), F32)],
        compiler_params=_params("arbitrary"),
        name="conv_sample",
    )(x, p_all, g, win, cw, wo, pg, pe, s0, s1)


def _rope(x, cos, sin_lo, sin_hi):
    half = ROPE_DIM // 2
    return x * cos + pltpu.roll(x, C_DH - half, 1) * sin_lo + pltpu.roll(x, half, 1) * sin_hi


def _c_in_kernel(x_ref, g_ref, w_ref, cos_ref, slo_ref, shi_ref, *refs, tails, dils):
    n_groups = len(C_GROUPS)
    perm_dils = sorted({d for d in dils if d > 1})
    if perm_dils:
        perm_ref, refs = refs[0], refs[1:]
    qkv_refs = refs[:3 * n_groups]
    z_ref = refs[3 * n_groups]
    tail_refs = refs[3 * n_groups + 1:]
    h = _rms(x_ref[0], g_ref[...]).astype(BF16)
    tm = h.shape[0]
    cos, slo, shi = cos_ref[...], slo_ref[...], shi_ref[...]

    for grp in range(n_groups):
        base = 3 * grp * C_WIDTH
        uq = _mm(h, w_ref[:, base:base + C_WIDTH])
        uk = _mm(h, w_ref[:, base + C_WIDTH:base + 2 * C_WIDTH])
        uv = _mm(h, w_ref[:, base + 2 * C_WIDTH:base + 3 * C_WIDTH])
        dil = dils[grp]
        rows = tails[grp]
        out_refs = qkv_refs[3 * grp:3 * grp + 3]
        pieces = ([], [], [])
        for hd in range(C_HEADS):
            hs = slice(hd * C_DH, (hd + 1) * C_DH)
            qh = _rope(uq[:, hs], cos, slo, shi) * Q_SCALE
            kh = _rope(uk[:, hs], cos, slo, shi)
            vh = uv[:, hs]
            tail_refs[2 * grp][0, :, hs] = kh[tm - rows:, :]
            tail_refs[2 * grp + 1][0, :, hs] = vh[tm - rows:, :]
            for ref, done, val in zip(out_refs, pieces, (qh, kh, vh)):
                if dil == 1:
                    ref[0, 0, :, hs] = val.astype(BF16)
                else:
                    done.append(val.astype(BF16))
        if dil > 1:
            perm = perm_ref[perm_dils.index(dil)]
            part = tm // dil
            for ref, done in zip(out_refs, pieces):
                for pair in range(C_HEADS // 2):
                    cols = slice(2 * pair * C_DH, (2 * pair + 2) * C_DH)
                    moved = _mm(perm, jnp.concatenate(done[2 * pair:2 * pair + 2], axis=1)).astype(BF16)
                    for r in range(dil):
                        ref[0, r, :, cols] = moved[r * part:(r + 1) * part, :]

    zb = 3 * n_groups * C_WIDTH
    z_ref[0] = _silu(_mm(h, w_ref[:, zb:zb + C_WIDTH])).astype(BF16)


def _c_in(x, g, w, cos, slo, shi, tm, per_row=False):
    B, T, D = x.shape
    row = lambda width: pl.BlockSpec((1, tm, width), lambda b, t: (b, t, 0))
    table = pl.BlockSpec((tm, C_DH), lambda b, t: (t, 0))
    dils = tuple(1 if per_row else dil for _, dil in C_GROUPS)
    qkv_specs, qkv_shapes = [], []
    for dil in dils:
        qkv_specs += [pl.BlockSpec((1, dil, tm // dil, C_WIDTH), lambda b, t: (b, 0, t, 0))] * 3
        qkv_shapes += [jax.ShapeDtypeStruct((B, dil, T // dil, C_WIDTH), BF16)] * 3
    tails, tail_specs, tail_shapes = [], [], []
    for win, _ in C_GROUPS:
        keep = T if per_row else min(win, T)
        rows = min(tm, keep)
        first_row = T - keep
        tails.append(rows)

        def tail_index(b, t, first_row=first_row, rows=rows):
            return (b, jnp.maximum(((t + 1) * tm - first_row) // rows - 1, 0), 0)

        tail_specs += [pl.BlockSpec((1, rows, C_WIDTH), tail_index)] * 2
        tail_shapes += [jax.ShapeDtypeStruct((B, keep, C_WIDTH), F32)] * 2
    n_qkv = len(qkv_specs)
    perms = []
    for dil in sorted({d for d in dils if d > 1}):
        out_row = jnp.arange(tm)
        src = (out_row % (tm // dil)) * dil + out_row // (tm // dil)
        perms.append(src[:, None] == jnp.arange(tm)[None, :])
    extra = [jnp.stack(perms).astype(BF16)] if perms else []
    outs = pl.pallas_call(
        functools.partial(_c_in_kernel, tails=tuple(tails), dils=dils),
        grid=(B, T // tm),
        in_specs=[row(D), _resident(g.shape), _resident(w.shape), table, table, table]
        + [_resident(a.shape) for a in extra],
        out_specs=qkv_specs + [row(C_WIDTH)] + tail_specs,
        out_shape=qkv_shapes + [jax.ShapeDtypeStruct((B, T, C_WIDTH), BF16)] + tail_shapes,
        compiler_params=_params("parallel", "arbitrary"),
        name="attn_in",
    )(x, g, w, cos, slo, shi, *extra)
    return outs[:n_qkv], outs[n_qkv], outs[n_qkv + 1:]


ATTN_CHUNK = 512
REGROUP_STRIDE = 4


def _attn_chunk_kernel(q_ref, kc_ref, vc_ref, *refs, span):
    acc_ref, st_ref = refs[-2:]
    kp_ref, vp_ref = refs[:-2] if len(refs) == 4 else (None, None)
    chunk = pl.program_id(2)
    qi = lax.broadcasted_iota(jnp.int32, (C_BLOCK, 2 * C_BLOCK), 0)
    kj = lax.broadcasted_iota(jnp.int32, (C_BLOCK, 2 * C_BLOCK), 1)
    dist = qi + C_BLOCK - kj
    band = (dist >= 0) & (dist <= span)
    bias_inner = jnp.where(band, 0.0, -1e30)
    bias_first = jnp.where(band & (kj >= jnp.where(chunk == 0, C_BLOCK, 0)), 0.0, -1e30)
    lane = lax.broadcasted_iota(jnp.int32, (C_BLOCK, LANES), 1)
    ones = jnp.ones((2 * C_BLOCK, LANES), BF16)

    blocks = [(cls, i) for cls in range(q_ref.shape[1]) for i in range(q_ref.shape[2] // C_BLOCK)]
    for cls, i in blocks:
        rows = slice(i * C_BLOCK, (i + 1) * C_BLOCK)

        def keys(prev_ref, cur_ref, cols):
            if i == 0:
                before = cur_ref[0, cls, rows, cols] if prev_ref is None else prev_ref[0, cls, :, cols]
                return jnp.concatenate([before, cur_ref[0, cls, rows, cols]], axis=0)
            return cur_ref[0, cls, (i - 1) * C_BLOCK:(i + 1) * C_BLOCK, cols]

        head_cols = [slice(hd * C_DH, (hd + 1) * C_DH) for hd in range(C_HEADS)]
        s_all = [lax.dot_general(q_ref[0, cls, rows, cols], keys(kp_ref, kc_ref, cols), NT_DIMS,
                                 preferred_element_type=F32) for cols in head_cols]
        stats = jnp.zeros((C_BLOCK, LANES), F32)
        for hd, cols in enumerate(head_cols):
            s = s_all[hd] + (bias_first if i == 0 else bias_inner)
            mx = jnp.max(s, axis=-1, keepdims=True)
            p = jnp.exp2(s - mx).astype(BF16)
            both = _mm(p, jnp.concatenate([keys(vp_ref, vc_ref, cols), ones], axis=1))
            acc_ref[0, cls, rows, cols] = both[:, :C_DH].astype(BF16)
            stats = jnp.where(lane == hd, mx, jnp.where(lane == C_HEADS + hd, both[:, C_DH:], stats))
        st_ref[0, cls, rows, :] = stats


def _attn_prompt_group(q, k, v, span):
    B, dil, n, _ = q.shape
    qc = min(n, ATTN_CHUNK)
    per_chunk = qc // C_BLOCK
    classes = min(dil, ATTN_CHUNK // qc)
    assert dil % classes == 0
    cur = lambda rows, width: pl.BlockSpec((1, classes, rows, width), lambda b, r, c: (b, r, c, 0))
    prev = pl.BlockSpec((1, 1, C_BLOCK, C_WIDTH), lambda b, r, c: (b, r, jnp.maximum(c * per_chunk - 1, 0), 0))
    before = [(k, prev), (v, prev)] if n > qc else []
    return pl.pallas_call(
        functools.partial(_attn_chunk_kernel, span=span),
        grid=(B, dil // classes, n // qc),
        in_specs=[cur(qc, C_WIDTH)] * 3 + [spec for _, spec in before],
        out_specs=[cur(qc, C_WIDTH), cur(qc, LANES)],
        out_shape=[jax.ShapeDtypeStruct((B, dil, n, C_WIDTH), BF16),
                   jax.ShapeDtypeStruct((B, dil, n, LANES), F32)],
        compiler_params=_params("parallel", "parallel", "arbitrary"),
        name="attn_chunks_d%d" % dil,
    )(q, k, v, *[a for a, _ in before])


def _merge_out_kernel(a0_ref, a1_ref, a2_ref, s0_ref, s1_ref, s2_ref, z_ref, x_ref, p_ref,
                      wo_ref, pg_ref, pe_ref, xo_ref, acc_s, st_s, half_s):
    tm = x_ref.shape[1]
    a_refs, s_refs = (a0_ref, a1_ref, a2_ref), (s0_ref, s1_ref, s2_ref)
    dils = [a_ref.shape[1] for a_ref in a_refs]
    for grp, dil in enumerate(dils):
        if dil == 1:
            continue
        two_steps = dil > REGROUP_STRIDE
        first, second = REGROUP_STRIDE, dil // REGROUP_STRIDE
        part = tm // first
        for r in range(dil):
            rows = pl.ds(r, tm // dil, stride=dil)
            acc = a_refs[grp][0, r].astype(F32)
            for hd in range(C_HEADS):
                piece = acc[:, hd * C_DH:(hd + 1) * C_DH]
                if two_steps:
                    half_s[hd, pl.ds((r % first) * part + r // first, tm // dil, stride=second), :] = piece
                else:
                    acc_s[grp * C_HEADS + hd, rows, :] = piece
            st_s[grp, rows, :] = s_refs[grp][0, r]
        if two_steps:
            for hd in range(C_HEADS):
                for r1 in range(first):
                    acc_s[grp * C_HEADS + hd, pl.ds(r1, part, stride=first), :] = (
                        half_s[hd, r1 * part:(r1 + 1) * part, :])

    def acc_of(grp, hd):
        if dils[grp] == 1:
            return a_refs[grp][0, 0, :, hd * C_DH:(hd + 1) * C_DH].astype(F32)
        return acc_s[grp * C_HEADS + hd]

    stats = [s_refs[grp][0, 0] if dils[grp] == 1 else st_s[grp] for grp in range(3)]
    mx = [st[:, 0:C_HEADS] for st in stats]
    ls = [st[:, C_HEADS:2 * C_HEADS] for st in stats]
    top = jnp.maximum(jnp.maximum(mx[0], mx[1]), mx[2])
    es = [jnp.exp2(m - top) for m in mx]
    den = es[0] * ls[0] + es[1] * ls[1] + es[2] * ls[2]
    coef = [e / den for e in es]
    x1 = x_ref[0]
    for pair in range(C_HEADS // 2):
        ys = []
        for hd in (2 * pair, 2 * pair + 1):
            hs = slice(hd * C_DH, (hd + 1) * C_DH)
            o = coef[0][:, hd:hd + 1] * acc_of(0, hd)
            for grp in (1, 2):
                o = o + coef[grp][:, hd:hd + 1] * acc_of(grp, hd)
            ys.append((o * z_ref[0, :, hs].astype(F32)).astype(BF16))
        x1 = x1 + _mm(jnp.concatenate(ys, axis=1), wo_ref[2 * pair * C_DH:(2 * pair + 2) * C_DH, :])
    xo_ref[0] = _embed(x1, p_ref[0], pg_ref, pe_ref)


def _merge_out(accs, stats, z, x, p_all, layer, wo, pg, pe, tm):
    B, T, D = x.shape
    row = lambda width: pl.BlockSpec((1, tm, width), lambda b, t: (b, t, 0))
    classes = lambda a: pl.BlockSpec((1, a.shape[1], tm // a.shape[1], a.shape[3]), lambda b, t: (b, 0, t, 0))
    p_spec = pl.BlockSpec((None, 1, tm, p_all.shape[-1]), lambda b, t: (layer, b, t, 0))
    return pl.pallas_call(
        _merge_out_kernel,
        grid=(B, T // tm),
        in_specs=[classes(a) for a in accs] + [classes(s) for s in stats] + [row(C_WIDTH), row(D), p_spec,
                  _resident(wo.shape), _resident(pg.shape), _resident(pe.shape)],
        out_specs=row(D),
        out_shape=jax.ShapeDtypeStruct((B, T, D), F32),
        scratch_shapes=[pltpu.VMEM((3 * C_HEADS, tm, C_DH), F32), pltpu.VMEM((3, tm, LANES), F32),
                        pltpu.VMEM((C_HEADS, tm, C_DH), F32)],
        compiler_params=_params("parallel", "parallel"),
        name="attn_merge_out",
    )(*accs, *stats, z, x, p_all, wo, pg, pe)


def _attn_sample_kernel(*refs):
    n_groups = len(C_GROUPS)
    z_ref, y_ref = refs[5 * n_groups], refs[5 * n_groups + 1]
    parts = []
    for grp in range(n_groups):
        q_ref, kn_ref, vn_ref, kc_ref, vc_ref = refs[5 * grp:5 * grp + 5]
        q = q_ref[0].astype(F32)
        s_old = jnp.sum(kc_ref[0] * q[None], axis=-1, keepdims=True)
        s_new = jnp.sum(kn_ref[0] * q, axis=-1, keepdims=True)
        mx = jnp.maximum(jnp.max(s_old, axis=0), s_new)
        p_old = jnp.exp2(s_old - mx[None])
        p_new = jnp.exp2(s_new - mx)
        l = jnp.sum(p_old, axis=0) + p_new
        acc = jnp.sum(p_old * vc_ref[0], axis=0) + p_new * vn_ref[0]
        parts.append((acc, mx, l))
    top = jnp.maximum(jnp.maximum(parts[0][1], parts[1][1]), parts[2][1])
    es = [jnp.exp2(m - top) for _, m, _ in parts]
    num = es[0] * parts[0][0] + es[1] * parts[1][0] + es[2] * parts[2][0]
    den = es[0] * parts[0][2] + es[1] * parts[1][2] + es[2] * parts[2][2]
    y_ref[0] = (num / den * z_ref[0].astype(F32)).astype(BF16)


def _attn_sample(qkv, tails, caches, z):
    Bs = z.shape[1]
    heads = lambda a: a.reshape(Bs, C_HEADS, C_DH)
    row = pl.BlockSpec((1, C_HEADS, C_DH), lambda b: (b, 0, 0))
    args, specs = [], []
    for grp, (win, dil) in enumerate(C_GROUPS):
        kc, vc = caches[grp]
        n_buf = kc.shape[1]
        assert n_buf == win and win % dil == 0, "sample window must be fully cached"
        span = win // dil
        view = lambda a: a.reshape(Bs, span, dil, C_HEADS, C_DH)
        cache_spec = pl.BlockSpec((1, span, None, C_HEADS, C_DH), lambda b: (b, 0, 0, 0, 0))
        args += [heads(qkv[3 * grp]), heads(tails[2 * grp]), heads(tails[2 * grp + 1]), view(kc), view(vc)]
        specs += [row, row, row, cache_spec, cache_spec]
    return pl.pallas_call(
        _attn_sample_kernel,
        grid=(Bs,),
        in_specs=specs + [row],
        out_specs=row,
        out_shape=jax.ShapeDtypeStruct((Bs, C_HEADS, C_DH), BF16),
        compiler_params=_params("parallel"),
        name="attn_sample",
    )(*args, heads(z))


def _pool_tail(x, acc, p, pg_ref, pe_ref, fg_ref):
    return _rms(_embed(x + acc, p, pg_ref, pe_ref), fg_ref[...])


def _pool_prompt_kernel(x_ref, p_ref, g_ref, win_ref, wgrp_ref, sc_ref, wo_ref, pg_ref, pe_ref, fg_ref,
                        xo_ref, st_ref, pad_s):
    E = wo_ref.shape[0]
    G = E // len(D_WINDOWS)
    t = pl.program_id(1)

    @pl.when(t == 0)
    def _():
        pad_s[0:POOL_PAD, :] = jnp.zeros((POOL_PAD, E), F32)

    x = x_ref[0]
    tm = x.shape[0]
    h = _rms(x, g_ref[...]).astype(BF16)
    pos = t * tm + lax.broadcasted_iota(jnp.int32, (tm, 1), 0)
    acc = jnp.zeros(x.shape, F32)
    for grp, w in enumerate(D_WINDOWS):
        cols = slice(grp * G, (grp + 1) * G)
        xp = _mm(h, win_ref[:, cols])
        z = _mm(h, win_ref[:, E + grp * G:E + (grp + 1) * G])
        pad_s[POOL_PAD:, cols] = xp
        wsum = pad_s[:, cols]
        shift = 1
        while shift < w:
            wsum = wsum + pltpu.roll(wsum, shift, 0)
            shift *= 2
        cnt = jnp.minimum(w, pos + 1).astype(F32)
        r = wsum[POOL_PAD:, :] / cnt - xp
        y = _mm(r.astype(BF16), wgrp_ref[grp]) * sc_ref[:, cols]
        acc = acc + _mm((y * _silu(z)).astype(BF16), wo_ref[cols, :])
        pad_s[0:POOL_PAD, cols] = xp[tm - POOL_PAD:, :]
    xo_ref[0] = _pool_tail(x, acc, p_ref[0], pg_ref, pe_ref, fg_ref)
    st_ref[0] = pad_s[0:POOL_PAD, :]


def _pool_prompt(x, p_all, layer, g, win, wgrp, sc, wo, pg, pe, fg, tm):
    B, T, D = x.shape
    E = wo.shape[0]
    row = lambda width: pl.BlockSpec((1, tm, width), lambda b, t: (b, t, 0))
    p_spec = pl.BlockSpec((None, 1, tm, p_all.shape[-1]), lambda b, t: (layer, b, t, 0))
    return pl.pallas_call(
        _pool_prompt_kernel,
        grid=(B, T // tm),
        in_specs=[row(D), p_spec] + [_resident(a.shape) for a in (g, win, wgrp, sc, wo, pg, pe, fg)],
        out_specs=[row(D), pl.BlockSpec((1, POOL_PAD, E), lambda b, t: (b, 0, 0))],
        out_shape=[jax.ShapeDtypeStruct((B, T, D), F32), jax.ShapeDtypeStruct((B, POOL_PAD, E), F32)],
        scratch_shapes=[pltpu.VMEM((POOL_PAD + tm, E), F32)],
        compiler_params=_params("parallel", "arbitrary"),
        name="pool_prompt",
    )(x, p_all, g, win, wgrp, sc, wo, pg, pe, fg)


def _pool_sample_kernel(x_ref, p_ref, g_ref, win_ref, wgrp_ref, sc_ref, wo_ref, pg_ref, pe_ref, fg_ref, st_ref,
                        xo_ref, xp_ref, *, pos):
    E = wo_ref.shape[0]
    G = E // len(D_WINDOWS)
    x = x_ref[0]
    h = _rms(x, g_ref[...]).astype(BF16)
    acc = jnp.zeros(x.shape, F32)
    for grp, w in enumerate(D_WINDOWS):
        cols = slice(grp * G, (grp + 1) * G)
        xp = _mm(h, win_ref[:, cols])
        z = _mm(h, win_ref[:, E + grp * G:E + (grp + 1) * G])
        wsum = xp
        for back in range(1, w):
            lo = (POOL_STATE - back) * E + grp * G
            wsum = wsum + st_ref[:, lo:lo + G]
        r = wsum / float(min(w, pos + 1)) - xp
        y = _mm(r.astype(BF16), wgrp_ref[grp]) * sc_ref[:, cols]
        acc = acc + _mm((y * _silu(z)).astype(BF16), wo_ref[cols, :])
        xp_ref[:, cols] = xp
    xo_ref[0] = _pool_tail(x, acc, p_ref[0], pg_ref, pe_ref, fg_ref)


def _pool_sample(x, p_all, layer, g, win, wgrp, sc, wo, pg, pe, fg, st, pos):
    _, M, D = x.shape
    E = wo.shape[0]
    p_spec = pl.BlockSpec((None, 1, M, p_all.shape[-1]), lambda i: (layer, 0, 0, 0))
    return pl.pallas_call(
        functools.partial(_pool_sample_kernel, pos=pos),
        grid=(1,),
        in_specs=[_resident(x.shape), p_spec] + [_resident(a.shape) for a in (g, win, wgrp, sc, wo, pg, pe, fg, st)],
        out_specs=[_whole(x.shape), _whole((M, E))],
        out_shape=[jax.ShapeDtypeStruct(x.shape, F32), jax.ShapeDtypeStruct((M, E), F32)],
        compiler_params=_params("arbitrary"),
        name="pool_sample",
    )(x, p_all, g, win, wgrp, sc, wo, pg, pe, fg, st)


def _rope_tables(pos):
    half = ROPE_DIM // 2
    inv = ROPE_THETA ** (-jnp.arange(half, dtype=F32) / half)
    ang = pos.astype(F32)[:, None] * inv[None, :]
    cos, sin = jnp.cos(ang), jnp.sin(ang)
    rest = C_DH - ROPE_DIM
    n = pos.shape[0]
    cos_t = jnp.concatenate([cos, cos, jnp.ones((n, rest), F32)], axis=1)
    sin_lo = jnp.concatenate([-sin, jnp.zeros((n, half + rest), F32)], axis=1)
    sin_hi = jnp.concatenate([jnp.zeros((n, half), F32), sin, jnp.zeros((n, rest), F32)], axis=1)
    return cos_t, sin_lo, sin_hi


def _row(a):
    return a.reshape(1, -1).astype(F32)


def kernel(x_prompt, x_sample, state_mlstm_C, state_mlstm_n, state_mlstm_m, state_conv, cache_k_w128, cache_v_w128, cache_k_w512, cache_v_w512, cache_k_w2048, cache_v_w2048, state_pool, p_prompt, p_sample, norm_g, pe_w, pg_w, final_g, a_w_in, a_b_if, a_norm_g, a_w_out, b_w_in, b_conv_w, b_w_out, c_w_in, c_w_out, d_w_in, d_w_grp, d_scale, d_w_out):
    B, T, D = x_prompt.shape
    Bs, Ts, _ = x_sample.shape
    assert Ts == 1 and norm_g.shape[0] == 4 and a_w_in.shape[0] == 1
    assert T % 512 == 0 and all(T % (dil * C_BLOCK) == 0 for _, dil in C_GROUPS)
    assert Bs % 8 == 0

    bf = lambda a: a.astype(BF16)
    qk = A_HEADS * A_DK
    n_main = 2 * qk + 3 * A_WIDTH
    a_w = bf(a_w_in[0])
    a_wg = bf(jnp.pad(a_w_in[0, :, n_main:], ((0, 0), (0, LANES - 2 * A_HEADS))))
    a_bif = jnp.pad(a_b_if[0], (0, LANES - 2 * A_HEADS)).reshape(1, LANES)
    a_ng = _row(a_norm_g[0])
    a_wo, b_wi, b_wo, c_wi, c_wo, d_wi, d_wg, d_wo = (
        bf(a_w_out[0]), bf(b_w_in[0]), bf(b_w_out[0]), bf(c_w_in[0]), bf(c_w_out[0]),
        bf(d_w_in[0]), bf(d_w_grp[0]), bf(d_w_out[0]))
    pg, pe = bf(pg_w), bf(pe_w)
    gs = [_row(norm_g[i]) for i in range(4)]
    fg = _row(final_g)
    b_cw = b_conv_w[0]
    d_sc = _row(d_scale[0])
    caches = [(cache_k_w128[0], cache_v_w128[0]), (cache_k_w512[0], cache_v_w512[0]),
              (cache_k_w2048[0], cache_v_w2048[0])]

    x = x_prompt
    q, k, v, oz, gates, kt = _a_in(x, gs[0], a_w, a_wg, a_bif, 512, transposed_k=True)
    hh, c_p, n_p, m_p = _mlstm_prompt(q, k, kt, v, gates)
    x = _a_out(hh, oz, a_ng, x, p_prompt, 0, a_wo, pg[0], pe[0], 512)
    x, conv_p = _conv_prompt(x, p_prompt, 1, gs[1], b_wi, b_cw, b_wo, pg[1], pe[1], 512)
    qkv, z, tails_p = _c_in(x, gs[2], c_wi, *_rope_tables(jnp.arange(T)), 256)
    accs, stats = [], []
    for grp, (win, dil) in enumerate(C_GROUPS):
        acc, st = _attn_prompt_group(qkv[3 * grp], qkv[3 * grp + 1], qkv[3 * grp + 2], win // dil)
        accs.append(acc)
        stats.append(st)
    x = _merge_out(accs, stats, z, x, p_prompt, 2, c_wo, pg[2], pe[2], 512)
    y_prompt, pool_p = _pool_prompt(x, p_prompt, 3, gs[3], d_wi, d_wg, d_sc, d_wo, pg[3], pe[3], fg, 512)

    xs = x_sample.reshape(1, Bs, D)
    ps = p_sample.reshape(p_sample.shape[0], 1, Bs, p_sample.shape[-1])
    q, k, v, oz, gates = _a_in(xs, gs[0], a_w, a_wg, a_bif, Bs)
    m_lanes = jnp.pad(state_mlstm_m[0], ((0, 0), (0, LANES - A_HEADS)))
    hh, c_s, n_s, m_s = _mlstm_sample(q[0], k[0], v[0], gates[0], state_mlstm_C[0], state_mlstm_n[0], m_lanes)
    m_s = m_s[:, :, 0].T
    xs = _a_out(hh.reshape(1, Bs, A_WIDTH), oz, a_ng, xs, ps, 0, a_wo, pg[0], pe[0], Bs)
    xs, cx = _conv_sample(xs, ps, 1, gs[1], b_wi, b_cw, b_wo, pg[1], pe[1],
                          state_conv[0, :, 0, :], state_conv[0, :, 1, :])
    conv_s = jnp.stack([state_conv[0, :, 1, :], cx], axis=1)
    tables = [jnp.broadcast_to(t, (Bs, C_DH)) for t in _rope_tables(PAST_LEN + jnp.arange(1))]
    qkv, z, tails_s = _c_in(xs, gs[2], c_wi, *tables, Bs, per_row=True)
    y = _attn_sample(qkv, tails_s, caches, z)
    xs = _out_embed(y.reshape(1, Bs, C_WIDTH), xs, ps, 2, c_wo, pg[2], pe[2], Bs)
    pool_flat = state_pool[0].reshape(Bs, POOL_STATE * state_pool.shape[-1])
    ys, xp = _pool_sample(xs, ps, 3, gs[3], d_wi, d_wg, d_sc, d_wo, pg[3], pe[3], fg, pool_flat, PAST_LEN)
    pool_s = jnp.concatenate([state_pool[0, :, 1:, :], xp[:, None, :]], axis=1)

    heads = lambda a, lead: a.reshape(1, lead, a.shape[1], C_HEADS, C_DH)
    kv_out = []
    for grp in range(len(C_GROUPS)):
        for j in range(2):
            kv_out += [heads(tails_p[2 * grp + j], B), tails_s[2 * grp + j].reshape(1, Bs, 1, C_HEADS, C_DH)]
    return (y_prompt, ys.reshape(Bs, 1, D),
            c_p[None], c_s[None], n_p[None], n_s[None], m_p[None, :, 0, :A_HEADS], m_s.reshape(1, Bs, A_HEADS),
            conv_p[None], conv_s[None],
            *kv_out,
            pool_p[None, :, 1:, :], pool_s[None])
```

```python
import functools

import jax
import jax.numpy as jnp
from jax import lax
from jax.experimental import pallas as pl
from jax.experimental.pallas import tpu as pltpu

F32 = jnp.float32
BF16 = jnp.bfloat16

EPS = 1e-6
PAST_LEN = 8192

A_HEADS = 8
A_DK = 128
A_DV = 256
A_WIDTH = A_HEADS * A_DV
A_CHUNK = 128
A_CHUNKS_PER_STEP = 4

CONV_W = 3

C_HEADS = 8
C_DH = 128
C_WIDTH = C_HEADS * C_DH
C_GROUPS = ((128, 1), (512, 4), (2048, 16))
C_BLOCK = 128
ROPE_DIM = C_DH // 4
ROPE_THETA = 500000.0
LOG2_E = 1.4426950408889634
Q_SCALE = C_DH ** -0.5 * LOG2_E

D_WINDOWS = (2, 4, 8, 16)
POOL_STATE = max(D_WINDOWS) - 1
POOL_PAD = POOL_STATE + 1

LANES = 128
VMEM_LIMIT_BYTES = 56 * 2 ** 20

NT_DIMS = (((1,), (1,)), ((), ()))
TN_DIMS = (((0,), (0,)), ((), ()))


def _params(*semantics):
    return pltpu.CompilerParams(dimension_semantics=semantics, vmem_limit_bytes=VMEM_LIMIT_BYTES)


def _resident(shape):
    zeros = (0,) * len(shape)
    return pl.BlockSpec(shape, lambda *_: zeros, pipeline_mode=pl.Buffered(1))


def _whole(shape):
    zeros = (0,) * len(shape)
    return pl.BlockSpec(shape, lambda *_: zeros)


def _mm(a, b):
    return jnp.dot(a, b, preferred_element_type=F32)


def _rms(x, g):
    return x * lax.rsqrt(jnp.mean(x * x, axis=-1, keepdims=True) + EPS) * g


def _sigmoid(x):
    return 0.5 * jnp.tanh(0.5 * x) + 0.5


def _silu(x):
    return x * _sigmoid(x)


def _log_sigmoid(x):
    return jnp.minimum(x, 0.0) - jnp.log(1.0 + jnp.exp(-jnp.abs(x)))


def _embed(x1, p, pg_ref, pe_ref):
    gate = _sigmoid(_mm(x1.astype(BF16), pg_ref[...]))
    return x1 + gate * _mm(p.astype(BF16), pe_ref[...])


def _a_in_kernel(x_ref, g_ref, w_ref, wg_ref, bif_ref,
                 q_ref, k_ref, v_ref, oz_ref, gate_ref, *maybe_kt_ref):
    h = _rms(x_ref[0], g_ref[...]).astype(BF16)
    qk = A_HEADS * A_DK

    def proj(lo, width):
        return _mm(h, w_ref[:, lo:lo + width])

    q_ref[0] = proj(0, qk).astype(BF16)
    k = proj(qk, qk) * (A_DK ** -0.5)
    k_ref[0] = k.astype(BF16)
    for kt_ref in maybe_kt_ref:
        kt_ref[0] = k.T.astype(BF16)
    v_ref[0] = proj(2 * qk, A_WIDTH).astype(BF16)
    oz_ref[0] = (_sigmoid(proj(2 * qk + A_WIDTH, A_WIDTH))
                 * _silu(proj(2 * qk + 2 * A_WIDTH, A_WIDTH))).astype(BF16)
    gates = _mm(h, wg_ref[...]) + bif_ref[...]
    lane = lax.broadcasted_iota(jnp.int32, gates.shape, 1)
    gate_ref[0] = jnp.where(lane < A_HEADS, gates, _log_sigmoid(gates))


def _a_in(x, g, w, wg, bif, tm, transposed_k=False):
    B, T, D = x.shape
    qk = A_HEADS * A_DK
    row = lambda width: pl.BlockSpec((1, tm, width), lambda b, t: (b, t, 0))
    out_specs = [row(qk), row(qk), row(A_WIDTH), row(A_WIDTH), row(LANES)]
    out_shape = [jax.ShapeDtypeStruct((B, T, qk), BF16), jax.ShapeDtypeStruct((B, T, qk), BF16),
                 jax.ShapeDtypeStruct((B, T, A_WIDTH), BF16), jax.ShapeDtypeStruct((B, T, A_WIDTH), BF16),
                 jax.ShapeDtypeStruct((B, T, LANES), F32)]
    if transposed_k:
        out_specs.append(pl.BlockSpec((1, qk, tm), lambda b, t: (b, 0, t)))
        out_shape.append(jax.ShapeDtypeStruct((B, qk, T), BF16))
    return pl.pallas_call(
        _a_in_kernel,
        grid=(B, T // tm),
        in_specs=[row(D), _resident(g.shape), _resident(w.shape), _resident(wg.shape), _resident(bif.shape)],
        out_specs=out_specs,
        out_shape=out_shape,
        compiler_params=_params("parallel", "parallel"),
        name="mlstm_in",
    )(x, g, w, wg, bif)


def _mlstm_chunk_kernel(q_ref, k_ref, kt_ref, v_ref, gate_ref,
                        hh_ref, c_out, n_out, m_out, m_s, *state_s):
    chunk = pl.program_id(1)
    ct_s, nb_s = state_s[:A_HEADS], state_s[A_HEADS:]

    @pl.when(chunk == 0)
    def _():
        for ref in state_s:
            ref[...] = jnp.zeros_like(ref)
        m_s[...] = jnp.zeros_like(m_s)

    L = A_CHUNK
    row = lax.broadcasted_iota(jnp.int32, (L, LANES), 0)
    ti = lax.broadcasted_iota(jnp.int32, (L, L), 0)
    si = lax.broadcasted_iota(jnp.int32, (L, L), 1)
    causal = ti >= si
    ones = jnp.ones((L, LANES), BF16)
    heads = range(A_HEADS)
    ks = [slice(h * A_DK, (h + 1) * A_DK) for h in heads]
    vs = [slice(h * A_DV, (h + 1) * A_DV) for h in heads]

    def prefix(x, op, identity):
        step = 1
        while step < L:
            x = op(x, jnp.where(row >= step, pltpu.roll(x, step, 0), identity))
            step *= 2
        return x

    def one_chunk(rows):
        gates = gate_ref[0, rows, :]
        b = pltpu.roll(prefix(gates, jnp.add, 0.0), LANES - A_HEADS, 1)
        c = gates - b
        m_row = m_s[0:1, :]
        top = jnp.maximum(m_row, prefix(c, jnp.maximum, -jnp.inf))
        top_last = top[L - 1:L, :]
        g_all = jnp.exp(m_row - top)
        floor_all = jnp.exp(-(b + top))
        w_all = jnp.exp(c - top_last)
        decay_row = jnp.exp(m_row - top_last)
        m_s[...] = jnp.broadcast_to(b[L - 1:L, :] + top_last, m_s.shape)
        c_t = c.T

        s_raw = [_mm(q_ref[0, rows, ks[h]], kt_ref[0, ks[h], rows]) for h in heads]
        upd = []
        for h in heads:
            w_b = jnp.broadcast_to(w_all[:, h:h + 1], (L, LANES))
            wv = jnp.concatenate([w_b, w_b], axis=1) * v_ref[0, rows, vs[h]].astype(F32)
            upd.append(_mm(kt_ref[0, ks[h], rows], jnp.concatenate([wv.astype(BF16), w_b.astype(BF16)], axis=1)))

        for h in heads:
            ct = ct_s[h][...]
            nb = nb_s[h][...]
            dmat = jnp.exp(jnp.where(causal, -top[:, h:h + 1] + c_t[h:h + 1, :], -jnp.inf))
            s = s_raw[h] * dmat
            gq = g_all[:, h:h + 1] * q_ref[0, rows, ks[h]].astype(F32)
            lhs = jnp.concatenate([s.astype(BF16), gq.astype(BF16)], axis=1)
            num = _mm(lhs, jnp.concatenate([v_ref[0, rows, vs[h]], ct.astype(BF16)], axis=0))
            den = _mm(lhs, jnp.concatenate([ones, nb.astype(BF16)], axis=0))
            scale = 1.0 / jnp.maximum(jnp.abs(den), floor_all[:, h:h + 1])
            hh_ref[0, rows, vs[h]] = (num * jnp.concatenate([scale, scale], axis=1)).astype(BF16)
            decay = decay_row[:, h:h + 1]
            ct_s[h][...] = decay * ct + upd[h][:, :A_DV]
            nb_s[h][...] = decay * nb + upd[h][:, A_DV:]

    for sub in range(q_ref.shape[1] // L):
        one_chunk(slice(sub * L, (sub + 1) * L))

    @pl.when(chunk == pl.num_programs(1) - 1)
    def _():
        for h in range(A_HEADS):
            c_out[0, h] = ct_s[h][...].T
            n_out[0, h:h + 1, :] = nb_s[h][...].T[0:1, :]
        m_out[0] = m_s[...]


def _mlstm_prompt(q, k, kt, v, gates):
    B, T, _ = q.shape
    L = A_CHUNK * A_CHUNKS_PER_STEP
    qk = A_HEADS * A_DK
    row = lambda width: pl.BlockSpec((1, L, width), lambda b, c: (b, c, 0))
    state = lambda *shape: pl.BlockSpec((1,) + shape, lambda b, c: (b,) + (0,) * len(shape))
    return pl.pallas_call(
        _mlstm_chunk_kernel,
        grid=(B, T // L),
        in_specs=[row(qk), row(qk), pl.BlockSpec((1, qk, L), lambda b, c: (b, 0, c)), row(A_WIDTH), row(LANES)],
        out_specs=[row(A_WIDTH), state(A_HEADS, A_DV, A_DK), state(A_HEADS, A_DK), state(A_HEADS, LANES)],
        out_shape=[jax.ShapeDtypeStruct((B, T, A_WIDTH), BF16),
                   jax.ShapeDtypeStruct((B, A_HEADS, A_DV, A_DK), F32),
                   jax.ShapeDtypeStruct((B, A_HEADS, A_DK), F32),
                   jax.ShapeDtypeStruct((B, A_HEADS, LANES), F32)],
        scratch_shapes=[pltpu.VMEM((A_HEADS, LANES), F32)] + [pltpu.VMEM((A_DK, A_DV), F32)] * A_HEADS
        + [pltpu.VMEM((A_DK, LANES), F32)] * A_HEADS,
        compiler_params=_params("parallel", "arbitrary"),
        name="mlstm_chunks",
    )(q, k, kt, v, gates)


def _mlstm_step_kernel(q_ref, k_ref, v_ref, gate_ref, m_ref, c_ref, n_ref,
                       hh_ref, c_out, n_out, m_out, hht_s):
    head = pl.program_id(0)
    Bs = q_ref.shape[0]
    q = q_ref[...].astype(F32)
    k = k_ref[...].astype(F32)
    n = n_ref[...]
    lane = lax.broadcasted_iota(jnp.int32, (Bs, LANES), 1)
    pick = lambda a, idx: jnp.sum(jnp.where(lane == idx, a, 0.0), axis=-1, keepdims=True)
    ig = pick(gate_ref[...], head)
    lf = pick(gate_ref[...], A_HEADS + head)
    m_prev = pick(m_ref[...], head)

    inter = lf + m_prev
    m_t = jnp.maximum(inter, ig)
    dm = jnp.exp(ig - m_t)
    g = jnp.exp(inter - m_t)
    s = jnp.sum(q * k, axis=-1, keepdims=True) * dm
    den = s + g * jnp.sum(q * n, axis=-1, keepdims=True)
    scale = 1.0 / jnp.maximum(jnp.abs(den), jnp.exp(-m_t))
    n_out[...] = g * n + dm * k
    m_out[...] = jnp.broadcast_to(m_t, (Bs, LANES))

    v_pad = jnp.concatenate([v_ref[...].astype(F32), jnp.zeros((LANES - Bs, A_DV), F32)], axis=0)
    vt = v_pad.T
    hht_s[...] = jnp.zeros_like(hht_s)
    for b in range(Bs):
        c_b = c_ref[b]
        one = slice(b, b + 1)
        v_col = vt[:, one]
        cq = jnp.sum(c_b * q[one, :], axis=-1, keepdims=True)
        hht_s[:, one] = (s[one, :] * v_col + g[one, :] * cq) * scale[one, :]
        c_out[b] = g[one, :] * c_b + (dm[one, :] * v_col) * k[one, :]
    hh_ref[...] = hht_s[...].T[0:Bs, :].astype(BF16)


def _mlstm_sample(q, k, v, gates, c0, n0, m0):
    Bs = q.shape[0]
    assert Bs <= LANES
    cols = lambda width: pl.BlockSpec((Bs, width), lambda h: (0, h))
    full = pl.BlockSpec((Bs, LANES), lambda h: (0, 0))
    c_spec = pl.BlockSpec((Bs, None, A_DV, A_DK), lambda h: (0, h, 0, 0))
    hh, c1, n1, m1 = pl.pallas_call(
        _mlstm_step_kernel,
        grid=(A_HEADS,),
        in_specs=[cols(A_DK), cols(A_DK), cols(A_DV), full, full, c_spec, cols(A_DK)],
        out_specs=[cols(A_DV), c_spec, cols(A_DK), pl.BlockSpec((None, Bs, LANES), lambda h: (h, 0, 0))],
        out_shape=[jax.ShapeDtypeStruct((Bs, A_WIDTH), BF16),
                   jax.ShapeDtypeStruct(c0.shape, F32), jax.ShapeDtypeStruct((Bs, A_HEADS * A_DK), F32),
                   jax.ShapeDtypeStruct((A_HEADS, Bs, LANES), F32)],
        scratch_shapes=[pltpu.VMEM((A_DV, LANES), F32)],
        compiler_params=_params("parallel"),
        name="mlstm_step",
    )(q, k, v, gates, m0, c0, n0.reshape(Bs, A_HEADS * A_DK))
    return hh, c1, n1.reshape(n0.shape), m1


def _a_out_kernel(hh_ref, oz_ref, ng_ref, x_ref, p_ref, wo_ref, pg_ref, pe_ref, xo_ref):
    x1 = x_ref[0]
    for h in range(A_HEADS):
        vs = slice(h * A_DV, (h + 1) * A_DV)
        hh = hh_ref[0, :, vs].astype(F32)
        hn = hh * lax.rsqrt(jnp.mean(hh * hh, axis=-1, keepdims=True) + EPS) * ng_ref[:, vs]
        y = hn * oz_ref[0, :, vs].astype(F32)
        x1 = x1 + _mm(y.astype(BF16), wo_ref[vs, :])
    xo_ref[0] = _embed(x1, p_ref[0], pg_ref, pe_ref)


def _a_out(hh, oz, ng, x, p_all, layer, wo, pg, pe, tm):
    B, T, D = x.shape
    row = lambda width: pl.BlockSpec((1, tm, width), lambda b, t: (b, t, 0))
    p_spec = pl.BlockSpec((None, 1, tm, p_all.shape[-1]), lambda b, t: (layer, b, t, 0))
    return pl.pallas_call(
        _a_out_kernel,
        grid=(B, T // tm),
        in_specs=[row(A_WIDTH), row(A_WIDTH), _resident(ng.shape), row(D), p_spec,
                  _resident(wo.shape), _resident(pg.shape), _resident(pe.shape)],
        out_specs=row(D),
        out_shape=jax.ShapeDtypeStruct((B, T, D), F32),
        compiler_params=_params("parallel", "parallel"),
        name="mlstm_out",
    )(hh, oz, ng, x, p_all, wo, pg, pe)


def _out_kernel(y_ref, x_ref, p_ref, wo_ref, pg_ref, pe_ref, xo_ref):
    x1 = x_ref[0] + _mm(y_ref[0], wo_ref[...])
    xo_ref[0] = _embed(x1, p_ref[0], pg_ref, pe_ref)


def _out_embed(y, x, p_all, layer, wo, pg, pe, tm):
    B, T, D = x.shape
    row = lambda width: pl.BlockSpec((1, tm, width), lambda b, t: (b, t, 0))
    p_spec = pl.BlockSpec((None, 1, tm, p_all.shape[-1]), lambda b, t: (layer, b, t, 0))
    return pl.pallas_call(
        _out_kernel,
        grid=(B, T // tm),
        in_specs=[row(y.shape[-1]), row(D), p_spec, _resident(wo.shape), _resident(pg.shape), _resident(pe.shape)],
        out_specs=row(D),
        out_shape=jax.ShapeDtypeStruct((B, T, D), F32),
        compiler_params=_params("parallel", "parallel"),
        name="out_embed",
    )(y, x, p_all, wo, pg, pe)


CONV_COLS = 512


def _conv_prompt_kernel(x_ref, p_ref, g_ref, win_ref, cw_ref, wo_ref, pg_ref, pe_ref,
                        xo_ref, st_ref, carry_s):
    E = wo_ref.shape[0]

    @pl.when(pl.program_id(1) == 0)
    def _():
        carry_s[...] = jnp.zeros_like(carry_s)

    x = x_ref[0]
    tm = x.shape[0]
    h = _rms(x, g_ref[...]).astype(BF16)
    row = lax.broadcasted_iota(jnp.int32, (tm, CONV_COLS), 0)
    acc = jnp.zeros(x.shape, F32)
    for c in range(E // CONV_COLS):
        cols = slice(c * CONV_COLS, (c + 1) * CONV_COLS)
        part = lambda i: _mm(h, win_ref[:, i * E + c * CONV_COLS:i * E + (c + 1) * CONV_COLS])
        bg, cg, xb, z = part(0), part(1), part(2), part(3)
        cx = cg * xb
        old = carry_s[0:1, cols]
        new = carry_s[1:2, cols]
        prev1 = jnp.where(row == 0, new, pltpu.roll(cx, 1, 0))
        prev2 = jnp.where(row == 0, old, jnp.where(row == 1, new, pltpu.roll(cx, 2, 0)))
        y = cw_ref[0:1, cols] * prev2 + cw_ref[1:2, cols] * prev1 + cw_ref[2:3, cols] * cx
        acc = acc + _mm((bg * y * _silu(z)).astype(BF16), wo_ref[cols, :])
        carry_s[0:2, cols] = cx[tm - 2:tm, :]
    xo_ref[0] = _embed(x + acc, p_ref[0], pg_ref, pe_ref)
    st_ref[0] = carry_s[0:2, :]


def _conv_prompt(x, p_all, layer, g, win, cw, wo, pg, pe, tm):
    B, T, D = x.shape
    E = wo.shape[0]
    row = lambda width: pl.BlockSpec((1, tm, width), lambda b, t: (b, t, 0))
    p_spec = pl.BlockSpec((None, 1, tm, p_all.shape[-1]), lambda b, t: (layer, b, t, 0))
    return pl.pallas_call(
        _conv_prompt_kernel,
        grid=(B, T // tm),
        in_specs=[row(D), p_spec, _resident(g.shape), _resident(win.shape), _resident(cw.shape),
                  _resident(wo.shape), _resident(pg.shape), _resident(pe.shape)],
        out_specs=[row(D), pl.BlockSpec((1, CONV_W - 1, E), lambda b, t: (b, 0, 0))],
        out_shape=[jax.ShapeDtypeStruct((B, T, D), F32), jax.ShapeDtypeStruct((B, CONV_W - 1, E), F32)],
        scratch_shapes=[pltpu.VMEM((8, E), F32)],
        compiler_params=_params("parallel", "arbitrary"),
        name="conv_prompt",
    )(x, p_all, g, win, cw, wo, pg, pe)


def _conv_sample_kernel(x_ref, p_ref, g_ref, win_ref, cw_ref, wo_ref, pg_ref, pe_ref, s0_ref, s1_ref,
                        xo_ref, cx_ref):
    E = wo_ref.shape[0]
    x = x_ref[0]
    h = _rms(x, g_ref[...]).astype(BF16)
    acc = jnp.zeros(x.shape, F32)
    for c in range(E // CONV_COLS):
        cols = slice(c * CONV_COLS, (c + 1) * CONV_COLS)
        part = lambda i: _mm(h, win_ref[:, i * E + c * CONV_COLS:i * E + (c + 1) * CONV_COLS])
        bg, cg, xb, z = part(0), part(1), part(2), part(3)
        cx = cg * xb
        y = cw_ref[0:1, cols] * s0_ref[:, cols] + cw_ref[1:2, cols] * s1_ref[:, cols] + cw_ref[2:3, cols] * cx
        acc = acc + _mm((bg * y * _silu(z)).astype(BF16), wo_ref[cols, :])
        cx_ref[:, cols] = cx
    xo_ref[0] = _embed(x + acc, p_ref[0], pg_ref, pe_ref)


def _conv_sample(x, p_all, layer, g, win, cw, wo, pg, pe, s0, s1):
    _, M, D = x.shape
    E = wo.shape[0]
    p_spec = pl.BlockSpec((None, 1, M, p_all.shape[-1]), lambda i: (layer, 0, 0, 0))
    return pl.pallas_call(
        _conv_sample_kernel,
        grid=(1,),
        in_specs=[_resident(x.shape), p_spec, _resident(g.shape), _resident(win.shape), _resident(cw.shape),
                  _resident(wo.shape), _resident(pg.shape), _resident(pe.shape),
                  _resident(s0.shape), _resident(s1.shape)],
        out_specs=[_whole(x.shape), _whole((M, E))],
        out_shape=[jax.ShapeDtypeStruct(x.shape, F32), jax.ShapeDtypeStruct((M, E), F32)],
        compiler_params=_params("arbitrary"),
        name="conv_sample",
    )(x, p_all, g, win, cw, wo, pg, pe, s0, s1)


def _rope(x, cos, sin_lo, sin_hi):
    half = ROPE_DIM // 2
    return x * cos + pltpu.roll(x, C_DH - half, 1) * sin_lo + pltpu.roll(x, half, 1) * sin_hi


def _c_in_kernel(x_ref, g_ref, w_ref, cos_ref, slo_ref, shi_ref, *refs, tails, dils):
    n_groups = len(C_GROUPS)
    perm_dils = sorted({d for d in dils if d > 1})
    if perm_dils:
        perm_ref, refs = refs[0], refs[1:]
    qkv_refs = refs[:3 * n_groups]
    z_ref = refs[3 * n_groups]
    tail_refs = refs[3 * n_groups + 1:]
    h = _rms(x_ref[0], g_ref[...]).astype(BF16)
    tm = h.shape[0]
    cos, slo, shi = cos_ref[...], slo_ref[...], shi_ref[...]

    for grp in range(n_groups):
        base = 3 * grp * C_WIDTH
        uq = _mm(h, w_ref[:, base:base + C_WIDTH])
        uk = _mm(h, w_ref[:, base + C_WIDTH:base + 2 * C_WIDTH])
        uv = _mm(h, w_ref[:, base + 2 * C_WIDTH:base + 3 * C_WIDTH])
        dil = dils[grp]
        rows = tails[grp]
        out_refs = qkv_refs[3 * grp:3 * grp + 3]
        pieces = ([], [], [])
        for hd in range(C_HEADS):
            hs = slice(hd * C_DH, (hd + 1) * C_DH)
            qh = _rope(uq[:, hs], cos, slo, shi) * Q_SCALE
            kh = _rope(uk[:, hs], cos, slo, shi)
            vh = uv[:, hs]
            tail_refs[2 * grp][0, :, hs] = kh[tm - rows:, :]
            tail_refs[2 * grp + 1][0, :, hs] = vh[tm - rows:, :]
            for ref, done, val in zip(out_refs, pieces, (qh, kh, vh)):
                if dil == 1:
                    ref[0, 0, :, hs] = val.astype(BF16)
                else:
                    done.append(val.astype(BF16))
        if dil > 1:
            perm = perm_ref[perm_dils.index(dil)]
            part = tm // dil
            for ref, done in zip(out_refs, pieces):
                for pair in range(C_HEADS // 2):
                    cols = slice(2 * pair * C_DH, (2 * pair + 2) * C_DH)
                    moved = _mm(perm, jnp.concatenate(done[2 * pair:2 * pair + 2], axis=1)).astype(BF16)
                    for r in range(dil):
                        ref[0, r, :, cols] = moved[r * part:(r + 1) * part, :]

    zb = 3 * n_groups * C_WIDTH
    z_ref[0] = _silu(_mm(h, w_ref[:, zb:zb + C_WIDTH])).astype(BF16)


def _c_in(x, g, w, cos, slo, shi, tm, per_row=False):
    B, T, D = x.shape
    row = lambda width: pl.BlockSpec((1, tm, width), lambda b, t: (b, t, 0))
    table = pl.BlockSpec((tm, C_DH), lambda b, t: (t, 0))
    dils = tuple(1 if per_row else dil for _, dil in C_GROUPS)
    qkv_specs, qkv_shapes = [], []
    for dil in dils:
        qkv_specs += [pl.BlockSpec((1, dil, tm // dil, C_WIDTH), lambda b, t: (b, 0, t, 0))] * 3
        qkv_shapes += [jax.ShapeDtypeStruct((B, dil, T // dil, C_WIDTH), BF16)] * 3
    tails, tail_specs, tail_shapes = [], [], []
    for win, _ in C_GROUPS:
        keep = T if per_row else min(win, T)
        rows = min(tm, keep)
        first_row = T - keep
        tails.append(rows)

        def tail_index(b, t, first_row=first_row, rows=rows):
            return (b, jnp.maximum(((t + 1) * tm - first_row) // rows - 1, 0), 0)

        tail_specs += [pl.BlockSpec((1, rows, C_WIDTH), tail_index)] * 2
        tail_shapes += [jax.ShapeDtypeStruct((B, keep, C_WIDTH), F32)] * 2
    n_qkv = len(qkv_specs)
    perms = []
    for dil in sorted({d for d in dils if d > 1}):
        out_row = jnp.arange(tm)
        src = (out_row % (tm // dil)) * dil + out_row // (tm // dil)
        perms.append(src[:, None] == jnp.arange(tm)[None, :])
    extra = [jnp.stack(perms).astype(BF16)] if perms else []
    outs = pl.pallas_call(
        functools.partial(_c_in_kernel, tails=tuple(tails), dils=dils),
        grid=(B, T // tm),
        in_specs=[row(D), _resident(g.shape), _resident(w.shape), table, table, table]
        + [_resident(a.shape) for a in extra],
        out_specs=qkv_specs + [row(C_WIDTH)] + tail_specs,
        out_shape=qkv_shapes + [jax.ShapeDtypeStruct((B, T, C_WIDTH), BF16)] + tail_shapes,
        compiler_params=_params("parallel", "arbitrary"),
        name="attn_in",
    )(x, g, w, cos, slo, shi, *extra)
    return outs[:n_qkv], outs[n_qkv], outs[n_qkv + 1:]


ATTN_CHUNK = 512
REGROUP_STRIDE = 4


def _attn_chunk_kernel(q_ref, kc_ref, vc_ref, *refs, span):
    acc_ref, st_ref = refs[-2:]
    kp_ref, vp_ref = refs[:-2] if len(refs) == 4 else (None, None)
    chunk = pl.program_id(2)
    qi = lax.broadcasted_iota(jnp.int32, (C_BLOCK, 2 * C_BLOCK), 0)
    kj = lax.broadcasted_iota(jnp.int32, (C_BLOCK, 2 * C_BLOCK), 1)
    dist = qi + C_BLOCK - kj
    band = (dist >= 0) & (dist <= span)
    bias_inner = jnp.where(band, 0.0, -1e30)
    bias_first = jnp.where(band & (kj >= jnp.where(chunk == 0, C_BLOCK, 0)), 0.0, -1e30)
    lane = lax.broadcasted_iota(jnp.int32, (C_BLOCK, LANES), 1)
    ones = jnp.ones((2 * C_BLOCK, LANES), BF16)

    blocks = [(cls, i) for cls in range(q_ref.shape[1]) for i in range(q_ref.shape[2] // C_BLOCK)]
    for cls, i in blocks:
        rows = slice(i * C_BLOCK, (i + 1) * C_BLOCK)

        def keys(prev_ref, cur_ref, cols):
            if i == 0:
                before = cur_ref[0, cls, rows, cols] if prev_ref is None else prev_ref[0, cls, :, cols]
                return jnp.concatenate([before, cur_ref[0, cls, rows, cols]], axis=0)
            return cur_ref[0, cls, (i - 1) * C_BLOCK:(i + 1) * C_BLOCK, cols]

        head_cols = [slice(hd * C_DH, (hd + 1) * C_DH) for hd in range(C_HEADS)]
        s_all = [lax.dot_general(q_ref[0, cls, rows, cols], keys(kp_ref, kc_ref, cols), NT_DIMS,
                                 preferred_element_type=F32) for cols in head_cols]
        stats = jnp.zeros((C_BLOCK, LANES), F32)
        for hd, cols in enumerate(head_cols):
            s = s_all[hd] + (bias_first if i == 0 else bias_inner)
            mx = jnp.max(s, axis=-1, keepdims=True)
            p = jnp.exp2(s - mx).astype(BF16)
            both = _mm(p, jnp.concatenate([keys(vp_ref, vc_ref, cols), ones], axis=1))
            acc_ref[0, cls, rows, cols] = both[:, :C_DH].astype(BF16)
            stats = jnp.where(lane == hd, mx, jnp.where(lane == C_HEADS + hd, both[:, C_DH:], stats))
        st_ref[0, cls, rows, :] = stats


def _attn_prompt_group(q, k, v, span):
    B, dil, n, _ = q.shape
    qc = min(n, ATTN_CHUNK)
    per_chunk = qc // C_BLOCK
    classes = min(dil, ATTN_CHUNK // qc)
    assert dil % classes == 0
    cur = lambda rows, width: pl.BlockSpec((1, classes, rows, width), lambda b, r, c: (b, r, c, 0))
    prev = pl.BlockSpec((1, 1, C_BLOCK, C_WIDTH), lambda b, r, c: (b, r, jnp.maximum(c * per_chunk - 1, 0), 0))
    before = [(k, prev), (v, prev)] if n > qc else []
    return pl.pallas_call(
        functools.partial(_attn_chunk_kernel, span=span),
        grid=(B, dil // classes, n // qc),
        in_specs=[cur(qc, C_WIDTH)] * 3 + [spec for _, spec in before],
        out_specs=[cur(qc, C_WIDTH), cur(qc, LANES)],
        out_shape=[jax.ShapeDtypeStruct((B, dil, n, C_WIDTH), BF16),
                   jax.ShapeDtypeStruct((B, dil, n, LANES), F32)],
        compiler_params=_params("parallel", "parallel", "arbitrary"),
        name="attn_chunks_d%d" % dil,
    )(q, k, v, *[a for a, _ in before])


def _merge_out_kernel(a0_ref, a1_ref, a2_ref, s0_ref, s1_ref, s2_ref, z_ref, x_ref, p_ref,
                      wo_ref, pg_ref, pe_ref, xo_ref, acc_s, st_s, half_s):
    tm = x_ref.shape[1]
    a_refs, s_refs = (a0_ref, a1_ref, a2_ref), (s0_ref, s1_ref, s2_ref)
    dils = [a_ref.shape[1] for a_ref in a_refs]
    for grp, dil in enumerate(dils):
        if dil == 1:
            continue
        two_steps = dil > REGROUP_STRIDE
        first, second = REGROUP_STRIDE, dil // REGROUP_STRIDE
        part = tm // first
        for r in range(dil):
            rows = pl.ds(r, tm // dil, stride=dil)
            acc = a_refs[grp][0, r].astype(F32)
            for hd in range(C_HEADS):
                piece = acc[:, hd * C_DH:(hd + 1) * C_DH]
                if two_steps:
                    half_s[hd, pl.ds((r % first) * part + r // first, tm // dil, stride=second), :] = piece
                else:
                    acc_s[grp * C_HEADS + hd, rows, :] = piece
            st_s[grp, rows, :] = s_refs[grp][0, r]
        if two_steps:
            for hd in range(C_HEADS):
                for r1 in range(first):
                    acc_s[grp * C_HEADS + hd, pl.ds(r1, part, stride=first), :] = (
                        half_s[hd, r1 * part:(r1 + 1) * part, :])

    def acc_of(grp, hd):
        if dils[grp] == 1:
            return a_refs[grp][0, 0, :, hd * C_DH:(hd + 1) * C_DH].astype(F32)
        return acc_s[grp * C_HEADS + hd]

    stats = [s_refs[grp][0, 0] if dils[grp] == 1 else st_s[grp] for grp in range(3)]
    mx = [st[:, 0:C_HEADS] for st in stats]
    ls = [st[:, C_HEADS:2 * C_HEADS] for st in stats]
    top = jnp.maximum(jnp.maximum(mx[0], mx[1]), mx[2])
    es = [jnp.exp2(m - top) for m in mx]
    den = es[0] * ls[0] + es[1] * ls[1] + es[2] * ls[2]
    coef = [e / den for e in es]
    x1 = x_ref[0]
    for pair in range(C_HEADS // 2):
        ys = []
        for hd in (2 * pair, 2 * pair + 1):
            hs = slice(hd * C_DH, (hd + 1) * C_DH)
            o = coef[0][:, hd:hd + 1] * acc_of(0, hd)
            for grp in (1, 2):
                o = o + coef[grp][:, hd:hd + 1] * acc_of(grp, hd)
            ys.append((o * z_ref[0, :, hs].astype(F32)).astype(BF16))
        x1 = x1 + _mm(jnp.concatenate(ys, axis=1), wo_ref[2 * pair * C_DH:(2 * pair + 2) * C_DH, :])
    xo_ref[0] = _embed(x1, p_ref[0], pg_ref, pe_ref)


def _merge_out(accs, stats, z, x, p_all, layer, wo, pg, pe, tm):
    B, T, D = x.shape
    row = lambda width: pl.BlockSpec((1, tm, width), lambda b, t: (b, t, 0))
    classes = lambda a: pl.BlockSpec((1, a.shape[1], tm // a.shape[1], a.shape[3]), lambda b, t: (b, 0, t, 0))
    p_spec = pl.BlockSpec((None, 1, tm, p_all.shape[-1]), lambda b, t: (layer, b, t, 0))
    return pl.pallas_call(
        _merge_out_kernel,
        grid=(B, T // tm),
        in_specs=[classes(a) for a in accs] + [classes(s) for s in stats] + [row(C_WIDTH), row(D), p_spec,
                  _resident(wo.shape), _resident(pg.shape), _resident(pe.shape)],
        out_specs=row(D),
        out_shape=jax.ShapeDtypeStruct((B, T, D), F32),
        scratch_shapes=[pltpu.VMEM((3 * C_HEADS, tm, C_DH), F32), pltpu.VMEM((3, tm, LANES), F32),
                        pltpu.VMEM((C_HEADS, tm, C_DH), F32)],
        compiler_params=_params("parallel", "parallel"),
        name="attn_merge_out",
    )(*accs, *stats, z, x, p_all, wo, pg, pe)


def _attn_sample_kernel(*refs):
    n_groups = len(C_GROUPS)
    z_ref, y_ref = refs[5 * n_groups], refs[5 * n_groups + 1]
    parts = []
    for grp in range(n_groups):
        q_ref, kn_ref, vn_ref, kc_ref, vc_ref = refs[5 * grp:5 * grp + 5]
        q = q_ref[0].astype(F32)
        s_old = jnp.sum(kc_ref[0] * q[None], axis=-1, keepdims=True)
        s_new = jnp.sum(kn_ref[0] * q, axis=-1, keepdims=True)
        mx = jnp.maximum(jnp.max(s_old, axis=0), s_new)
        p_old = jnp.exp2(s_old - mx[None])
        p_new = jnp.exp2(s_new - mx)
        l = jnp.sum(p_old, axis=0) + p_new
        acc = jnp.sum(p_old * vc_ref[0], axis=0) + p_new * vn_ref[0]
        parts.append((acc, mx, l))
    top = jnp.maximum(jnp.maximum(parts[0][1], parts[1][1]), parts[2][1])
    es = [jnp.exp2(m - top) for _, m, _ in parts]
    num = es[0] * parts[0][0] + es[1] * parts[1][0] + es[2] * parts[2][0]
    den = es[0] * parts[0][2] + es[1] * parts[1][2] + es[2] * parts[2][2]
    y_ref[0] = (num / den * z_ref[0].astype(F32)).astype(BF16)


def _attn_sample(qkv, tails, caches, z):
    Bs = z.shape[1]
    heads = lambda a: a.reshape(Bs, C_HEADS, C_DH)
    row = pl.BlockSpec((1, C_HEADS, C_DH), lambda b: (b, 0, 0))
    args, specs = [], []
    for grp, (win, dil) in enumerate(C_GROUPS):
        kc, vc = caches[grp]
        n_buf = kc.shape[1]
        assert n_buf == win and win % dil == 0, "sample window must be fully cached"
        span = win // dil
        view = lambda a: a.reshape(Bs, span, dil, C_HEADS, C_DH)
        cache_spec = pl.BlockSpec((1, span, None, C_HEADS, C_DH), lambda b: (b, 0, 0, 0, 0))
        args += [heads(qkv[3 * grp]), heads(tails[2 * grp]), heads(tails[2 * grp + 1]), view(kc), view(vc)]
        specs += [row, row, row, cache_spec, cache_spec]
    return pl.pallas_call(
        _attn_sample_kernel,
        grid=(Bs,),
        in_specs=specs + [row],
        out_specs=row,
        out_shape=jax.ShapeDtypeStruct((Bs, C_HEADS, C_DH), BF16),
        compiler_params=_params("parallel"),
        name="attn_sample",
    )(*args, heads(z))


def _pool_tail(x, acc, p, pg_ref, pe_ref, fg_ref):
    return _rms(_embed(x + acc, p, pg_ref, pe_ref), fg_ref[...])


def _pool_prompt_kernel(x_ref, p_ref, g_ref, win_ref, wgrp_ref, sc_ref, wo_ref, pg_ref, pe_ref, fg_ref,
                        xo_ref, st_ref, pad_s):
    E = wo_ref.shape[0]
    G = E // len(D_WINDOWS)
    t = pl.program_id(1)

    @pl.when(t == 0)
    def _():
        pad_s[0:POOL_PAD, :] = jnp.zeros((POOL_PAD, E), F32)

    x = x_ref[0]
    tm = x.shape[0]
    h = _rms(x, g_ref[...]).astype(BF16)
    pos = t * tm + lax.broadcasted_iota(jnp.int32, (tm, 1), 0)
    acc = jnp.zeros(x.shape, F32)
    for grp, w in enumerate(D_WINDOWS):
        cols = slice(grp * G, (grp + 1) * G)
        xp = _mm(h, win_ref[:, cols])
        z = _mm(h, win_ref[:, E + grp * G:E + (grp + 1) * G])
        pad_s[POOL_PAD:, cols] = xp
        wsum = pad_s[:, cols]
        shift = 1
        while shift < w:
            wsum = wsum + pltpu.roll(wsum, shift, 0)
            shift *= 2
        cnt = jnp.minimum(w, pos + 1).astype(F32)
        r = wsum[POOL_PAD:, :] / cnt - xp
        y = _mm(r.astype(BF16), wgrp_ref[grp]) * sc_ref[:, cols]
        acc = acc + _mm((y * _silu(z)).astype(BF16), wo_ref[cols, :])
        pad_s[0:POOL_PAD, cols] = xp[tm - POOL_PAD:, :]
    xo_ref[0] = _pool_tail(x, acc, p_ref[0], pg_ref, pe_ref, fg_ref)
    st_ref[0] = pad_s[0:POOL_PAD, :]


def _pool_prompt(x, p_all, layer, g, win, wgrp, sc, wo, pg, pe, fg, tm):
    B, T, D = x.shape
    E = wo.shape[0]
    row = lambda width: pl.BlockSpec((1, tm, width), lambda b, t: (b, t, 0))
    p_spec = pl.BlockSpec((None, 1, tm, p_all.shape[-1]), lambda b, t: (layer, b, t, 0))
    return pl.pallas_call(
        _pool_prompt_kernel,
        grid=(B, T // tm),
        in_specs=[row(D), p_spec] + [_resident(a.shape) for a in (g, win, wgrp, sc, wo, pg, pe, fg)],
        out_specs=[row(D), pl.BlockSpec((1, POOL_PAD, E), lambda b, t: (b, 0, 0))],
        out_shape=[jax.ShapeDtypeStruct((B, T, D), F32), jax.ShapeDtypeStruct((B, POOL_PAD, E), F32)],
        scratch_shapes=[pltpu.VMEM((POOL_PAD + tm, E), F32)],
        compiler_params=_params("parallel", "arbitrary"),
        name="pool_prompt",
    )(x, p_all, g, win, wgrp, sc, wo, pg, pe, fg)


def _pool_sample_kernel(x_ref, p_ref, g_ref, win_ref, wgrp_ref, sc_ref, wo_ref, pg_ref, pe_ref, fg_ref, st_ref,
                        xo_ref, xp_ref, *, pos):
    E = wo_ref.shape[0]
    G = E // len(D_WINDOWS)
    x = x_ref[0]
    h = _rms(x, g_ref[...]).astype(BF16)
    acc = jnp.zeros(x.shape, F32)
    for grp, w in enumerate(D_WINDOWS):
        cols = slice(grp * G, (grp + 1) * G)
        xp = _mm(h, win_ref[:, cols])
        z = _mm(h, win_ref[:, E + grp * G:E + (grp + 1) * G])
        wsum = xp
        for back in range(1, w):
            lo = (POOL_STATE - back) * E + grp * G
            wsum = wsum + st_ref[:, lo:lo + G]
        r = wsum / float(min(w, pos + 1)) - xp
        y = _mm(r.astype(BF16), wgrp_ref[grp]) * sc_ref[:, cols]
        acc = acc + _mm((y * _silu(z)).astype(BF16), wo_ref[cols, :])
        xp_ref[:, cols] = xp
    xo_ref[0] = _pool_tail(x, acc, p_ref[0], pg_ref, pe_ref, fg_ref)


def _pool_sample(x, p_all, layer, g, win, wgrp, sc, wo, pg, pe, fg, st, pos):
    _, M, D = x.shape
    E = wo.shape[0]
    p_spec = pl.BlockSpec((None, 1, M, p_all.shape[-1]), lambda i: (layer, 0, 0, 0))
    return pl.pallas_call(
        functools.partial(_pool_sample_kernel, pos=pos),
        grid=(1,),
        in_specs=[_resident(x.shape), p_spec] + [_resident(a.shape) for a in (g, win, wgrp, sc, wo, pg, pe, fg, st)],
        out_specs=[_whole(x.shape), _whole((M, E))],
        out_shape=[jax.ShapeDtypeStruct(x.shape, F32), jax.ShapeDtypeStruct((M, E), F32)],
        compiler_params=_params("arbitrary"),
        name="pool_sample",
    )(x, p_all, g, win, wgrp, sc, wo, pg, pe, fg, st)


def _rope_tables(pos):
    half = ROPE_DIM // 2
    inv = ROPE_THETA ** (-jnp.arange(half, dtype=F32) / half)
    ang = pos.astype(F32)[:, None] * inv[None, :]
    cos, sin = jnp.cos(ang), jnp.sin(ang)
    rest = C_DH - ROPE_DIM
    n = pos.shape[0]
    cos_t = jnp.concatenate([cos, cos, jnp.ones((n, rest), F32)], axis=1)
    sin_lo = jnp.concatenate([-sin, jnp.zeros((n, half + rest), F32)], axis=1)
    sin_hi = jnp.concatenate([jnp.zeros((n, half), F32), sin, jnp.zeros((n, rest), F32)], axis=1)
    return cos_t, sin_lo, sin_hi


def _row(a):
    return a.reshape(1, -1).astype(F32)


def kernel(x_prompt, x_sample, state_mlstm_C, state_mlstm_n, state_mlstm_m, state_conv, cache_k_w128, cache_v_w128, cache_k_w512, cache_v_w512, cache_k_w2048, cache_v_w2048, state_pool, p_prompt, p_sample, norm_g, pe_w, pg_w, final_g, a_w_in, a_b_if, a_norm_g, a_w_out, b_w_in, b_conv_w, b_w_out, c_w_in, c_w_out, d_w_in, d_w_grp, d_scale, d_w_out):
    B, T, D = x_prompt.shape
    Bs, Ts, _ = x_sample.shape
    assert Ts == 1 and norm_g.shape[0] == 4 and a_w_in.shape[0] == 1
    assert T % 512 == 0 and all(T % (dil * C_BLOCK) == 0 for _, dil in C_GROUPS)
    assert Bs % 8 == 0

    bf = lambda a: a.astype(BF16)
    qk = A_HEADS * A_DK
    n_main = 2 * qk + 3 * A_WIDTH
    a_w = bf(a_w_in[0])
    a_wg = bf(jnp.pad(a_w_in[0, :, n_main:], ((0, 0), (0, LANES - 2 * A_HEADS))))
    a_bif = jnp.pad(a_b_if[0], (0, LANES - 2 * A_HEADS)).reshape(1, LANES)
    a_ng = _row(a_norm_g[0])
    a_wo, b_wi, b_wo, c_wi, c_wo, d_wi, d_wg, d_wo = (
        bf(a_w_out[0]), bf(b_w_in[0]), bf(b_w_out[0]), bf(c_w_in[0]), bf(c_w_out[0]),
        bf(d_w_in[0]), bf(d_w_grp[0]), bf(d_w_out[0]))
    pg, pe = bf(pg_w), bf(pe_w)
    gs = [_row(norm_g[i]) for i in range(4)]
    fg = _row(final_g)
    b_cw = b_conv_w[0]
    d_sc = _row(d_scale[0])
    caches = [(cache_k_w128[0], cache_v_w128[0]), (cache_k_w512[0], cache_v_w512[0]),
              (cache_k_w2048[0], cache_v_w2048[0])]

    x = x_prompt
    q, k, v, oz, gates, kt = _a_in(x, gs[0], a_w, a_wg, a_bif, 512, transposed_k=True)
    hh, c_p, n_p, m_p = _mlstm_prompt(q, k, kt, v, gates)
    x = _a_out(hh, oz, a_ng, x, p_prompt, 0, a_wo, pg[0], pe[0], 512)
    x, conv_p = _conv_prompt(x, p_prompt, 1, gs[1], b_wi, b_cw, b_wo, pg[1], pe[1], 512)
    qkv, z, tails_p = _c_in(x, gs[2], c_wi, *_rope_tables(jnp.arange(T)), 256)
    accs, stats = [], []
    for grp, (win, dil) in enumerate(C_GROUPS):
        acc, st = _attn_prompt_group(qkv[3 * grp], qkv[3 * grp + 1], qkv[3 * grp + 2], win // dil)
        accs.append(acc)
        stats.append(st)
    x = _merge_out(accs, stats, z, x, p_prompt, 2, c_wo, pg[2], pe[2], 512)
    y_prompt, pool_p = _pool_prompt(x, p_prompt, 3, gs[3], d_wi, d_wg, d_sc, d_wo, pg[3], pe[3], fg, 512)

    xs = x_sample.reshape(1, Bs, D)
    ps = p_sample.reshape(p_sample.shape[0], 1, Bs, p_sample.shape[-1])
    q, k, v, oz, gates = _a_in(xs, gs[0], a_w, a_wg, a_bif, Bs)
    m_lanes = jnp.pad(state_mlstm_m[0], ((0, 0), (0, LANES - A_HEADS)))
    hh, c_s, n_s, m_s = _mlstm_sample(q[0], k[0], v[0], gates[0], state_mlstm_C[0], state_mlstm_n[0], m_lanes)
    m_s = m_s[:, :, 0].T
    xs = _a_out(hh.reshape(1, Bs, A_WIDTH), oz, a_ng, xs, ps, 0, a_wo, pg[0], pe[0], Bs)
    xs, cx = _conv_sample(xs, ps, 1, gs[1], b_wi, b_cw, b_wo, pg[1], pe[1],
                          state_conv[0, :, 0, :], state_conv[0, :, 1, :])
    conv_s = jnp.stack([state_conv[0, :, 1, :], cx], axis=1)
    tables = [jnp.broadcast_to(t, (Bs, C_DH)) for t in _rope_tables(PAST_LEN + jnp.arange(1))]
    qkv, z, tails_s = _c_in(xs, gs[2], c_wi, *tables, Bs, per_row=True)
    y = _attn_sample(qkv, tails_s, caches, z)
    xs = _out_embed(y.reshape(1, Bs, C_WIDTH), xs, ps, 2, c_wo, pg[2], pe[2], Bs)
    pool_flat = state_pool[0].reshape(Bs, POOL_STATE * state_pool.shape[-1])
    ys, xp = _pool_sample(xs, ps, 3, gs[3], d_wi, d_wg, d_sc, d_wo, pg[3], pe[3], fg, pool_flat, PAST_LEN)
    pool_s = jnp.concatenate([state_pool[0, :, 1:, :], xp[:, None, :]], axis=1)

    heads = lambda a, lead: a.reshape(1, lead, a.shape[1], C_HEADS, C_DH)
    kv_out = []
    for grp in range(len(C_GROUPS)):
        for j in range(2):
            kv_out += [heads(tails_p[2 * grp + j], B), tails_s[2 * grp + j].reshape(1, Bs, 1, C_HEADS, C_DH)]
    return (y_prompt, ys.reshape(Bs, 1, D),
            c_p[None], c_s[None], n_p[None], n_s[None], m_p[None, :, 0, :A_HEADS], m_s.reshape(1, Bs, A_HEADS),
            conv_p[None], conv_s[None],
            *kv_out,
            pool_p[None, :, 1:, :], pool_s[None])
```

```python
import functools

import jax
import jax.numpy as jnp
from jax import lax
from jax.experimental import pallas as pl
from jax.experimental.pallas import tpu as pltpu

F32 = jnp.float32
BF16 = jnp.bfloat16

EPS = 1e-6
PAST_LEN = 8192

A_HEADS = 8
A_DK = 128
A_DV = 256
A_WIDTH = A_HEADS * A_DV
A_CHUNK = 128
A_CHUNKS_PER_STEP = 4

CONV_W = 3

C_HEADS = 8
C_DH = 128
C_WIDTH = C_HEADS * C_DH
C_GROUPS = ((128, 1), (512, 4), (2048, 16))
C_BLOCK = 128
ROPE_DIM = C_DH // 4
ROPE_THETA = 500000.0
LOG2_E = 1.4426950408889634
Q_SCALE = C_DH ** -0.5 * LOG2_E

D_WINDOWS = (2, 4, 8, 16)
POOL_STATE = max(D_WINDOWS) - 1
POOL_PAD = POOL_STATE + 1

LANES = 128
VMEM_LIMIT_BYTES = 56 * 2 ** 20

NT_DIMS = (((1,), (1,)), ((), ()))
TN_DIMS = (((0,), (0,)), ((), ()))


def _params(*semantics):
    return pltpu.CompilerParams(dimension_semantics=semantics, vmem_limit_bytes=VMEM_LIMIT_BYTES)


def _resident(shape):
    zeros = (0,) * len(shape)
    return pl.BlockSpec(shape, lambda *_: zeros, pipeline_mode=pl.Buffered(1))


def _whole(shape):
    zeros = (0,) * len(shape)
    return pl.BlockSpec(shape, lambda *_: zeros)


def _mm(a, b):
    return jnp.dot(a, b, preferred_element_type=F32)


def _rms(x, g):
    return x * lax.rsqrt(jnp.mean(x * x, axis=-1, keepdims=True) + EPS) * g


def _sigmoid(x):
    return 0.5 * jnp.tanh(0.5 * x) + 0.5


def _silu(x):
    return x * _sigmoid(x)


def _log_sigmoid(x):
    return jnp.minimum(x, 0.0) - jnp.log(1.0 + jnp.exp(-jnp.abs(x)))


def _embed(x1, p, pg_ref, pe_ref):
    gate = _sigmoid(_mm(x1.astype(BF16), pg_ref[...]))
    return x1 + gate * _mm(p.astype(BF16), pe_ref[...])


def _a_in_kernel(x_ref, g_ref, w_ref, wg_ref, bif_ref,
                 q_ref, k_ref, v_ref, oz_ref, gate_ref, *maybe_kt_ref):
    h = _rms(x_ref[0], g_ref[...]).astype(BF16)
    qk = A_HEADS * A_DK

    def proj(lo, width):
        return _mm(h, w_ref[:, lo:lo + width])

    q_ref[0] = proj(0, qk).astype(BF16)
    k = proj(qk, qk) * (A_DK ** -0.5)
    k_ref[0] = k.astype(BF16)
    for kt_ref in maybe_kt_ref:
        kt_ref[0] = k.T.astype(BF16)
    v_ref[0] = proj(2 * qk, A_WIDTH).astype(BF16)
    oz_ref[0] = (_sigmoid(proj(2 * qk + A_WIDTH, A_WIDTH))
                 * _silu(proj(2 * qk + 2 * A_WIDTH, A_WIDTH))).astype(BF16)
    gates = _mm(h, wg_ref[...]) + bif_ref[...]
    lane = lax.broadcasted_iota(jnp.int32, gates.shape, 1)
    gate_ref[0] = jnp.where(lane < A_HEADS, gates, _log_sigmoid(gates))


def _a_in(x, g, w, wg, bif, tm, transposed_k=False):
    B, T, D = x.shape
    qk = A_HEADS * A_DK
    row = lambda width: pl.BlockSpec((1, tm, width), lambda b, t: (b, t, 0))
    out_specs = [row(qk), row(qk), row(A_WIDTH), row(A_WIDTH), row(LANES)]
    out_shape = [jax.ShapeDtypeStruct((B, T, qk), BF16), jax.ShapeDtypeStruct((B, T, qk), BF16),
                 jax.ShapeDtypeStruct((B, T, A_WIDTH), BF16), jax.ShapeDtypeStruct((B, T, A_WIDTH), BF16),
                 jax.ShapeDtypeStruct((B, T, LANES), F32)]
    if transposed_k:
        out_specs.append(pl.BlockSpec((1, qk, tm), lambda b, t: (b, 0, t)))
        out_shape.append(jax.ShapeDtypeStruct((B, qk, T), BF16))
    return pl.pallas_call(
        _a_in_kernel,
        grid=(B, T // tm),
        in_specs=[row(D), _resident(g.shape), _resident(w.shape), _resident(wg.shape), _resident(bif.shape)],
        out_specs=out_specs,
        out_shape=out_shape,
        compiler_params=_params("parallel", "parallel"),
        name="mlstm_in",
    )(x, g, w, wg, bif)


def _mlstm_chunk_kernel(q_ref, k_ref, kt_ref, v_ref, gate_ref,
                        hh_ref, c_out, n_out, m_out, m_s, *state_s):
    chunk = pl.program_id(1)
    ct_s, nb_s = state_s[:A_HEADS], state_s[A_HEADS:]

    @pl.when(chunk == 0)
    def _():
        for ref in state_s:
            ref[...] = jnp.zeros_like(ref)
        m_s[...] = jnp.zeros_like(m_s)

    L = A_CHUNK
    row = lax.broadcasted_iota(jnp.int32, (L, LANES), 0)
    ti = lax.broadcasted_iota(jnp.int32, (L, L), 0)
    si = lax.broadcasted_iota(jnp.int32, (L, L), 1)
    causal = ti >= si
    ones = jnp.ones((L, LANES), BF16)
    heads = range(A_HEADS)
    ks = [slice(h * A_DK, (h + 1) * A_DK) for h in heads]
    vs = [slice(h * A_DV, (h + 1) * A_DV) for h in heads]

    def prefix(x, op, identity):
        step = 1
        while step < L:
            x = op(x, jnp.where(row >= step, pltpu.roll(x, step, 0), identity))
            step *= 2
        return x

    def one_chunk(rows):
        gates = gate_ref[0, rows, :]
        b = pltpu.roll(prefix(gates, jnp.add, 0.0), LANES - A_HEADS, 1)
        c = gates - b
        m_row = m_s[0:1, :]
        top = jnp.maximum(m_row, prefix(c, jnp.maximum, -jnp.inf))
        top_last = top[L - 1:L, :]
        g_all = jnp.exp(m_row - top)
        floor_all = jnp.exp(-(b + top))
        w_all = jnp.exp(c - top_last)
        decay_row = jnp.exp(m_row - top_last)
        m_s[...] = jnp.broadcast_to(b[L - 1:L, :] + top_last, m_s.shape)
        c_t = c.T

        s_raw = [_mm(q_ref[0, rows, ks[h]], kt_ref[0, ks[h], rows]) for h in heads]
        upd = []
        for h in heads:
            w_b = jnp.broadcast_to(w_all[:, h:h + 1], (L, LANES))
            wv = jnp.concatenate([w_b, w_b], axis=1) * v_ref[0, rows, vs[h]].astype(F32)
            upd.append(_mm(kt_ref[0, ks[h], rows], jnp.concatenate([wv.astype(BF16), w_b.astype(BF16)], axis=1)))

        for h in heads:
            ct = ct_s[h][...]
            nb = nb_s[h][...]
            dmat = jnp.exp(jnp.where(causal, -top[:, h:h + 1] + c_t[h:h + 1, :], -jnp.inf))
            s = s_raw[h] * dmat
            gq = g_all[:, h:h + 1] * q_ref[0, rows, ks[h]].astype(F32)
            lhs = jnp.concatenate([s.astype(BF16), gq.astype(BF16)], axis=1)
            num = _mm(lhs, jnp.concatenate([v_ref[0, rows, vs[h]], ct.astype(BF16)], axis=0))
            den = _mm(lhs, jnp.concatenate([ones, nb.astype(BF16)], axis=0))
            scale = 1.0 / jnp.maximum(jnp.abs(den), floor_all[:, h:h + 1])
            hh_ref[0, rows, vs[h]] = (num * jnp.concatenate([scale, scale], axis=1)).astype(BF16)
            decay = decay_row[:, h:h + 1]
            ct_s[h][...] = decay * ct + upd[h][:, :A_DV]
            nb_s[h][...] = decay * nb + upd[h][:, A_DV:]

    for sub in range(q_ref.shape[1] // L):
        one_chunk(slice(sub * L, (sub + 1) * L))

    @pl.when(chunk == pl.num_programs(1) - 1)
    def _():
        for h in range(A_HEADS):
            c_out[0, h] = ct_s[h][...].T
            n_out[0, h:h + 1, :] = nb_s[h][...].T[0:1, :]
        m_out[0] = m_s[...]


def _mlstm_prompt(q, k, kt, v, gates):
    B, T, _ = q.shape
    L = A_CHUNK * A_CHUNKS_PER_STEP
    qk = A_HEADS * A_DK
    row = lambda width: pl.BlockSpec((1, L, width), lambda b, c: (b, c, 0))
    state = lambda *shape: pl.BlockSpec((1,) + shape, lambda b, c: (b,) + (0,) * len(shape))
    return pl.pallas_call(
        _mlstm_chunk_kernel,
        grid=(B, T // L),
        in_specs=[row(qk), row(qk), pl.BlockSpec((1, qk, L), lambda b, c: (b, 0, c)), row(A_WIDTH), row(LANES)],
        out_specs=[row(A_WIDTH), state(A_HEADS, A_DV, A_DK), state(A_HEADS, A_DK), state(A_HEADS, LANES)],
        out_shape=[jax.ShapeDtypeStruct((B, T, A_WIDTH), BF16),
                   jax.ShapeDtypeStruct((B, A_HEADS, A_DV, A_DK), F32),
                   jax.ShapeDtypeStruct((B, A_HEADS, A_DK), F32),
                   jax.ShapeDtypeStruct((B, A_HEADS, LANES), F32)],
        scratch_shapes=[pltpu.VMEM((A_HEADS, LANES), F32)] + [pltpu.VMEM((A_DK, A_DV), F32)] * A_HEADS
        + [pltpu.VMEM((A_DK, LANES), F32)] * A_HEADS,
        compiler_params=_params("parallel", "arbitrary"),
        name="mlstm_chunks",
    )(q, k, kt, v, gates)


def _mlstm_step_kernel(q_ref, k_ref, v_ref, gate_ref, m_ref, c_ref, n_ref,
                       hh_ref, c_out, n_out, m_out, hht_s):
    head = pl.program_id(0)
    Bs = q_ref.shape[0]
    q = q_ref[...].astype(F32)
    k = k_ref[...].astype(F32)
    n = n_ref[...]
    lane = lax.broadcasted_iota(jnp.int32, (Bs, LANES), 1)
    pick = lambda a, idx: jnp.sum(jnp.where(lane == idx, a, 0.0), axis=-1, keepdims=True)
    ig = pick(gate_ref[...], head)
    lf = pick(gate_ref[...], A_HEADS + head)
    m_prev = pick(m_ref[...], head)

    inter = lf + m_prev
    m_t = jnp.maximum(inter, ig)
    dm = jnp.exp(ig - m_t)
    g = jnp.exp(inter - m_t)
    s = jnp.sum(q * k, axis=-1, keepdims=True) * dm
    den = s + g * jnp.sum(q * n, axis=-1, keepdims=True)
    scale = 1.0 / jnp.maximum(jnp.abs(den), jnp.exp(-m_t))
    n_out[...] = g * n + dm * k
    m_out[...] = jnp.broadcast_to(m_t, (Bs, LANES))

    v_pad = jnp.concatenate([v_ref[...].astype(F32), jnp.zeros((LANES - Bs, A_DV), F32)], axis=0)
    vt = v_pad.T
    hht_s[...] = jnp.zeros_like(hht_s)
    for b in range(Bs):
        c_b = c_ref[b]
        one = slice(b, b + 1)
        v_col = vt[:, one]
        cq = jnp.sum(c_b * q[one, :], axis=-1, keepdims=True)
        hht_s[:, one] = (s[one, :] * v_col + g[one, :] * cq) * scale[one, :]
        c_out[b] = g[one, :] * c_b + (dm[one, :] * v_col) * k[one, :]
    hh_ref[...] = hht_s[...].T[0:Bs, :].astype(BF16)


def _mlstm_sample(q, k, v, gates, c0, n0, m0):
    Bs = q.shape[0]
    assert Bs <= LANES
    cols = lambda width: pl.BlockSpec((Bs, width), lambda h: (0, h))
    full = pl.BlockSpec((Bs, LANES), lambda h: (0, 0))
    c_spec = pl.BlockSpec((Bs, None, A_DV, A_DK), lambda h: (0, h, 0, 0))
    hh, c1, n1, m1 = pl.pallas_call(
        _mlstm_step_kernel,
        grid=(A_HEADS,),
        in_specs=[cols(A_DK), cols(A_DK), cols(A_DV), full, full, c_spec, cols(A_DK)],
        out_specs=[cols(A_DV), c_spec, cols(A_DK), pl.BlockSpec((None, Bs, LANES), lambda h: (h, 0, 0))],
        out_shape=[jax.ShapeDtypeStruct((Bs, A_WIDTH), BF16),
                   jax.ShapeDtypeStruct(c0.shape, F32), jax.ShapeDtypeStruct((Bs, A_HEADS * A_DK), F32),
                   jax.ShapeDtypeStruct((A_HEADS, Bs, LANES), F32)],
        scratch_shapes=[pltpu.VMEM((A_DV, LANES), F32)],
        compiler_params=_params("parallel"),
        name="mlstm_step",
    )(q, k, v, gates, m0, c0, n0.reshape(Bs, A_HEADS * A_DK))
    return hh, c1, n1.reshape(n0.shape), m1


def _a_out_kernel(hh_ref, oz_ref, ng_ref, x_ref, p_ref, wo_ref, pg_ref, pe_ref, xo_ref):
    x1 = x_ref[0]
    for h in range(A_HEADS):
        vs = slice(h * A_DV, (h + 1) * A_DV)
        hh = hh_ref[0, :, vs].astype(F32)
        hn = hh * lax.rsqrt(jnp.mean(hh * hh, axis=-1, keepdims=True) + EPS) * ng_ref[:, vs]
        y = hn * oz_ref[0, :, vs].astype(F32)
        x1 = x1 + _mm(y.astype(BF16), wo_ref[vs, :])
    xo_ref[0] = _embed(x1, p_ref[0], pg_ref, pe_ref)


def _a_out(hh, oz, ng, x, p_all, layer, wo, pg, pe, tm):
    B, T, D = x.shape
    row = lambda width: pl.BlockSpec((1, tm, width), lambda b, t: (b, t, 0))
    p_spec = pl.BlockSpec((None, 1, tm, p_all.shape[-1]), lambda b, t: (layer, b, t, 0))
    return pl.pallas_call(
        _a_out_kernel,
        grid=(B, T // tm),
        in_specs=[row(A_WIDTH), row(A_WIDTH), _resident(ng.shape), row(D), p_spec,
                  _resident(wo.shape), _resident(pg.shape), _resident(pe.shape)],
        out_specs=row(D),
        out_shape=jax.ShapeDtypeStruct((B, T, D), F32),
        compiler_params=_params("parallel", "parallel"),
        name="mlstm_out",
    )(hh, oz, ng, x, p_all, wo, pg, pe)


def _out_kernel(y_ref, x_ref, p_ref, wo_ref, pg_ref, pe_ref, xo_ref):
    x1 = x_ref[0] + _mm(y_ref[0], wo_ref[...])
    xo_ref[0] = _embed(x1, p_ref[0], pg_ref, pe_ref)


def _out_embed(y, x, p_all, layer, wo, pg, pe, tm):
    B, T, D = x.shape
    row = lambda width: pl.BlockSpec((1, tm, width), lambda b, t: (b, t, 0))
    p_spec = pl.BlockSpec((None, 1, tm, p_all.shape[-1]), lambda b, t: (layer, b, t, 0))
    return pl.pallas_call(
        _out_kernel,
        grid=(B, T // tm),
        in_specs=[row(y.shape[-1]), row(D), p_spec, _resident(wo.shape), _resident(pg.shape), _resident(pe.shape)],
        out_specs=row(D),
        out_shape=jax.ShapeDtypeStruct((B, T, D), F32),
        compiler_params=_params("parallel", "parallel"),
        name="out_embed",
    )(y, x, p_all, wo, pg, pe)


CONV_COLS = 512


def _conv_prompt_kernel(x_ref, p_ref, g_ref, win_ref, cw_ref, wo_ref, pg_ref, pe_ref,
                        xo_ref, st_ref, carry_s):
    E = wo_ref.shape[0]

    @pl.when(pl.program_id(1) == 0)
    def _():
        carry_s[...] = jnp.zeros_like(carry_s)

    x = x_ref[0]
    tm = x.shape[0]
    h = _rms(x, g_ref[...]).astype(BF16)
    row = lax.broadcasted_iota(jnp.int32, (tm, CONV_COLS), 0)
    acc = jnp.zeros(x.shape, F32)
    for c in range(E // CONV_COLS):
        cols = slice(c * CONV_COLS, (c + 1) * CONV_COLS)
        part = lambda i: _mm(h, win_ref[:, i * E + c * CONV_COLS:i * E + (c + 1) * CONV_COLS])
        bg, cg, xb, z = part(0), part(1), part(2), part(3)
        cx = cg * xb
        old = carry_s[0:1, cols]
        new = carry_s[1:2, cols]
        prev1 = jnp.where(row == 0, new, pltpu.roll(cx, 1, 0))
        prev2 = jnp.where(row == 0, old, jnp.where(row == 1, new, pltpu.roll(cx, 2, 0)))
        y = cw_ref[0:1, cols] * prev2 + cw_ref[1:2, cols] * prev1 + cw_ref[2:3, cols] * cx
        acc = acc + _mm((bg * y * _silu(z)).astype(BF16), wo_ref[cols, :])
        carry_s[0:2, cols] = cx[tm - 2:tm, :]
    xo_ref[0] = _embed(x + acc, p_ref[0], pg_ref, pe_ref)
    st_ref[0] = carry_s[0:2, :]


def _conv_prompt(x, p_all, layer, g, win, cw, wo, pg, pe, tm):
    B, T, D = x.shape
    E = wo.shape[0]
    row = lambda width: pl.BlockSpec((1, tm, width), lambda b, t: (b, t, 0))
    p_spec = pl.BlockSpec((None, 1, tm, p_all.shape[-1]), lambda b, t: (layer, b, t, 0))
    return pl.pallas_call(
        _conv_prompt_kernel,
        grid=(B, T // tm),
        in_specs=[row(D), p_spec, _resident(g.shape), _resident(win.shape), _resident(cw.shape),
                  _resident(wo.shape), _resident(pg.shape), _resident(pe.shape)],
        out_specs=[row(D), pl.BlockSpec((1, CONV_W - 1, E), lambda b, t: (b, 0, 0))],
        out_shape=[jax.ShapeDtypeStruct((B, T, D), F32), jax.ShapeDtypeStruct((B, CONV_W - 1, E), F32)],
        scratch_shapes=[pltpu.VMEM((8, E), F32)],
        compiler_params=_params("parallel", "arbitrary"),
        name="conv_prompt",
    )(x, p_all, g, win, cw, wo, pg, pe)


def _conv_sample_kernel(x_ref, p_ref, g_ref, win_ref, cw_ref, wo_ref, pg_ref, pe_ref, s0_ref, s1_ref,
                        xo_ref, cx_ref):
    E = wo_ref.shape[0]
    x = x_ref[0]
    h = _rms(x, g_ref[...]).astype(BF16)
    acc = jnp.zeros(x.shape, F32)
    for c in range(E // CONV_COLS):
        cols = slice(c * CONV_COLS, (c + 1) * CONV_COLS)
        part = lambda i: _mm(h, win_ref[:, i * E + c * CONV_COLS:i * E + (c + 1) * CONV_COLS])
        bg, cg, xb, z = part(0), part(1), part(2), part(3)
        cx = cg * xb
        y = cw_ref[0:1, cols] * s0_ref[:, cols] + cw_ref[1:2, cols] * s1_ref[:, cols] + cw_ref[2:3, cols] * cx
        acc = acc + _mm((bg * y * _silu(z)).astype(BF16), wo_ref[cols, :])
        cx_ref[:, cols] = cx
    xo_ref[0] = _embed(x + acc, p_ref[0], pg_ref, pe_ref)


def _conv_sample(x, p_all, layer, g, win, cw, wo, pg, pe, s0, s1):
    _, M, D = x.shape
    E = wo.shape[0]
    p_spec = pl.BlockSpec((None, 1, M, p_all.shape[-1]), lambda i: (layer, 0, 0, 0))
    return pl.pallas_call(
        _conv_sample_kernel,
        grid=(1,),
        in_specs=[_resident(x.shape), p_spec, _resident(g.shape), _resident(win.shape), _resident(cw.shape),
                  _resident(wo.shape), _resident(pg.shape), _resident(pe.shape),
                  _resident(s0.shape), _resident(s1.shape)],
        out_specs=[_whole(x.shape), _whole((M, E))],
        out_shape=[jax.ShapeDtypeStruct(x.shape, F32), jax.ShapeDtypeStruct((M, E), F32)],
        compiler_params=_params("arbitrary"),
        name="conv_sample",
    )(x, p_all, g, win, cw, wo, pg, pe, s0, s1)


def _rope(x, cos, sin_lo, sin_hi):
    half = ROPE_DIM // 2
    return x * cos + pltpu.roll(x, C_DH - half, 1) * sin_lo + pltpu.roll(x, half, 1) * sin_hi


def _c_in_kernel(x_ref, g_ref, w_ref, cos_ref, slo_ref, shi_ref, *refs, tails, dils):
    n_groups = len(C_GROUPS)
    perm_dils = sorted({d for d in dils if d > 1})
    if perm_dils:
        perm_ref, refs = refs[0], refs[1:]
    qkv_refs = refs[:3 * n_groups]
    z_ref = refs[3 * n_groups]
    tail_refs = refs[3 * n_groups + 1:]
    h = _rms(x_ref[0], g_ref[...]).astype(BF16)
    tm = h.shape[0]
    cos, slo, shi = cos_ref[...], slo_ref[...], shi_ref[...]

    for grp in range(n_groups):
        base = 3 * grp * C_WIDTH
        uq = _mm(h, w_ref[:, base:base + C_WIDTH])
        uk = _mm(h, w_ref[:, base + C_WIDTH:base + 2 * C_WIDTH])
        uv = _mm(h, w_ref[:, base + 2 * C_WIDTH:base + 3 * C_WIDTH])
        dil = dils[grp]
        rows = tails[grp]
        out_refs = qkv_refs[3 * grp:3 * grp + 3]
        pieces = ([], [], [])
        for hd in range(C_HEADS):
            hs = slice(hd * C_DH, (hd + 1) * C_DH)
            qh = _rope(uq[:, hs], cos, slo, shi) * Q_SCALE
            kh = _rope(uk[:, hs], cos, slo, shi)
            vh = uv[:, hs]
            tail_refs[2 * grp][0, :, hs] = kh[tm - rows:, :]
            tail_refs[2 * grp + 1][0, :, hs] = vh[tm - rows:, :]
            for ref, done, val in zip(out_refs, pieces, (qh, kh, vh)):
                if dil == 1:
                    ref[0, 0, :, hs] = val.astype(BF16)
                else:
                    done.append(val.astype(BF16))
        if dil > 1:
            perm = perm_ref[perm_dils.index(dil)]
            part = tm // dil
            for ref, done in zip(out_refs, pieces):
                for pair in range(C_HEADS // 2):
                    cols = slice(2 * pair * C_DH, (2 * pair + 2) * C_DH)
                    moved = _mm(perm, jnp.concatenate(done[2 * pair:2 * pair + 2], axis=1)).astype(BF16)
                    for r in range(dil):
                        ref[0, r, :, cols] = moved[r * part:(r + 1) * part, :]

    zb = 3 * n_groups * C_WIDTH
    z_ref[0] = _silu(_mm(h, w_ref[:, zb:zb + C_WIDTH])).astype(BF16)


def _c_in(x, g, w, cos, slo, shi, tm, per_row=False):
    B, T, D = x.shape
    row = lambda width: pl.BlockSpec((1, tm, width), lambda b, t: (b, t, 0))
    table = pl.BlockSpec((tm, C_DH), lambda b, t: (t, 0))
    dils = tuple(1 if per_row else dil for _, dil in C_GROUPS)
    qkv_specs, qkv_shapes = [], []
    for dil in dils:
        qkv_specs += [pl.BlockSpec((1, dil, tm // dil, C_WIDTH), lambda b, t: (b, 0, t, 0))] * 3
        qkv_shapes += [jax.ShapeDtypeStruct((B, dil, T // dil, C_WIDTH), BF16)] * 3
    tails, tail_specs, tail_shapes = [], [], []
    for win, _ in C_GROUPS:
        keep = T if per_row else min(win, T)
        rows = min(tm, keep)
        first_row = T - keep
        tails.append(rows)

        def tail_index(b, t, first_row=first_row, rows=rows):
            return (b, jnp.maximum(((t + 1) * tm - first_row) // rows - 1, 0), 0)

        tail_specs += [pl.BlockSpec((1, rows, C_WIDTH), tail_index)] * 2
        tail_shapes += [jax.ShapeDtypeStruct((B, keep, C_WIDTH), F32)] * 2
    n_qkv = len(qkv_specs)
    perms = []
    for dil in sorted({d for d in dils if d > 1}):
        out_row = jnp.arange(tm)
        src = (out_row % (tm // dil)) * dil + out_row // (tm // dil)
        perms.append(src[:, None] == jnp.arange(tm)[None, :])
    extra = [jnp.stack(perms).astype(BF16)] if perms else []
    outs = pl.pallas_call(
        functools.partial(_c_in_kernel, tails=tuple(tails), dils=dils),
        grid=(B, T // tm),
        in_specs=[row(D), _resident(g.shape), _resident(w.shape), table, table, table]
        + [_resident(a.shape) for a in extra],
        out_specs=qkv_specs + [row(C_WIDTH)] + tail_specs,
        out_shape=qkv_shapes + [jax.ShapeDtypeStruct((B, T, C_WIDTH), BF16)] + tail_shapes,
        compiler_params=_params("parallel", "arbitrary"),
        name="attn_in",
    )(x, g, w, cos, slo, shi, *extra)
    return outs[:n_qkv], outs[n_qkv], outs[n_qkv + 1:]


ATTN_CHUNK = 1024
REGROUP_STRIDE = 4


def _attn_chunk_kernel(q_ref, kc_ref, vc_ref, *refs, span):
    acc_ref, st_ref = refs[-2:]
    kp_ref, vp_ref = refs[:-2] if len(refs) == 4 else (None, None)
    chunk = pl.program_id(2)
    qi = lax.broadcasted_iota(jnp.int32, (C_BLOCK, 2 * C_BLOCK), 0)
    kj = lax.broadcasted_iota(jnp.int32, (C_BLOCK, 2 * C_BLOCK), 1)
    dist = qi + C_BLOCK - kj
    band = (dist >= 0) & (dist <= span)
    bias_inner = jnp.where(band, 0.0, -1e30)
    bias_first = jnp.where(band & (kj >= jnp.where(chunk == 0, C_BLOCK, 0)), 0.0, -1e30)
    lane = lax.broadcasted_iota(jnp.int32, (C_BLOCK, LANES), 1)
    ones = jnp.ones((2 * C_BLOCK, LANES), BF16)

    blocks = [(cls, i) for cls in range(q_ref.shape[1]) for i in range(q_ref.shape[2] // C_BLOCK)]
    for cls, i in blocks:
        rows = slice(i * C_BLOCK, (i + 1) * C_BLOCK)

        def keys(prev_ref, cur_ref, cols):
            if i == 0:
                before = cur_ref[0, cls, rows, cols] if prev_ref is None else prev_ref[0, cls, :, cols]
                return jnp.concatenate([before, cur_ref[0, cls, rows, cols]], axis=0)
            return cur_ref[0, cls, (i - 1) * C_BLOCK:(i + 1) * C_BLOCK, cols]

        head_cols = [slice(hd * C_DH, (hd + 1) * C_DH) for hd in range(C_HEADS)]
        s_all = [lax.dot_general(q_ref[0, cls, rows, cols], keys(kp_ref, kc_ref, cols), NT_DIMS,
                                 preferred_element_type=F32) for cols in head_cols]
        stats = jnp.zeros((C_BLOCK, LANES), F32)
        for hd, cols in enumerate(head_cols):
            s = s_all[hd] + (bias_first if i == 0 else bias_inner)
            mx = jnp.max(s, axis=-1, keepdims=True)
            p = jnp.exp2(s - mx).astype(BF16)
            both = _mm(p, jnp.concatenate([keys(vp_ref, vc_ref, cols), ones], axis=1))
            acc_ref[0, cls, rows, cols] = both[:, :C_DH].astype(BF16)
            stats = jnp.where(lane == hd, mx, jnp.where(lane == C_HEADS + hd, both[:, C_DH:], stats))
        st_ref[0, cls, rows, :] = stats


def _attn_prompt_group(q, k, v, span):
    B, dil, n, _ = q.shape
    qc = min(n, ATTN_CHUNK)
    per_chunk = qc // C_BLOCK
    classes = min(dil, ATTN_CHUNK // qc)
    assert dil % classes == 0
    cur = lambda rows, width: pl.BlockSpec((1, classes, rows, width), lambda b, r, c: (b, r, c, 0))
    prev = pl.BlockSpec((1, 1, C_BLOCK, C_WIDTH), lambda b, r, c: (b, r, jnp.maximum(c * per_chunk - 1, 0), 0))
    before = [(k, prev), (v, prev)] if n > qc else []
    return pl.pallas_call(
        functools.partial(_attn_chunk_kernel, span=span),
        grid=(B, dil // classes, n // qc),
        in_specs=[cur(qc, C_WIDTH)] * 3 + [spec for _, spec in before],
        out_specs=[cur(qc, C_WIDTH), cur(qc, LANES)],
        out_shape=[jax.ShapeDtypeStruct((B, dil, n, C_WIDTH), BF16),
                   jax.ShapeDtypeStruct((B, dil, n, LANES), F32)],
        compiler_params=_params("parallel", "parallel", "arbitrary"),
        name="attn_chunks_d%d" % dil,
    )(q, k, v, *[a for a, _ in before])


def _merge_out_kernel(a0_ref, a1_ref, a2_ref, s0_ref, s1_ref, s2_ref, z_ref, x_ref, p_ref,
                      wo_ref, pg_ref, pe_ref, xo_ref, acc_s, st_s, half_s):
    tm = x_ref.shape[1]
    a_refs, s_refs = (a0_ref, a1_ref, a2_ref), (s0_ref, s1_ref, s2_ref)
    dils = [a_ref.shape[1] for a_ref in a_refs]
    for grp, dil in enumerate(dils):
        if dil == 1:
            continue
        two_steps = dil > REGROUP_STRIDE
        first, second = REGROUP_STRIDE, dil // REGROUP_STRIDE
        part = tm // first
        for r in range(dil):
            rows = pl.ds(r, tm // dil, stride=dil)
            acc = a_refs[grp][0, r].astype(F32)
            for hd in range(C_HEADS):
                piece = acc[:, hd * C_DH:(hd + 1) * C_DH]
                if two_steps:
                    half_s[hd, pl.ds((r % first) * part + r // first, tm // dil, stride=second), :] = piece
                else:
                    acc_s[grp * C_HEADS + hd, rows, :] = piece
            st_s[grp, rows, :] = s_refs[grp][0, r]
        if two_steps:
            for hd in range(C_HEADS):
                for r1 in range(first):
                    acc_s[grp * C_HEADS + hd, pl.ds(r1, part, stride=first), :] = (
                        half_s[hd, r1 * part:(r1 + 1) * part, :])

    def acc_of(grp, hd):
        if dils[grp] == 1:
            return a_refs[grp][0, 0, :, hd * C_DH:(hd + 1) * C_DH].astype(F32)
        return acc_s[grp * C_HEADS + hd]

    stats = [s_refs[grp][0, 0] if dils[grp] == 1 else st_s[grp] for grp in range(3)]
    mx = [st[:, 0:C_HEADS] for st in stats]
    ls = [st[:, C_HEADS:2 * C_HEADS] for st in stats]
    top = jnp.maximum(jnp.maximum(mx[0], mx[1]), mx[2])
    es = [jnp.exp2(m - top) for m in mx]
    den = es[0] * ls[0] + es[1] * ls[1] + es[2] * ls[2]
    coef = [e / den for e in es]
    x1 = x_ref[0]
    for pair in range(C_HEADS // 2):
        ys = []
        for hd in (2 * pair, 2 * pair + 1):
            hs = slice(hd * C_DH, (hd + 1) * C_DH)
            o = coef[0][:, hd:hd + 1] * acc_of(0, hd)
            for grp in (1, 2):
                o = o + coef[grp][:, hd:hd + 1] * acc_of(grp, hd)
            ys.append((o * z_ref[0, :, hs].astype(F32)).astype(BF16))
        x1 = x1 + _mm(jnp.concatenate(ys, axis=1), wo_ref[2 * pair * C_DH:(2 * pair + 2) * C_DH, :])
    xo_ref[0] = _embed(x1, p_ref[0], pg_ref, pe_ref)


def _merge_out(accs, stats, z, x, p_all, layer, wo, pg, pe, tm):
    B, T, D = x.shape
    row = lambda width: pl.BlockSpec((1, tm, width), lambda b, t: (b, t, 0))
    classes = lambda a: pl.BlockSpec((1, a.shape[1], tm // a.shape[1], a.shape[3]), lambda b, t: (b, 0, t, 0))
    p_spec = pl.BlockSpec((None, 1, tm, p_all.shape[-1]), lambda b, t: (layer, b, t, 0))
    return pl.pallas_call(
        _merge_out_kernel,
        grid=(B, T // tm),
        in_specs=[classes(a) for a in accs] + [classes(s) for s in stats] + [row(C_WIDTH), row(D), p_spec,
                  _resident(wo.shape), _resident(pg.shape), _resident(pe.shape)],
        out_specs=row(D),
        out_shape=jax.ShapeDtypeStruct((B, T, D), F32),
        scratch_shapes=[pltpu.VMEM((3 * C_HEADS, tm, C_DH), F32), pltpu.VMEM((3, tm, LANES), F32),
                        pltpu.VMEM((C_HEADS, tm, C_DH), F32)],
        compiler_params=_params("parallel", "parallel"),
        name="attn_merge_out",
    )(*accs, *stats, z, x, p_all, wo, pg, pe)


def _attn_sample_kernel(*refs):
    n_groups = len(C_GROUPS)
    z_ref, y_ref = refs[5 * n_groups], refs[5 * n_groups + 1]
    parts = []
    for grp in range(n_groups):
        q_ref, kn_ref, vn_ref, kc_ref, vc_ref = refs[5 * grp:5 * grp + 5]
        q = q_ref[0].astype(F32)
        s_old = jnp.sum(kc_ref[0] * q[None], axis=-1, keepdims=True)
        s_new = jnp.sum(kn_ref[0] * q, axis=-1, keepdims=True)
        mx = jnp.maximum(jnp.max(s_old, axis=0), s_new)
        p_old = jnp.exp2(s_old - mx[None])
        p_new = jnp.exp2(s_new - mx)
        l = jnp.sum(p_old, axis=0) + p_new
        acc = jnp.sum(p_old * vc_ref[0], axis=0) + p_new * vn_ref[0]
        parts.append((acc, mx, l))
    top = jnp.maximum(jnp.maximum(parts[0][1], parts[1][1]), parts[2][1])
    es = [jnp.exp2(m - top) for _, m, _ in parts]
    num = es[0] * parts[0][0] + es[1] * parts[1][0] + es[2] * parts[2][0]
    den = es[0] * parts[0][2] + es[1] * parts[1][2] + es[2] * parts[2][2]
    y_ref[0] = (num / den * z_ref[0].astype(F32)).astype(BF16)


def _attn_sample(qkv, tails, caches, z):
    Bs = z.shape[1]
    heads = lambda a: a.reshape(Bs, C_HEADS, C_DH)
    row = pl.BlockSpec((1, C_HEADS, C_DH), lambda b: (b, 0, 0))
    args, specs = [], []
    for grp, (win, dil) in enumerate(C_GROUPS):
        kc, vc = caches[grp]
        n_buf = kc.shape[1]
        assert n_buf == win and win % dil == 0, "sample window must be fully cached"
        span = win // dil
        view = lambda a: a.reshape(Bs, span, dil, C_HEADS, C_DH)
        cache_spec = pl.BlockSpec((1, span, None, C_HEADS, C_DH), lambda b: (b, 0, 0, 0, 0))
        args += [heads(qkv[3 * grp]), heads(tails[2 * grp]), heads(tails[2 * grp + 1]), view(kc), view(vc)]
        specs += [row, row, row, cache_spec, cache_spec]
    return pl.pallas_call(
        _attn_sample_kernel,
        grid=(Bs,),
        in_specs=specs + [row],
        out_specs=row,
        out_shape=jax.ShapeDtypeStruct((Bs, C_HEADS, C_DH), BF16),
        compiler_params=_params("parallel"),
        name="attn_sample",
    )(*args, heads(z))


def _pool_tail(x, acc, p, pg_ref, pe_ref, fg_ref):
    return _rms(_embed(x + acc, p, pg_ref, pe_ref), fg_ref[...])


def _pool_prompt_kernel(x_ref, p_ref, g_ref, win_ref, wgrp_ref, sc_ref, wo_ref, pg_ref, pe_ref, fg_ref,
                        xo_ref, st_ref, pad_s):
    E = wo_ref.shape[0]
    G = E // len(D_WINDOWS)
    t = pl.program_id(1)

    @pl.when(t == 0)
    def _():
        pad_s[0:POOL_PAD, :] = jnp.zeros((POOL_PAD, E), F32)

    x = x_ref[0]
    tm = x.shape[0]
    h = _rms(x, g_ref[...]).astype(BF16)
    pos = t * tm + lax.broadcasted_iota(jnp.int32, (tm, 1), 0)
    acc = jnp.zeros(x.shape, F32)
    for grp, w in enumerate(D_WINDOWS):
        cols = slice(grp * G, (grp + 1) * G)
        xp = _mm(h, win_ref[:, cols])
        z = _mm(h, win_ref[:, E + grp * G:E + (grp + 1) * G])
        pad_s[POOL_PAD:, cols] = xp
        wsum = pad_s[:, cols]
        shift = 1
        while shift < w:
            wsum = wsum + pltpu.roll(wsum, shift, 0)
            shift *= 2
        cnt = jnp.minimum(w, pos + 1).astype(F32)
        r = wsum[POOL_PAD:, :] / cnt - xp
        y = _mm(r.astype(BF16), wgrp_ref[grp]) * sc_ref[:, cols]
        acc = acc + _mm((y * _silu(z)).astype(BF16), wo_ref[cols, :])
        pad_s[0:POOL_PAD, cols] = xp[tm - POOL_PAD:, :]
    xo_ref[0] = _pool_tail(x, acc, p_ref[0], pg_ref, pe_ref, fg_ref)
    st_ref[0] = pad_s[0:POOL_PAD, :]


def _pool_prompt(x, p_all, layer, g, win, wgrp, sc, wo, pg, pe, fg, tm):
    B, T, D = x.shape
    E = wo.shape[0]
    row = lambda width: pl.BlockSpec((1, tm, width), lambda b, t: (b, t, 0))
    p_spec = pl.BlockSpec((None, 1, tm, p_all.shape[-1]), lambda b, t: (layer, b, t, 0))
    return pl.pallas_call(
        _pool_prompt_kernel,
        grid=(B, T // tm),
        in_specs=[row(D), p_spec] + [_resident(a.shape) for a in (g, win, wgrp, sc, wo, pg, pe, fg)],
        out_specs=[row(D), pl.BlockSpec((1, POOL_PAD, E), lambda b, t: (b, 0, 0))],
        out_shape=[jax.ShapeDtypeStruct((B, T, D), F32), jax.ShapeDtypeStruct((B, POOL_PAD, E), F32)],
        scratch_shapes=[pltpu.VMEM((POOL_PAD + tm, E), F32)],
        compiler_params=_params("parallel", "arbitrary"),
        name="pool_prompt",
    )(x, p_all, g, win, wgrp, sc, wo, pg, pe, fg)


def _pool_sample_kernel(x_ref, p_ref, g_ref, win_ref, wgrp_ref, sc_ref, wo_ref, pg_ref, pe_ref, fg_ref, st_ref,
                        xo_ref, xp_ref, *, pos):
    E = wo_ref.shape[0]
    G = E // len(D_WINDOWS)
    x = x_ref[0]
    h = _rms(x, g_ref[...]).astype(BF16)
    acc = jnp.zeros(x.shape, F32)
    for grp, w in enumerate(D_WINDOWS):
        cols = slice(grp * G, (grp + 1) * G)
        xp = _mm(h, win_ref[:, cols])
        z = _mm(h, win_ref[:, E + grp * G:E + (grp + 1) * G])
        wsum = xp
        for back in range(1, w):
            lo = (POOL_STATE - back) * E + grp * G
            wsum = wsum + st_ref[:, lo:lo + G]
        r = wsum / float(min(w, pos + 1)) - xp
        y = _mm(r.astype(BF16), wgrp_ref[grp]) * sc_ref[:, cols]
        acc = acc + _mm((y * _silu(z)).astype(BF16), wo_ref[cols, :])
        xp_ref[:, cols] = xp
    xo_ref[0] = _pool_tail(x, acc, p_ref[0], pg_ref, pe_ref, fg_ref)


def _pool_sample(x, p_all, layer, g, win, wgrp, sc, wo, pg, pe, fg, st, pos):
    _, M, D = x.shape
    E = wo.shape[0]
    p_spec = pl.BlockSpec((None, 1, M, p_all.shape[-1]), lambda i: (layer, 0, 0, 0))
    return pl.pallas_call(
        functools.partial(_pool_sample_kernel, pos=pos),
        grid=(1,),
        in_specs=[_resident(x.shape), p_spec] + [_resident(a.shape) for a in (g, win, wgrp, sc, wo, pg, pe, fg, st)],
        out_specs=[_whole(x.shape), _whole((M, E))],
        out_shape=[jax.ShapeDtypeStruct(x.shape, F32), jax.ShapeDtypeStruct((M, E), F32)],
        compiler_params=_params("arbitrary"),
        name="pool_sample",
    )(x, p_all, g, win, wgrp, sc, wo, pg, pe, fg, st)


def _rope_tables(pos):
    half = ROPE_DIM // 2
    inv = ROPE_THETA ** (-jnp.arange(half, dtype=F32) / half)
    ang = pos.astype(F32)[:, None] * inv[None, :]
    cos, sin = jnp.cos(ang), jnp.sin(ang)
    rest = C_DH - ROPE_DIM
    n = pos.shape[0]
    cos_t = jnp.concatenate([cos, cos, jnp.ones((n, rest), F32)], axis=1)
    sin_lo = jnp.concatenate([-sin, jnp.zeros((n, half + rest), F32)], axis=1)
    sin_hi = jnp.concatenate([jnp.zeros((n, half), F32), sin, jnp.zeros((n, rest), F32)], axis=1)
    return cos_t, sin_lo, sin_hi


def _row(a):
    return a.reshape(1, -1).astype(F32)


def kernel(x_prompt, x_sample, state_mlstm_C, state_mlstm_n, state_mlstm_m, state_conv, cache_k_w128, cache_v_w128, cache_k_w512, cache_v_w512, cache_k_w2048, cache_v_w2048, state_pool, p_prompt, p_sample, norm_g, pe_w, pg_w, final_g, a_w_in, a_b_if, a_norm_g, a_w_out, b_w_in, b_conv_w, b_w_out, c_w_in, c_w_out, d_w_in, d_w_grp, d_scale, d_w_out):
    B, T, D = x_prompt.shape
    Bs, Ts, _ = x_sample.shape
    assert Ts == 1 and norm_g.shape[0] == 4 and a_w_in.shape[0] == 1
    assert T % 512 == 0 and all(T % (dil * C_BLOCK) == 0 for _, dil in C_GROUPS)
    assert Bs % 8 == 0

    bf = lambda a: a.astype(BF16)
    qk = A_HEADS * A_DK
    n_main = 2 * qk + 3 * A_WIDTH
    a_w = bf(a_w_in[0])
    a_wg = bf(jnp.pad(a_w_in[0, :, n_main:], ((0, 0), (0, LANES - 2 * A_HEADS))))
    a_bif = jnp.pad(a_b_if[0], (0, LANES - 2 * A_HEADS)).reshape(1, LANES)
    a_ng = _row(a_norm_g[0])
    a_wo, b_wi, b_wo, c_wi, c_wo, d_wi, d_wg, d_wo = (
        bf(a_w_out[0]), bf(b_w_in[0]), bf(b_w_out[0]), bf(c_w_in[0]), bf(c_w_out[0]),
        bf(d_w_in[0]), bf(d_w_grp[0]), bf(d_w_out[0]))
    pg, pe = bf(pg_w), bf(pe_w)
    gs = [_row(norm_g[i]) for i in range(4)]
    fg = _row(final_g)
    b_cw = b_conv_w[0]
    d_sc = _row(d_scale[0])
    caches = [(cache_k_w128[0], cache_v_w128[0]), (cache_k_w512[0], cache_v_w512[0]),
              (cache_k_w2048[0], cache_v_w2048[0])]

    x = x_prompt
    q, k, v, oz, gates, kt = _a_in(x, gs[0], a_w, a_wg, a_bif, 512, transposed_k=True)
    hh, c_p, n_p, m_p = _mlstm_prompt(q, k, kt, v, gates)
    x = _a_out(hh, oz, a_ng, x, p_prompt, 0, a_wo, pg[0], pe[0], 512)
    x, conv_p = _conv_prompt(x, p_prompt, 1, gs[1], b_wi, b_cw, b_wo, pg[1], pe[1], 512)
    qkv, z, tails_p = _c_in(x, gs[2], c_wi, *_rope_tables(jnp.arange(T)), 256)
    accs, stats = [], []
    for grp, (win, dil) in enumerate(C_GROUPS):
        acc, st = _attn_prompt_group(qkv[3 * grp], qkv[3 * grp + 1], qkv[3 * grp + 2], win // dil)
        accs.append(acc)
        stats.append(st)
    x = _merge_out(accs, stats, z, x, p_prompt, 2, c_wo, pg[2], pe[2], 512)
    y_prompt, pool_p = _pool_prompt(x, p_prompt, 3, gs[3], d_wi, d_wg, d_sc, d_wo, pg[3], pe[3], fg, 512)

    xs = x_sample.reshape(1, Bs, D)
    ps = p_sample.reshape(p_sample.shape[0], 1, Bs, p_sample.shape[-1])
    q, k, v, oz, gates = _a_in(xs, gs[0], a_w, a_wg, a_bif, Bs)
    m_lanes = jnp.pad(state_mlstm_m[0], ((0, 0), (0, LANES - A_HEADS)))
    hh, c_s, n_s, m_s = _mlstm_sample(q[0], k[0], v[0], gates[0], state_mlstm_C[0], state_mlstm_n[0], m_lanes)
    m_s = m_s[:, :, 0].T
    xs = _a_out(hh.reshape(1, Bs, A_WIDTH), oz, a_ng, xs, ps, 0, a_wo, pg[0], pe[0], Bs)
    xs, cx = _conv_sample(xs, ps, 1, gs[1], b_wi, b_cw, b_wo, pg[1], pe[1],
                          state_conv[0, :, 0, :], state_conv[0, :, 1, :])
    conv_s = jnp.stack([state_conv[0, :, 1, :], cx], axis=1)
    tables = [jnp.broadcast_to(t, (Bs, C_DH)) for t in _rope_tables(PAST_LEN + jnp.arange(1))]
    qkv, z, tails_s = _c_in(xs, gs[2], c_wi, *tables, Bs, per_row=True)
    y = _attn_sample(qkv, tails_s, caches, z)
    xs = _out_embed(y.reshape(1, Bs, C_WIDTH), xs, ps, 2, c_wo, pg[2], pe[2], Bs)
    pool_flat = state_pool[0].reshape(Bs, POOL_STATE * state_pool.shape[-1])
    ys, xp = _pool_sample(xs, ps, 3, gs[3], d_wi, d_wg, d_sc, d_wo, pg[3], pe[3], fg, pool_flat, PAST_LEN)
    pool_s = jnp.concatenate([state_pool[0, :, 1:, :], xp[:, None, :]], axis=1)

    heads = lambda a, lead: a.reshape(1, lead, a.shape[1], C_HEADS, C_DH)
    kv_out = []
    for grp in range(len(C_GROUPS)):
        for j in range(2):
            kv_out += [heads(tails_p[2 * grp + j], B), tails_s[2 * grp + j].reshape(1, Bs, 1, C_HEADS, C_DH)]
    return (y_prompt, ys.reshape(Bs, 1, D),
            c_p[None], c_s[None], n_p[None], n_s[None], m_p[None, :, 0, :A_HEADS], m_s.reshape(1, Bs, A_HEADS),
            conv_p[None], conv_s[None],
            *kv_out,
            pool_p[None, :, 1:, :], pool_s[None])
```

```python
import functools

import jax
import jax.numpy as jnp
from jax import lax
from jax.experimental import pallas as pl
from jax.experimental.pallas import tpu as pltpu

F32 = jnp.float32
BF16 = jnp.bfloat16

EPS = 1e-6
PAST_LEN = 8192

A_HEADS = 8
A_DK = 128
A_DV = 256
A_WIDTH = A_HEADS * A_DV
A_CHUNK = 128
A_CHUNKS_PER_STEP = 4

CONV_W = 3

C_HEADS = 8
C_DH = 128
C_WIDTH = C_HEADS * C_DH
C_GROUPS = ((128, 1), (512, 4), (2048, 16))
C_BLOCK = 128
ROPE_DIM = C_DH // 4
ROPE_THETA = 500000.0
LOG2_E = 1.4426950408889634
Q_SCALE = C_DH ** -0.5 * LOG2_E

D_WINDOWS = (2, 4, 8, 16)
POOL_STATE = max(D_WINDOWS) - 1
POOL_PAD = POOL_STATE + 1

LANES = 128
VMEM_LIMIT_BYTES = 56 * 2 ** 20

NT_DIMS = (((1,), (1,)), ((), ()))
TN_DIMS = (((0,), (0,)), ((), ()))


def _params(*semantics):
    return pltpu.CompilerParams(dimension_semantics=semantics, vmem_limit_bytes=VMEM_LIMIT_BYTES)


def _resident(shape):
    zeros = (0,) * len(shape)
    return pl.BlockSpec(shape, lambda *_: zeros, pipeline_mode=pl.Buffered(1))


def _whole(shape):
    zeros = (0,) * len(shape)
    return pl.BlockSpec(shape, lambda *_: zeros)


def _mm(a, b):
    return jnp.dot(a, b, preferred_element_type=F32)


def _rms(x, g):
    return x * lax.rsqrt(jnp.mean(x * x, axis=-1, keepdims=True) + EPS) * g


def _sigmoid(x):
    return 0.5 * jnp.tanh(0.5 * x) + 0.5


def _silu(x):
    return x * _sigmoid(x)


def _log_sigmoid(x):
    return jnp.minimum(x, 0.0) - jnp.log(1.0 + jnp.exp(-jnp.abs(x)))


def _embed(x1, p, pg_ref, pe_ref):
    gate = _sigmoid(_mm(x1.astype(BF16), pg_ref[...]))
    return x1 + gate * _mm(p.astype(BF16), pe_ref[...])


def _a_in_kernel(x_ref, g_ref, w_ref, wg_ref, bif_ref,
                 q_ref, k_ref, v_ref, oz_ref, gate_ref, *maybe_kt_ref):
    h = _rms(x_ref[0], g_ref[...]).astype(BF16)
    qk = A_HEADS * A_DK

    def proj(lo, width):
        return _mm(h, w_ref[:, lo:lo + width])

    q_ref[0] = proj(0, qk).astype(BF16)
    k = proj(qk, qk) * (A_DK ** -0.5)
    k_ref[0] = k.astype(BF16)
    for kt_ref in maybe_kt_ref:
        kt_ref[0] = k.T.astype(BF16)
    v_ref[0] = proj(2 * qk, A_WIDTH).astype(BF16)
    oz_ref[0] = (_sigmoid(proj(2 * qk + A_WIDTH, A_WIDTH))
                 * _silu(proj(2 * qk + 2 * A_WIDTH, A_WIDTH))).astype(BF16)
    gates = _mm(h, wg_ref[...]) + bif_ref[...]
    lane = lax.broadcasted_iota(jnp.int32, gates.shape, 1)
    gate_ref[0] = jnp.where(lane < A_HEADS, gates, _log_sigmoid(gates))


def _a_in(x, g, w, wg, bif, tm, transposed_k=False):
    B, T, D = x.shape
    qk = A_HEADS * A_DK
    row = lambda width: pl.BlockSpec((1, tm, width), lambda b, t: (b, t, 0))
    out_specs = [row(qk), row(qk), row(A_WIDTH), row(A_WIDTH), row(LANES)]
    out_shape = [jax.ShapeDtypeStruct((B, T, qk), BF16), jax.ShapeDtypeStruct((B, T, qk), BF16),
                 jax.ShapeDtypeStruct((B, T, A_WIDTH), BF16), jax.ShapeDtypeStruct((B, T, A_WIDTH), BF16),
                 jax.ShapeDtypeStruct((B, T, LANES), F32)]
    if transposed_k:
        out_specs.append(pl.BlockSpec((1, qk, tm), lambda b, t: (b, 0, t)))
        out_shape.append(jax.ShapeDtypeStruct((B, qk, T), BF16))
    return pl.pallas_call(
        _a_in_kernel,
        grid=(B, T // tm),
        in_specs=[row(D), _resident(g.shape), _resident(w.shape), _resident(wg.shape), _resident(bif.shape)],
        out_specs=out_specs,
        out_shape=out_shape,
        compiler_params=_params("parallel", "parallel"),
        name="mlstm_in",
    )(x, g, w, wg, bif)


def _mlstm_chunk_kernel(q_ref, k_ref, kt_ref, v_ref, gate_ref,
                        hh_ref, c_out, n_out, m_out, m_s, *state_s):
    chunk = pl.program_id(1)
    ct_s, nb_s = state_s[:A_HEADS], state_s[A_HEADS:]

    @pl.when(chunk == 0)
    def _():
        for ref in state_s:
            ref[...] = jnp.zeros_like(ref)
        m_s[...] = jnp.zeros_like(m_s)

    L = A_CHUNK
    row = lax.broadcasted_iota(jnp.int32, (L, LANES), 0)
    ti = lax.broadcasted_iota(jnp.int32, (L, L), 0)
    si = lax.broadcasted_iota(jnp.int32, (L, L), 1)
    causal = ti >= si
    ones = jnp.ones((L, LANES), BF16)
    heads = range(A_HEADS)
    ks = [slice(h * A_DK, (h + 1) * A_DK) for h in heads]
    vs = [slice(h * A_DV, (h + 1) * A_DV) for h in heads]

    def prefix(x, op, identity):
        step = 1
        while step < L:
            x = op(x, jnp.where(row >= step, pltpu.roll(x, step, 0), identity))
            step *= 2
        return x

    def one_chunk(rows):
        gates = gate_ref[0, rows, :]
        b = pltpu.roll(prefix(gates, jnp.add, 0.0), LANES - A_HEADS, 1)
        c = gates - b
        m_row = m_s[0:1, :]
        top = jnp.maximum(m_row, prefix(c, jnp.maximum, -jnp.inf))
        top_last = top[L - 1:L, :]
        g_all = jnp.exp(m_row - top)
        floor_all = jnp.exp(-(b + top))
        w_all = jnp.exp(c - top_last)
        decay_row = jnp.exp(m_row - top_last)
        m_s[...] = jnp.broadcast_to(b[L - 1:L, :] + top_last, m_s.shape)
        c_t = c.T

        s_raw = [_mm(q_ref[0, rows, ks[h]], kt_ref[0, ks[h], rows]) for h in heads]
        upd = []
        for h in heads:
            w_b = jnp.broadcast_to(w_all[:, h:h + 1], (L, LANES))
            wv = jnp.concatenate([w_b, w_b], axis=1) * v_ref[0, rows, vs[h]].astype(F32)
            upd.append(_mm(kt_ref[0, ks[h], rows], jnp.concatenate([wv.astype(BF16), w_b.astype(BF16)], axis=1)))

        for h in heads:
            ct = ct_s[h][...]
            nb = nb_s[h][...]
            dmat = jnp.exp(jnp.where(causal, -top[:, h:h + 1] + c_t[h:h + 1, :], -jnp.inf))
            s = s_raw[h] * dmat
            gq = g_all[:, h:h + 1] * q_ref[0, rows, ks[h]].astype(F32)
            lhs = jnp.concatenate([s.astype(BF16), gq.astype(BF16)], axis=1)
            num = _mm(lhs, jnp.concatenate([v_ref[0, rows, vs[h]], ct.astype(BF16)], axis=0))
            den = _mm(lhs, jnp.concatenate([ones, nb.astype(BF16)], axis=0))
            scale = 1.0 / jnp.maximum(jnp.abs(den), floor_all[:, h:h + 1])
            hh_ref[0, rows, vs[h]] = (num * jnp.concatenate([scale, scale], axis=1)).astype(BF16)
            decay = decay_row[:, h:h + 1]
            ct_s[h][...] = decay * ct + upd[h][:, :A_DV]
            nb_s[h][...] = decay * nb + upd[h][:, A_DV:]

    for sub in range(q_ref.shape[1] // L):
        one_chunk(slice(sub * L, (sub + 1) * L))

    @pl.when(chunk == pl.num_programs(1) - 1)
    def _():
        for h in range(A_HEADS):
            c_out[0, h] = ct_s[h][...].T
            n_out[0, h:h + 1, :] = nb_s[h][...].T[0:1, :]
        m_out[0] = m_s[...]


def _mlstm_prompt(q, k, kt, v, gates):
    B, T, _ = q.shape
    L = A_CHUNK * A_CHUNKS_PER_STEP
    qk = A_HEADS * A_DK
    row = lambda width: pl.BlockSpec((1, L, width), lambda b, c: (b, c, 0))
    state = lambda *shape: pl.BlockSpec((1,) + shape, lambda b, c: (b,) + (0,) * len(shape))
    return pl.pallas_call(
        _mlstm_chunk_kernel,
        grid=(B, T // L),
        in_specs=[row(qk), row(qk), pl.BlockSpec((1, qk, L), lambda b, c: (b, 0, c)), row(A_WIDTH), row(LANES)],
        out_specs=[row(A_WIDTH), state(A_HEADS, A_DV, A_DK), state(A_HEADS, A_DK), state(A_HEADS, LANES)],
        out_shape=[jax.ShapeDtypeStruct((B, T, A_WIDTH), BF16),
                   jax.ShapeDtypeStruct((B, A_HEADS, A_DV, A_DK), F32),
                   jax.ShapeDtypeStruct((B, A_HEADS, A_DK), F32),
                   jax.ShapeDtypeStruct((B, A_HEADS, LANES), F32)],
        scratch_shapes=[pltpu.VMEM((A_HEADS, LANES), F32)] + [pltpu.VMEM((A_DK, A_DV), F32)] * A_HEADS
        + [pltpu.VMEM((A_DK, LANES), F32)] * A_HEADS,
        compiler_params=_params("parallel", "arbitrary"),
        name="mlstm_chunks",
    )(q, k, kt, v, gates)


def _mlstm_step_kernel(q_ref, k_ref, v_ref, gate_ref, m_ref, c_ref, n_ref,
                       hh_ref, c_out, n_out, m_out, hht_s):
    head = pl.program_id(0)
    Bs = q_ref.shape[0]
    q = q_ref[...].astype(F32)
    k = k_ref[...].astype(F32)
    n = n_ref[...]
    lane = lax.broadcasted_iota(jnp.int32, (Bs, LANES), 1)
    pick = lambda a, idx: jnp.sum(jnp.where(lane == idx, a, 0.0), axis=-1, keepdims=True)
    ig = pick(gate_ref[...], head)
    lf = pick(gate_ref[...], A_HEADS + head)
    m_prev = pick(m_ref[...], head)

    inter = lf + m_prev
    m_t = jnp.maximum(inter, ig)
    dm = jnp.exp(ig - m_t)
    g = jnp.exp(inter - m_t)
    s = jnp.sum(q * k, axis=-1, keepdims=True) * dm
    den = s + g * jnp.sum(q * n, axis=-1, keepdims=True)
    scale = 1.0 / jnp.maximum(jnp.abs(den), jnp.exp(-m_t))
    n_out[...] = g * n + dm * k
    m_out[...] = jnp.broadcast_to(m_t, (Bs, LANES))

    v_pad = jnp.concatenate([v_ref[...].astype(F32), jnp.zeros((LANES - Bs, A_DV), F32)], axis=0)
    vt = v_pad.T
    hht_s[...] = jnp.zeros_like(hht_s)
    for b in range(Bs):
        c_b = c_ref[b]
        one = slice(b, b + 1)
        v_col = vt[:, one]
        cq = jnp.sum(c_b * q[one, :], axis=-1, keepdims=True)
        hht_s[:, one] = (s[one, :] * v_col + g[one, :] * cq) * scale[one, :]
        c_out[b] = g[one, :] * c_b + (dm[one, :] * v_col) * k[one, :]
    hh_ref[...] = hht_s[...].T[0:Bs, :].astype(BF16)


def _mlstm_sample(q, k, v, gates, c0, n0, m0):
    Bs = q.shape[0]
    assert Bs <= LANES
    cols = lambda width: pl.BlockSpec((Bs, width), lambda h: (0, h))
    full = pl.BlockSpec((Bs, LANES), lambda h: (0, 0))
    c_spec = pl.BlockSpec((Bs, None, A_DV, A_DK), lambda h: (0, h, 0, 0))
    hh, c1, n1, m1 = pl.pallas_call(
        _mlstm_step_kernel,
        grid=(A_HEADS,),
        in_specs=[cols(A_DK), cols(A_DK), cols(A_DV), full, full, c_spec, cols(A_DK)],
        out_specs=[cols(A_DV), c_spec, cols(A_DK), pl.BlockSpec((None, Bs, LANES), lambda h: (h, 0, 0))],
        out_shape=[jax.ShapeDtypeStruct((Bs, A_WIDTH), BF16),
                   jax.ShapeDtypeStruct(c0.shape, F32), jax.ShapeDtypeStruct((Bs, A_HEADS * A_DK), F32),
                   jax.ShapeDtypeStruct((A_HEADS, Bs, LANES), F32)],
        scratch_shapes=[pltpu.VMEM((A_DV, LANES), F32)],
        compiler_params=_params("parallel"),
        name="mlstm_step",
    )(q, k, v, gates, m0, c0, n0.reshape(Bs, A_HEADS * A_DK))
    return hh, c1, n1.reshape(n0.shape), m1


def _a_out_kernel(hh_ref, oz_ref, ng_ref, x_ref, p_ref, wo_ref, pg_ref, pe_ref, xo_ref):
    x1 = x_ref[0]
    for h in range(A_HEADS):
        vs = slice(h * A_DV, (h + 1) * A_DV)
        hh = hh_ref[0, :, vs].astype(F32)
        hn = hh * lax.rsqrt(jnp.mean(hh * hh, axis=-1, keepdims=True) + EPS) * ng_ref[:, vs]
        y = hn * oz_ref[0, :, vs].astype(F32)
        x1 = x1 + _mm(y.astype(BF16), wo_ref[vs, :])
    xo_ref[0] = _embed(x1, p_ref[0], pg_ref, pe_ref)


def _a_out(hh, oz, ng, x, p_all, layer, wo, pg, pe, tm):
    B, T, D = x.shape
    row = lambda width: pl.BlockSpec((1, tm, width), lambda b, t: (b, t, 0))
    p_spec = pl.BlockSpec((None, 1, tm, p_all.shape[-1]), lambda b, t: (layer, b, t, 0))
    return pl.pallas_call(
        _a_out_kernel,
        grid=(B, T // tm),
        in_specs=[row(A_WIDTH), row(A_WIDTH), _resident(ng.shape), row(D), p_spec,
                  _resident(wo.shape), _resident(pg.shape), _resident(pe.shape)],
        out_specs=row(D),
        out_shape=jax.ShapeDtypeStruct((B, T, D), F32),
        compiler_params=_params("parallel", "parallel"),
        name="mlstm_out",
    )(hh, oz, ng, x, p_all, wo, pg, pe)


def _out_kernel(y_ref, x_ref, p_ref, wo_ref, pg_ref, pe_ref, xo_ref):
    x1 = x_ref[0] + _mm(y_ref[0], wo_ref[...])
    xo_ref[0] = _embed(x1, p_ref[0], pg_ref, pe_ref)


def _out_embed(y, x, p_all, layer, wo, pg, pe, tm):
    B, T, D = x.shape
    row = lambda width: pl.BlockSpec((1, tm, width), lambda b, t: (b, t, 0))
    p_spec = pl.BlockSpec((None, 1, tm, p_all.shape[-1]), lambda b, t: (layer, b, t, 0))
    return pl.pallas_call(
        _out_kernel,
        grid=(B, T // tm),
        in_specs=[row(y.shape[-1]), row(D), p_spec, _resident(wo.shape), _resident(pg.shape), _resident(pe.shape)],
        out_specs=row(D),
        out_shape=jax.ShapeDtypeStruct((B, T, D), F32),
        compiler_params=_params("parallel", "parallel"),
        name="out_embed",
    )(y, x, p_all, wo, pg, pe)


CONV_COLS = 512


def _conv_prompt_kernel(x_ref, p_ref, g_ref, win_ref, cw_ref, wo_ref, pg_ref, pe_ref,
                        xo_ref, st_ref, carry_s):
    E = wo_ref.shape[0]

    @pl.when(pl.program_id(1) == 0)
    def _():
        carry_s[...] = jnp.zeros_like(carry_s)

    x = x_ref[0]
    tm = x.shape[0]
    h = _rms(x, g_ref[...]).astype(BF16)
    row = lax.broadcasted_iota(jnp.int32, (tm, CONV_COLS), 0)
    acc = jnp.zeros(x.shape, F32)
    for c in range(E // CONV_COLS):
        cols = slice(c * CONV_COLS, (c + 1) * CONV_COLS)
        part = lambda i: _mm(h, win_ref[:, i * E + c * CONV_COLS:i * E + (c + 1) * CONV_COLS])
        bg, cg, xb, z = part(0), part(1), part(2), part(3)
        cx = cg * xb
        old = carry_s[0:1, cols]
        new = carry_s[1:2, cols]
        prev1 = jnp.where(row == 0, new, pltpu.roll(cx, 1, 0))
        prev2 = jnp.where(row == 0, old, jnp.where(row == 1, new, pltpu.roll(cx, 2, 0)))
        y = cw_ref[0:1, cols] * prev2 + cw_ref[1:2, cols] * prev1 + cw_ref[2:3, cols] * cx
        acc = acc + _mm((bg * y * _silu(z)).astype(BF16), wo_ref[cols, :])
        carry_s[0:2, cols] = cx[tm - 2:tm, :]
    xo_ref[0] = _embed(x + acc, p_ref[0], pg_ref, pe_ref)
    st_ref[0] = carry_s[0:2, :]


def _conv_prompt(x, p_all, layer, g, win, cw, wo, pg, pe, tm):
    B, T, D = x.shape
    E = wo.shape[0]
    row = lambda width: pl.BlockSpec((1, tm, width), lambda b, t: (b, t, 0))
    p_spec = pl.BlockSpec((None, 1, tm, p_all.shape[-1]), lambda b, t: (layer, b, t, 0))
    return pl.pallas_call(
        _conv_prompt_kernel,
        grid=(B, T // tm),
        in_specs=[row(D), p_spec, _resident(g.shape), _resident(win.shape), _resident(cw.shape),
                  _resident(wo.shape), _resident(pg.shape), _resident(pe.shape)],
        out_specs=[row(D), pl.BlockSpec((1, CONV_W - 1, E), lambda b, t: (b, 0, 0))],
        out_shape=[jax.ShapeDtypeStruct((B, T, D), F32), jax.ShapeDtypeStruct((B, CONV_W - 1, E), F32)],
        scratch_shapes=[pltpu.VMEM((8, E), F32)],
        compiler_params=_params("parallel", "arbitrary"),
        name="conv_prompt",
    )(x, p_all, g, win, cw, wo, pg, pe)


def _conv_sample_kernel(x_ref, p_ref, g_ref, win_ref, cw_ref, wo_ref, pg_ref, pe_ref, s0_ref, s1_ref,
                        xo_ref, cx_ref):
    E = wo_ref.shape[0]
    x = x_ref[0]
    h = _rms(x, g_ref[...]).astype(BF16)
    acc = jnp.zeros(x.shape, F32)
    for c in range(E // CONV_COLS):
        cols = slice(c * CONV_COLS, (c + 1) * CONV_COLS)
        part = lambda i: _mm(h, win_ref[:, i * E + c * CONV_COLS:i * E + (c + 1) * CONV_COLS])
        bg, cg, xb, z = part(0), part(1), part(2), part(3)
        cx = cg * xb
        y = cw_ref[0:1, cols] * s0_ref[:, cols] + cw_ref[1:2, cols] * s1_ref[:, cols] + cw_ref[2:3, cols] * cx
        acc = acc + _mm((bg * y * _silu(z)).astype(BF16), wo_ref[cols, :])
        cx_ref[:, cols] = cx
    xo_ref[0] = _embed(x + acc, p_ref[0], pg_ref, pe_ref)


def _conv_sample(x, p_all, layer, g, win, cw, wo, pg, pe, s0, s1):
    _, M, D = x.shape
    E = wo.shape[0]
    p_spec = pl.BlockSpec((None, 1, M, p_all.shape[-1]), lambda i: (layer, 0, 0, 0))
    return pl.pallas_call(
        _conv_sample_kernel,
        grid=(1,),
        in_specs=[_resident(x.shape), p_spec, _resident(g.shape), _resident(win.shape), _resident(cw.shape),
                  _resident(wo.shape), _resident(pg.shape), _resident(pe.shape),
                  _resident(s0.shape), _resident(s1.shape)],
        out_specs=[_whole(x.shape), _whole((M, E))],
        out_shape=[jax.ShapeDtypeStruct(x.shape, F32), jax.ShapeDtypeStruct((M, E), F32)],
        compiler_params=_params("arbitrary"),
        name="conv_sample",
    )(x, p_all, g, win, cw, wo, pg, pe, s0, s1)


def _rope(x, cos, sin_lo, sin_hi):
    half = ROPE_DIM // 2
    return x * cos + pltpu.roll(x, C_DH - half, 1) * sin_lo + pltpu.roll(x, half, 1) * sin_hi


def _c_in_kernel(x_ref, g_ref, w_ref, cos_ref, slo_ref, shi_ref, *refs, tails, dils):
    n_groups = len(C_GROUPS)
    perm_dils = sorted({d for d in dils if d > 1})
    if perm_dils:
        perm_ref, refs = refs[0], refs[1:]
    qkv_refs = refs[:3 * n_groups]
    z_ref = refs[3 * n_groups]
    tail_refs = refs[3 * n_groups + 1:]
    h = _rms(x_ref[0], g_ref[...]).astype(BF16)
    tm = h.shape[0]
    cos, slo, shi = cos_ref[...], slo_ref[...], shi_ref[...]

    for grp in range(n_groups):
        base = 3 * grp * C_WIDTH
        uq = _mm(h, w_ref[:, base:base + C_WIDTH])
        uk = _mm(h, w_ref[:, base + C_WIDTH:base + 2 * C_WIDTH])
        uv = _mm(h, w_ref[:, base + 2 * C_WIDTH:base + 3 * C_WIDTH])
        dil = dils[grp]
        rows = tails[grp]
        out_refs = qkv_refs[3 * grp:3 * grp + 3]
        pieces = ([], [], [])
        for hd in range(C_HEADS):
            hs = slice(hd * C_DH, (hd + 1) * C_DH)
            qh = _rope(uq[:, hs], cos, slo, shi) * Q_SCALE
            kh = _rope(uk[:, hs], cos, slo, shi)
            vh = uv[:, hs]
            tail_refs[2 * grp][0, :, hs] = kh[tm - rows:, :]
            tail_refs[2 * grp + 1][0, :, hs] = vh[tm - rows:, :]
            for ref, done, val in zip(out_refs, pieces, (qh, kh, vh)):
                if dil == 1:
                    ref[0, 0, :, hs] = val.astype(BF16)
                else:
                    done.append(val.astype(BF16))
        if dil > 1:
            perm = perm_ref[perm_dils.index(dil)]
            part = tm // dil
            for ref, done in zip(out_refs, pieces):
                for pair in range(C_HEADS // 2):
                    cols = slice(2 * pair * C_DH, (2 * pair + 2) * C_DH)
                    moved = _mm(perm, jnp.concatenate(done[2 * pair:2 * pair + 2], axis=1)).astype(BF16)
                    for r in range(dil):
                        ref[0, r, :, cols] = moved[r * part:(r + 1) * part, :]

    zb = 3 * n_groups * C_WIDTH
    z_ref[0] = _silu(_mm(h, w_ref[:, zb:zb + C_WIDTH])).astype(BF16)


def _c_in(x, g, w, cos, slo, shi, tm, per_row=False):
    B, T, D = x.shape
    row = lambda width: pl.BlockSpec((1, tm, width), lambda b, t: (b, t, 0))
    table = pl.BlockSpec((tm, C_DH), lambda b, t: (t, 0))
    dils = tuple(1 if per_row else dil for _, dil in C_GROUPS)
    qkv_specs, qkv_shapes = [], []
    for dil in dils:
        qkv_specs += [pl.BlockSpec((1, dil, tm // dil, C_WIDTH), lambda b, t: (b, 0, t, 0))] * 3
        qkv_shapes += [jax.ShapeDtypeStruct((B, dil, T // dil, C_WIDTH), BF16)] * 3
    tails, tail_specs, tail_shapes = [], [], []
    for win, _ in C_GROUPS:
        keep = T if per_row else min(win, T)
        rows = min(tm, keep)
        first_row = T - keep
        tails.append(rows)

        def tail_index(b, t, first_row=first_row, rows=rows):
            return (b, jnp.maximum(((t + 1) * tm - first_row) // rows - 1, 0), 0)

        tail_specs += [pl.BlockSpec((1, rows, C_WIDTH), tail_index)] * 2
        tail_shapes += [jax.ShapeDtypeStruct((B, keep, C_WIDTH), F32)] * 2
    n_qkv = len(qkv_specs)
    perms = []
    for dil in sorted({d for d in dils if d > 1}):
        out_row = jnp.arange(tm)
        src = (out_row % (tm // dil)) * dil + out_row // (tm // dil)
        perms.append(src[:, None] == jnp.arange(tm)[None, :])
    extra = [jnp.stack(perms).astype(BF16)] if perms else []
    outs = pl.pallas_call(
        functools.partial(_c_in_kernel, tails=tuple(tails), dils=dils),
        grid=(B, T // tm),
        in_specs=[row(D), _resident(g.shape), _resident(w.shape), table, table, table]
        + [_resident(a.shape) for a in extra],
        out_specs=qkv_specs + [row(C_WIDTH)] + tail_specs,
        out_shape=qkv_shapes + [jax.ShapeDtypeStruct((B, T, C_WIDTH), BF16)] + tail_shapes,
        compiler_params=_params("parallel", "arbitrary"),
        name="attn_in",
    )(x, g, w, cos, slo, shi, *extra)
    return outs[:n_qkv], outs[n_qkv], outs[n_qkv + 1:]


ATTN_CHUNK = 2048
REGROUP_STRIDE = 4


def _attn_chunk_kernel(q_ref, kc_ref, vc_ref, *refs, span):
    acc_ref, st_ref = refs[-2:]
    kp_ref, vp_ref = refs[:-2] if len(refs) == 4 else (None, None)
    chunk = pl.program_id(2)
    qi = lax.broadcasted_iota(jnp.int32, (C_BLOCK, 2 * C_BLOCK), 0)
    kj = lax.broadcasted_iota(jnp.int32, (C_BLOCK, 2 * C_BLOCK), 1)
    dist = qi + C_BLOCK - kj
    band = (dist >= 0) & (dist <= span)
    bias_inner = jnp.where(band, 0.0, -1e30)
    bias_first = jnp.where(band & (kj >= jnp.where(chunk == 0, C_BLOCK, 0)), 0.0, -1e30)
    lane = lax.broadcasted_iota(jnp.int32, (C_BLOCK, LANES), 1)
    ones = jnp.ones((2 * C_BLOCK, LANES), BF16)

    blocks = [(cls, i) for cls in range(q_ref.shape[1]) for i in range(q_ref.shape[2] // C_BLOCK)]
    for cls, i in blocks:
        rows = slice(i * C_BLOCK, (i + 1) * C_BLOCK)

        def keys(prev_ref, cur_ref, cols):
            if i == 0:
                before = cur_ref[0, cls, rows, cols] if prev_ref is None else prev_ref[0, cls, :, cols]
                return jnp.concatenate([before, cur_ref[0, cls, rows, cols]], axis=0)
            return cur_ref[0, cls, (i - 1) * C_BLOCK:(i + 1) * C_BLOCK, cols]

        head_cols = [slice(hd * C_DH, (hd + 1) * C_DH) for hd in range(C_HEADS)]
        s_all = [lax.dot_general(q_ref[0, cls, rows, cols], keys(kp_ref, kc_ref, cols), NT_DIMS,
                                 preferred_element_type=F32) for cols in head_cols]
        stats = jnp.zeros((C_BLOCK, LANES), F32)
        for hd, cols in enumerate(head_cols):
            s = s_all[hd] + (bias_first if i == 0 else bias_inner)
            mx = jnp.max(s, axis=-1, keepdims=True)
            p = jnp.exp2(s - mx).astype(BF16)
            both = _mm(p, jnp.concatenate([keys(vp_ref, vc_ref, cols), ones], axis=1))
            acc_ref[0, cls, rows, cols] = both[:, :C_DH].astype(BF16)
            stats = jnp.where(lane == hd, mx, jnp.where(lane == C_HEADS + hd, both[:, C_DH:], stats))
        st_ref[0, cls, rows, :] = stats


def _attn_prompt_group(q, k, v, span):
    B, dil, n, _ = q.shape
    qc = min(n, ATTN_CHUNK)
    per_chunk = qc // C_BLOCK
    classes = min(dil, ATTN_CHUNK // qc)
    assert dil % classes == 0
    cur = lambda rows, width: pl.BlockSpec((1, classes, rows, width), lambda b, r, c: (b, r, c, 0))
    prev = pl.BlockSpec((1, 1, C_BLOCK, C_WIDTH), lambda b, r, c: (b, r, jnp.maximum(c * per_chunk - 1, 0), 0))
    before = [(k, prev), (v, prev)] if n > qc else []
    return pl.pallas_call(
        functools.partial(_attn_chunk_kernel, span=span),
        grid=(B, dil // classes, n // qc),
        in_specs=[cur(qc, C_WIDTH)] * 3 + [spec for _, spec in before],
        out_specs=[cur(qc, C_WIDTH), cur(qc, LANES)],
        out_shape=[jax.ShapeDtypeStruct((B, dil, n, C_WIDTH), BF16),
                   jax.ShapeDtypeStruct((B, dil, n, LANES), F32)],
        compiler_params=_params("parallel", "parallel", "arbitrary"),
        name="attn_chunks_d%d" % dil,
    )(q, k, v, *[a for a, _ in before])


def _merge_out_kernel(a0_ref, a1_ref, a2_ref, s0_ref, s1_ref, s2_ref, z_ref, x_ref, p_ref,
                      wo_ref, pg_ref, pe_ref, xo_ref, acc_s, st_s, half_s):
    tm = x_ref.shape[1]
    a_refs, s_refs = (a0_ref, a1_ref, a2_ref), (s0_ref, s1_ref, s2_ref)
    dils = [a_ref.shape[1] for a_ref in a_refs]
    for grp, dil in enumerate(dils):
        if dil == 1:
            continue
        two_steps = dil > REGROUP_STRIDE
        first, second = REGROUP_STRIDE, dil // REGROUP_STRIDE
        part = tm // first
        for r in range(dil):
            rows = pl.ds(r, tm // dil, stride=dil)
            acc = a_refs[grp][0, r].astype(F32)
            for hd in range(C_HEADS):
                piece = acc[:, hd * C_DH:(hd + 1) * C_DH]
                if two_steps:
                    half_s[hd, pl.ds((r % first) * part + r // first, tm // dil, stride=second), :] = piece
                else:
                    acc_s[grp * C_HEADS + hd, rows, :] = piece
            st_s[grp, rows, :] = s_refs[grp][0, r]
        if two_steps:
            for hd in range(C_HEADS):
                for r1 in range(first):
                    acc_s[grp * C_HEADS + hd, pl.ds(r1, part, stride=first), :] = (
                        half_s[hd, r1 * part:(r1 + 1) * part, :])

    def acc_of(grp, hd):
        if dils[grp] == 1:
            return a_refs[grp][0, 0, :, hd * C_DH:(hd + 1) * C_DH].astype(F32)
        return acc_s[grp * C_HEADS + hd]

    stats = [s_refs[grp][0, 0] if dils[grp] == 1 else st_s[grp] for grp in range(3)]
    mx = [st[:, 0:C_HEADS] for st in stats]
    ls = [st[:, C_HEADS:2 * C_HEADS] for st in stats]
    top = jnp.maximum(jnp.maximum(mx[0], mx[1]), mx[2])
    es = [jnp.exp2(m - top) for m in mx]
    den = es[0] * ls[0] + es[1] * ls[1] + es[2] * ls[2]
    coef = [e / den for e in es]
    x1 = x_ref[0]
    for pair in range(C_HEADS // 2):
        ys = []
        for hd in (2 * pair, 2 * pair + 1):
            hs = slice(hd * C_DH, (hd + 1) * C_DH)
            o = coef[0][:, hd:hd + 1] * acc_of(0, hd)
            for grp in (1, 2):
                o = o + coef[grp][:, hd:hd + 1] * acc_of(grp, hd)
            ys.append((o * z_ref[0, :, hs].astype(F32)).astype(BF16))
        x1 = x1 + _mm(jnp.concatenate(ys, axis=1), wo_ref[2 * pair * C_DH:(2 * pair + 2) * C_DH, :])
    xo_ref[0] = _embed(x1, p_ref[0], pg_ref, pe_ref)


def _merge_out(accs, stats, z, x, p_all, layer, wo, pg, pe, tm):
    B, T, D = x.shape
    row = lambda width: pl.BlockSpec((1, tm, width), lambda b, t: (b, t, 0))
    classes = lambda a: pl.BlockSpec((1, a.shape[1], tm // a.shape[1], a.shape[3]), lambda b, t: (b, 0, t, 0))
    p_spec = pl.BlockSpec((None, 1, tm, p_all.shape[-1]), lambda b, t: (layer, b, t, 0))
    return pl.pallas_call(
        _merge_out_kernel,
        grid=(B, T // tm),
        in_specs=[classes(a) for a in accs] + [classes(s) for s in stats] + [row(C_WIDTH), row(D), p_spec,
                  _resident(wo.shape), _resident(pg.shape), _resident(pe.shape)],
        out_specs=row(D),
        out_shape=jax.ShapeDtypeStruct((B, T, D), F32),
        scratch_shapes=[pltpu.VMEM((3 * C_HEADS, tm, C_DH), F32), pltpu.VMEM((3, tm, LANES), F32),
                        pltpu.VMEM((C_HEADS, tm, C_DH), F32)],
        compiler_params=_params("parallel", "parallel"),
        name="attn_merge_out",
    )(*accs, *stats, z, x, p_all, wo, pg, pe)


def _attn_sample_kernel(*refs):
    n_groups = len(C_GROUPS)
    z_ref, y_ref = refs[5 * n_groups], refs[5 * n_groups + 1]
    parts = []
    for grp in range(n_groups):
        q_ref, kn_ref, vn_ref, kc_ref, vc_ref = refs[5 * grp:5 * grp + 5]
        q = q_ref[0].astype(F32)
        s_old = jnp.sum(kc_ref[0] * q[None], axis=-1, keepdims=True)
        s_new = jnp.sum(kn_ref[0] * q, axis=-1, keepdims=True)
        mx = jnp.maximum(jnp.max(s_old, axis=0), s_new)
        p_old = jnp.exp2(s_old - mx[None])
        p_new = jnp.exp2(s_new - mx)
        l = jnp.sum(p_old, axis=0) + p_new
        acc = jnp.sum(p_old * vc_ref[0], axis=0) + p_new * vn_ref[0]
        parts.append((acc, mx, l))
    top = jnp.maximum(jnp.maximum(parts[0][1], parts[1][1]), parts[2][1])
    es = [jnp.exp2(m - top) for _, m, _ in parts]
    num = es[0] * parts[0][0] + es[1] * parts[1][0] + es[2] * parts[2][0]
    den = es[0] * parts[0][2] + es[1] * parts[1][2] + es[2] * parts[2][2]
    y_ref[0] = (num / den * z_ref[0].astype(F32)).astype(BF16)


def _attn_sample(qkv, tails, caches, z):
    Bs = z.shape[1]
    heads = lambda a: a.reshape(Bs, C_HEADS, C_DH)
    row = pl.BlockSpec((1, C_HEADS, C_DH), lambda b: (b, 0, 0))
    args, specs = [], []
    for grp, (win, dil) in enumerate(C_GROUPS):
        kc, vc = caches[grp]
        n_buf = kc.shape[1]
        assert n_buf == win and win % dil == 0, "sample window must be fully cached"
        span = win // dil
        view = lambda a: a.reshape(Bs, span, dil, C_HEADS, C_DH)
        cache_spec = pl.BlockSpec((1, span, None, C_HEADS, C_DH), lambda b: (b, 0, 0, 0, 0))
        args += [heads(qkv[3 * grp]), heads(tails[2 * grp]), heads(tails[2 * grp + 1]), view(kc), view(vc)]
        specs += [row, row, row, cache_spec, cache_spec]
    return pl.pallas_call(
        _attn_sample_kernel,
        grid=(Bs,),
        in_specs=specs + [row],
        out_specs=row,
        out_shape=jax.ShapeDtypeStruct((Bs, C_HEADS, C_DH), BF16),
        compiler_params=_params("parallel"),
        name="attn_sample",
    )(*args, heads(z))


def _pool_tail(x, acc, p, pg_ref, pe_ref, fg_ref):
    return _rms(_embed(x + acc, p, pg_ref, pe_ref), fg_ref[...])


def _pool_prompt_kernel(x_ref, p_ref, g_ref, win_ref, wgrp_ref, sc_ref, wo_ref, pg_ref, pe_ref, fg_ref,
                        xo_ref, st_ref, pad_s):
    E = wo_ref.shape[0]
    G = E // len(D_WINDOWS)
    t = pl.program_id(1)

    @pl.when(t == 0)
    def _():
        pad_s[0:POOL_PAD, :] = jnp.zeros((POOL_PAD, E), F32)

    x = x_ref[0]
    tm = x.shape[0]
    h = _rms(x, g_ref[...]).astype(BF16)
    pos = t * tm + lax.broadcasted_iota(jnp.int32, (tm, 1), 0)
    acc = jnp.zeros(x.shape, F32)
    for grp, w in enumerate(D_WINDOWS):
        cols = slice(grp * G, (grp + 1) * G)
        xp = _mm(h, win_ref[:, cols])
        z = _mm(h, win_ref[:, E + grp * G:E + (grp + 1) * G])
        pad_s[POOL_PAD:, cols] = xp
        wsum = pad_s[:, cols]
        shift = 1
        while shift < w:
            wsum = wsum + pltpu.roll(wsum, shift, 0)
            shift *= 2
        cnt = jnp.minimum(w, pos + 1).astype(F32)
        r = wsum[POOL_PAD:, :] / cnt - xp
        y = _mm(r.astype(BF16), wgrp_ref[grp]) * sc_ref[:, cols]
        acc = acc + _mm((y * _silu(z)).astype(BF16), wo_ref[cols, :])
        pad_s[0:POOL_PAD, cols] = xp[tm - POOL_PAD:, :]
    xo_ref[0] = _pool_tail(x, acc, p_ref[0], pg_ref, pe_ref, fg_ref)
    st_ref[0] = pad_s[0:POOL_PAD, :]


def _pool_prompt(x, p_all, layer, g, win, wgrp, sc, wo, pg, pe, fg, tm):
    B, T, D = x.shape
    E = wo.shape[0]
    row = lambda width: pl.BlockSpec((1, tm, width), lambda b, t: (b, t, 0))
    p_spec = pl.BlockSpec((None, 1, tm, p_all.shape[-1]), lambda b, t: (layer, b, t, 0))
    return pl.pallas_call(
        _pool_prompt_kernel,
        grid=(B, T // tm),
        in_specs=[row(D), p_spec] + [_resident(a.shape) for a in (g, win, wgrp, sc, wo, pg, pe, fg)],
        out_specs=[row(D), pl.BlockSpec((1, POOL_PAD, E), lambda b, t: (b, 0, 0))],
        out_shape=[jax.ShapeDtypeStruct((B, T, D), F32), jax.ShapeDtypeStruct((B, POOL_PAD, E), F32)],
        scratch_shapes=[pltpu.VMEM((POOL_PAD + tm, E), F32)],
        compiler_params=_params("parallel", "arbitrary"),
        name="pool_prompt",
    )(x, p_all, g, win, wgrp, sc, wo, pg, pe, fg)


def _pool_sample_kernel(x_ref, p_ref, g_ref, win_ref, wgrp_ref, sc_ref, wo_ref, pg_ref, pe_ref, fg_ref, st_ref,
                        xo_ref, xp_ref, *, pos):
    E = wo_ref.shape[0]
    G = E // len(D_WINDOWS)
    x = x_ref[0]
    h = _rms(x, g_ref[...]).astype(BF16)
    acc = jnp.zeros(x.shape, F32)
    for grp, w in enumerate(D_WINDOWS):
        cols = slice(grp * G, (grp + 1) * G)
        xp = _mm(h, win_ref[:, cols])
        z = _mm(h, win_ref[:, E + grp * G:E + (grp + 1) * G])
        wsum = xp
        for back in range(1, w):
            lo = (POOL_STATE - back) * E + grp * G
            wsum = wsum + st_ref[:, lo:lo + G]
        r = wsum / float(min(w, pos + 1)) - xp
        y = _mm(r.astype(BF16), wgrp_ref[grp]) * sc_ref[:, cols]
        acc = acc + _mm((y * _silu(z)).astype(BF16), wo_ref[cols, :])
        xp_ref[:, cols] = xp
    xo_ref[0] = _pool_tail(x, acc, p_ref[0], pg_ref, pe_ref, fg_ref)


def _pool_sample(x, p_all, layer, g, win, wgrp, sc, wo, pg, pe, fg, st, pos):
    _, M, D = x.shape
    E = wo.shape[0]
    p_spec = pl.BlockSpec((None, 1, M, p_all.shape[-1]), lambda i: (layer, 0, 0, 0))
    return pl.pallas_call(
        functools.partial(_pool_sample_kernel, pos=pos),
        grid=(1,),
        in_specs=[_resident(x.shape), p_spec] + [_resident(a.shape) for a in (g, win, wgrp, sc, wo, pg, pe, fg, st)],
        out_specs=[_whole(x.shape), _whole((M, E))],
        out_shape=[jax.ShapeDtypeStruct(x.shape, F32), jax.ShapeDtypeStruct((M, E), F32)],
        compiler_params=_params("arbitrary"),
        name="pool_sample",
    )(x, p_all, g, win, wgrp, sc, wo, pg, pe, fg, st)


def _rope_tables(pos):
    half = ROPE_DIM // 2
    inv = ROPE_THETA ** (-jnp.arange(half, dtype=F32) / half)
    ang = pos.astype(F32)[:, None] * inv[None, :]
    cos, sin = jnp.cos(ang), jnp.sin(ang)
    rest = C_DH - ROPE_DIM
    n = pos.shape[0]
    cos_t = jnp.concatenate([cos, cos, jnp.ones((n, rest), F32)], axis=1)
    sin_lo = jnp.concatenate([-sin, jnp.zeros((n, half + rest), F32)], axis=1)
    sin_hi = jnp.concatenate([jnp.zeros((n, half), F32), sin, jnp.zeros((n, rest), F32)], axis=1)
    return cos_t, sin_lo, sin_hi


def _row(a):
    return a.reshape(1, -1).astype(F32)


def kernel(x_prompt, x_sample, state_mlstm_C, state_mlstm_n, state_mlstm_m, state_conv, cache_k_w128, cache_v_w128, cache_k_w512, cache_v_w512, cache_k_w2048, cache_v_w2048, state_pool, p_prompt, p_sample, norm_g, pe_w, pg_w, final_g, a_w_in, a_b_if, a_norm_g, a_w_out, b_w_in, b_conv_w, b_w_out, c_w_in, c_w_out, d_w_in, d_w_grp, d_scale, d_w_out):
    B, T, D = x_prompt.shape
    Bs, Ts, _ = x_sample.shape
    assert Ts == 1 and norm_g.shape[0] == 4 and a_w_in.shape[0] == 1
    assert T % 512 == 0 and all(T % (dil * C_BLOCK) == 0 for _, dil in C_GROUPS)
    assert Bs % 8 == 0

    bf = lambda a: a.astype(BF16)
    qk = A_HEADS * A_DK
    n_main = 2 * qk + 3 * A_WIDTH
    a_w = bf(a_w_in[0])
    a_wg = bf(jnp.pad(a_w_in[0, :, n_main:], ((0, 0), (0, LANES - 2 * A_HEADS))))
    a_bif = jnp.pad(a_b_if[0], (0, LANES - 2 * A_HEADS)).reshape(1, LANES)
    a_ng = _row(a_norm_g[0])
    a_wo, b_wi, b_wo, c_wi, c_wo, d_wi, d_wg, d_wo = (
        bf(a_w_out[0]), bf(b_w_in[0]), bf(b_w_out[0]), bf(c_w_in[0]), bf(c_w_out[0]),
        bf(d_w_in[0]), bf(d_w_grp[0]), bf(d_w_out[0]))
    pg, pe = bf(pg_w), bf(pe_w)
    gs = [_row(norm_g[i]) for i in range(4)]
    fg = _row(final_g)
    b_cw = b_conv_w[0]
    d_sc = _row(d_scale[0])
    caches = [(cache_k_w128[0], cache_v_w128[0]), (cache_k_w512[0], cache_v_w512[0]),
              (cache_k_w2048[0], cache_v_w2048[0])]

    x = x_prompt
    q, k, v, oz, gates, kt = _a_in(x, gs[0], a_w, a_wg, a_bif, 512, transposed_k=True)
    hh, c_p, n_p, m_p = _mlstm_prompt(q, k, kt, v, gates)
    x = _a_out(hh, oz, a_ng, x, p_prompt, 0, a_wo, pg[0], pe[0], 512)
    x, conv_p = _conv_prompt(x, p_prompt, 1, gs[1], b_wi, b_cw, b_wo, pg[1], pe[1], 512)
    qkv, z, tails_p = _c_in(x, gs[2], c_wi, *_rope_tables(jnp.arange(T)), 256)
    accs, stats = [], []
    for grp, (win, dil) in enumerate(C_GROUPS):
        acc, st = _attn_prompt_group(qkv[3 * grp], qkv[3 * grp + 1], qkv[3 * grp + 2], win // dil)
        accs.append(acc)
        stats.append(st)
    x = _merge_out(accs, stats, z, x, p_prompt, 2, c_wo, pg[2], pe[2], 512)
    y_prompt, pool_p = _pool_prompt(x, p_prompt, 3, gs[3], d_wi, d_wg, d_sc, d_wo, pg[3], pe[3], fg, 512)

    xs = x_sample.reshape(1, Bs, D)
    ps = p_sample.reshape(p_sample.shape[0], 1, Bs, p_sample.shape[-1])
    q, k, v, oz, gates = _a_in(xs, gs[0], a_w, a_wg, a_bif, Bs)
    m_lanes = jnp.pad(state_mlstm_m[0], ((0, 0), (0, LANES - A_HEADS)))
    hh, c_s, n_s, m_s = _mlstm_sample(q[0], k[0], v[0], gates[0], state_mlstm_C[0], state_mlstm_n[0], m_lanes)
    m_s = m_s[:, :, 0].T
    xs = _a_out(hh.reshape(1, Bs, A_WIDTH), oz, a_ng, xs, ps, 0, a_wo, pg[0], pe[0], Bs)
    xs, cx = _conv_sample(xs, ps, 1, gs[1], b_wi, b_cw, b_wo, pg[1], pe[1],
                          state_conv[0, :, 0, :], state_conv[0, :, 1, :])
    conv_s = jnp.stack([state_conv[0, :, 1, :], cx], axis=1)
    tables = [jnp.broadcast_to(t, (Bs, C_DH)) for t in _rope_tables(PAST_LEN + jnp.arange(1))]
    qkv, z, tails_s = _c_in(xs, gs[2], c_wi, *tables, Bs, per_row=True)
    y = _attn_sample(qkv, tails_s, caches, z)
    xs = _out_embed(y.reshape(1, Bs, C_WIDTH), xs, ps, 2, c_wo, pg[2], pe[2], Bs)
    pool_flat = state_pool[0].reshape(Bs, POOL_STATE * state_pool.shape[-1])
    ys, xp = _pool_sample(xs, ps, 3, gs[3], d_wi, d_wg, d_sc, d_wo, pg[3], pe[3], fg, pool_flat, PAST_LEN)
    pool_s = jnp.concatenate([state_pool[0, :, 1:, :], xp[:, None, :]], axis=1)

    heads = lambda a, lead: a.reshape(1, lead, a.shape[1], C_HEADS, C_DH)
    kv_out = []
    for grp in range(len(C_GROUPS)):
        for j in range(2):
            kv_out += [heads(tails_p[2 * grp + j], B), tails_s[2 * grp + j].reshape(1, Bs, 1, C_HEADS, C_DH)]
    return (y_prompt, ys.reshape(Bs, 1, D),
            c_p[None], c_s[None], n_p[None], n_s[None], m_p[None, :, 0, :A_HEADS], m_s.reshape(1, Bs, A_HEADS),
            conv_p[None], conv_s[None],
            *kv_out,
            pool_p[None, :, 1:, :], pool_s[None])
```
